```python
import math
import jax, jax.numpy as jnp
from jax import lax
import numpy as np

D_MODEL = 1024
BATCH = 8
SEQ = 2048
DEPTH = 1
DEC_BATCH = 32
DEC_SEQ = 1
PAST_LEN = 8192
PAGE_SIZE = 128

MIX_WIDTH = D_MODEL
W_DIFF = MIX_WIDTH // 2
W_DELTA = MIX_WIDTH - W_DIFF
H_DIFF = 4
DH_DIFF = W_DIFF // (2 * H_DIFF)
DV_DIFF = 2 * DH_DIFF
H_DELTA = 4
DK_DELTA = W_DELTA // H_DELTA
DV_DELTA = W_DELTA // H_DELTA
ROT_DIM = DH_DIFF // 4
ROPE_THETA = 500000.0
CONV_W = 4
CONV_CH = 3 * W_DELTA
CHUNK = 64
Q_BLOCK = 128
OFF_DQ = 0
OFF_DK = OFF_DQ + W_DIFF
OFF_DV = OFF_DK + W_DIFF
OFF_DG = OFF_DV + W_DIFF
OFF_LQ = OFF_DG + W_DIFF
OFF_LK = OFF_LQ + W_DELTA
OFF_LV = OFF_LK + W_DELTA
OFF_LZ = OFF_LV + W_DELTA
OFF_LB = OFF_LZ + W_DELTA
OFF_LA = OFF_LB + H_DELTA
P_IN = OFF_LA + H_DELTA
DEEPNORM_ALPHA = (2 * DEPTH) ** 0.25
DEEPNORM_BETA = (8 * DEPTH) ** -0.25
LN_EPS = 1e-5
HEAD_NORM_EPS = 1e-5
DELTA_NORM_EPS = 1e-6
L2_EPS = 1e-6
NEG_INF = -1e30

kernel_name = 'hymba_diffattn_gated_deltanet_step'


def lambda_init_for(layer):
    return 0.8 - 0.6 * math.exp(-0.3 * layer)


def rms_norm(x, w, eps):
    xf = x.astype(jnp.float32)
    return xf * lax.rsqrt(jnp.mean(xf * xf, axis=-1, keepdims=True) + eps) * w.astype(jnp.float32)


def layer_norm(x, g, b):
    xf = x.astype(jnp.float32)
    mu = jnp.mean(xf, axis=-1, keepdims=True)
    var = jnp.mean(jnp.square(xf - mu), axis=-1, keepdims=True)
    return ((xf - mu) * lax.rsqrt(var + LN_EPS) * g.astype(jnp.float32) + b.astype(jnp.float32)).astype(x.dtype)


def l2_normalize(x):
    xf = x.astype(jnp.float32)
    return xf * lax.rsqrt(jnp.sum(xf * xf, axis=-1, keepdims=True) + L2_EPS)


def rope_partial(x, pos):
    half = ROT_DIM // 2
    inv = ROPE_THETA ** (-jnp.arange(half, dtype=jnp.float32) / half)
    ang = pos.astype(jnp.float32)[:, None] * inv[None, :]
    cos = jnp.cos(ang)[None, :, None, None, :]
    sin = jnp.sin(ang)[None, :, None, None, :]
    xf = x.astype(jnp.float32)
    x1, x2, rest = xf[..., :half], xf[..., half:ROT_DIM], xf[..., ROT_DIM:]
    out = jnp.concatenate([x1 * cos - x2 * sin, x2 * cos + x1 * sin, rest], axis=-1)
    return out.astype(x.dtype)


def diff_attention(q, q_pos, segments, lam):
    qf = q.astype(jnp.float32) * DH_DIFF ** -0.5
    s = jnp.concatenate([jnp.einsum('bqhmd,bkhmd->bhmqk', qf, k.astype(jnp.float32)) for k, _, _ in segments], axis=-1)
    k_pos = jnp.concatenate([kp for _, _, kp in segments])
    s = jnp.where(k_pos[None, :] <= q_pos[:, None], s, NEG_INF)
    p = jax.nn.softmax(s, axis=-1)
    a = p[:, :, 0] - lam * p[:, :, 1]
    out = None
    start = 0
    for _, v, kp in segments:
        n = kp.shape[0]
        part = jnp.einsum('bhqk,bkhe->bqhe', a[..., start:start + n], v.astype(jnp.float32))
        out = part if out is None else out + part
        start += n
    return out


def prompt_diff_attention(q, k, v, pos, lam):
    B, T = q.shape[0], q.shape[1]
    nb = T // Q_BLOCK
    qb = q.reshape((B, nb, Q_BLOCK) + q.shape[2:]).swapaxes(0, 1)
    pb = pos.reshape(nb, Q_BLOCK)
    out = lax.map(lambda a: diff_attention(a[0], a[1], [(k, v, pos)], lam), (qb, pb))
    return out.swapaxes(0, 1).reshape(B, T, H_DIFF, DV_DIFF)


def causal_conv(u, prev, w):
    T = u.shape[1]
    xp = jnp.concatenate([prev.astype(u.dtype), u], axis=1)
    out = xp[:, 0:T] * w[0]
    for j in range(1, CONV_W):
        out = out + xp[:, j:j + T] * w[j]
    return jax.nn.silu(out), xp[:, xp.shape[1] - (CONV_W - 1):]


def gated_delta_rule(q, k, v, g, beta, s0):
    B, T, H, DK = q.shape
    DV = v.shape[-1]
    pad = (-T) % CHUNK
    n = (T + pad) // CHUNK

    def chunks_vec(a):
        a = jnp.pad(a.astype(jnp.float32), [(0, 0), (0, pad), (0, 0), (0, 0)])
        return a.reshape(B, n, CHUNK, H, a.shape[-1]).transpose(1, 0, 3, 2, 4)

    def chunks_scalar(a):
        a = jnp.pad(a.astype(jnp.float32), [(0, 0), (0, pad), (0, 0)])
        return a.reshape(B, n, CHUNK, H).transpose(1, 0, 3, 2)

    qc = chunks_vec(q) * DK ** -0.5
    kc = chunks_vec(k)
    vc = chunks_vec(v)
    bc = chunks_scalar(beta)
    gc = jnp.cumsum(chunks_scalar(g), axis=-1)
    idx = jnp.arange(CHUNK)
    incl = idx[:, None] >= idx[None, :]
    strict = idx[:, None] > idx[None, :]
    decay = jnp.exp(jnp.where(incl, gc[..., :, None] - gc[..., None, :], -jnp.inf))
    kb = kc * bc[..., None]
    a_mat = jnp.where(strict, jnp.einsum('nbhcd,nbhed->nbhce', kb, kc) * decay, 0.0)
    eye = jnp.eye(CHUNK, dtype=jnp.float32)
    t_mat = lax.linalg.triangular_solve(a_mat + eye, jnp.broadcast_to(eye, a_mat.shape),
                                        left_side=True, lower=True, unit_diagonal=True)
    u = t_mat @ (vc * bc[..., None])
    w = t_mat @ (kb * jnp.exp(gc)[..., None])

    def step(S, inp):
        q_i, k_i, u_i, w_i, g_i, d_i = inp
        v_new = u_i - w_i @ S
        attn = jnp.where(incl, (q_i @ jnp.swapaxes(k_i, -1, -2)) * d_i, 0.0)
        o = (q_i * jnp.exp(g_i)[..., None]) @ S + attn @ v_new
        g_last = g_i[..., -1]
        S = S * jnp.exp(g_last)[..., None, None] + jnp.einsum(
            'bhck,bhcv->bhkv', k_i * jnp.exp(g_last[..., None] - g_i)[..., None], v_new)
        return S, o

    S, o = lax.scan(step, s0.astype(jnp.float32), (qc, kc, u, w, gc, decay))
    o = o.transpose(1, 0, 3, 2, 4).reshape(B, n * CHUNK, H, DV)[:, :T]
    return o, S


def hybrid_layer(x, pos, lam_init, attend, conv_prev, s_prev,
                 w_in, conv_w, a_log, dt_bias, delta_norm_w, diff_lambda, diff_norm_w, w_out, ln_g, ln_b):
    B, T, _ = x.shape
    h = jnp.einsum('btd,dp->btp', x, w_in)
    q = rope_partial(h[..., OFF_DQ:OFF_DK].reshape(B, T, H_DIFF, 2, DH_DIFF), pos)
    k = rope_partial(h[..., OFF_DK:OFF_DV].reshape(B, T, H_DIFF, 2, DH_DIFF), pos)
    v = h[..., OFF_DV:OFF_DG].reshape(B, T, H_DIFF, DV_DIFF)
    gate_a = h[..., OFF_DG:OFF_LQ]
    lam_p = diff_lambda.astype(jnp.float32)
    lam = jnp.exp(jnp.sum(lam_p[0] * lam_p[1])) - jnp.exp(jnp.sum(lam_p[2] * lam_p[3])) + lam_init
    o_a = attend(q, k, v, lam)
    o_a = rms_norm(o_a, diff_norm_w, HEAD_NORM_EPS) * (1.0 - lam_init)
    o_a = o_a.reshape(B, T, W_DIFF).astype(x.dtype) * jax.nn.silu(gate_a)
    c, conv_new = causal_conv(h[..., OFF_LQ:OFF_LZ], conv_prev, conv_w)
    lq = l2_normalize(c[..., 0:W_DELTA].reshape(B, T, H_DELTA, DK_DELTA))
    lk = l2_normalize(c[..., W_DELTA:2 * W_DELTA].reshape(B, T, H_DELTA, DK_DELTA))
    lv = c[..., 2 * W_DELTA:].reshape(B, T, H_DELTA, DV_DELTA)
    gate_b = h[..., OFF_LZ:OFF_LB]
    beta = jax.nn.sigmoid(h[..., OFF_LB:OFF_LA].astype(jnp.float32))
    g = -jnp.exp(a_log.astype(jnp.float32)) * jax.nn.softplus(
        h[..., OFF_LA:P_IN].astype(jnp.float32) + dt_bias.astype(jnp.float32))
    o_b, s_new = gated_delta_rule(lq, lk, lv, g, beta, s_prev)
    o_b = rms_norm(o_b, delta_norm_w, DELTA_NORM_EPS).reshape(B, T, W_DELTA).astype(x.dtype) * jax.nn.silu(gate_b)
    mix = jnp.einsum('btw,wd->btd', jnp.concatenate([o_a, o_b], axis=-1), w_out)
    y = layer_norm(DEEPNORM_ALPHA * x + mix, ln_g, ln_b)
    return y, k, v.astype(x.dtype), s_new, conv_new


def setup_inputs(seed: int = 0) -> dict:
    key = jax.random.key(seed)
    ks = jax.random.split(key, 20)
    f32 = jnp.float32
    n_pages = PAST_LEN // PAGE_SIZE
    used = DEC_BATCH * n_pages
    n_phys = used + max(1, used // 4)
    x_prompt = jax.random.normal(ks[0], (BATCH, SEQ, D_MODEL), f32)
    x_sample = jax.random.normal(ks[1], (DEC_BATCH, DEC_SEQ, D_MODEL), f32)
    cache_k = jax.random.normal(ks[2], (DEPTH, n_phys, PAGE_SIZE, H_DIFF, 2, DH_DIFF), f32)
    cache_v = jax.random.normal(ks[3], (DEPTH, n_phys, PAGE_SIZE, H_DIFF, DV_DIFF), f32)
    page_table = jax.random.permutation(ks[4], n_phys)[:used].reshape(DEC_BATCH, n_pages).astype(jnp.int32)
    state_delta = 0.1 * jax.random.normal(ks[5], (DEPTH, DEC_BATCH, H_DELTA, DK_DELTA, DV_DELTA), f32)
    state_conv = jax.random.normal(ks[6], (DEPTH, DEC_BATCH, CONV_W - 1, CONV_CH), f32)
    w_in = jax.random.normal(ks[7], (DEPTH, D_MODEL, P_IN), f32) * D_MODEL ** -0.5
    conv_w = jax.random.normal(ks[8], (DEPTH, CONV_W, CONV_CH), f32) * CONV_W ** -0.5
    a_log = jnp.log(jax.random.uniform(ks[9], (DEPTH, H_DELTA), f32, minval=1.0, maxval=16.0))
    dt = jnp.exp(jax.random.uniform(ks[10], (DEPTH, H_DELTA), f32) * (math.log(0.1) - math.log(0.001)) + math.log(0.001))
    dt_bias = dt + jnp.log(-jnp.expm1(-dt))
    delta_norm_w = 1.0 + 0.02 * jax.random.normal(ks[11], (DEPTH, DV_DELTA), f32)
    diff_lambda = 0.1 * jax.random.normal(ks[12], (DEPTH, 4, DH_DIFF), f32)
    diff_norm_w = 1.0 + 0.02 * jax.random.normal(ks[13], (DEPTH, DV_DIFF), f32)
    w_out = jax.random.normal(ks[14], (DEPTH, MIX_WIDTH, D_MODEL), f32) * (MIX_WIDTH ** -0.5 * DEEPNORM_BETA)
    ln_g = 1.0 + 0.02 * jax.random.normal(ks[15], (DEPTH, D_MODEL), f32)
    ln_b = 0.02 * jax.random.normal(ks[16], (DEPTH, D_MODEL), f32)
    return {'x_prompt': x_prompt, 'x_sample': x_sample, 'cache_k': cache_k, 'cache_v': cache_v,
            'page_table': page_table, 'state_delta': state_delta, 'state_conv': state_conv,
            'w_in': w_in, 'conv_w': conv_w, 'a_log': a_log, 'dt_bias': dt_bias, 'delta_norm_w': delta_norm_w,
            'diff_lambda': diff_lambda, 'diff_norm_w': diff_norm_w, 'w_out': w_out, 'ln_g': ln_g, 'ln_b': ln_b}


def reference(x_prompt, x_sample, cache_k, cache_v, page_table, state_delta, state_conv,
              w_in, conv_w, a_log, dt_bias, delta_norm_w, diff_lambda, diff_norm_w, w_out, ln_g, ln_b):
    Bp, Tp = x_prompt.shape[0], x_prompt.shape[1]
    Bs, Ts = x_sample.shape[0], x_sample.shape[1]
    past_len = page_table.shape[1] * cache_k.shape[2]
    pos_p = jnp.arange(Tp, dtype=jnp.int32)
    pos_s = past_len + jnp.arange(Ts, dtype=jnp.int32)
    pos_past = jnp.arange(past_len, dtype=jnp.int32)
    hp, hs = x_prompt, x_sample
    kp_l, vp_l, sp_l, cp_l, ks_l, vs_l, ss_l, cs_l = [], [], [], [], [], [], [], []
    for l in range(DEPTH):
        lam_init = lambda_init_for(l)
        params = (w_in[l], conv_w[l], a_log[l], dt_bias[l], delta_norm_w[l], diff_lambda[l],
                  diff_norm_w[l], w_out[l], ln_g[l], ln_b[l])
        attend_p = lambda q, k, v, lam: prompt_diff_attention(q, k, v, pos_p, lam)
        conv0 = jnp.zeros((Bp, CONV_W - 1, CONV_CH), hp.dtype)
        s0 = jnp.zeros((Bp, H_DELTA, DK_DELTA, DV_DELTA), jnp.float32)
        hp, kp, vp, sp, cp = hybrid_layer(hp, pos_p, lam_init, attend_p, conv0, s0, *params)
        pk = cache_k[l, page_table].reshape(Bs, past_len, H_DIFF, 2, DH_DIFF)
        pv = cache_v[l, page_table].reshape(Bs, past_len, H_DIFF, DV_DIFF)
        def attend_s(q, k, v, lam, pk=pk, pv=pv):
            return diff_attention(q, pos_s, [(pk, pv, pos_past), (k, v, pos_s)], lam)
        hs, ksn, vsn, ssn, csn = hybrid_layer(hs, pos_s, lam_init, attend_s, state_conv[l], state_delta[l], *params)
        kp_l.append(kp); vp_l.append(vp); sp_l.append(sp); cp_l.append(cp)
        ks_l.append(ksn); vs_l.append(vsn); ss_l.append(ssn); cs_l.append(csn)
    return (hp, hs, jnp.stack(kp_l), jnp.stack(vp_l), jnp.stack(sp_l), jnp.stack(cp_l),
            jnp.stack(ks_l), jnp.stack(vs_l), jnp.stack(ss_l), jnp.stack(cs_l))
```

```python
import functools
import math

import jax
import jax.numpy as jnp
from jax import lax
from jax.experimental import pallas as pl
from jax.experimental.pallas import tpu as pltpu

F32 = jnp.float32
BF16 = jnp.bfloat16
HIGHEST = lax.Precision.HIGHEST

H_DIFF = 4
DH_DIFF = 64
DV_DIFF = 2 * DH_DIFF
W_DIFF = H_DIFF * DV_DIFF
H_DELTA = 4
DK_DELTA = 128
DV_DELTA = 128
W_DELTA = H_DELTA * DK_DELTA
ROT_DIM = DH_DIFF // 4
ROPE_THETA = 500000.0
CONV_W = 4
CONV_CH = 3 * W_DELTA
CHUNK = 64
P_MAIN = 4 * W_DIFF + 4 * W_DELTA
LN_EPS = 1e-5
HEAD_NORM_EPS = 1e-5
DELTA_NORM_EPS = 1e-6
L2_EPS = 1e-6
NEG_INF = -1e30

LANES = 128
PAIR = 2 * CHUNK
VMEM_LIMIT = 56 * 1024 * 1024
DECODE_PAGES_PER_STEP = 8

_NT = (((1,), (1,)), ((), ()))


def _sigmoid(x):
    return 1.0 / (1.0 + jnp.exp(-x))


def _silu(x):
    return x * _sigmoid(x)


def _softplus(x):
    return jnp.maximum(x, 0.0) + jnp.log1p(jnp.exp(-jnp.abs(x)))


def _lambda_init(layer):
    return 0.8 - 0.6 * math.exp(-0.3 * layer)


def _rope_tables(pos):
    half = ROT_DIM // 2
    inv = ROPE_THETA ** (-jnp.arange(half, dtype=F32) / half)
    ang = pos.astype(F32)[:, None] * inv[None, :]
    cos, sin = jnp.cos(ang), jnp.sin(ang)
    t = pos.shape[0]
    rest = DH_DIFF - ROT_DIM
    cos_m = jnp.concatenate([cos, cos, jnp.ones((t, rest), F32)], axis=-1)
    sin_lo = jnp.concatenate([-sin, jnp.zeros((t, half + rest), F32)], axis=-1)
    sin_hi = jnp.concatenate([jnp.zeros((t, half), F32), sin, jnp.zeros((t, rest), F32)], axis=-1)
    reps = W_DIFF // DH_DIFF
    return jnp.tile(cos_m, (1, reps)), jnp.tile(sin_lo, (1, reps)), jnp.tile(sin_hi, (1, reps))


def _in_proj_kernel(x_ref, w_ref, wt_ref, cos_ref, slo_ref, shi_ref,
                    q_ref, k_ref, kb_ref, v_ref, vb_ref, ga_ref, cin_ref, z_ref, tail_ref):
    xb = x_ref[0].astype(BF16)
    half = ROT_DIM // 2

    def proj(c0, width):
        return jnp.dot(xb, w_ref[:, c0:c0 + width], preferred_element_type=F32)

    def rope(h):
        return (h * cos_ref[...]
                + pltpu.roll(h, W_DIFF - half, 1) * slo_ref[...]
                + pltpu.roll(h, half, 1) * shi_ref[...])

    q_ref[0] = rope(proj(0, W_DIFF)).astype(BF16)
    k = rope(proj(W_DIFF, W_DIFF))
    k_ref[0] = k
    kb_ref[0] = k.astype(BF16)
    v = proj(2 * W_DIFF, W_DIFF)
    v_ref[0] = v
    vb_ref[0] = v.astype(BF16)
    ga_ref[0] = proj(3 * W_DIFF, W_DIFF)
    for j in range(3):
        cin_ref[0, :, j * W_DELTA:(j + 1) * W_DELTA] = proj(4 * W_DIFF + j * W_DELTA, W_DELTA)
    z_ref[0] = proj(4 * W_DIFF + 3 * W_DELTA, W_DELTA)
    tail_ref[0] = jnp.dot(xb, wt_ref[...], preferred_element_type=F32)


def _in_proj(x, w_main, w_tail, tables):
    b, t, d = x.shape
    tm = min(512, t)
    assert t % tm == 0
    cos_t, slo_t, shi_t = tables
    row = lambda width: pl.BlockSpec((1, tm, width), lambda ti, bi: (bi, ti, 0))
    tab = pl.BlockSpec((tm, W_DIFF), lambda ti, bi: (ti, 0))
    const = lambda shape: pl.BlockSpec(shape, lambda ti, bi: (0, 0))
    outs = [(W_DIFF, BF16), (W_DIFF, F32), (W_DIFF, BF16), (W_DIFF, F32), (W_DIFF, BF16),
            (W_DIFF, F32), (CONV_CH, F32), (W_DELTA, F32), (LANES, F32)]
    return pl.pallas_call(
        _in_proj_kernel,
        grid=(t // tm, b),
        in_specs=[row(d), const(w_main.shape), const(w_tail.shape), tab, tab, tab],
        out_specs=[row(wd) for wd, _ in outs],
        out_shape=[jax.ShapeDtypeStruct((b, t, wd), dt) for wd, dt in outs],
        compiler_params=pltpu.CompilerParams(
            dimension_semantics=("arbitrary", "arbitrary"), vmem_limit_bytes=VMEM_LIMIT),
        name="in_proj",
    )(x, w_main, w_tail, cos_t, slo_t, shi_t)


def _diff_lambda_value(dl_ref, lam_init):
    dl = dl_ref[...]
    a = jnp.sum(dl[0:1] * dl[1:2], axis=1, keepdims=True)
    b = jnp.sum(dl[2:3] * dl[3:4], axis=1, keepdims=True)
    return jnp.exp(a) - jnp.exp(b) + lam_init


def _head_norm_gate(o, normw, gate, lam_init):
    ms = jnp.mean(o * o, axis=-1, keepdims=True)
    o = o * lax.rsqrt(ms + HEAD_NORM_EPS) * normw * (1.0 - lam_init)
    return o * _silu(gate)


def _prompt_attn_kernel(q_ref, k_ref, v_ref, ga_ref, dl_ref, nw_ref, o_ref, *, blk, lam_init):
    i = pl.program_id(2)
    lane = lax.broadcasted_iota(jnp.int32, (blk, DV_DIFF), 1)
    qs = q_ref[0].astype(F32) * DH_DIFF ** -0.5
    q_maps = (jnp.where(lane < DH_DIFF, qs, 0.0).astype(BF16), jnp.where(lane >= DH_DIFF, qs, 0.0).astype(BF16))

    def step(j, carry, masked):
        r0 = pl.multiple_of(j * blk, blk)
        k = k_ref[0, pl.ds(r0, blk), :]
        v = v_ref[0, pl.ds(r0, blk), :]
        new = []
        for qm, (m, l, acc) in zip(q_maps, carry):
            s = lax.dot_general(qm, k, _NT, preferred_element_type=F32)
            if masked:
                row = lax.broadcasted_iota(jnp.int32, (blk, blk), 0)
                col = lax.broadcasted_iota(jnp.int32, (blk, blk), 1)
                s = jnp.where(col <= row, s, NEG_INF)
            m_new = jnp.maximum(m, jnp.max(s, axis=-1, keepdims=True))
            p = jnp.exp(s - m_new)
            alpha = jnp.exp(m - m_new)
            l = alpha * l + jnp.sum(p, axis=-1, keepdims=True)
            acc = alpha * acc + jnp.dot(p.astype(BF16), v, preferred_element_type=F32)
            new.append((m_new, l, acc))
        return tuple(new)

    init = tuple((jnp.full((blk, 1), NEG_INF, F32), jnp.zeros((blk, 1), F32), jnp.zeros((blk, DV_DIFF), F32))
                 for _ in range(2))
    carry = lax.fori_loop(0, i, lambda j, c: step(j, c, False), init)
    (_, l1, acc1), (_, l2, acc2) = step(i, carry, True)
    lam = _diff_lambda_value(dl_ref, lam_init)
    o = acc1 / l1 - lam * (acc2 / l2)
    o_ref[0] = _head_norm_gate(o, nw_ref[...], ga_ref[0], lam_init).astype(BF16)


def _prompt_attn(q, kb, vb, ga, diff_lambda, norm_w, lam_init):
    b, t, _ = q.shape
    blk = min(256, t)
    assert t % blk == 0
    qspec = pl.BlockSpec((1, blk, DV_DIFF), lambda bi, h, i: (bi, i, h))
    kvspec = pl.BlockSpec((1, t, DV_DIFF), lambda bi, h, i: (bi, 0, h))
    return pl.pallas_call(
        functools.partial(_prompt_attn_kernel, blk=blk, lam_init=lam_init),
        grid=(b, H_DIFF, t // blk),
        in_specs=[qspec, kvspec, kvspec, qspec,
                  pl.BlockSpec(diff_lambda.shape, lambda bi, h, i: (0, 0)),
                  pl.BlockSpec((1, DV_DIFF), lambda bi, h, i: (0, 0))],
        out_specs=qspec,
        out_shape=jax.ShapeDtypeStruct((b, t, W_DIFF), BF16),
        compiler_params=pltpu.CompilerParams(
            dimension_semantics=("arbitrary", "arbitrary", "arbitrary"), vmem_limit_bytes=VMEM_LIMIT),
        name="prompt_attn",
    )(q, kb, vb, ga, diff_lambda, norm_w.reshape(1, DV_DIFF))


def _decode_attn_kernel(pt_ref, q_ref, kn_ref, vn_ref, ga_ref, dl_ref, nw_ref, *rest, pages, lam_init):
    kp_refs, vp_refs = rest[:pages], rest[pages:2 * pages]
    o_ref, m_ref, l_ref, acc_ref = rest[2 * pages:]
    g = pl.program_id(1)
    n_maps = 2 * H_DIFF

    @pl.when(g == 0)
    def _():
        m_ref[...] = jnp.full(m_ref.shape, NEG_INF, F32)
        l_ref[...] = jnp.zeros(l_ref.shape, F32)
        acc_ref[...] = jnp.zeros(acc_ref.shape, F32)

    qs = q_ref[0].astype(F32) * DH_DIFF ** -0.5
    rows = lax.broadcasted_iota(jnp.int32, (n_maps, W_DIFF), 0)
    lanes = lax.broadcasted_iota(jnp.int32, (n_maps, W_DIFF), 1)
    q_blk = jnp.where((lanes >> 6) == rows, jnp.broadcast_to(qs, (n_maps, W_DIFF)), 0.0).astype(BF16)

    s = jnp.concatenate(
        [lax.dot_general(q_blk, kp[0].astype(BF16), _NT, preferred_element_type=F32) for kp in kp_refs], axis=1)
    m_prev = m_ref[:, 0:1]
    m_new = jnp.maximum(m_prev, jnp.max(s, axis=-1, keepdims=True))
    p = jnp.exp(s - m_new)
    alpha = jnp.exp(m_prev - m_new)
    l_new = alpha * l_ref[:, 0:1] + jnp.sum(p, axis=-1, keepdims=True)
    pv = jnp.zeros(acc_ref.shape, F32)
    for i, vp in enumerate(vp_refs):
        pv = pv + jnp.dot(p[:, i * LANES:(i + 1) * LANES].astype(BF16), vp[0].astype(BF16),
                          preferred_element_type=F32)
    acc_new = alpha * acc_ref[...] + pv
    m_ref[...] = jnp.broadcast_to(m_new, m_ref.shape)
    l_ref[...] = jnp.broadcast_to(l_new, l_ref.shape)
    acc_ref[...] = acc_new

    @pl.when(g == pl.num_programs(1) - 1)
    def _():
        k_self = jnp.broadcast_to(kn_ref[0].astype(BF16), (8, W_DIFF))
        s_self = lax.dot_general(q_blk, k_self, _NT, preferred_element_type=F32)[:, 0:1]
        m_f = jnp.maximum(m_new, s_self)
        a_f = jnp.exp(m_new - m_f)
        p_self = jnp.exp(s_self - m_f)
        l_f = a_f * l_new + p_self
        v_self = vn_ref[0].astype(BF16).astype(F32)
        acc_f = (a_f * acc_new + p_self.astype(BF16).astype(F32) * v_self) / l_f
        lam = _diff_lambda_value(dl_ref, lam_init)
        for h in range(H_DIFF):
            hs = slice(h * DV_DIFF, (h + 1) * DV_DIFF)
            o = acc_f[2 * h:2 * h + 1, hs] - lam * acc_f[2 * h + 1:2 * h + 2, hs]
            o_ref[0, :, hs] = _head_norm_gate(o, nw_ref[...], ga_ref[0, :, hs], lam_init).astype(BF16)


def _decode_attn(q, k_new, v_new, ga, cache_k, cache_v, page_table, diff_lambda, norm_w, lam_init):
    bs = q.shape[0]
    n_phys, page = cache_k.shape[0], cache_k.shape[1]
    n_pages = page_table.shape[1]
    pages = math.gcd(DECODE_PAGES_PER_STEP, n_pages)
    assert page == LANES
    ck = cache_k.reshape(n_phys, page, W_DIFF)
    cv = cache_v.reshape(n_phys, page, W_DIFF)
    pt = page_table.reshape(-1)

    def page_spec(i):
        return pl.BlockSpec((1, page, W_DIFF), lambda b, g, pt_ref: (pt_ref[b * n_pages + g * pages + i], 0, 0))

    tok = pl.BlockSpec((1, 1, W_DIFF), lambda b, g, pt_ref: (b, 0, 0))
    n_maps = 2 * H_DIFF
    grid_spec = pltpu.PrefetchScalarGridSpec(
        num_scalar_prefetch=1,
        grid=(bs, n_pages // pages),
        in_specs=[tok, tok, tok, tok,
                  pl.BlockSpec(diff_lambda.shape, lambda b, g, pt_ref: (0, 0)),
                  pl.BlockSpec((1, DV_DIFF), lambda b, g, pt_ref: (0, 0))]
                 + [page_spec(i) for i in range(pages)] * 2,
        out_specs=tok,
        scratch_shapes=[pltpu.VMEM((n_maps, LANES), F32), pltpu.VMEM((n_maps, LANES), F32),
                        pltpu.VMEM((n_maps, W_DIFF), F32)],
    )
    return pl.pallas_call(
        functools.partial(_decode_attn_kernel, pages=pages, lam_init=lam_init),
        grid_spec=grid_spec,
        out_shape=jax.ShapeDtypeStruct((bs, 1, W_DIFF), BF16),
        compiler_params=pltpu.CompilerParams(
            dimension_semantics=("arbitrary", "arbitrary"), vmem_limit_bytes=VMEM_LIMIT),
        name="decode_attn",
    )(pt, q, k_new, v_new, ga, diff_lambda, norm_w.reshape(1, DV_DIFF), *([ck] * pages), *([cv] * pages))


def _gate_rows(tail, alog_row, dtb_row):
    beta = _sigmoid(tail)
    g = -jnp.exp(alog_row) * _softplus(tail + dtb_row)
    return beta, g


def _lane_bcast(x, lane, rows):
    return jnp.broadcast_to(x[:, lane:lane + 1], (rows, LANES))


def _l2norm(x):
    return x * lax.rsqrt(jnp.sum(x * x, axis=-1, keepdims=True) + L2_EPS)


def _unit_lower_inverse(a_strict):
    n = a_strict.shape[0]
    eye = (lax.broadcasted_iota(jnp.int32, (n, n), 0) == lax.broadcasted_iota(jnp.int32, (n, n), 1)).astype(F32)
    mm = lambda x, y: jnp.dot(x, y, precision=HIGHEST, preferred_element_type=F32)
    b = -a_strict
    p = eye + b
    power = 2
    while power < CHUNK:
        b = mm(b, b)
        p = p + mm(p, b)
        power *= 2
    return p


def _delta_prep_kernel(cin_ref, prev_ref, cw_ref, tail_ref, alog_ref, dtb_ref,
                       u_ref, w_ref, qg_ref, at_ref, kdt_ref, el_ref, ext_ref):
    t = pl.program_id(1)
    pad = 8
    keep = CONV_W - 1

    @pl.when(t == 0)
    def _():
        ext_ref[pad - keep:pad, :] = prev_ref[0]

    @pl.when(t > 0)
    def _():
        ext_ref[pad - keep:pad, :] = ext_ref[pad + PAIR - keep:pad + PAIR, :]

    ext_ref[pad:pad + PAIR, :] = cin_ref[0]
    conv = ext_ref[pad - keep:pad - keep + PAIR, :] * cw_ref[0:1, :]
    for j in range(1, CONV_W):
        conv = conv + ext_ref[pad - keep + j:pad - keep + j + PAIR, :] * cw_ref[j:j + 1, :]
    conv = _silu(conv)

    beta_all, g_all = _gate_rows(tail_ref[0], alog_ref[...], dtb_ref[...])
    r = lax.broadcasted_iota(jnp.int32, (PAIR, PAIR), 0)
    c = lax.broadcasted_iota(jnp.int32, (PAIR, PAIR), 1)
    same = (r >> 6) == (c >> 6)
    incl = same & (c <= r)
    strict = same & (c < r)
    eye = r == c
    ones = jnp.ones((PAIR, PAIR), F32)
    gc_all = jnp.dot(incl.astype(F32), g_all, precision=HIGHEST, preferred_element_type=F32)

    for h in range(H_DELTA):
        hs = slice(h * DK_DELTA, (h + 1) * DK_DELTA)
        qn = _l2norm(conv[:, hs])
        kn = _l2norm(conv[:, W_DELTA + h * DK_DELTA:W_DELTA + (h + 1) * DK_DELTA])
        vh = conv[:, 2 * W_DELTA + h * DV_DELTA:2 * W_DELTA + (h + 1) * DV_DELTA]
        beta = _lane_bcast(beta_all, h, PAIR)
        gc = _lane_bcast(gc_all, H_DELTA + h, PAIR)
        gc_cols = jnp.dot(ones, jnp.where(eye, gc, 0.0), precision=HIGHEST, preferred_element_type=F32)
        decay = jnp.exp(jnp.where(incl, gc - gc_cols, -jnp.inf))
        kbeta = kn * beta
        knb = kn.astype(BF16)
        a_mat = jnp.where(strict, lax.dot_general(kbeta.astype(BF16), knb, _NT, preferred_element_type=F32) * decay,
                          0.0)
        t_mat = _unit_lower_inverse(a_mat).astype(BF16)
        u_ref[0, h] = jnp.dot(t_mat, (vh * beta).astype(BF16), preferred_element_type=F32)
        w_ref[0, h] = jnp.dot(t_mat, (kbeta * jnp.exp(gc)).astype(BF16), preferred_element_type=F32).astype(BF16)
        qs = qn * DK_DELTA ** -0.5
        attn = jnp.where(incl, lax.dot_general(qs.astype(BF16), knb, _NT, preferred_element_type=F32) * decay, 0.0)
        at_ref[0, h] = jnp.concatenate([attn[:CHUNK, :CHUNK], attn[CHUNK:, CHUNK:]], axis=0).astype(BF16)
        qg_ref[0, h] = (qs * jnp.exp(gc)).astype(BF16)
        g_last = jnp.concatenate([jnp.broadcast_to(gc[CHUNK - 1:CHUNK], (CHUNK, LANES)),
                                  jnp.broadcast_to(gc[PAIR - 1:PAIR], (CHUNK, LANES))], axis=0)
        kdt_ref[0, h, 0] = (kn * jnp.exp(g_last - gc)).T.astype(BF16)
        e_last = jnp.exp(g_last)
        el_ref[0, h, 0] = jnp.concatenate([e_last[0:8], e_last[CHUNK:CHUNK + 8]], axis=0)


def _delta_prep(cin, conv_prev, conv_w, tail, alog_row, dtb_row):
    b, t, _ = cin.shape
    assert t % PAIR == 0
    nblk = t // PAIR
    per_head = lambda width: pl.BlockSpec((1, H_DELTA, PAIR, width), lambda bi, ti: (bi, 0, ti, 0))
    const = lambda shape: pl.BlockSpec(shape, lambda bi, ti: (0,) * len(shape))
    out_shape = [
        jax.ShapeDtypeStruct((b, H_DELTA, t, DV_DELTA), F32),
        jax.ShapeDtypeStruct((b, H_DELTA, t, DK_DELTA), BF16),
        jax.ShapeDtypeStruct((b, H_DELTA, t, DK_DELTA), BF16),
        jax.ShapeDtypeStruct((b, H_DELTA, t, CHUNK), BF16),
        jax.ShapeDtypeStruct((b, H_DELTA, nblk, DK_DELTA, PAIR), BF16),
        jax.ShapeDtypeStruct((b, H_DELTA, nblk, 16, LANES), F32),
    ]
    out_specs = [per_head(DV_DELTA), per_head(DK_DELTA), per_head(DK_DELTA), per_head(CHUNK),
                 pl.BlockSpec((1, H_DELTA, 1, DK_DELTA, PAIR), lambda bi, ti: (bi, 0, ti, 0, 0)),
                 pl.BlockSpec((1, H_DELTA, 1, 16, LANES), lambda bi, ti: (bi, 0, ti, 0, 0))]
    return pl.pallas_call(
        _delta_prep_kernel,
        grid=(b, nblk),
        in_specs=[pl.BlockSpec((1, PAIR, CONV_CH), lambda bi, ti: (bi, ti, 0)),
                  pl.BlockSpec((1, CONV_W - 1, CONV_CH), lambda bi, ti: (bi, 0, 0)),
                  const(conv_w.shape),
                  pl.BlockSpec((1, PAIR, LANES), lambda bi, ti: (bi, ti, 0)),
                  const(alog_row.shape), const(dtb_row.shape)],
        out_specs=out_specs,
        out_shape=out_shape,
        scratch_shapes=[pltpu.VMEM((8 + PAIR, CONV_CH), F32)],
        compiler_params=pltpu.CompilerParams(
            dimension_semantics=("arbitrary", "arbitrary"), vmem_limit_bytes=VMEM_LIMIT),
        name="delta_prep",
    )(cin, conv_prev, conv_w, tail, alog_row, dtb_row)


def _delta_norm_gate(o, normw, z):
    ms = jnp.mean(o * o, axis=-1, keepdims=True)
    return o * lax.rsqrt(ms + DELTA_NORM_EPS) * normw * _silu(z)


def _delta_scan_kernel(u_ref, w_ref, qg_ref, at_ref, kdt_ref, el_ref, z_ref, s0_ref, nw_ref,
                       o_ref, sfin_ref, s_scr, *, nb, nblk):
    t = pl.program_id(1)

    @pl.when(t == 0)
    def _():
        s_scr[...] = s0_ref[...]

    def block(jb, carry):
        r0 = pl.multiple_of(jb * PAIR, PAIR)
        for bi in range(nb):
            for h in range(H_DELTA):
                kdt = kdt_ref[bi, h, jb]
                el = el_ref[bi, h, jb]
                hs = slice(h * DV_DELTA, (h + 1) * DV_DELTA)
                for ci in range(2):
                    rows = pl.ds(pl.multiple_of(r0 + ci * CHUNK, CHUNK), CHUNK)
                    s = s_scr[bi, h]
                    wq = jnp.concatenate([w_ref[bi, h, rows, :], qg_ref[bi, h, rows, :]], axis=0)
                    res = jnp.dot(wq, s.astype(BF16), preferred_element_type=F32)
                    v_new = (u_ref[bi, h, rows, :] - res[:CHUNK]).astype(BF16)
                    o = res[CHUNK:] + jnp.dot(at_ref[bi, h, rows, :], v_new, preferred_element_type=F32)
                    decay = jnp.broadcast_to(el[ci * 8:ci * 8 + 1], (DK_DELTA, DV_DELTA))
                    s_scr[bi, h] = s * decay + jnp.dot(kdt[:, ci * CHUNK:(ci + 1) * CHUNK], v_new,
                                                       preferred_element_type=F32)
                    o_ref[bi, rows, hs] = _delta_norm_gate(o, nw_ref[...], z_ref[bi, rows, hs]).astype(BF16)
        return carry

    lax.fori_loop(0, nblk, block, 0)

    @pl.when(t == pl.num_programs(1) - 1)
    def _():
        sfin_ref[...] = s_scr[...]


def _delta_scan(prep, z, s0, norm_w):
    u, w, qg, at, kdt, el = prep
    b, _, t, _ = u.shape
    nb = 2 if b % 2 == 0 else 1
    ts = min(512, t)
    assert t % ts == 0
    nblk = ts // PAIR
    per_head = lambda width: pl.BlockSpec((nb, H_DELTA, ts, width), lambda bi, ti: (bi, 0, ti, 0))
    state = pl.BlockSpec((nb, H_DELTA, DK_DELTA, DV_DELTA), lambda bi, ti: (bi, 0, 0, 0))
    tok = pl.BlockSpec((nb, ts, W_DELTA), lambda bi, ti: (bi, ti, 0))
    return pl.pallas_call(
        functools.partial(_delta_scan_kernel, nb=nb, nblk=nblk),
        grid=(b // nb, t // ts),
        in_specs=[per_head(DV_DELTA), per_head(DK_DELTA), per_head(DK_DELTA), per_head(CHUNK),
                  pl.BlockSpec((nb, H_DELTA, nblk, DK_DELTA, PAIR), lambda bi, ti: (bi, 0, ti, 0, 0)),
                  pl.BlockSpec((nb, H_DELTA, nblk, 16, LANES), lambda bi, ti: (bi, 0, ti, 0, 0)),
                  tok, state, pl.BlockSpec((1, DV_DELTA), lambda bi, ti: (0, 0))],
        out_specs=[tok, state],
        out_shape=[jax.ShapeDtypeStruct((b, t, W_DELTA), BF16),
                   jax.ShapeDtypeStruct((b, H_DELTA, DK_DELTA, DV_DELTA), F32)],
        scratch_shapes=[pltpu.VMEM((nb, H_DELTA, DK_DELTA, DV_DELTA), F32)],
        compiler_params=pltpu.CompilerParams(
            dimension_semantics=("arbitrary", "arbitrary"), vmem_limit_bytes=VMEM_LIMIT),
        name="delta_scan",
    )(u, w, qg, at, kdt, el, z, s0, norm_w.reshape(1, DV_DELTA))


def _delta_step_kernel(cin_ref, prev_ref, cw_ref, tail_ref, alog_ref, dtb_ref, z_ref, s0_ref, nw_ref,
                       o_ref, s_ref):
    prev = prev_ref[0]
    conv = prev[0:1] * cw_ref[0:1, :]
    for j in range(1, CONV_W - 1):
        conv = conv + prev[j:j + 1] * cw_ref[j:j + 1, :]
    conv = _silu(conv + cin_ref[0] * cw_ref[CONV_W - 1:CONV_W, :])
    beta_all, g_all = _gate_rows(tail_ref[0], alog_ref[...], dtb_ref[...])
    for h in range(H_DELTA):
        hs = slice(h * DK_DELTA, (h + 1) * DK_DELTA)
        qs = _l2norm(conv[:, hs]) * DK_DELTA ** -0.5
        kn = _l2norm(conv[:, W_DELTA + h * DK_DELTA:W_DELTA + (h + 1) * DK_DELTA])
        vh = conv[:, 2 * W_DELTA + h * DV_DELTA:2 * W_DELTA + (h + 1) * DV_DELTA]
        beta = beta_all[:, h:h + 1]
        eg = jnp.exp(g_all[:, H_DELTA + h:H_DELTA + h + 1])
        s = s0_ref[0, h]
        lhs = jnp.concatenate([kn * (beta * eg), qs * eg, jnp.zeros((6, DK_DELTA), F32)], axis=0).astype(BF16)
        res = jnp.dot(lhs, s.astype(BF16), preferred_element_type=F32)
        v_new = vh * beta - res[0:1]
        qk = jnp.sum(qs.astype(BF16).astype(F32) * kn.astype(BF16).astype(F32), axis=-1, keepdims=True)
        o = res[1:2] + qk * v_new
        k_col = jnp.broadcast_to(kn, (DK_DELTA, DK_DELTA)).T
        s_ref[0, h] = s * eg + k_col * v_new
        o_ref[0, :, hs] = _delta_norm_gate(o, nw_ref[...], z_ref[0, :, hs]).astype(BF16)


def _delta_step(cin, conv_prev, conv_w, tail, alog_row, dtb_row, z, s0, norm_w):
    bs = cin.shape[0]
    tok = lambda width: pl.BlockSpec((1, 1, width), lambda b: (b, 0, 0))
    const = lambda shape: pl.BlockSpec(shape, lambda b: (0,) * len(shape))
    state = pl.BlockSpec((1, H_DELTA, DK_DELTA, DV_DELTA), lambda b: (b, 0, 0, 0))
    return pl.pallas_call(
        _delta_step_kernel,
        grid=(bs,),
        in_specs=[tok(CONV_CH), pl.BlockSpec((1, CONV_W - 1, CONV_CH), lambda b: (b, 0, 0)), const(conv_w.shape),
                  tok(LANES), const(alog_row.shape), const(dtb_row.shape), tok(W_DELTA), state,
                  const((1, DV_DELTA))],
        out_specs=[tok(W_DELTA), state],
        out_shape=[jax.ShapeDtypeStruct((bs, 1, W_DELTA), BF16),
                   jax.ShapeDtypeStruct((bs, H_DELTA, DK_DELTA, DV_DELTA), F32)],
        compiler_params=pltpu.CompilerParams(dimension_semantics=("arbitrary",)),
        name="delta_step",
    )(cin, conv_prev, conv_w, tail, alog_row, dtb_row, z, s0, norm_w.reshape(1, DV_DELTA))


def _out_proj_kernel(oa_ref, ob_ref, x_ref, w_ref, g_ref, b_ref, y_ref, *, alpha):
    mix = jnp.dot(oa_ref[...], w_ref[:W_DIFF, :], preferred_element_type=F32)
    mix = mix + jnp.dot(ob_ref[...], w_ref[W_DIFF:, :], preferred_element_type=F32)
    r = alpha * x_ref[...] + mix
    mu = jnp.mean(r, axis=-1, keepdims=True)
    var = jnp.mean(jnp.square(r - mu), axis=-1, keepdims=True)
    y_ref[...] = (r - mu) * lax.rsqrt(var + LN_EPS) * g_ref[...] + b_ref[...]


def _out_proj(oa, ob, x, w_out_b, ln_g, ln_b, alpha):
    m, d = x.shape
    tm = min(512, m)
    assert m % tm == 0
    row = lambda width: pl.BlockSpec((tm, width), lambda i: (i, 0))
    const = lambda shape: pl.BlockSpec(shape, lambda i: (0, 0))
    return pl.pallas_call(
        functools.partial(_out_proj_kernel, alpha=alpha),
        grid=(m // tm,),
        in_specs=[row(W_DIFF), row(W_DELTA), row(d), const(w_out_b.shape), const((1, d)), const((1, d))],
        out_specs=row(d),
        out_shape=jax.ShapeDtypeStruct((m, d), F32),
        compiler_params=pltpu.CompilerParams(dimension_semantics=("arbitrary",), vmem_limit_bytes=VMEM_LIMIT),
        name="out_proj",
    )(oa, ob, x, w_out_b, ln_g.reshape(1, d), ln_b.reshape(1, d))


def _pad_lanes(vec, offset):
    return jnp.zeros((1, LANES), F32).at[0, offset:offset + vec.shape[0]].set(vec.astype(F32))


def kernel(x_prompt, x_sample, cache_k, cache_v, page_table, state_delta, state_conv, w_in, conv_w, a_log,
           dt_bias, delta_norm_w, diff_lambda, diff_norm_w, w_out, ln_g, ln_b):
    depth = w_in.shape[0]
    bp, tp, d = x_prompt.shape
    bs, ts, _ = x_sample.shape
    assert ts == 1 and w_in.shape[2] == P_MAIN + 2 * H_DELTA and d == w_out.shape[2]
    past_len = page_table.shape[1] * cache_k.shape[2]
    alpha = (2 * depth) ** 0.25
    tables_p = _rope_tables(jnp.arange(tp, dtype=jnp.int32))
    tables_s = _rope_tables(jnp.full((bs,), past_len, jnp.int32))

    hp, hs = x_prompt, x_sample
    outs = [[] for _ in range(8)]
    for l in range(depth):
        lam_init = _lambda_init(l)
        w_main = w_in[l, :, :P_MAIN].astype(BF16)
        w_tail = jnp.zeros((d, LANES), BF16).at[:, :2 * H_DELTA].set(w_in[l, :, P_MAIN:].astype(BF16))
        w_out_b = w_out[l].astype(BF16)
        alog_row = _pad_lanes(a_log[l], H_DELTA)
        dtb_row = _pad_lanes(dt_bias[l], H_DELTA)

        q, k, kb, v, vb, ga, cin, z, tail = _in_proj(hp, w_main, w_tail, tables_p)
        oa = _prompt_attn(q, kb, vb, ga, diff_lambda[l], diff_norm_w[l], lam_init)
        prep = _delta_prep(cin, jnp.zeros((bp, CONV_W - 1, CONV_CH), F32), conv_w[l], tail, alog_row, dtb_row)
        ob, sp = _delta_scan(prep, z, jnp.zeros((bp, H_DELTA, DK_DELTA, DV_DELTA), F32), delta_norm_w[l])
        hp = _out_proj(oa.reshape(bp * tp, W_DIFF), ob.reshape(bp * tp, W_DELTA), hp.reshape(bp * tp, d),
                       w_out_b, ln_g[l], ln_b[l], alpha).reshape(bp, tp, d)
        outs[0].append(k.reshape(bp, tp, H_DIFF, 2, DH_DIFF))
        outs[1].append(v.reshape(bp, tp, H_DIFF, DV_DIFF))
        outs[2].append(sp)
        outs[3].append(cin[:, tp - (CONV_W - 1):, :])

        sq, sk, _, sv, _, sga, scin, sz, stail = _in_proj(hs.reshape(1, bs, d), w_main, w_tail, tables_s)
        tok = lambda a: a.reshape(bs, 1, a.shape[-1])
        soa = _decode_attn(tok(sq), tok(sk), tok(sv), tok(sga), cache_k[l], cache_v[l], page_table,
                           diff_lambda[l], diff_norm_w[l], lam_init)
        sob, ss = _delta_step(tok(scin), state_conv[l], conv_w[l], tok(stail), alog_row, dtb_row, tok(sz),
                              state_delta[l], delta_norm_w[l])
        hs = _out_proj(soa.reshape(bs, W_DIFF), sob.reshape(bs, W_DELTA), hs.reshape(bs, d),
                       w_out_b, ln_g[l], ln_b[l], alpha).reshape(bs, 1, d)
        outs[4].append(sk.reshape(bs, 1, H_DIFF, 2, DH_DIFF))
        outs[5].append(sv.reshape(bs, 1, H_DIFF, DV_DIFF))
        outs[6].append(ss)
        outs[7].append(jnp.concatenate([state_conv[l][:, 1:, :], tok(scin)], axis=1))
    return (hp, hs) + tuple(jnp.stack(o) for o in outs)
```

```python
import functools
import math

import jax
import jax.numpy as jnp
from jax import lax
from jax.experimental import pallas as pl
from jax.experimental.pallas import tpu as pltpu

F32 = jnp.float32
BF16 = jnp.bfloat16
HIGHEST = lax.Precision.HIGHEST

H_DIFF = 4
DH_DIFF = 64
DV_DIFF = 2 * DH_DIFF
W_DIFF = H_DIFF * DV_DIFF
H_DELTA = 4
DK_DELTA = 128
DV_DELTA = 128
W_DELTA = H_DELTA * DK_DELTA
ROT_DIM = DH_DIFF // 4
ROPE_THETA = 500000.0
CONV_W = 4
CONV_CH = 3 * W_DELTA
CHUNK = 64
P_MAIN = 4 * W_DIFF + 4 * W_DELTA
LN_EPS = 1e-5
HEAD_NORM_EPS = 1e-5
DELTA_NORM_EPS = 1e-6
L2_EPS = 1e-6
NEG_INF = -1e30

LANES = 128
PAIR = 2 * CHUNK
VMEM_LIMIT = 56 * 1024 * 1024
DECODE_PAGES_PER_STEP = 8

_NT = (((1,), (1,)), ((), ()))


def _sigmoid(x):
    return 1.0 / (1.0 + jnp.exp(-x))


def _silu(x):
    return x * _sigmoid(x)


def _softplus(x):
    return jnp.maximum(x, 0.0) + jnp.log1p(jnp.exp(-jnp.abs(x)))


def _lambda_init(layer):
    return 0.8 - 0.6 * math.exp(-0.3 * layer)


def _rope_tables(pos):
    half = ROT_DIM // 2
    inv = ROPE_THETA ** (-jnp.arange(half, dtype=F32) / half)
    ang = pos.astype(F32)[:, None] * inv[None, :]
    cos, sin = jnp.cos(ang), jnp.sin(ang)
    t = pos.shape[0]
    rest = DH_DIFF - ROT_DIM
    cos_m = jnp.concatenate([cos, cos, jnp.ones((t, rest), F32)], axis=-1)
    sin_lo = jnp.concatenate([-sin, jnp.zeros((t, half + rest), F32)], axis=-1)
    sin_hi = jnp.concatenate([jnp.zeros((t, half), F32), sin, jnp.zeros((t, rest), F32)], axis=-1)
    reps = W_DIFF // DH_DIFF
    return jnp.tile(cos_m, (1, reps)), jnp.tile(sin_lo, (1, reps)), jnp.tile(sin_hi, (1, reps))


def _in_proj_kernel(x_ref, w_ref, wt_ref, cos_ref, slo_ref, shi_ref,
                    q_ref, k_ref, kb_ref, v_ref, vb_ref, ga_ref, cin_ref, z_ref, tail_ref):
    xb = x_ref[0].astype(BF16)
    half = ROT_DIM // 2

    def proj(c0, width):
        return jnp.dot(xb, w_ref[:, c0:c0 + width], preferred_element_type=F32)

    def rope(h):
        return (h * cos_ref[...]
                + pltpu.roll(h, W_DIFF - half, 1) * slo_ref[...]
                + pltpu.roll(h, half, 1) * shi_ref[...])

    q_ref[0] = rope(proj(0, W_DIFF)).astype(BF16)
    k = rope(proj(W_DIFF, W_DIFF))
    k_ref[0] = k
    kb_ref[0] = k.astype(BF16)
    v = proj(2 * W_DIFF, W_DIFF)
    v_ref[0] = v
    vb_ref[0] = v.astype(BF16)
    ga_ref[0] = proj(3 * W_DIFF, W_DIFF)
    for j in range(3):
        cin_ref[0, :, j * W_DELTA:(j + 1) * W_DELTA] = proj(4 * W_DIFF + j * W_DELTA, W_DELTA)
    z_ref[0] = proj(4 * W_DIFF + 3 * W_DELTA, W_DELTA)
    tail_ref[0] = jnp.dot(xb, wt_ref[...], preferred_element_type=F32)


def _in_proj(x, w_main, w_tail, tables):
    b, t, d = x.shape
    tm = min(512, t)
    assert t % tm == 0
    cos_t, slo_t, shi_t = tables
    row = lambda width: pl.BlockSpec((1, tm, width), lambda ti, bi: (bi, ti, 0))
    tab = pl.BlockSpec((tm, W_DIFF), lambda ti, bi: (ti, 0))
    const = lambda shape: pl.BlockSpec(shape, lambda ti, bi: (0, 0))
    outs = [(W_DIFF, BF16), (W_DIFF, F32), (W_DIFF, BF16), (W_DIFF, F32), (W_DIFF, BF16),
            (W_DIFF, F32), (CONV_CH, F32), (W_DELTA, F32), (LANES, F32)]
    return pl.pallas_call(
        _in_proj_kernel,
        grid=(t // tm, b),
        in_specs=[row(d), const(w_main.shape), const(w_tail.shape), tab, tab, tab],
        out_specs=[row(wd) for wd, _ in outs],
        out_shape=[jax.ShapeDtypeStruct((b, t, wd), dt) for wd, dt in outs],
        compiler_params=pltpu.CompilerParams(
            dimension_semantics=("arbitrary", "arbitrary"), vmem_limit_bytes=VMEM_LIMIT),
        name="in_proj",
    )(x, w_main, w_tail, cos_t, slo_t, shi_t)


def _diff_lambda_value(dl_ref, lam_init):
    dl = dl_ref[...]
    a = jnp.sum(dl[0:1] * dl[1:2], axis=1, keepdims=True)
    b = jnp.sum(dl[2:3] * dl[3:4], axis=1, keepdims=True)
    return jnp.exp(a) - jnp.exp(b) + lam_init


def _head_norm_gate(o, normw, gate, lam_init):
    ms = jnp.mean(o * o, axis=-1, keepdims=True)
    o = o * lax.rsqrt(ms + HEAD_NORM_EPS) * normw * (1.0 - lam_init)
    return o * _silu(gate)


def _prompt_attn_kernel(q_ref, k_ref, v_ref, ga_ref, dl_ref, nw_ref, o_ref, s_scr, m_scr, l_scr, acc_scr,
                        *, blk, lam_init):
    i = pl.program_id(1)
    lane = lax.broadcasted_iota(jnp.int32, (blk, DV_DIFF), 1)
    heads = [slice(h * DV_DIFF, (h + 1) * DV_DIFF) for h in range(H_DIFF)]
    lane_chunks = [slice(c * LANES, (c + 1) * LANES) for c in range(blk // LANES)]
    q_maps = []
    for hs in heads:
        qs = q_ref[0, :, hs].astype(F32) * DH_DIFF ** -0.5
        q_maps.append(jnp.where(lane < DH_DIFF, qs, 0.0).astype(BF16))
        q_maps.append(jnp.where(lane >= DH_DIFF, qs, 0.0).astype(BF16))

    def fold(x, op):
        r = x[:, lane_chunks[0]]
        for c in lane_chunks[1:]:
            r = op(r, x[:, c])
        return r

    m_scr[...] = jnp.full(m_scr.shape, NEG_INF, F32)

    def score_block(j, masked):
        rows = pl.ds(pl.multiple_of(j * blk, blk), blk)
        for h, hs in enumerate(heads):
            k = k_ref[0, rows, hs]
            for mi in (2 * h, 2 * h + 1):
                s = lax.dot_general(q_maps[mi], k, _NT, preferred_element_type=F32)
                if masked:
                    row = lax.broadcasted_iota(jnp.int32, (blk, blk), 0)
                    col = lax.broadcasted_iota(jnp.int32, (blk, blk), 1)
                    s = jnp.where(col <= row, s, NEG_INF)
                s_scr[j, mi] = s
                m_scr[mi] = jnp.maximum(m_scr[mi], fold(s, jnp.maximum))

    def score_loop(j, carry):
        score_block(j, False)
        return carry

    lax.fori_loop(0, i, score_loop, 0)
    score_block(i, True)
    for mi in range(2 * H_DIFF):
        m_scr[mi] = jnp.broadcast_to(jnp.max(m_scr[mi], axis=-1, keepdims=True), (blk, LANES))

    l_scr[...] = jnp.zeros(l_scr.shape, F32)
    acc_scr[...] = jnp.zeros(acc_scr.shape, F32)

    def prob_block(j, carry):
        rows = pl.ds(pl.multiple_of(j * blk, blk), blk)
        for h, hs in enumerate(heads):
            v = v_ref[0, rows, hs]
            for mi in (2 * h, 2 * h + 1):
                m = m_scr[mi]
                p = jnp.concatenate([jnp.exp(s_scr[j, mi, :, c] - m) for c in lane_chunks], axis=1)
                l_scr[mi] = l_scr[mi] + fold(p, jnp.add)
                acc_scr[mi] = acc_scr[mi] + jnp.dot(p.astype(BF16), v, preferred_element_type=F32)
        return carry

    lax.fori_loop(0, i + 1, prob_block, 0)

    lam = _diff_lambda_value(dl_ref, lam_init)
    for h, hs in enumerate(heads):
        l1 = jnp.sum(l_scr[2 * h], axis=-1, keepdims=True)
        l2 = jnp.sum(l_scr[2 * h + 1], axis=-1, keepdims=True)
        o = acc_scr[2 * h] / l1 - lam * (acc_scr[2 * h + 1] / l2)
        o_ref[0, :, hs] = _head_norm_gate(o, nw_ref[...], ga_ref[0, :, hs], lam_init).astype(BF16)


def _prompt_attn(q, kb, vb, ga, diff_lambda, norm_w, lam_init):
    b, t, _ = q.shape
    blk = min(256, t)
    assert t % blk == 0
    n_maps = 2 * H_DIFF
    qspec = pl.BlockSpec((1, blk, W_DIFF), lambda bi, i: (bi, i, 0))
    kvspec = pl.BlockSpec((1, t, W_DIFF), lambda bi, i: (bi, 0, 0))
    return pl.pallas_call(
        functools.partial(_prompt_attn_kernel, blk=blk, lam_init=lam_init),
        grid=(b, t // blk),
        in_specs=[qspec, kvspec, kvspec, qspec,
                  pl.BlockSpec(diff_lambda.shape, lambda bi, i: (0, 0)),
                  pl.BlockSpec((1, DV_DIFF), lambda bi, i: (0, 0))],
        out_specs=qspec,
        out_shape=jax.ShapeDtypeStruct((b, t, W_DIFF), BF16),
        scratch_shapes=[pltpu.VMEM((t // blk, n_maps, blk, blk), F32),
                        pltpu.VMEM((n_maps, blk, LANES), F32),
                        pltpu.VMEM((n_maps, blk, LANES), F32),
                        pltpu.VMEM((n_maps, blk, DV_DIFF), F32)],
        compiler_params=pltpu.CompilerParams(
            dimension_semantics=("arbitrary", "arbitrary"), vmem_limit_bytes=VMEM_LIMIT),
        name="prompt_attn",
    )(q, kb, vb, ga, diff_lambda, norm_w.reshape(1, DV_DIFF))


def _decode_attn_kernel(pt_ref, q_ref, kn_ref, vn_ref, ga_ref, dl_ref, nw_ref, *rest, pages, lam_init):
    kp_refs, vp_refs = rest[:pages], rest[pages:2 * pages]
    o_ref, m_ref, l_ref, acc_ref = rest[2 * pages:]
    g = pl.program_id(1)
    n_maps = 2 * H_DIFF
    page = kp_refs[0].shape[2]

    @pl.when(g == 0)
    def _():
        m_ref[...] = jnp.full(m_ref.shape, NEG_INF, F32)
        l_ref[...] = jnp.zeros(l_ref.shape, F32)
        acc_ref[...] = jnp.zeros(acc_ref.shape, F32)

    qs = q_ref[0].astype(F32) * DH_DIFF ** -0.5
    rows = lax.broadcasted_iota(jnp.int32, (n_maps, W_DIFF), 0)
    lanes = lax.broadcasted_iota(jnp.int32, (n_maps, W_DIFF), 1)
    q_blk = jnp.where((lanes >> 6) == rows, jnp.broadcast_to(qs, (n_maps, W_DIFF)), 0.0).astype(BF16)

    s = jnp.concatenate(
        [jnp.dot(q_blk, kp[0].astype(BF16), preferred_element_type=F32) for kp in kp_refs], axis=1)
    m_prev = m_ref[:, 0:1]
    m_new = jnp.maximum(m_prev, jnp.max(s, axis=-1, keepdims=True))
    p = jnp.exp(s - m_new)
    alpha = jnp.exp(m_prev - m_new)
    l_new = alpha * l_ref[:, 0:1] + jnp.sum(p, axis=-1, keepdims=True)
    pb = p.astype(BF16)
    for h in range(H_DIFF):
        pv = jnp.zeros((n_maps, DV_DIFF), F32)
        for i, vp in enumerate(vp_refs):
            v_head = vp[0, pl.ds(h, page, stride=H_DIFF), :].astype(BF16)
            pv = pv + jnp.dot(pb[:, i * page:(i + 1) * page], v_head, preferred_element_type=F32)
        acc_ref[h] = alpha * acc_ref[h] + pv
    m_ref[...] = jnp.broadcast_to(m_new, m_ref.shape)
    l_ref[...] = jnp.broadcast_to(l_new, l_ref.shape)

    @pl.when(g == pl.num_programs(1) - 1)
    def _():
        k_self = jnp.broadcast_to(kn_ref[0].astype(BF16), (8, W_DIFF))
        s_self = lax.dot_general(q_blk, k_self, _NT, preferred_element_type=F32)[:, 0:1]
        m_f = jnp.maximum(m_new, s_self)
        a_f = jnp.exp(m_new - m_f)
        p_self = jnp.exp(s_self - m_f)
        l_f = a_f * l_new + p_self
        pv_self = p_self.astype(BF16).astype(F32) * vn_ref[0].astype(BF16).astype(F32)
        lam = _diff_lambda_value(dl_ref, lam_init)
        for h in range(H_DIFF):
            hs = slice(h * DV_DIFF, (h + 1) * DV_DIFF)
            a = (a_f * acc_ref[h] + pv_self[:, hs]) / l_f
            o = a[2 * h:2 * h + 1] - lam * a[2 * h + 1:2 * h + 2]
            o_ref[0, :, hs] = _head_norm_gate(o, nw_ref[...], ga_ref[0, :, hs], lam_init).astype(BF16)


def _decode_attn(q, k_new, v_new, ga, cache_k, cache_v, layer, page_table, diff_lambda, norm_w, lam_init):
    bs = q.shape[0]
    depth, n_phys, page = cache_k.shape[:3]
    n_pages = page_table.shape[1]
    pages = math.gcd(DECODE_PAGES_PER_STEP, n_pages)
    assert page == LANES
    ck = jnp.transpose(cache_k, (0, 1, 3, 4, 5, 2)).reshape(depth * n_phys, W_DIFF, page)
    cv = cache_v.reshape(depth * n_phys, page * H_DIFF, DV_DIFF)
    pt = page_table.reshape(-1) + layer * n_phys

    def k_spec(i):
        return pl.BlockSpec((1, W_DIFF, page), lambda b, g, pt_ref: (pt_ref[b * n_pages + g * pages + i], 0, 0))

    def v_spec(i):
        return pl.BlockSpec((1, page * H_DIFF, DV_DIFF),
                            lambda b, g, pt_ref: (pt_ref[b * n_pages + g * pages + i], 0, 0))

    tok = pl.BlockSpec((1, 1, W_DIFF), lambda b, g, pt_ref: (b, 0, 0))
    n_maps = 2 * H_DIFF
    grid_spec = pltpu.PrefetchScalarGridSpec(
        num_scalar_prefetch=1,
        grid=(bs, n_pages // pages),
        in_specs=[tok, tok, tok, tok,
                  pl.BlockSpec(diff_lambda.shape, lambda b, g, pt_ref: (0, 0)),
                  pl.BlockSpec((1, DV_DIFF), lambda b, g, pt_ref: (0, 0))]
                 + [k_spec(i) for i in range(pages)] + [v_spec(i) for i in range(pages)],
        out_specs=tok,
        scratch_shapes=[pltpu.VMEM((n_maps, LANES), F32), pltpu.VMEM((n_maps, LANES), F32),
                        pltpu.VMEM((H_DIFF, n_maps, DV_DIFF), F32)],
    )
    return pl.pallas_call(
        functools.partial(_decode_attn_kernel, pages=pages, lam_init=lam_init),
        grid_spec=grid_spec,
        out_shape=jax.ShapeDtypeStruct((bs, 1, W_DIFF), BF16),
        compiler_params=pltpu.CompilerParams(
            dimension_semantics=("arbitrary", "arbitrary"), vmem_limit_bytes=VMEM_LIMIT),
        name="decode_attn",
    )(pt, q, k_new, v_new, ga, diff_lambda, norm_w.reshape(1, DV_DIFF), *([ck] * pages), *([cv] * pages))


def _gate_rows(tail, alog_row, dtb_row):
    beta = _sigmoid(tail)
    g = -jnp.exp(alog_row) * _softplus(tail + dtb_row)
    return beta, g


def _lane_bcast(x, lane, rows):
    return jnp.broadcast_to(x[:, lane:lane + 1], (rows, LANES))


def _l2norm(x):
    return x * lax.rsqrt(jnp.sum(x * x, axis=-1, keepdims=True) + L2_EPS)


def _split_bf16(x):
    hi = x.astype(BF16)
    return hi, (x - hi.astype(F32)).astype(BF16)


def _dot_split(lhs, rhs):
    d = lambda a, b: jnp.dot(a, b, preferred_element_type=F32)
    return d(lhs[0], rhs[0]) + d(lhs[0], rhs[1]) + d(lhs[1], rhs[0])


def _unit_lower_inverses(a_mats):
    n = a_mats[0].shape[0]
    eye = (lax.broadcasted_iota(jnp.int32, (n, n), 0) == lax.broadcasted_iota(jnp.int32, (n, n), 1)).astype(F32)
    xs = [-a for a in a_mats]
    ps = [eye + x for x in xs]
    splits = [_split_bf16(x) for x in xs]
    xs = [_dot_split(s, s) for s in splits]
    power = 2
    while 2 * power < CHUNK:
        both = [_dot_split(_split_bf16(x), _split_bf16(jnp.concatenate([p, x], axis=1))) for p, x in zip(ps, xs)]
        ps = [p + b[:, :n] for p, b in zip(ps, both)]
        xs = [b[:, n:] for b in both]
        power *= 2
    return [p + _dot_split(_split_bf16(x), _split_bf16(p)) for p, x in zip(ps, xs)]


def _delta_prep_kernel(cin_ref, prev_ref, cw_ref, tail_ref, alog_ref, dtb_ref,
                       u_ref, w_ref, qg_ref, at_ref, kdt_ref, el_ref, ext_ref, *, pairs):
    t = pl.program_id(1)
    pad = 8
    keep = CONV_W - 1
    rows = pairs * PAIR

    @pl.when(t == 0)
    def _():
        ext_ref[pad - keep:pad, :] = prev_ref[0]

    @pl.when(t > 0)
    def _():
        ext_ref[pad - keep:pad, :] = ext_ref[pad + rows - keep:pad + rows, :]

    ext_ref[pad:pad + rows, :] = cin_ref[0]

    r = lax.broadcasted_iota(jnp.int32, (PAIR, PAIR), 0)
    c = lax.broadcasted_iota(jnp.int32, (PAIR, PAIR), 1)
    same = (r >> 6) == (c >> 6)
    incl = same & (c <= r)
    strict = same & (c < r)

    chains = []
    for pi in range(pairs):
        base = pad - keep + pi * PAIR
        conv = ext_ref[base:base + PAIR, :] * cw_ref[0:1, :]
        for j in range(1, CONV_W):
            conv = conv + ext_ref[base + j:base + j + PAIR, :] * cw_ref[j:j + 1, :]
        conv = _silu(conv)
        rs = slice(pi * PAIR, (pi + 1) * PAIR)
        beta_all, g_all = _gate_rows(tail_ref[0, rs, :], alog_ref[...], dtb_ref[...])
        gc_all = jnp.dot(incl.astype(F32), g_all, precision=HIGHEST, preferred_element_type=F32)
        for h in range(H_DELTA):
            hs = slice(h * DK_DELTA, (h + 1) * DK_DELTA)
            qs = _l2norm(conv[:, hs]) * DK_DELTA ** -0.5
            kn = _l2norm(conv[:, W_DELTA + h * DK_DELTA:W_DELTA + (h + 1) * DK_DELTA])
            vh = conv[:, 2 * W_DELTA + h * DV_DELTA:2 * W_DELTA + (h + 1) * DV_DELTA]
            beta = _lane_bcast(beta_all, h, PAIR)
            gc = _lane_bcast(gc_all, H_DELTA + h, PAIR)
            chains.append(dict(pi=pi, h=h, rs=rs, qs=qs, kn=kn, knb=kn.astype(BF16), vh=vh, beta=beta, gc=gc,
                               kbeta=kn * beta))

    for ch in chains:
        ch["kk"] = lax.dot_general(ch["kbeta"].astype(BF16), ch["knb"], _NT, preferred_element_type=F32)
        ch["qk"] = lax.dot_general(ch["qs"].astype(BF16), ch["knb"], _NT, preferred_element_type=F32)
    for ch in chains:
        gc = ch["gc"]
        ch["decay"] = jnp.exp(jnp.where(incl, gc - gc.T, -jnp.inf))
    t_mats = _unit_lower_inverses([jnp.where(strict, ch["kk"] * ch["decay"], 0.0) for ch in chains])
    for ch, t_mat in zip(chains, t_mats):
        rhs = jnp.concatenate([ch["vh"] * ch["beta"], ch["kbeta"] * jnp.exp(ch["gc"])], axis=1).astype(BF16)
        ch["uw"] = jnp.dot(t_mat.astype(BF16), rhs, preferred_element_type=F32)
    for ch in chains:
        pi, h, rs, gc = ch["pi"], ch["h"], ch["rs"], ch["gc"]
        u_ref[0, h, rs, :] = ch["uw"][:, :DV_DELTA]
        w_ref[0, h, rs, :] = ch["uw"][:, DV_DELTA:].astype(BF16)
        attn = jnp.where(incl, ch["qk"] * ch["decay"], 0.0)
        at_ref[0, h, rs, :] = jnp.concatenate([attn[:CHUNK, :CHUNK], attn[CHUNK:, CHUNK:]], axis=0).astype(BF16)
        qg_ref[0, h, rs, :] = (ch["qs"] * jnp.exp(gc)).astype(BF16)
        g_last = jnp.concatenate([jnp.broadcast_to(gc[CHUNK - 1:CHUNK], (CHUNK, LANES)),
                                  jnp.broadcast_to(gc[PAIR - 1:PAIR], (CHUNK, LANES))], axis=0)
        kdt_ref[0, h, pi] = (ch["kn"] * jnp.exp(g_last - gc)).T.astype(BF16)
        e_last = jnp.exp(g_last)
        el_ref[0, h, pi] = jnp.concatenate([e_last[0:8], e_last[CHUNK:CHUNK + 8]], axis=0)


def _delta_prep(cin, conv_prev, conv_w, tail, alog_row, dtb_row):
    b, t, _ = cin.shape
    assert t % PAIR == 0
    npair = t // PAIR
    pairs = 2 if npair % 2 == 0 else 1
    rows = pairs * PAIR
    per_head = lambda width: pl.BlockSpec((1, H_DELTA, rows, width), lambda bi, ti: (bi, 0, ti, 0))
    const = lambda shape: pl.BlockSpec(shape, lambda bi, ti: (0,) * len(shape))
    out_shape = [
        jax.ShapeDtypeStruct((b, H_DELTA, t, DV_DELTA), F32),
        jax.ShapeDtypeStruct((b, H_DELTA, t, DK_DELTA), BF16),
        jax.ShapeDtypeStruct((b, H_DELTA, t, DK_DELTA), BF16),
        jax.ShapeDtypeStruct((b, H_DELTA, t, CHUNK), BF16),
        jax.ShapeDtypeStruct((b, H_DELTA, npair, DK_DELTA, PAIR), BF16),
        jax.ShapeDtypeStruct((b, H_DELTA, npair, 16, LANES), F32),
    ]
    out_specs = [per_head(DV_DELTA), per_head(DK_DELTA), per_head(DK_DELTA), per_head(CHUNK),
                 pl.BlockSpec((1, H_DELTA, pairs, DK_DELTA, PAIR), lambda bi, ti: (bi, 0, ti, 0, 0)),
                 pl.BlockSpec((1, H_DELTA, pairs, 16, LANES), lambda bi, ti: (bi, 0, ti, 0, 0))]
    return pl.pallas_call(
        functools.partial(_delta_prep_kernel, pairs=pairs),
        grid=(b, npair // pairs),
        in_specs=[pl.BlockSpec((1, rows, CONV_CH), lambda bi, ti: (bi, ti, 0)),
                  pl.BlockSpec((1, CONV_W - 1, CONV_CH), lambda bi, ti: (bi, 0, 0)),
                  const(conv_w.shape),
                  pl.BlockSpec((1, rows, LANES), lambda bi, ti: (bi, ti, 0)),
                  const(alog_row.shape), const(dtb_row.shape)],
        out_specs=out_specs,
        out_shape=out_shape,
        scratch_shapes=[pltpu.VMEM((8 + rows, CONV_CH), F32)],
        compiler_params=pltpu.CompilerParams(
            dimension_semantics=("arbitrary", "arbitrary"), vmem_limit_bytes=VMEM_LIMIT),
        name="delta_prep",
    )(cin, conv_prev, conv_w, tail, alog_row, dtb_row)


def _delta_norm_gate(o, normw, z):
    ms = jnp.mean(o * o, axis=-1, keepdims=True)
    return o * lax.rsqrt(ms + DELTA_NORM_EPS) * normw * _silu(z)


def _delta_scan_kernel(u_ref, w_ref, qg_ref, at_ref, kdt_ref, el_ref, z_ref, s0_ref, nw_ref,
                       o_ref, sfin_ref, s_scr, *, nb, nblk):
    t = pl.program_id(1)

    @pl.when(t == 0)
    def _():
        s_scr[...] = s0_ref[...]

    def block(jb, carry):
        r0 = pl.multiple_of(jb * PAIR, PAIR)
        seqs = [(bi, h) for bi in range(nb) for h in range(H_DELTA)]
        for ci in range(2):
            rows = pl.ds(pl.multiple_of(r0 + ci * CHUNK, CHUNK), CHUNK)
            states = [s_scr[bi, h] for bi, h in seqs]
            res = [jnp.dot(jnp.concatenate([w_ref[bi, h, rows, :], qg_ref[bi, h, rows, :]], axis=0),
                           s.astype(BF16), preferred_element_type=F32) for (bi, h), s in zip(seqs, states)]
            v_new = [(u_ref[bi, h, rows, :] - r[:CHUNK]).astype(BF16) for (bi, h), r in zip(seqs, res)]
            upd = [jnp.dot(kdt_ref[bi, h, jb, :, ci * CHUNK:(ci + 1) * CHUNK], v, preferred_element_type=F32)
                   for (bi, h), v in zip(seqs, v_new)]
            intra = [jnp.dot(at_ref[bi, h, rows, :], v, preferred_element_type=F32) for (bi, h), v in zip(seqs, v_new)]
            for (bi, h), s, r, du, oi in zip(seqs, states, res, upd, intra):
                decay = jnp.broadcast_to(el_ref[bi, h, jb, ci * 8:ci * 8 + 1, :], (DK_DELTA, DV_DELTA))
                s_scr[bi, h] = s * decay + du
                hs = slice(h * DV_DELTA, (h + 1) * DV_DELTA)
                o_ref[bi, rows, hs] = _delta_norm_gate(r[CHUNK:] + oi, nw_ref[...], z_ref[bi, rows, hs]).astype(BF16)
        return carry

    lax.fori_loop(0, nblk, block, 0)

    @pl.when(t == pl.num_programs(1) - 1)
    def _():
        sfin_ref[...] = s_scr[...]


def _delta_scan(prep, z, s0, norm_w):
    u, w, qg, at, kdt, el = prep
    b, _, t, _ = u.shape
    nb = 2 if b % 2 == 0 else 1
    ts = min(512, t)
    assert t % ts == 0
    nblk = ts // PAIR
    per_head = lambda width: pl.BlockSpec((nb, H_DELTA, ts, width), lambda bi, ti: (bi, 0, ti, 0))
    state = pl.BlockSpec((nb, H_DELTA, DK_DELTA, DV_DELTA), lambda bi, ti: (bi, 0, 0, 0))
    tok = pl.BlockSpec((nb, ts, W_DELTA), lambda bi, ti: (bi, ti, 0))
    return pl.pallas_call(
        functools.partial(_delta_scan_kernel, nb=nb, nblk=nblk),
        grid=(b // nb, t // ts),
        in_specs=[per_head(DV_DELTA), per_head(DK_DELTA), per_head(DK_DELTA), per_head(CHUNK),
                  pl.BlockSpec((nb, H_DELTA, nblk, DK_DELTA, PAIR), lambda bi, ti: (bi, 0, ti, 0, 0)),
                  pl.BlockSpec((nb, H_DELTA, nblk, 16, LANES), lambda bi, ti: (bi, 0, ti, 0, 0)),
                  tok, state, pl.BlockSpec((1, DV_DELTA), lambda bi, ti: (0, 0))],
        out_specs=[tok, state],
        out_shape=[jax.ShapeDtypeStruct((b, t, W_DELTA), BF16),
                   jax.ShapeDtypeStruct((b, H_DELTA, DK_DELTA, DV_DELTA), F32)],
        scratch_shapes=[pltpu.VMEM((nb, H_DELTA, DK_DELTA, DV_DELTA), F32)],
        compiler_params=pltpu.CompilerParams(
            dimension_semantics=("arbitrary", "arbitrary"), vmem_limit_bytes=VMEM_LIMIT),
        name="delta_scan",
    )(u, w, qg, at, kdt, el, z, s0, norm_w.reshape(1, DV_DELTA))


def _delta_step_kernel(cin_ref, prev_ref, cw_ref, tail_ref, alog_ref, dtb_ref, z_ref, s0_ref, nw_ref,
                       o_ref, s_ref):
    prev = prev_ref[0]
    conv = prev[0:1] * cw_ref[0:1, :]
    for j in range(1, CONV_W - 1):
        conv = conv + prev[j:j + 1] * cw_ref[j:j + 1, :]
    conv = _silu(conv + cin_ref[0] * cw_ref[CONV_W - 1:CONV_W, :])
    beta_all, g_all = _gate_rows(tail_ref[0], alog_ref[...], dtb_ref[...])
    for h in range(H_DELTA):
        hs = slice(h * DK_DELTA, (h + 1) * DK_DELTA)
        qs = _l2norm(conv[:, hs]) * DK_DELTA ** -0.5
        kn = _l2norm(conv[:, W_DELTA + h * DK_DELTA:W_DELTA + (h + 1) * DK_DELTA])
        vh = conv[:, 2 * W_DELTA + h * DV_DELTA:2 * W_DELTA + (h + 1) * DV_DELTA]
        beta = beta_all[:, h:h + 1]
        eg = jnp.exp(g_all[:, H_DELTA + h:H_DELTA + h + 1])
        s = s0_ref[0, h]
        lhs = jnp.concatenate([kn * (beta * eg), qs * eg, jnp.zeros((6, DK_DELTA), F32)], axis=0).astype(BF16)
        res = jnp.dot(lhs, s.astype(BF16), preferred_element_type=F32)
        v_new = vh * beta - res[0:1]
        qk = jnp.sum(qs.astype(BF16).astype(F32) * kn.astype(BF16).astype(F32), axis=-1, keepdims=True)
        o = res[1:2] + qk * v_new
        k_col = jnp.broadcast_to(kn, (DK_DELTA, DK_DELTA)).T
        s_ref[0, h] = s * eg + k_col * v_new
        o_ref[0, :, hs] = _delta_norm_gate(o, nw_ref[...], z_ref[0, :, hs]).astype(BF16)


def _delta_step(cin, conv_prev, conv_w, tail, alog_row, dtb_row, z, s0, norm_w):
    bs = cin.shape[0]
    tok = lambda width: pl.BlockSpec((1, 1, width), lambda b: (b, 0, 0))
    const = lambda shape: pl.BlockSpec(shape, lambda b: (0,) * len(shape))
    state = pl.BlockSpec((1, H_DELTA, DK_DELTA, DV_DELTA), lambda b: (b, 0, 0, 0))
    return pl.pallas_call(
        _delta_step_kernel,
        grid=(bs,),
        in_specs=[tok(CONV_CH), pl.BlockSpec((1, CONV_W - 1, CONV_CH), lambda b: (b, 0, 0)), const(conv_w.shape),
                  tok(LANES), const(alog_row.shape), const(dtb_row.shape), tok(W_DELTA), state,
                  const((1, DV_DELTA))],
        out_specs=[tok(W_DELTA), state],
        out_shape=[jax.ShapeDtypeStruct((bs, 1, W_DELTA), BF16),
                   jax.ShapeDtypeStruct((bs, H_DELTA, DK_DELTA, DV_DELTA), F32)],
        compiler_params=pltpu.CompilerParams(dimension_semantics=("arbitrary",)),
        name="delta_step",
    )(cin, conv_prev, conv_w, tail, alog_row, dtb_row, z, s0, norm_w.reshape(1, DV_DELTA))


def _out_proj_kernel(oa_ref, ob_ref, x_ref, w_ref, g_ref, b_ref, y_ref, *, alpha):
    mix = jnp.dot(oa_ref[...], w_ref[:W_DIFF, :], preferred_element_type=F32)
    mix = mix + jnp.dot(ob_ref[...], w_ref[W_DIFF:, :], preferred_element_type=F32)
    r = alpha * x_ref[...] + mix
    mu = jnp.mean(r, axis=-1, keepdims=True)
    var = jnp.mean(jnp.square(r - mu), axis=-1, keepdims=True)
    y_ref[...] = (r - mu) * lax.rsqrt(var + LN_EPS) * g_ref[...] + b_ref[...]


def _out_proj(oa, ob, x, w_out_b, ln_g, ln_b, alpha):
    m, d = x.shape
    tm = min(512, m)
    assert m % tm == 0
    row = lambda width: pl.BlockSpec((tm, width), lambda i: (i, 0))
    const = lambda shape: pl.BlockSpec(shape, lambda i: (0, 0))
    return pl.pallas_call(
        functools.partial(_out_proj_kernel, alpha=alpha),
        grid=(m // tm,),
        in_specs=[row(W_DIFF), row(W_DELTA), row(d), const(w_out_b.shape), const((1, d)), const((1, d))],
        out_specs=row(d),
        out_shape=jax.ShapeDtypeStruct((m, d), F32),
        compiler_params=pltpu.CompilerParams(dimension_semantics=("arbitrary",), vmem_limit_bytes=VMEM_LIMIT),
        name="out_proj",
    )(oa, ob, x, w_out_b, ln_g.reshape(1, d), ln_b.reshape(1, d))


def _pad_lanes(vec, offset):
    return jnp.zeros((1, LANES), F32).at[0, offset:offset + vec.shape[0]].set(vec.astype(F32))


def kernel(x_prompt, x_sample, cache_k, cache_v, page_table, state_delta, state_conv, w_in, conv_w, a_log,
           dt_bias, delta_norm_w, diff_lambda, diff_norm_w, w_out, ln_g, ln_b):
    depth = w_in.shape[0]
    bp, tp, d = x_prompt.shape
    bs, ts, _ = x_sample.shape
    assert ts == 1 and w_in.shape[2] == P_MAIN + 2 * H_DELTA and d == w_out.shape[2]
    past_len = page_table.shape[1] * cache_k.shape[2]
    alpha = (2 * depth) ** 0.25
    tables_p = _rope_tables(jnp.arange(tp, dtype=jnp.int32))
    tables_s = _rope_tables(jnp.full((bs,), past_len, jnp.int32))

    hp, hs = x_prompt, x_sample
    outs = [[] for _ in range(8)]
    for l in range(depth):
        lam_init = _lambda_init(l)
        w_main = w_in[l, :, :P_MAIN].astype(BF16)
        w_tail = jnp.zeros((d, LANES), BF16).at[:, :2 * H_DELTA].set(w_in[l, :, P_MAIN:].astype(BF16))
        w_out_b = w_out[l].astype(BF16)
        alog_row = _pad_lanes(a_log[l], H_DELTA)
        dtb_row = _pad_lanes(dt_bias[l], H_DELTA)

        q, k, kb, v, vb, ga, cin, z, tail = _in_proj(hp, w_main, w_tail, tables_p)
        oa = _prompt_attn(q, kb, vb, ga, diff_lambda[l], diff_norm_w[l], lam_init)
        prep = _delta_prep(cin, jnp.zeros((bp, CONV_W - 1, CONV_CH), F32), conv_w[l], tail, alog_row, dtb_row)
        ob, sp = _delta_scan(prep, z, jnp.zeros((bp, H_DELTA, DK_DELTA, DV_DELTA), F32), delta_norm_w[l])
        hp = _out_proj(oa.reshape(bp * tp, W_DIFF), ob.reshape(bp * tp, W_DELTA), hp.reshape(bp * tp, d),
                       w_out_b, ln_g[l], ln_b[l], alpha).reshape(bp, tp, d)
        outs[0].append(k.reshape(bp, tp, H_DIFF, 2, DH_DIFF))
        outs[1].append(v.reshape(bp, tp, H_DIFF, DV_DIFF))
        outs[2].append(sp)
        outs[3].append(cin[:, tp - (CONV_W - 1):, :])

        sq, sk, _, sv, _, sga, scin, sz, stail = _in_proj(hs.reshape(1, bs, d), w_main, w_tail, tables_s)
        tok = lambda a: a.reshape(bs, 1, a.shape[-1])
        soa = _decode_attn(tok(sq), tok(sk), tok(sv), tok(sga), cache_k, cache_v, l, page_table,
                           diff_lambda[l], diff_norm_w[l], lam_init)
        sob, ss = _delta_step(tok(scin), state_conv[l], conv_w[l], tok(stail), alog_row, dtb_row, tok(sz),
                              state_delta[l], delta_norm_w[l])
        hs = _out_proj(soa.reshape(bs, W_DIFF), sob.reshape(bs, W_DELTA), hs.reshape(bs, d),
                       w_out_b, ln_g[l], ln_b[l], alpha).reshape(bs, 1, d)
        outs[4].append(sk.reshape(bs, 1, H_DIFF, 2, DH_DIFF))
        outs[5].append(sv.reshape(bs, 1, H_DIFF, DV_DIFF))
        outs[6].append(ss)
        outs[7].append(jnp.concatenate([state_conv[l][:, 1:, :], tok(scin)], axis=1))
    return (hp, hs) + tuple(jnp.stack(o) for o in outs)
```

```python
import functools
import math

import jax
import jax.numpy as jnp
from jax import lax
from jax.experimental import pallas as pl
from jax.experimental.pallas import tpu as pltpu

F32 = jnp.float32
BF16 = jnp.bfloat16
HIGHEST = lax.Precision.HIGHEST

H_DIFF = 4
DH_DIFF = 64
DV_DIFF = 2 * DH_DIFF
W_DIFF = H_DIFF * DV_DIFF
H_DELTA = 4
DK_DELTA = 128
DV_DELTA = 128
W_DELTA = H_DELTA * DK_DELTA
ROT_DIM = DH_DIFF // 4
ROPE_THETA = 500000.0
CONV_W = 4
CONV_CH = 3 * W_DELTA
CHUNK = 64
P_MAIN = 4 * W_DIFF + 4 * W_DELTA
LN_EPS = 1e-5
HEAD_NORM_EPS = 1e-5
DELTA_NORM_EPS = 1e-6
L2_EPS = 1e-6
NEG_INF = -1e30

LANES = 128
PAIR = 2 * CHUNK
VMEM_LIMIT = 56 * 1024 * 1024
DECODE_PAGES_PER_STEP = 16

_NT = (((1,), (1,)), ((), ()))


def _sigmoid(x):
    return 0.5 * jnp.tanh(0.5 * x) + 0.5


def _silu(x):
    return x * _sigmoid(x)


def _softplus(x):
    return jnp.maximum(x, 0.0) + jnp.log1p(jnp.exp(-jnp.abs(x)))


def _lambda_init(layer):
    return 0.8 - 0.6 * math.exp(-0.3 * layer)


def _rope_tables(pos):
    half = ROT_DIM // 2
    inv = ROPE_THETA ** (-jnp.arange(half, dtype=F32) / half)
    ang = pos.astype(F32)[:, None] * inv[None, :]
    cos, sin = jnp.cos(ang), jnp.sin(ang)
    t = pos.shape[0]
    rest = DH_DIFF - ROT_DIM
    cos_m = jnp.concatenate([cos, cos, jnp.ones((t, rest), F32)], axis=-1)
    sin_lo = jnp.concatenate([-sin, jnp.zeros((t, half + rest), F32)], axis=-1)
    sin_hi = jnp.concatenate([jnp.zeros((t, half), F32), sin, jnp.zeros((t, rest), F32)], axis=-1)
    reps = W_DIFF // DH_DIFF
    return jnp.tile(cos_m, (1, reps)), jnp.tile(sin_lo, (1, reps)), jnp.tile(sin_hi, (1, reps))


def _in_proj_kernel(x_ref, w_ref, wt_ref, cos_ref, slo_ref, shi_ref,
                    q_ref, k_ref, kb_ref, v_ref, vb_ref, ga_ref, cin_ref, z_ref, tail_ref):
    xb = x_ref[0].astype(BF16)
    half = ROT_DIM // 2

    def proj(c0, width):
        return jnp.dot(xb, w_ref[:, c0:c0 + width], preferred_element_type=F32)

    def rope(h):
        return (h * cos_ref[...]
                + pltpu.roll(h, W_DIFF - half, 1) * slo_ref[...]
                + pltpu.roll(h, half, 1) * shi_ref[...])

    q_ref[0] = rope(proj(0, W_DIFF)).astype(BF16)
    k = rope(proj(W_DIFF, W_DIFF))
    k_ref[0] = k
    kb_ref[0] = k.astype(BF16)
    v = proj(2 * W_DIFF, W_DIFF)
    v_ref[0] = v
    vb_ref[0] = v.astype(BF16)
    ga_ref[0] = proj(3 * W_DIFF, W_DIFF)
    for j in range(3):
        cin_ref[0, :, j * W_DELTA:(j + 1) * W_DELTA] = proj(4 * W_DIFF + j * W_DELTA, W_DELTA)
    z_ref[0] = proj(4 * W_DIFF + 3 * W_DELTA, W_DELTA)
    tail_ref[0] = jnp.dot(xb, wt_ref[...], preferred_element_type=F32)


def _in_proj(x, w_main, w_tail, tables):
    b, t, d = x.shape
    tm = min(512, t)
    assert t % tm == 0
    cos_t, slo_t, shi_t = tables
    row = lambda width: pl.BlockSpec((1, tm, width), lambda ti, bi: (bi, ti, 0))
    tab = pl.BlockSpec((tm, W_DIFF), lambda ti, bi: (ti, 0))
    const = lambda shape: pl.BlockSpec(shape, lambda ti, bi: (0, 0))
    outs = [(W_DIFF, BF16), (W_DIFF, F32), (W_DIFF, BF16), (W_DIFF, F32), (W_DIFF, BF16),
            (W_DIFF, F32), (CONV_CH, F32), (W_DELTA, F32), (LANES, F32)]
    return pl.pallas_call(
        _in_proj_kernel,
        grid=(t // tm, b),
        in_specs=[row(d), const(w_main.shape), const(w_tail.shape), tab, tab, tab],
        out_specs=[row(wd) for wd, _ in outs],
        out_shape=[jax.ShapeDtypeStruct((b, t, wd), dt) for wd, dt in outs],
        compiler_params=pltpu.CompilerParams(
            dimension_semantics=("arbitrary", "arbitrary"), vmem_limit_bytes=VMEM_LIMIT),
        name="in_proj",
    )(x, w_main, w_tail, cos_t, slo_t, shi_t)


def _diff_lambda_value(dl_ref, lam_init):
    dl = dl_ref[...]
    a = jnp.sum(dl[0:1] * dl[1:2], axis=1, keepdims=True)
    b = jnp.sum(dl[2:3] * dl[3:4], axis=1, keepdims=True)
    return jnp.exp(a) - jnp.exp(b) + lam_init


def _head_norm_gate(o, normw, gate, lam_init):
    ms = jnp.mean(o * o, axis=-1, keepdims=True)
    o = o * lax.rsqrt(ms + HEAD_NORM_EPS) * normw * (1.0 - lam_init)
    return o * _silu(gate)


def _prompt_attn_kernel(q_ref, k_ref, v_ref, ga_ref, dl_ref, nw_ref, o_ref, s_scr, m_scr, l_scr, acc_scr,
                        *, blk, lam_init):
    i = pl.program_id(1)
    lane = lax.broadcasted_iota(jnp.int32, (blk, DV_DIFF), 1)
    heads = [slice(h * DV_DIFF, (h + 1) * DV_DIFF) for h in range(H_DIFF)]
    lane_chunks = [slice(c * LANES, (c + 1) * LANES) for c in range(blk // LANES)]
    q_maps = []
    for hs in heads:
        qs = q_ref[0, :, hs].astype(F32) * DH_DIFF ** -0.5
        q_maps.append(jnp.where(lane < DH_DIFF, qs, 0.0).astype(BF16))
        q_maps.append(jnp.where(lane >= DH_DIFF, qs, 0.0).astype(BF16))

    def fold(x, op):
        r = x[:, lane_chunks[0]]
        for c in lane_chunks[1:]:
            r = op(r, x[:, c])
        return r

    m_scr[...] = jnp.full(m_scr.shape, NEG_INF, F32)

    def score_block(j, masked):
        rows = pl.ds(pl.multiple_of(j * blk, blk), blk)
        for h, hs in enumerate(heads):
            k = k_ref[0, rows, hs]
            for mi in (2 * h, 2 * h + 1):
                s = lax.dot_general(q_maps[mi], k, _NT, preferred_element_type=F32)
                if masked:
                    row = lax.broadcasted_iota(jnp.int32, (blk, blk), 0)
                    col = lax.broadcasted_iota(jnp.int32, (blk, blk), 1)
                    s = jnp.where(col <= row, s, NEG_INF)
                s_scr[j, mi] = s
                m_scr[mi] = jnp.maximum(m_scr[mi], fold(s, jnp.maximum))

    def score_loop(j, carry):
        score_block(j, False)
        return carry

    lax.fori_loop(0, i, score_loop, 0)
    score_block(i, True)
    for mi in range(2 * H_DIFF):
        m_scr[mi] = jnp.broadcast_to(jnp.max(m_scr[mi], axis=-1, keepdims=True), (blk, LANES))

    l_scr[...] = jnp.zeros(l_scr.shape, F32)
    acc_scr[...] = jnp.zeros(acc_scr.shape, F32)

    def prob_block(j, carry):
        rows = pl.ds(pl.multiple_of(j * blk, blk), blk)
        for h, hs in enumerate(heads):
            v = v_ref[0, rows, hs]
            for mi in (2 * h, 2 * h + 1):
                m = m_scr[mi]
                p = jnp.concatenate([jnp.exp(s_scr[j, mi, :, c] - m) for c in lane_chunks], axis=1)
                l_scr[mi] = l_scr[mi] + fold(p, jnp.add)
                acc_scr[mi] = acc_scr[mi] + jnp.dot(p.astype(BF16), v, preferred_element_type=F32)
        return carry

    lax.fori_loop(0, i + 1, prob_block, 0)

    lam = _diff_lambda_value(dl_ref, lam_init)
    for h, hs in enumerate(heads):
        l1 = jnp.sum(l_scr[2 * h], axis=-1, keepdims=True)
        l2 = jnp.sum(l_scr[2 * h + 1], axis=-1, keepdims=True)
        o = acc_scr[2 * h] / l1 - lam * (acc_scr[2 * h + 1] / l2)
        o_ref[0, :, hs] = _head_norm_gate(o, nw_ref[...], ga_ref[0, :, hs], lam_init).astype(BF16)


def _prompt_attn(q, kb, vb, ga, diff_lambda, norm_w, lam_init):
    b, t, _ = q.shape
    blk = min(256, t)
    assert t % blk == 0
    n_maps = 2 * H_DIFF
    qspec = pl.BlockSpec((1, blk, W_DIFF), lambda bi, i: (bi, i, 0))
    kvspec = pl.BlockSpec((1, t, W_DIFF), lambda bi, i: (bi, 0, 0))
    return pl.pallas_call(
        functools.partial(_prompt_attn_kernel, blk=blk, lam_init=lam_init),
        grid=(b, t // blk),
        in_specs=[qspec, kvspec, kvspec, qspec,
                  pl.BlockSpec(diff_lambda.shape, lambda bi, i: (0, 0)),
                  pl.BlockSpec((1, DV_DIFF), lambda bi, i: (0, 0))],
        out_specs=qspec,
        out_shape=jax.ShapeDtypeStruct((b, t, W_DIFF), BF16),
        scratch_shapes=[pltpu.VMEM((t // blk, n_maps, blk, blk), F32),
                        pltpu.VMEM((n_maps, blk, LANES), F32),
                        pltpu.VMEM((n_maps, blk, LANES), F32),
                        pltpu.VMEM((n_maps, blk, DV_DIFF), F32)],
        compiler_params=pltpu.CompilerParams(
            dimension_semantics=("arbitrary", "arbitrary"), vmem_limit_bytes=VMEM_LIMIT),
        name="prompt_attn",
    )(q, kb, vb, ga, diff_lambda, norm_w.reshape(1, DV_DIFF))


class _DecodeStep:
    def __init__(self, g, n_g, in_refs, o_ref, scratch_refs, pages, lam_init):
        self.g, self.n_g, self.lam_init = g, n_g, lam_init
        self.q_ref, self.kn_ref, self.vn_ref, self.ga_ref, self.dl_ref, self.nw_ref = in_refs[:6]
        self.kp_refs, self.vp_refs = in_refs[6:6 + pages], in_refs[6 + pages:]
        self.o_ref = o_ref
        self.qcol_ref, self.m_ref, self.l_ref, self.acc_ref = scratch_refs

    def _query(self):
        return self.q_ref[0].astype(F32) * DH_DIFF ** -0.5

    def init(self):
        @pl.when(self.g == 0)
        def _():
            self.m_ref[...] = jnp.full(self.m_ref.shape, NEG_INF, F32)
            self.l_ref[...] = jnp.zeros(self.l_ref.shape, F32)
            self.acc_ref[...] = jnp.zeros(self.acc_ref.shape, F32)
            qs = self._query()
            for c in range(W_DIFF // LANES):
                cs = slice(c * LANES, (c + 1) * LANES)
                self.qcol_ref[cs, :] = jnp.broadcast_to(qs[:, cs], (LANES, LANES)).T

    def main(self):
        _decode_pages(self.kp_refs, self.vp_refs, self.qcol_ref, self.m_ref, self.l_ref, self.acc_ref)

    def finalize(self):
        @pl.when(self.g == self.n_g - 1)
        def _():
            _decode_finish(self._query(), self.kn_ref, self.vn_ref, self.ga_ref, self.dl_ref, self.nw_ref,
                           self.o_ref, self.m_ref, self.l_ref, self.acc_ref, self.lam_init)


def _decode_pages(kp_refs, vp_refs, qcol_ref, m_ref, l_ref, acc_ref):
    n_maps = 2 * H_DIFF
    page = kp_refs[0].shape[2]
    qcol = qcol_ref[...]
    s = jnp.concatenate(
        [jnp.sum((kp[0] * qcol).reshape(n_maps, DH_DIFF, page), axis=1) for kp in kp_refs], axis=1)
    m_prev = m_ref[:, 0:1]
    m_new = jnp.maximum(m_prev, jnp.max(s, axis=-1, keepdims=True))
    p = jnp.exp(s - m_new)
    alpha = jnp.exp(m_prev - m_new)
    l_new = alpha * l_ref[:, 0:1] + jnp.sum(p, axis=-1, keepdims=True)
    pb = p.astype(BF16)
    for h in range(H_DIFF):
        pv = jnp.zeros((n_maps, DV_DIFF), F32)
        for i, vp in enumerate(vp_refs):
            v_head = vp[0, pl.ds(h, page, stride=H_DIFF), :].astype(BF16)
            pv = pv + jnp.dot(pb[:, i * page:(i + 1) * page], v_head, preferred_element_type=F32)
        acc_ref[h] = alpha * acc_ref[h] + pv
    m_ref[...] = jnp.broadcast_to(m_new, m_ref.shape)
    l_ref[...] = jnp.broadcast_to(l_new, l_ref.shape)


def _decode_finish(qs, kn_ref, vn_ref, ga_ref, dl_ref, nw_ref, o_ref, m_ref, l_ref, acc_ref, lam_init):
    n_maps = 2 * H_DIFF
    rows = lax.broadcasted_iota(jnp.int32, (n_maps, W_DIFF), 0)
    lanes = lax.broadcasted_iota(jnp.int32, (n_maps, W_DIFF), 1)
    q_blk = jnp.where((lanes >> 6) == rows, jnp.broadcast_to(qs, (n_maps, W_DIFF)), 0.0).astype(BF16)
    k_self = jnp.broadcast_to(kn_ref[0].astype(BF16), (8, W_DIFF))
    s_self = lax.dot_general(q_blk, k_self, _NT, preferred_element_type=F32)[:, 0:1]
    m_past = m_ref[:, 0:1]
    m_f = jnp.maximum(m_past, s_self)
    a_f = jnp.exp(m_past - m_f)
    p_self = jnp.exp(s_self - m_f)
    l_f = a_f * l_ref[:, 0:1] + p_self
    pv_self = p_self.astype(BF16).astype(F32) * vn_ref[0].astype(BF16).astype(F32)
    lam = _diff_lambda_value(dl_ref, lam_init)
    for h in range(H_DIFF):
        hs = slice(h * DV_DIFF, (h + 1) * DV_DIFF)
        a = (a_f * acc_ref[h] + pv_self[:, hs]) / l_f
        o = a[2 * h:2 * h + 1] - lam * a[2 * h + 1:2 * h + 2]
        o_ref[0, :, hs] = _head_norm_gate(o, nw_ref[...], ga_ref[0, :, hs], lam_init).astype(BF16)


class _DecodeRider:
    def __init__(self, q, k_new, v_new, ga, cache_k, cache_v, layer, page_table, diff_lambda, norm_w, lam_init,
                 seq0, n_seq, grid):
        depth, n_phys, page = cache_k.shape[:3]
        n_pages = page_table.shape[1]
        n_steps = grid[0] * grid[1]
        assert page == LANES
        pages = n_seq * n_pages // n_steps
        self.ok = pages >= 1 and pages * n_steps == n_seq * n_pages and n_pages % pages == 0
        if not self.ok:
            return
        spq = n_pages // pages
        self.pages, self.spq, self.lam_init, self.n_inner = pages, spq, lam_init, grid[1]
        ck = jnp.transpose(cache_k, (0, 1, 3, 4, 5, 2)).reshape(depth * n_phys, W_DIFF, page)
        cv = cache_v.reshape(depth * n_phys, page * H_DIFF, DV_DIFF)
        self.page_ids = page_table.reshape(-1) + layer * n_phys
        step = lambda i0, i1: i0 * grid[1] + i1
        seq = lambda i0, i1: seq0 + step(i0, i1) // spq
        first = lambda i0, i1: seq0 * n_pages + step(i0, i1) * pages

        def page_spec(shape, i):
            return pl.BlockSpec(shape, lambda i0, i1, pt: (pt[first(i0, i1) + i], 0, 0))

        tok = pl.BlockSpec((1, 1, W_DIFF), lambda i0, i1, pt: (seq(i0, i1), 0, 0))
        const = lambda shape: pl.BlockSpec(shape, lambda i0, i1, pt: (0, 0))
        self.inputs = [q, k_new, v_new, ga, diff_lambda, norm_w.reshape(1, DV_DIFF)] + [ck] * pages + [cv] * pages
        self.in_specs = ([tok, tok, tok, tok, const(diff_lambda.shape), const((1, DV_DIFF))]
                         + [page_spec((1, W_DIFF, page), i) for i in range(pages)]
                         + [page_spec((1, page * H_DIFF, DV_DIFF), i) for i in range(pages)])
        self.out_spec = pl.BlockSpec((1, 1, W_DIFF), lambda i0, i1, pt: (step(i0, i1) // spq, 0, 0))
        self.out_shape = jax.ShapeDtypeStruct((n_seq, 1, W_DIFF), BF16)
        n_maps = 2 * H_DIFF
        self.scratch_shapes = [pltpu.VMEM((W_DIFF, page), F32),
                               pltpu.VMEM((n_maps, LANES), F32), pltpu.VMEM((n_maps, LANES), F32),
                               pltpu.VMEM((H_DIFF, n_maps, DV_DIFF), F32)]

    def step(self, in_refs, out_ref, scratch_refs):
        step = pl.program_id(0) * self.n_inner + pl.program_id(1)
        return _DecodeStep(lax.rem(step, self.spq), self.spq, in_refs, out_ref, scratch_refs, self.pages,
                           self.lam_init)


def _ride(host_kernel, n_in, n_out, n_scratch, rider):
    n_rin = len(rider.inputs)

    def kernel(pt_ref, *refs):
        del pt_ref
        host_in, refs = refs[:n_in], refs[n_in:]
        rider_in, refs = refs[:n_rin], refs[n_rin:]
        host_out, refs = refs[:n_out], refs[n_out:]
        rider_out, refs = refs[0], refs[1:]
        host_scratch, rider_scratch = refs[:n_scratch], refs[n_scratch:]
        host_kernel(*host_in, *host_out, *host_scratch, decode=rider.step(rider_in, rider_out, rider_scratch))

    return kernel


def _hosted_call(host_kernel, grid, in_specs, out_specs, out_shape, scratch_shapes, inputs, name, rider):
    params = pltpu.CompilerParams(dimension_semantics=("arbitrary", "arbitrary"), vmem_limit_bytes=VMEM_LIMIT)
    if rider is None:
        outs = pl.pallas_call(host_kernel, grid=grid, in_specs=in_specs, out_specs=out_specs, out_shape=out_shape,
                              scratch_shapes=scratch_shapes, compiler_params=params, name=name)(*inputs)
        return outs, None
    grid_spec = pltpu.PrefetchScalarGridSpec(
        num_scalar_prefetch=1, grid=grid,
        in_specs=list(in_specs) + rider.in_specs,
        out_specs=list(out_specs) + [rider.out_spec],
        scratch_shapes=list(scratch_shapes) + rider.scratch_shapes)
    outs = pl.pallas_call(
        _ride(host_kernel, len(in_specs), len(out_specs), len(scratch_shapes), rider),
        grid_spec=grid_spec, out_shape=list(out_shape) + [rider.out_shape],
        compiler_params=params, name=name)(rider.page_ids, *inputs, *rider.inputs)
    return outs[:-1], outs[-1]


def _decode_attn_kernel(pt_ref, *refs, rider):
    del pt_ref
    n_rin = len(rider.inputs)
    decode = rider.step(refs[:n_rin], refs[n_rin], refs[n_rin + 1:])
    decode.init()
    decode.main()
    decode.finalize()


def _decode_attn(rider_args, n_seq):
    n_pages = rider_args["page_table"].shape[1]
    grid = (n_seq, n_pages // math.gcd(DECODE_PAGES_PER_STEP, n_pages))
    rider = _DecodeRider(**rider_args, seq0=0, n_seq=n_seq, grid=grid)
    grid_spec = pltpu.PrefetchScalarGridSpec(
        num_scalar_prefetch=1, grid=grid, in_specs=rider.in_specs,
        out_specs=rider.out_spec, scratch_shapes=rider.scratch_shapes)
    return pl.pallas_call(
        functools.partial(_decode_attn_kernel, rider=rider), grid_spec=grid_spec, out_shape=rider.out_shape,
        compiler_params=pltpu.CompilerParams(
            dimension_semantics=("arbitrary", "arbitrary"), vmem_limit_bytes=VMEM_LIMIT),
        name="decode_attn")(rider.page_ids, *rider.inputs)


def _gate_rows(tail, alog_row, dtb_row):
    beta = _sigmoid(tail)
    g = -jnp.exp(alog_row) * _softplus(tail + dtb_row)
    return beta, g


def _lane_bcast(x, lane, rows):
    return jnp.broadcast_to(x[:, lane:lane + 1], (rows, LANES))


def _l2norm(x):
    return x * lax.rsqrt(jnp.sum(x * x, axis=-1, keepdims=True) + L2_EPS)


def _split_bf16(x):
    hi = x.astype(BF16)
    return hi, (x - hi.astype(F32)).astype(BF16)


def _dot_split(lhs, rhs):
    d = lambda a, b: jnp.dot(a, b, preferred_element_type=F32)
    return d(lhs[0], rhs[0]) + d(lhs[0], rhs[1]) + d(lhs[1], rhs[0])


def _unit_lower_inverses(a_mats):
    n = a_mats[0].shape[0]
    eye = (lax.broadcasted_iota(jnp.int32, (n, n), 0) == lax.broadcasted_iota(jnp.int32, (n, n), 1)).astype(F32)
    xs = [-a for a in a_mats]
    ps = [eye + x for x in xs]
    splits = [_split_bf16(x) for x in xs]
    xs = [_dot_split(s, s) for s in splits]
    power = 2
    while 2 * power < CHUNK:
        both = [_dot_split(_split_bf16(x), _split_bf16(jnp.concatenate([p, x], axis=1))) for p, x in zip(ps, xs)]
        ps = [p + b[:, :n] for p, b in zip(ps, both)]
        xs = [b[:, n:] for b in both]
        power *= 2
    return [p + _dot_split(_split_bf16(x), _split_bf16(p)) for p, x in zip(ps, xs)]


def _delta_prep_kernel(cin_ref, prev_ref, cw_ref, tail_ref, alog_ref, dtb_ref,
                       u_ref, w_ref, qg_ref, at_ref, kdt_ref, el_ref, ext_ref, *, pairs, decode=None):
    t = pl.program_id(1)
    pad = 8
    keep = CONV_W - 1
    rows = pairs * PAIR

    @pl.when(t == 0)
    def _():
        ext_ref[pad - keep:pad, :] = prev_ref[0]

    @pl.when(t > 0)
    def _():
        ext_ref[pad - keep:pad, :] = ext_ref[pad + rows - keep:pad + rows, :]

    if decode is not None:
        decode.init()
    ext_ref[pad:pad + rows, :] = cin_ref[0]

    r = lax.broadcasted_iota(jnp.int32, (PAIR, PAIR), 0)
    c = lax.broadcasted_iota(jnp.int32, (PAIR, PAIR), 1)
    same = (r >> 6) == (c >> 6)
    incl = same & (c <= r)
    strict = same & (c < r)

    chains = []
    for pi in range(pairs):
        base = pad - keep + pi * PAIR
        conv = ext_ref[base:base + PAIR, :] * cw_ref[0:1, :]
        for j in range(1, CONV_W):
            conv = conv + ext_ref[base + j:base + j + PAIR, :] * cw_ref[j:j + 1, :]
        conv = _silu(conv)
        rs = slice(pi * PAIR, (pi + 1) * PAIR)
        beta_all, g_all = _gate_rows(tail_ref[0, rs, :], alog_ref[...], dtb_ref[...])
        gc_all = jnp.dot(incl.astype(F32), g_all, precision=HIGHEST, preferred_element_type=F32)
        for h in range(H_DELTA):
            hs = slice(h * DK_DELTA, (h + 1) * DK_DELTA)
            qs = _l2norm(conv[:, hs]) * DK_DELTA ** -0.5
            kn = _l2norm(conv[:, W_DELTA + h * DK_DELTA:W_DELTA + (h + 1) * DK_DELTA])
            vh = conv[:, 2 * W_DELTA + h * DV_DELTA:2 * W_DELTA + (h + 1) * DV_DELTA]
            beta = _lane_bcast(beta_all, h, PAIR)
            gc = _lane_bcast(gc_all, H_DELTA + h, PAIR)
            chains.append(dict(pi=pi, h=h, rs=rs, qs=qs, kn=kn, knb=kn.astype(BF16), vh=vh, beta=beta, gc=gc,
                               kbeta=kn * beta))

    for ch in chains:
        ch["kk"] = lax.dot_general(ch["kbeta"].astype(BF16), ch["knb"], _NT, preferred_element_type=F32)
        ch["qk"] = lax.dot_general(ch["qs"].astype(BF16), ch["knb"], _NT, preferred_element_type=F32)
    for ch in chains:
        gc = ch["gc"]
        ch["decay"] = jnp.exp(jnp.where(incl, gc - gc.T, -jnp.inf))
    t_mats = _unit_lower_inverses([jnp.where(strict, ch["kk"] * ch["decay"], 0.0) for ch in chains])
    for ch, t_mat in zip(chains, t_mats):
        rhs = jnp.concatenate([ch["vh"] * ch["beta"], ch["kbeta"] * jnp.exp(ch["gc"])], axis=1).astype(BF16)
        ch["uw"] = jnp.dot(t_mat.astype(BF16), rhs, preferred_element_type=F32)
    for ch in chains:
        pi, h, rs, gc = ch["pi"], ch["h"], ch["rs"], ch["gc"]
        u_ref[0, h, rs, :] = ch["uw"][:, :DV_DELTA]
        w_ref[0, h, rs, :] = ch["uw"][:, DV_DELTA:].astype(BF16)
        attn = jnp.where(incl, ch["qk"] * ch["decay"], 0.0)
        at_ref[0, h, rs, :] = jnp.concatenate([attn[:CHUNK, :CHUNK], attn[CHUNK:, CHUNK:]], axis=0).astype(BF16)
        qg_ref[0, h, rs, :] = (ch["qs"] * jnp.exp(gc)).astype(BF16)
        g_last = jnp.concatenate([jnp.broadcast_to(gc[CHUNK - 1:CHUNK], (CHUNK, LANES)),
                                  jnp.broadcast_to(gc[PAIR - 1:PAIR], (CHUNK, LANES))], axis=0)
        kdt_ref[0, h, pi] = (ch["kn"] * jnp.exp(g_last - gc)).T.astype(BF16)
        e_last = jnp.exp(g_last)
        el_ref[0, h, pi] = jnp.concatenate([e_last[0:8], e_last[CHUNK:CHUNK + 8]], axis=0)

    if decode is not None:
        decode.main()
        decode.finalize()


def _delta_prep_grid(t):
    assert t % PAIR == 0
    npair = t // PAIR
    pairs = 2 if npair % 2 == 0 else 1
    return pairs, npair // pairs


def _delta_prep(cin, conv_prev, conv_w, tail, alog_row, dtb_row, rider):
    b, t, _ = cin.shape
    npair = t // PAIR
    pairs, nt = _delta_prep_grid(t)
    rows = pairs * PAIR
    per_head = lambda width: pl.BlockSpec((1, H_DELTA, rows, width), lambda bi, ti, *_: (bi, 0, ti, 0))
    const = lambda shape: pl.BlockSpec(shape, lambda bi, ti, *_: (0,) * len(shape))
    out_shape = [
        jax.ShapeDtypeStruct((b, H_DELTA, t, DV_DELTA), F32),
        jax.ShapeDtypeStruct((b, H_DELTA, t, DK_DELTA), BF16),
        jax.ShapeDtypeStruct((b, H_DELTA, t, DK_DELTA), BF16),
        jax.ShapeDtypeStruct((b, H_DELTA, t, CHUNK), BF16),
        jax.ShapeDtypeStruct((b, H_DELTA, npair, DK_DELTA, PAIR), BF16),
        jax.ShapeDtypeStruct((b, H_DELTA, npair, 16, LANES), F32),
    ]
    out_specs = [per_head(DV_DELTA), per_head(DK_DELTA), per_head(DK_DELTA), per_head(CHUNK),
                 pl.BlockSpec((1, H_DELTA, pairs, DK_DELTA, PAIR), lambda bi, ti, *_: (bi, 0, ti, 0, 0)),
                 pl.BlockSpec((1, H_DELTA, pairs, 16, LANES), lambda bi, ti, *_: (bi, 0, ti, 0, 0))]
    return _hosted_call(
        functools.partial(_delta_prep_kernel, pairs=pairs),
        grid=(b, nt),
        in_specs=[pl.BlockSpec((1, rows, CONV_CH), lambda bi, ti, *_: (bi, ti, 0)),
                  pl.BlockSpec((1, CONV_W - 1, CONV_CH), lambda bi, ti, *_: (bi, 0, 0)),
                  const(conv_w.shape),
                  pl.BlockSpec((1, rows, LANES), lambda bi, ti, *_: (bi, ti, 0)),
                  const(alog_row.shape), const(dtb_row.shape)],
        out_specs=out_specs,
        out_shape=out_shape,
        scratch_shapes=[pltpu.VMEM((8 + rows, CONV_CH), F32)],
        inputs=(cin, conv_prev, conv_w, tail, alog_row, dtb_row),
        name="delta_prep", rider=rider)


def _delta_norm_gate(o, normw, z):
    ms = jnp.mean(o * o, axis=-1, keepdims=True)
    return o * lax.rsqrt(ms + DELTA_NORM_EPS) * normw * _silu(z)


def _delta_scan_kernel(u_ref, w_ref, qg_ref, at_ref, kdt_ref, el_ref, z_ref, s0_ref, nw_ref,
                       o_ref, sfin_ref, s_scr, *, nb, nblk):
    t = pl.program_id(1)

    @pl.when(t == 0)
    def _():
        s_scr[...] = s0_ref[...]

    def block(jb, carry):
        r0 = pl.multiple_of(jb * PAIR, PAIR)
        seqs = [(bi, h) for bi in range(nb) for h in range(H_DELTA)]
        for ci in range(2):
            rows = pl.ds(pl.multiple_of(r0 + ci * CHUNK, CHUNK), CHUNK)
            states = [s_scr[bi, h] for bi, h in seqs]
            res = [jnp.dot(jnp.concatenate([w_ref[bi, h, rows, :], qg_ref[bi, h, rows, :]], axis=0),
                           s.astype(BF16), preferred_element_type=F32) for (bi, h), s in zip(seqs, states)]
            v_new = [(u_ref[bi, h, rows, :] - r[:CHUNK]).astype(BF16) for (bi, h), r in zip(seqs, res)]
            upd = [jnp.dot(kdt_ref[bi, h, jb, :, ci * CHUNK:(ci + 1) * CHUNK], v, preferred_element_type=F32)
                   for (bi, h), v in zip(seqs, v_new)]
            intra = [jnp.dot(at_ref[bi, h, rows, :], v, preferred_element_type=F32) for (bi, h), v in zip(seqs, v_new)]
            for (bi, h), s, r, du, oi in zip(seqs, states, res, upd, intra):
                decay = jnp.broadcast_to(el_ref[bi, h, jb, ci * 8:ci * 8 + 1, :], (DK_DELTA, DV_DELTA))
                s_scr[bi, h] = s * decay + du
                hs = slice(h * DV_DELTA, (h + 1) * DV_DELTA)
                o_ref[bi, rows, hs] = _delta_norm_gate(r[CHUNK:] + oi, nw_ref[...], z_ref[bi, rows, hs]).astype(BF16)
        return carry

    lax.fori_loop(0, nblk, block, 0)

    @pl.when(t == pl.num_programs(1) - 1)
    def _():
        sfin_ref[...] = s_scr[...]


def _delta_scan(prep, z, s0, norm_w):
    u, w, qg, at, kdt, el = prep
    b, _, t, _ = u.shape
    nb = 2 if b % 2 == 0 else 1
    ts = min(512, t)
    assert t % ts == 0
    nblk = ts // PAIR
    per_head = lambda width: pl.BlockSpec((nb, H_DELTA, ts, width), lambda bi, ti: (bi, 0, ti, 0))
    state = pl.BlockSpec((nb, H_DELTA, DK_DELTA, DV_DELTA), lambda bi, ti: (bi, 0, 0, 0))
    tok = pl.BlockSpec((nb, ts, W_DELTA), lambda bi, ti: (bi, ti, 0))
    return pl.pallas_call(
        functools.partial(_delta_scan_kernel, nb=nb, nblk=nblk),
        grid=(b // nb, t // ts),
        in_specs=[per_head(DV_DELTA), per_head(DK_DELTA), per_head(DK_DELTA), per_head(CHUNK),
                  pl.BlockSpec((nb, H_DELTA, nblk, DK_DELTA, PAIR), lambda bi, ti: (bi, 0, ti, 0, 0)),
                  pl.BlockSpec((nb, H_DELTA, nblk, 16, LANES), lambda bi, ti: (bi, 0, ti, 0, 0)),
                  tok, state, pl.BlockSpec((1, DV_DELTA), lambda bi, ti: (0, 0))],
        out_specs=[tok, state],
        out_shape=[jax.ShapeDtypeStruct((b, t, W_DELTA), BF16),
                   jax.ShapeDtypeStruct((b, H_DELTA, DK_DELTA, DV_DELTA), F32)],
        scratch_shapes=[pltpu.VMEM((nb, H_DELTA, DK_DELTA, DV_DELTA), F32)],
        compiler_params=pltpu.CompilerParams(
            dimension_semantics=("arbitrary", "arbitrary"), vmem_limit_bytes=VMEM_LIMIT),
        name="delta_scan",
    )(u, w, qg, at, kdt, el, z, s0, norm_w.reshape(1, DV_DELTA))


def _delta_step_kernel(cin_ref, prev_ref, cw_ref, tail_ref, alog_ref, dtb_ref, z_ref, s0_ref, nw_ref,
                       o_ref, s_ref):
    prev = prev_ref[0]
    conv = prev[0:1] * cw_ref[0:1, :]
    for j in range(1, CONV_W - 1):
        conv = conv + prev[j:j + 1] * cw_ref[j:j + 1, :]
    conv = _silu(conv + cin_ref[0] * cw_ref[CONV_W - 1:CONV_W, :])
    beta_all, g_all = _gate_rows(tail_ref[0], alog_ref[...], dtb_ref[...])
    for h in range(H_DELTA):
        hs = slice(h * DK_DELTA, (h + 1) * DK_DELTA)
        qs = _l2norm(conv[:, hs]) * DK_DELTA ** -0.5
        kn = _l2norm(conv[:, W_DELTA + h * DK_DELTA:W_DELTA + (h + 1) * DK_DELTA])
        vh = conv[:, 2 * W_DELTA + h * DV_DELTA:2 * W_DELTA + (h + 1) * DV_DELTA]
        beta = beta_all[:, h:h + 1]
        eg = jnp.exp(g_all[:, H_DELTA + h:H_DELTA + h + 1])
        s = s0_ref[0, h]
        lhs = jnp.concatenate([kn * (beta * eg), qs * eg, jnp.zeros((6, DK_DELTA), F32)], axis=0).astype(BF16)
        res = jnp.dot(lhs, s.astype(BF16), preferred_element_type=F32)
        v_new = vh * beta - res[0:1]
        qk = jnp.sum(qs.astype(BF16).astype(F32) * kn.astype(BF16).astype(F32), axis=-1, keepdims=True)
        o = res[1:2] + qk * v_new
        k_col = jnp.broadcast_to(kn, (DK_DELTA, DK_DELTA)).T
        s_ref[0, h] = s * eg + k_col * v_new
        o_ref[0, :, hs] = _delta_norm_gate(o, nw_ref[...], z_ref[0, :, hs]).astype(BF16)


def _delta_step(cin, conv_prev, conv_w, tail, alog_row, dtb_row, z, s0, norm_w):
    bs = cin.shape[0]
    tok = lambda width: pl.BlockSpec((1, 1, width), lambda b: (b, 0, 0))
    const = lambda shape: pl.BlockSpec(shape, lambda b: (0,) * len(shape))
    state = pl.BlockSpec((1, H_DELTA, DK_DELTA, DV_DELTA), lambda b: (b, 0, 0, 0))
    return pl.pallas_call(
        _delta_step_kernel,
        grid=(bs,),
        in_specs=[tok(CONV_CH), pl.BlockSpec((1, CONV_W - 1, CONV_CH), lambda b: (b, 0, 0)), const(conv_w.shape),
                  tok(LANES), const(alog_row.shape), const(dtb_row.shape), tok(W_DELTA), state,
                  const((1, DV_DELTA))],
        out_specs=[tok(W_DELTA), state],
        out_shape=[jax.ShapeDtypeStruct((bs, 1, W_DELTA), BF16),
                   jax.ShapeDtypeStruct((bs, H_DELTA, DK_DELTA, DV_DELTA), F32)],
        compiler_params=pltpu.CompilerParams(dimension_semantics=("arbitrary",)),
        name="delta_step",
    )(cin, conv_prev, conv_w, tail, alog_row, dtb_row, z, s0, norm_w.reshape(1, DV_DELTA))


def _out_proj_kernel(oa_ref, ob_ref, x_ref, w_ref, g_ref, b_ref, y_ref, *, alpha):
    mix = jnp.dot(oa_ref[...], w_ref[:W_DIFF, :], preferred_element_type=F32)
    mix = mix + jnp.dot(ob_ref[...], w_ref[W_DIFF:, :], preferred_element_type=F32)
    r = alpha * x_ref[...] + mix
    mu = jnp.mean(r, axis=-1, keepdims=True)
    var = jnp.mean(jnp.square(r - mu), axis=-1, keepdims=True)
    y_ref[...] = (r - mu) * lax.rsqrt(var + LN_EPS) * g_ref[...] + b_ref[...]


def _out_proj(oa, ob, x, w_out_b, ln_g, ln_b, alpha):
    m, d = x.shape
    tm = min(512, m)
    assert m % tm == 0
    row = lambda width: pl.BlockSpec((tm, width), lambda i: (i, 0))
    const = lambda shape: pl.BlockSpec(shape, lambda i: (0, 0))
    return pl.pallas_call(
        functools.partial(_out_proj_kernel, alpha=alpha),
        grid=(m // tm,),
        in_specs=[row(W_DIFF), row(W_DELTA), row(d), const(w_out_b.shape), const((1, d)), const((1, d))],
        out_specs=row(d),
        out_shape=jax.ShapeDtypeStruct((m, d), F32),
        compiler_params=pltpu.CompilerParams(dimension_semantics=("arbitrary",), vmem_limit_bytes=VMEM_LIMIT),
        name="out_proj",
    )(oa, ob, x, w_out_b, ln_g.reshape(1, d), ln_b.reshape(1, d))


def _pad_lanes(vec, offset):
    return jnp.zeros((1, LANES), F32).at[0, offset:offset + vec.shape[0]].set(vec.astype(F32))


def kernel(x_prompt, x_sample, cache_k, cache_v, page_table, state_delta, state_conv, w_in, conv_w, a_log,
           dt_bias, delta_norm_w, diff_lambda, diff_norm_w, w_out, ln_g, ln_b):
    depth = w_in.shape[0]
    bp, tp, d = x_prompt.shape
    bs, ts, _ = x_sample.shape
    assert ts == 1 and w_in.shape[2] == P_MAIN + 2 * H_DELTA and d == w_out.shape[2]
    past_len = page_table.shape[1] * cache_k.shape[2]
    alpha = (2 * depth) ** 0.25
    tables_p = _rope_tables(jnp.arange(tp, dtype=jnp.int32))
    tables_s = _rope_tables(jnp.full((bs,), past_len, jnp.int32))

    hp, hs = x_prompt, x_sample
    outs = [[] for _ in range(8)]
    for l in range(depth):
        lam_init = _lambda_init(l)
        w_main = w_in[l, :, :P_MAIN].astype(BF16)
        w_tail = jnp.zeros((d, LANES), BF16).at[:, :2 * H_DELTA].set(w_in[l, :, P_MAIN:].astype(BF16))
        w_out_b = w_out[l].astype(BF16)
        alog_row = _pad_lanes(a_log[l], H_DELTA)
        dtb_row = _pad_lanes(dt_bias[l], H_DELTA)

        sq, sk, _, sv, _, sga, scin, sz, stail = _in_proj(hs.reshape(1, bs, d), w_main, w_tail, tables_s)
        tok = lambda a: a.reshape(bs, 1, a.shape[-1])
        decode_args = dict(q=tok(sq), k_new=tok(sk), v_new=tok(sv), ga=tok(sga), cache_k=cache_k, cache_v=cache_v,
                           layer=l, page_table=page_table, diff_lambda=diff_lambda[l], norm_w=diff_norm_w[l],
                           lam_init=lam_init)
        rider = _DecodeRider(**decode_args, seq0=0, n_seq=bs, grid=(bp, _delta_prep_grid(tp)[1]))
        if not rider.ok:
            rider = None

        q, k, kb, v, vb, ga, cin, z, tail = _in_proj(hp, w_main, w_tail, tables_p)
        oa = _prompt_attn(q, kb, vb, ga, diff_lambda[l], diff_norm_w[l], lam_init)
        prep, soa = _delta_prep(cin, jnp.zeros((bp, CONV_W - 1, CONV_CH), F32), conv_w[l], tail, alog_row,
                                dtb_row, rider)
        ob, sp = _delta_scan(prep, z, jnp.zeros((bp, H_DELTA, DK_DELTA, DV_DELTA), F32), delta_norm_w[l])
        hp = _out_proj(oa.reshape(bp * tp, W_DIFF), ob.reshape(bp * tp, W_DELTA), hp.reshape(bp * tp, d),
                       w_out_b, ln_g[l], ln_b[l], alpha).reshape(bp, tp, d)
        outs[0].append(k.reshape(bp, tp, H_DIFF, 2, DH_DIFF))
        outs[1].append(v.reshape(bp, tp, H_DIFF, DV_DIFF))
        outs[2].append(sp)
        outs[3].append(cin[:, tp - (CONV_W - 1):, :])

        if rider is None:
            soa = _decode_attn(decode_args, bs)
        sob, ss = _delta_step(tok(scin), state_conv[l], conv_w[l], tok(stail), alog_row, dtb_row, tok(sz),
                              state_delta[l], delta_norm_w[l])
        hs = _out_proj(soa.reshape(bs, W_DIFF), sob.reshape(bs, W_DELTA), hs.reshape(bs, d),
                       w_out_b, ln_g[l], ln_b[l], alpha).reshape(bs, 1, d)
        outs[4].append(sk.reshape(bs, 1, H_DIFF, 2, DH_DIFF))
        outs[5].append(sv.reshape(bs, 1, H_DIFF, DV_DIFF))
        outs[6].append(ss)
        outs[7].append(jnp.concatenate([state_conv[l][:, 1:, :], tok(scin)], axis=1))
    return (hp, hs) + tuple(jnp.stack(o) for o in outs)
```

```python
import functools
import math

import jax
import jax.numpy as jnp
from jax import lax
from jax.experimental import pallas as pl
from jax.experimental.pallas import tpu as pltpu

F32 = jnp.float32
BF16 = jnp.bfloat16
HIGHEST = lax.Precision.HIGHEST

H_DIFF = 4
DH_DIFF = 64
DV_DIFF = 2 * DH_DIFF
W_DIFF = H_DIFF * DV_DIFF
H_DELTA = 4
DK_DELTA = 128
DV_DELTA = 128
W_DELTA = H_DELTA * DK_DELTA
ROT_DIM = DH_DIFF // 4
ROPE_THETA = 500000.0
CONV_W = 4
CONV_CH = 3 * W_DELTA
CHUNK = 64
P_MAIN = 4 * W_DIFF + 4 * W_DELTA
LN_EPS = 1e-5
HEAD_NORM_EPS = 1e-5
DELTA_NORM_EPS = 1e-6
L2_EPS = 1e-6
NEG_INF = -1e30

LANES = 128
PAIR = 2 * CHUNK
VMEM_LIMIT = 56 * 1024 * 1024
DECODE_PAGES_PER_STEP = 16

_NT = (((1,), (1,)), ((), ()))


def _sigmoid(x):
    return 0.5 * jnp.tanh(0.5 * x) + 0.5


def _silu(x):
    return x * _sigmoid(x)


def _softplus(x):
    return jnp.maximum(x, 0.0) + jnp.log1p(jnp.exp(-jnp.abs(x)))


def _lambda_init(layer):
    return 0.8 - 0.6 * math.exp(-0.3 * layer)


def _rope_tables(pos):
    half = ROT_DIM // 2
    inv = ROPE_THETA ** (-jnp.arange(half, dtype=F32) / half)
    ang = pos.astype(F32)[:, None] * inv[None, :]
    cos, sin = jnp.cos(ang), jnp.sin(ang)
    t = pos.shape[0]
    rest = DH_DIFF - ROT_DIM
    cos_m = jnp.concatenate([cos, cos, jnp.ones((t, rest), F32)], axis=-1)
    sin_lo = jnp.concatenate([-sin, jnp.zeros((t, half + rest), F32)], axis=-1)
    sin_hi = jnp.concatenate([jnp.zeros((t, half), F32), sin, jnp.zeros((t, rest), F32)], axis=-1)
    reps = W_DIFF // DH_DIFF
    return jnp.tile(cos_m, (1, reps)), jnp.tile(sin_lo, (1, reps)), jnp.tile(sin_hi, (1, reps))


def _in_proj_kernel(x_ref, w_ref, wt_ref, cos_ref, slo_ref, shi_ref,
                    q_ref, k_ref, kb_ref, v_ref, vb_ref, ga_ref, cin_ref, z_ref, tail_ref, *tail_t_ref):
    xb = x_ref[0].astype(BF16)
    half = ROT_DIM // 2
    tm = xb.shape[0]

    def proj(c0, width):
        return jnp.dot(xb, w_ref[:, c0:c0 + width], preferred_element_type=F32)

    def rope(h):
        return (h * cos_ref[...]
                + pltpu.roll(h, W_DIFF - half, 1) * slo_ref[...]
                + pltpu.roll(h, half, 1) * shi_ref[...])

    q_ref[0] = rope(proj(0, W_DIFF)).astype(BF16)
    k = rope(proj(W_DIFF, W_DIFF))
    k_ref[0] = k
    kb_ref[0] = k.astype(BF16)
    v = proj(2 * W_DIFF, W_DIFF)
    for h in range(H_DIFF):
        v_ref[0, pl.ds(h, tm, stride=H_DIFF), :] = v[:, h * DV_DIFF:(h + 1) * DV_DIFF]
    vb_ref[0] = v.astype(BF16)
    ga_ref[0] = proj(3 * W_DIFF, W_DIFF)
    for j in range(3):
        cin_ref[0, :, j * W_DELTA:(j + 1) * W_DELTA] = proj(4 * W_DIFF + j * W_DELTA, W_DELTA)
    z_ref[0] = proj(4 * W_DIFF + 3 * W_DELTA, W_DELTA)
    tail = jnp.dot(xb, wt_ref[...], preferred_element_type=F32)
    tail_ref[0] = tail
    if tail_t_ref:
        for c in range(tm // LANES):
            cs = slice(c * LANES, (c + 1) * LANES)
            tail_t_ref[0][0, :, cs] = tail[cs, :].T[:2 * H_DELTA, :]


def _in_proj(x, w_main, w_tail, tables, gates_time_major):
    b, t, d = x.shape
    tm = min(512, t)
    assert t % tm == 0 and (not gates_time_major or tm % LANES == 0)
    cos_t, slo_t, shi_t = tables
    row = lambda width: pl.BlockSpec((1, tm, width), lambda ti, bi: (bi, ti, 0))
    tab = pl.BlockSpec((tm, W_DIFF), lambda ti, bi: (ti, 0))
    const = lambda shape: pl.BlockSpec(shape, lambda ti, bi: (0, 0))
    outs = [(W_DIFF, BF16), (W_DIFF, F32), (W_DIFF, BF16), None, (W_DIFF, BF16),
            (W_DIFF, F32), (CONV_CH, F32), (W_DELTA, F32), (LANES, F32)]
    out_specs = [row(o[0]) if o else pl.BlockSpec((1, tm * H_DIFF, DV_DIFF), lambda ti, bi: (bi, ti, 0))
                 for o in outs]
    out_shape = [jax.ShapeDtypeStruct((b, t, o[0]), o[1]) if o else
                 jax.ShapeDtypeStruct((b, t * H_DIFF, DV_DIFF), F32) for o in outs]
    if gates_time_major:
        out_specs.append(pl.BlockSpec((1, 2 * H_DELTA, tm), lambda ti, bi: (bi, 0, ti)))
        out_shape.append(jax.ShapeDtypeStruct((b, 2 * H_DELTA, t), F32))
    return pl.pallas_call(
        _in_proj_kernel,
        grid=(t // tm, b),
        in_specs=[row(d), const(w_main.shape), const(w_tail.shape), tab, tab, tab],
        out_specs=out_specs,
        out_shape=out_shape,
        compiler_params=pltpu.CompilerParams(
            dimension_semantics=("arbitrary", "arbitrary"), vmem_limit_bytes=VMEM_LIMIT),
        name="in_proj",
    )(x, w_main, w_tail, cos_t, slo_t, shi_t)


def _diff_lambda_value(dl_ref, lam_init):
    dl = dl_ref[...]
    a = jnp.sum(dl[0:1] * dl[1:2], axis=1, keepdims=True)
    b = jnp.sum(dl[2:3] * dl[3:4], axis=1, keepdims=True)
    return jnp.exp(a) - jnp.exp(b) + lam_init


def _head_norm_gate(o, normw, gate, lam_init):
    ms = jnp.mean(o * o, axis=-1, keepdims=True)
    o = o * lax.rsqrt(ms + HEAD_NORM_EPS) * normw * (1.0 - lam_init)
    return o * _silu(gate)


def _prompt_attn_kernel(q_ref, k_ref, v_ref, ga_ref, dl_ref, nw_ref, o_ref, s_scr, m_scr, l_scr, acc_scr,
                        *, blk, lam_init):
    i = pl.program_id(1)
    lane = lax.broadcasted_iota(jnp.int32, (blk, DV_DIFF), 1)
    heads = [slice(h * DV_DIFF, (h + 1) * DV_DIFF) for h in range(H_DIFF)]
    lane_chunks = [slice(c * LANES, (c + 1) * LANES) for c in range(blk // LANES)]
    q_maps = []
    for hs in heads:
        qs = q_ref[0, :, hs].astype(F32) * DH_DIFF ** -0.5
        q_maps.append(jnp.where(lane < DH_DIFF, qs, 0.0).astype(BF16))
        q_maps.append(jnp.where(lane >= DH_DIFF, qs, 0.0).astype(BF16))

    def fold(x, op):
        r = x[:, lane_chunks[0]]
        for c in lane_chunks[1:]:
            r = op(r, x[:, c])
        return r

    m_scr[...] = jnp.full(m_scr.shape, NEG_INF, F32)

    def score_block(j, masked):
        rows = pl.ds(pl.multiple_of(j * blk, blk), blk)
        for h, hs in enumerate(heads):
            k = k_ref[0, rows, hs]
            for mi in (2 * h, 2 * h + 1):
                s = lax.dot_general(q_maps[mi], k, _NT, preferred_element_type=F32)
                if masked:
                    row = lax.broadcasted_iota(jnp.int32, (blk, blk), 0)
                    col = lax.broadcasted_iota(jnp.int32, (blk, blk), 1)
                    s = jnp.where(col <= row, s, NEG_INF)
                s_scr[j, mi] = s
                m_scr[mi] = jnp.maximum(m_scr[mi], fold(s, jnp.maximum))

    def score_loop(j, carry):
        score_block(j, False)
        return carry

    lax.fori_loop(0, i, score_loop, 0)
    score_block(i, True)
    for mi in range(2 * H_DIFF):
        m_scr[mi] = jnp.broadcast_to(jnp.max(m_scr[mi], axis=-1, keepdims=True), (blk, LANES))

    l_scr[...] = jnp.zeros(l_scr.shape, F32)
    acc_scr[...] = jnp.zeros(acc_scr.shape, F32)

    def prob_block(j, carry):
        rows = pl.ds(pl.multiple_of(j * blk, blk), blk)
        for h, hs in enumerate(heads):
            v = v_ref[0, rows, hs]
            for mi in (2 * h, 2 * h + 1):
                m = m_scr[mi]
                p = jnp.concatenate([jnp.exp(s_scr[j, mi, :, c] - m) for c in lane_chunks], axis=1)
                l_scr[mi] = l_scr[mi] + fold(p, jnp.add)
                acc_scr[mi] = acc_scr[mi] + jnp.dot(p.astype(BF16), v, preferred_element_type=F32)
        return carry

    lax.fori_loop(0, i + 1, prob_block, 0)

    lam = _diff_lambda_value(dl_ref, lam_init)
    for h, hs in enumerate(heads):
        l1 = jnp.sum(l_scr[2 * h], axis=-1, keepdims=True)
        l2 = jnp.sum(l_scr[2 * h + 1], axis=-1, keepdims=True)
        o = acc_scr[2 * h] / l1 - lam * (acc_scr[2 * h + 1] / l2)
        o_ref[0, :, hs] = _head_norm_gate(o, nw_ref[...], ga_ref[0, :, hs], lam_init).astype(BF16)


def _prompt_attn(q, kb, vb, ga, diff_lambda, norm_w, lam_init):
    b, t, _ = q.shape
    blk = min(256, t)
    assert t % blk == 0
    n_maps = 2 * H_DIFF
    qspec = pl.BlockSpec((1, blk, W_DIFF), lambda bi, i: (bi, i, 0))
    kvspec = pl.BlockSpec((1, t, W_DIFF), lambda bi, i: (bi, 0, 0))
    return pl.pallas_call(
        functools.partial(_prompt_attn_kernel, blk=blk, lam_init=lam_init),
        grid=(b, t // blk),
        in_specs=[qspec, kvspec, kvspec, qspec,
                  pl.BlockSpec(diff_lambda.shape, lambda bi, i: (0, 0)),
                  pl.BlockSpec((1, DV_DIFF), lambda bi, i: (0, 0))],
        out_specs=qspec,
        out_shape=jax.ShapeDtypeStruct((b, t, W_DIFF), BF16),
        scratch_shapes=[pltpu.VMEM((t // blk, n_maps, blk, blk), F32),
                        pltpu.VMEM((n_maps, blk, LANES), F32),
                        pltpu.VMEM((n_maps, blk, LANES), F32),
                        pltpu.VMEM((n_maps, blk, DV_DIFF), F32)],
        compiler_params=pltpu.CompilerParams(
            dimension_semantics=("arbitrary", "arbitrary"), vmem_limit_bytes=VMEM_LIMIT),
        name="prompt_attn",
    )(q, kb, vb, ga, diff_lambda, norm_w.reshape(1, DV_DIFF))


class _DecodeStep:
    def __init__(self, g, n_g, in_refs, o_ref, scratch_refs, pages, lam_init):
        self.g, self.n_g, self.lam_init = g, n_g, lam_init
        self.q_ref, self.kn_ref, self.vn_ref, self.ga_ref, self.dl_ref, self.nw_ref = in_refs[:6]
        self.kp_refs, self.vp_refs = in_refs[6:6 + pages], in_refs[6 + pages:]
        self.o_ref = o_ref
        self.qcol_ref, self.m_ref, self.l_ref, self.acc_ref = scratch_refs

    def _query(self):
        return self.q_ref[0].astype(F32) * DH_DIFF ** -0.5

    def init(self):
        @pl.when(self.g == 0)
        def _():
            self.m_ref[...] = jnp.full(self.m_ref.shape, NEG_INF, F32)
            self.l_ref[...] = jnp.zeros(self.l_ref.shape, F32)
            self.acc_ref[...] = jnp.zeros(self.acc_ref.shape, F32)
            qs = self._query()
            for c in range(W_DIFF // LANES):
                cs = slice(c * LANES, (c + 1) * LANES)
                self.qcol_ref[cs, :] = jnp.broadcast_to(qs[:, cs], (LANES, LANES)).T

    def main(self):
        _decode_pages(self.kp_refs, self.vp_refs, self.qcol_ref, self.m_ref, self.l_ref, self.acc_ref)

    def finalize(self):
        @pl.when(self.g == self.n_g - 1)
        def _():
            _decode_finish(self._query(), self.kn_ref, self.vn_ref, self.ga_ref, self.dl_ref, self.nw_ref,
                           self.o_ref, self.m_ref, self.l_ref, self.acc_ref, self.lam_init)


def _decode_pages(kp_refs, vp_refs, qcol_ref, m_ref, l_ref, acc_ref):
    n_maps = 2 * H_DIFF
    page = kp_refs[0].shape[2]
    qcol = qcol_ref[...]
    s = jnp.concatenate(
        [jnp.sum((kp[0] * qcol).reshape(n_maps, DH_DIFF, page), axis=1) for kp in kp_refs], axis=1)
    m_prev = m_ref[:, 0:1]
    m_new = jnp.maximum(m_prev, jnp.max(s, axis=-1, keepdims=True))
    p = jnp.exp(s - m_new)
    alpha = jnp.exp(m_prev - m_new)
    l_new = alpha * l_ref[:, 0:1] + jnp.sum(p, axis=-1, keepdims=True)
    pb = p.astype(BF16)
    for h in range(H_DIFF):
        pv = jnp.zeros((n_maps, DV_DIFF), F32)
        for i, vp in enumerate(vp_refs):
            v_head = vp[0, pl.ds(h, page, stride=H_DIFF), :].astype(BF16)
            pv = pv + jnp.dot(pb[:, i * page:(i + 1) * page], v_head, preferred_element_type=F32)
        acc_ref[h] = alpha * acc_ref[h] + pv
    m_ref[...] = jnp.broadcast_to(m_new, m_ref.shape)
    l_ref[...] = jnp.broadcast_to(l_new, l_ref.shape)


def _decode_finish(qs, kn_ref, vn_ref, ga_ref, dl_ref, nw_ref, o_ref, m_ref, l_ref, acc_ref, lam_init):
    n_maps = 2 * H_DIFF
    rows = lax.broadcasted_iota(jnp.int32, (n_maps, W_DIFF), 0)
    lanes = lax.broadcasted_iota(jnp.int32, (n_maps, W_DIFF), 1)
    q_blk = jnp.where((lanes >> 6) == rows, jnp.broadcast_to(qs, (n_maps, W_DIFF)), 0.0).astype(BF16)
    k_self = jnp.broadcast_to(kn_ref[0].astype(BF16), (8, W_DIFF))
    s_self = lax.dot_general(q_blk, k_self, _NT, preferred_element_type=F32)[:, 0:1]
    m_past = m_ref[:, 0:1]
    m_f = jnp.maximum(m_past, s_self)
    a_f = jnp.exp(m_past - m_f)
    p_self = jnp.exp(s_self - m_f)
    l_f = a_f * l_ref[:, 0:1] + p_self
    pv_self = p_self.astype(BF16).astype(F32) * vn_ref[0].astype(BF16).astype(F32)
    lam = _diff_lambda_value(dl_ref, lam_init)
    for h in range(H_DIFF):
        hs = slice(h * DV_DIFF, (h + 1) * DV_DIFF)
        a = (a_f * acc_ref[h] + pv_self[:, hs]) / l_f
        o = a[2 * h:2 * h + 1] - lam * a[2 * h + 1:2 * h + 2]
        o_ref[0, :, hs] = _head_norm_gate(o, nw_ref[...], ga_ref[0, :, hs], lam_init).astype(BF16)


class _DecodeRider:
    def __init__(self, q, k_new, v_new, ga, cache_k, cache_v, layer, page_table, diff_lambda, norm_w, lam_init,
                 seq0, n_seq, grid):
        depth, n_phys, page = cache_k.shape[:3]
        n_pages = page_table.shape[1]
        n_steps = grid[0] * grid[1]
        assert page == LANES
        pages = n_seq * n_pages // n_steps
        self.ok = pages >= 1 and pages * n_steps == n_seq * n_pages and n_pages % pages == 0
        if not self.ok:
            return
        spq = n_pages // pages
        self.pages, self.spq, self.lam_init, self.n_inner = pages, spq, lam_init, grid[1]
        ck = jnp.transpose(cache_k, (0, 1, 3, 4, 5, 2)).reshape(depth * n_phys, W_DIFF, page)
        cv = cache_v.reshape(depth * n_phys, page * H_DIFF, DV_DIFF)
        self.page_ids = page_table.reshape(-1) + layer * n_phys
        step = lambda i0, i1: i0 * grid[1] + i1
        seq = lambda i0, i1: seq0 + step(i0, i1) // spq
        first = lambda i0, i1: seq0 * n_pages + step(i0, i1) * pages

        def page_spec(shape, i):
            return pl.BlockSpec(shape, lambda i0, i1, pt: (pt[first(i0, i1) + i], 0, 0))

        tok = pl.BlockSpec((1, 1, W_DIFF), lambda i0, i1, pt: (seq(i0, i1), 0, 0))
        const = lambda shape: pl.BlockSpec(shape, lambda i0, i1, pt: (0, 0))
        self.inputs = [q, k_new, v_new, ga, diff_lambda, norm_w.reshape(1, DV_DIFF)] + [ck] * pages + [cv] * pages
        self.in_specs = ([tok, tok, tok, tok, const(diff_lambda.shape), const((1, DV_DIFF))]
                         + [page_spec((1, W_DIFF, page), i) for i in range(pages)]
                         + [page_spec((1, page * H_DIFF, DV_DIFF), i) for i in range(pages)])
        self.out_spec = pl.BlockSpec((1, 1, W_DIFF), lambda i0, i1, pt: (step(i0, i1) // spq, 0, 0))
        self.out_shape = jax.ShapeDtypeStruct((n_seq, 1, W_DIFF), BF16)
        n_maps = 2 * H_DIFF
        self.scratch_shapes = [pltpu.VMEM((W_DIFF, page), F32),
                               pltpu.VMEM((n_maps, LANES), F32), pltpu.VMEM((n_maps, LANES), F32),
                               pltpu.VMEM((H_DIFF, n_maps, DV_DIFF), F32)]

    def step(self, in_refs, out_ref, scratch_refs):
        step = pl.program_id(0) * self.n_inner + pl.program_id(1)
        return _DecodeStep(lax.rem(step, self.spq), self.spq, in_refs, out_ref, scratch_refs, self.pages,
                           self.lam_init)


def _ride(host_kernel, n_in, n_out, n_scratch, rider):
    n_rin = len(rider.inputs)

    def kernel(pt_ref, *refs):
        del pt_ref
        host_in, refs = refs[:n_in], refs[n_in:]
        rider_in, refs = refs[:n_rin], refs[n_rin:]
        host_out, refs = refs[:n_out], refs[n_out:]
        rider_out, refs = refs[0], refs[1:]
        host_scratch, rider_scratch = refs[:n_scratch], refs[n_scratch:]
        host_kernel(*host_in, *host_out, *host_scratch, decode=rider.step(rider_in, rider_out, rider_scratch))

    return kernel


def _hosted_call(host_kernel, grid, in_specs, out_specs, out_shape, scratch_shapes, inputs, name, rider):
    params = pltpu.CompilerParams(dimension_semantics=("arbitrary", "arbitrary"), vmem_limit_bytes=VMEM_LIMIT)
    if rider is None:
        outs = pl.pallas_call(host_kernel, grid=grid, in_specs=in_specs, out_specs=out_specs, out_shape=out_shape,
                              scratch_shapes=scratch_shapes, compiler_params=params, name=name)(*inputs)
        return outs, None
    grid_spec = pltpu.PrefetchScalarGridSpec(
        num_scalar_prefetch=1, grid=grid,
        in_specs=list(in_specs) + rider.in_specs,
        out_specs=list(out_specs) + [rider.out_spec],
        scratch_shapes=list(scratch_shapes) + rider.scratch_shapes)
    outs = pl.pallas_call(
        _ride(host_kernel, len(in_specs), len(out_specs), len(scratch_shapes), rider),
        grid_spec=grid_spec, out_shape=list(out_shape) + [rider.out_shape],
        compiler_params=params, name=name)(rider.page_ids, *inputs, *rider.inputs)
    return outs[:-1], outs[-1]


def _decode_attn_kernel(pt_ref, *refs, rider):
    del pt_ref
    n_rin = len(rider.inputs)
    decode = rider.step(refs[:n_rin], refs[n_rin], refs[n_rin + 1:])
    decode.init()
    decode.main()
    decode.finalize()


def _decode_attn(rider_args, n_seq):
    n_pages = rider_args["page_table"].shape[1]
    grid = (n_seq, n_pages // math.gcd(DECODE_PAGES_PER_STEP, n_pages))
    rider = _DecodeRider(**rider_args, seq0=0, n_seq=n_seq, grid=grid)
    grid_spec = pltpu.PrefetchScalarGridSpec(
        num_scalar_prefetch=1, grid=grid, in_specs=rider.in_specs,
        out_specs=rider.out_spec, scratch_shapes=rider.scratch_shapes)
    return pl.pallas_call(
        functools.partial(_decode_attn_kernel, rider=rider), grid_spec=grid_spec, out_shape=rider.out_shape,
        compiler_params=pltpu.CompilerParams(
            dimension_semantics=("arbitrary", "arbitrary"), vmem_limit_bytes=VMEM_LIMIT),
        name="decode_attn")(rider.page_ids, *rider.inputs)


def _gate_rows(tail, alog_row, dtb_row):
    beta = _sigmoid(tail)
    g = -jnp.exp(alog_row) * _softplus(tail + dtb_row)
    return beta, g


def _lane_bcast(x, lane, rows):
    return jnp.broadcast_to(x[:, lane:lane + 1], (rows, LANES))


def _l2norm(x):
    return x * lax.rsqrt(jnp.sum(x * x, axis=-1, keepdims=True) + L2_EPS)


def _split_bf16(x):
    hi = x.astype(BF16)
    return hi, (x - hi.astype(F32)).astype(BF16)


def _dot_split(lhs, rhs):
    d = lambda a, b: jnp.dot(a, b, preferred_element_type=F32)
    return d(lhs[0], rhs[0]) + d(lhs[0], rhs[1]) + d(lhs[1], rhs[0])


def _unit_lower_inverses(a_mats):
    n = a_mats[0].shape[0]
    eye = (lax.broadcasted_iota(jnp.int32, (n, n), 0) == lax.broadcasted_iota(jnp.int32, (n, n), 1)).astype(F32)
    xs = [-a for a in a_mats]
    ps = [eye + x for x in xs]
    splits = [_split_bf16(x) for x in xs]
    xs = [_dot_split(s, s) for s in splits]
    power = 2
    while 2 * power < CHUNK:
        both = [_dot_split(_split_bf16(x), _split_bf16(jnp.concatenate([p, x], axis=1))) for p, x in zip(ps, xs)]
        ps = [p + b[:, :n] for p, b in zip(ps, both)]
        xs = [b[:, n:] for b in both]
        power *= 2
    return [p + _dot_split(_split_bf16(x), _split_bf16(p)) for p, x in zip(ps, xs)]


def _delta_prep_kernel(cin_ref, prev_ref, cw_ref, tail_ref, alog_ref, dtb_ref,
                       u_ref, w_ref, qg_ref, at_ref, kdt_ref, el_ref, ext_ref, *, pairs, decode=None):
    t = pl.program_id(1)
    pad = 8
    keep = CONV_W - 1
    rows = pairs * PAIR

    @pl.when(t == 0)
    def _():
        ext_ref[pad - keep:pad, :] = prev_ref[0]

    @pl.when(t > 0)
    def _():
        ext_ref[pad - keep:pad, :] = ext_ref[pad + rows - keep:pad + rows, :]

    if decode is not None:
        decode.init()
    ext_ref[pad:pad + rows, :] = cin_ref[0]

    r = lax.broadcasted_iota(jnp.int32, (PAIR, PAIR), 0)
    c = lax.broadcasted_iota(jnp.int32, (PAIR, PAIR), 1)
    same = (r >> 6) == (c >> 6)
    incl = same & (c <= r)
    strict = same & (c < r)
    incl_t = same & (r <= c)
    chunk_end = r == (c | (CHUNK - 1))

    chains = []
    for pi in range(pairs):
        base = pad - keep + pi * PAIR
        conv = ext_ref[base:base + PAIR, :] * cw_ref[0:1, :]
        for j in range(1, CONV_W):
            conv = conv + ext_ref[base + j:base + j + PAIR, :] * cw_ref[j:j + 1, :]
        conv = _silu(conv)
        rs = slice(pi * PAIR, (pi + 1) * PAIR)
        beta_t, g_t = _gate_rows(tail_ref[0, :, rs], alog_ref[...], dtb_ref[...])
        gc_t = jnp.dot(g_t, incl_t.astype(F32), precision=HIGHEST, preferred_element_type=F32)
        g_last_t = jnp.dot(gc_t, chunk_end.astype(F32), precision=HIGHEST, preferred_element_type=F32)
        for h in range(H_DELTA):
            hs = slice(h * DK_DELTA, (h + 1) * DK_DELTA)
            qs = _l2norm(conv[:, hs]) * DK_DELTA ** -0.5
            kn = _l2norm(conv[:, W_DELTA + h * DK_DELTA:W_DELTA + (h + 1) * DK_DELTA])
            vh = conv[:, 2 * W_DELTA + h * DV_DELTA:2 * W_DELTA + (h + 1) * DV_DELTA]
            beta = jnp.broadcast_to(beta_t[h:h + 1, :], (PAIR, PAIR)).T
            gc_cols = jnp.broadcast_to(gc_t[H_DELTA + h:H_DELTA + h + 1, :], (PAIR, PAIR))
            g_last_cols = jnp.broadcast_to(g_last_t[H_DELTA + h:H_DELTA + h + 1, :], (PAIR, PAIR))
            chains.append(dict(pi=pi, h=h, rs=rs, qs=qs, kn=kn, vh=vh, beta=beta, gc=gc_cols.T, gc_cols=gc_cols,
                               g_last_cols=g_last_cols, kbeta=kn * beta))

    for ch in chains:
        ch["kn_t"] = ch["kn"].T
        kn_tb = ch["kn_t"].astype(BF16)
        ch["kk"] = jnp.dot(ch["kbeta"].astype(BF16), kn_tb, preferred_element_type=F32)
        ch["qk"] = jnp.dot(ch["qs"].astype(BF16), kn_tb, preferred_element_type=F32)
    for ch in chains:
        ch["decay"] = jnp.exp(jnp.where(incl, ch["gc"] - ch["gc_cols"], -jnp.inf))
    t_mats = _unit_lower_inverses([jnp.where(strict, ch["kk"] * ch["decay"], 0.0) for ch in chains])
    for ch, t_mat in zip(chains, t_mats):
        rhs = jnp.concatenate([ch["vh"] * ch["beta"], ch["kbeta"] * jnp.exp(ch["gc"])], axis=1).astype(BF16)
        ch["uw"] = jnp.dot(t_mat.astype(BF16), rhs, preferred_element_type=F32)
    for ch in chains:
        pi, h, rs, gc = ch["pi"], ch["h"], ch["rs"], ch["gc"]
        u_ref[0, h, rs, :] = ch["uw"][:, :DV_DELTA]
        w_ref[0, h, rs, :] = ch["uw"][:, DV_DELTA:].astype(BF16)
        attn = jnp.where(incl, ch["qk"] * ch["decay"], 0.0)
        at_ref[0, h, rs, :] = jnp.concatenate([attn[:CHUNK, :CHUNK], attn[CHUNK:, CHUNK:]], axis=0).astype(BF16)
        qg_ref[0, h, rs, :] = (ch["qs"] * jnp.exp(gc)).astype(BF16)
        kdt_ref[0, h, pi] = (ch["kn_t"] * jnp.exp(ch["g_last_cols"] - ch["gc_cols"])).astype(BF16)
        el_ref[0, h, pi] = jnp.exp(jnp.concatenate([jnp.broadcast_to(gc[CHUNK - 1:CHUNK], (8, LANES)),
                                                    jnp.broadcast_to(gc[PAIR - 1:PAIR], (8, LANES))], axis=0))

    if decode is not None:
        decode.main()
        decode.finalize()


def _delta_prep_grid(t):
    assert t % PAIR == 0
    npair = t // PAIR
    pairs = 2 if npair % 2 == 0 else 1
    return pairs, npair // pairs


def _delta_prep(cin, conv_prev, conv_w, tail, alog_row, dtb_row, rider):
    b, t, _ = cin.shape
    npair = t // PAIR
    pairs, nt = _delta_prep_grid(t)
    rows = pairs * PAIR
    per_head = lambda width: pl.BlockSpec((1, H_DELTA, rows, width), lambda bi, ti, *_: (bi, 0, ti, 0))
    const = lambda shape: pl.BlockSpec(shape, lambda bi, ti, *_: (0,) * len(shape))
    out_shape = [
        jax.ShapeDtypeStruct((b, H_DELTA, t, DV_DELTA), F32),
        jax.ShapeDtypeStruct((b, H_DELTA, t, DK_DELTA), BF16),
        jax.ShapeDtypeStruct((b, H_DELTA, t, DK_DELTA), BF16),
        jax.ShapeDtypeStruct((b, H_DELTA, t, CHUNK), BF16),
        jax.ShapeDtypeStruct((b, H_DELTA, npair, DK_DELTA, PAIR), BF16),
        jax.ShapeDtypeStruct((b, H_DELTA, npair, 16, LANES), F32),
    ]
    out_specs = [per_head(DV_DELTA), per_head(DK_DELTA), per_head(DK_DELTA), per_head(CHUNK),
                 pl.BlockSpec((1, H_DELTA, pairs, DK_DELTA, PAIR), lambda bi, ti, *_: (bi, 0, ti, 0, 0)),
                 pl.BlockSpec((1, H_DELTA, pairs, 16, LANES), lambda bi, ti, *_: (bi, 0, ti, 0, 0))]
    return _hosted_call(
        functools.partial(_delta_prep_kernel, pairs=pairs),
        grid=(b, nt),
        in_specs=[pl.BlockSpec((1, rows, CONV_CH), lambda bi, ti, *_: (bi, ti, 0)),
                  pl.BlockSpec((1, CONV_W - 1, CONV_CH), lambda bi, ti, *_: (bi, 0, 0)),
                  const(conv_w.shape),
                  pl.BlockSpec((1, 2 * H_DELTA, rows), lambda bi, ti, *_: (bi, 0, ti)),
                  const(alog_row.shape), const(dtb_row.shape)],
        out_specs=out_specs,
        out_shape=out_shape,
        scratch_shapes=[pltpu.VMEM((8 + rows, CONV_CH), F32)],
        inputs=(cin, conv_prev, conv_w, tail, alog_row, dtb_row),
        name="delta_prep", rider=rider)


def _delta_norm_gate(o, normw, z):
    ms = jnp.mean(o * o, axis=-1, keepdims=True)
    return o * lax.rsqrt(ms + DELTA_NORM_EPS) * normw * _silu(z)


def _delta_scan_kernel(u_ref, w_ref, qg_ref, at_ref, kdt_ref, el_ref, z_ref, s0_ref, nw_ref,
                       o_ref, sfin_ref, s_scr, *, nb, nblk):
    t = pl.program_id(1)

    @pl.when(t == 0)
    def _():
        s_scr[...] = s0_ref[...]

    def block(jb, carry):
        r0 = pl.multiple_of(jb * PAIR, PAIR)
        seqs = [(bi, h) for bi in range(nb) for h in range(H_DELTA)]
        for ci in range(2):
            rows = pl.ds(pl.multiple_of(r0 + ci * CHUNK, CHUNK), CHUNK)
            states = [s_scr[bi, h] for bi, h in seqs]
            res = [jnp.dot(jnp.concatenate([w_ref[bi, h, rows, :], qg_ref[bi, h, rows, :]], axis=0),
                           s.astype(BF16), preferred_element_type=F32) for (bi, h), s in zip(seqs, states)]
            v_new = [(u_ref[bi, h, rows, :] - r[:CHUNK]).astype(BF16) for (bi, h), r in zip(seqs, res)]
            upd = [jnp.dot(kdt_ref[bi, h, jb, :, ci * CHUNK:(ci + 1) * CHUNK], v, preferred_element_type=F32)
                   for (bi, h), v in zip(seqs, v_new)]
            intra = [jnp.dot(at_ref[bi, h, rows, :], v, preferred_element_type=F32) for (bi, h), v in zip(seqs, v_new)]
            for (bi, h), s, r, du, oi in zip(seqs, states, res, upd, intra):
                decay = jnp.broadcast_to(el_ref[bi, h, jb, ci * 8:ci * 8 + 1, :], (DK_DELTA, DV_DELTA))
                s_scr[bi, h] = s * decay + du
                hs = slice(h * DV_DELTA, (h + 1) * DV_DELTA)
                o_ref[bi, rows, hs] = _delta_norm_gate(r[CHUNK:] + oi, nw_ref[...], z_ref[bi, rows, hs]).astype(BF16)
        return carry

    lax.fori_loop(0, nblk, block, 0)

    @pl.when(t == pl.num_programs(1) - 1)
    def _():
        sfin_ref[...] = s_scr[...]


def _delta_scan(prep, z, s0, norm_w):
    u, w, qg, at, kdt, el = prep
    b, _, t, _ = u.shape
    nb = 2 if b % 2 == 0 else 1
    ts = min(512, t)
    assert t % ts == 0
    nblk = ts // PAIR
    per_head = lambda width: pl.BlockSpec((nb, H_DELTA, ts, width), lambda bi, ti: (bi, 0, ti, 0))
    state = pl.BlockSpec((nb, H_DELTA, DK_DELTA, DV_DELTA), lambda bi, ti: (bi, 0, 0, 0))
    tok = pl.BlockSpec((nb, ts, W_DELTA), lambda bi, ti: (bi, ti, 0))
    return pl.pallas_call(
        functools.partial(_delta_scan_kernel, nb=nb, nblk=nblk),
        grid=(b // nb, t // ts),
        in_specs=[per_head(DV_DELTA), per_head(DK_DELTA), per_head(DK_DELTA), per_head(CHUNK),
                  pl.BlockSpec((nb, H_DELTA, nblk, DK_DELTA, PAIR), lambda bi, ti: (bi, 0, ti, 0, 0)),
                  pl.BlockSpec((nb, H_DELTA, nblk, 16, LANES), lambda bi, ti: (bi, 0, ti, 0, 0)),
                  tok, state, pl.BlockSpec((1, DV_DELTA), lambda bi, ti: (0, 0))],
        out_specs=[tok, state],
        out_shape=[jax.ShapeDtypeStruct((b, t, W_DELTA), BF16),
                   jax.ShapeDtypeStruct((b, H_DELTA, DK_DELTA, DV_DELTA), F32)],
        scratch_shapes=[pltpu.VMEM((nb, H_DELTA, DK_DELTA, DV_DELTA), F32)],
        compiler_params=pltpu.CompilerParams(
            dimension_semantics=("arbitrary", "arbitrary"), vmem_limit_bytes=VMEM_LIMIT),
        name="delta_scan",
    )(u, w, qg, at, kdt, el, z, s0, norm_w.reshape(1, DV_DELTA))


def _delta_step_kernel(cin_ref, prev_ref, cw_ref, tail_ref, alog_ref, dtb_ref, z_ref, s0_ref, nw_ref,
                       o_ref, s_ref):
    prev = prev_ref[0]
    conv = prev[0:1] * cw_ref[0:1, :]
    for j in range(1, CONV_W - 1):
        conv = conv + prev[j:j + 1] * cw_ref[j:j + 1, :]
    conv = _silu(conv + cin_ref[0] * cw_ref[CONV_W - 1:CONV_W, :])
    beta_all, g_all = _gate_rows(tail_ref[0], alog_ref[...], dtb_ref[...])
    for h in range(H_DELTA):
        hs = slice(h * DK_DELTA, (h + 1) * DK_DELTA)
        qs = _l2norm(conv[:, hs]) * DK_DELTA ** -0.5
        kn = _l2norm(conv[:, W_DELTA + h * DK_DELTA:W_DELTA + (h + 1) * DK_DELTA])
        vh = conv[:, 2 * W_DELTA + h * DV_DELTA:2 * W_DELTA + (h + 1) * DV_DELTA]
        beta = beta_all[:, h:h + 1]
        eg = jnp.exp(g_all[:, H_DELTA + h:H_DELTA + h + 1])
        s = s0_ref[0, h]
        lhs = jnp.concatenate([kn * (beta * eg), qs * eg, jnp.zeros((6, DK_DELTA), F32)], axis=0).astype(BF16)
        res = jnp.dot(lhs, s.astype(BF16), preferred_element_type=F32)
        v_new = vh * beta - res[0:1]
        qk = jnp.sum(qs.astype(BF16).astype(F32) * kn.astype(BF16).astype(F32), axis=-1, keepdims=True)
        o = res[1:2] + qk * v_new
        k_col = jnp.broadcast_to(kn, (DK_DELTA, DK_DELTA)).T
        s_ref[0, h] = s * eg + k_col * v_new
        o_ref[0, :, hs] = _delta_norm_gate(o, nw_ref[...], z_ref[0, :, hs]).astype(BF16)


def _delta_step(cin, conv_prev, conv_w, tail, alog_row, dtb_row, z, s0, norm_w):
    bs = cin.shape[0]
    tok = lambda width: pl.BlockSpec((1, 1, width), lambda b: (b, 0, 0))
    const = lambda shape: pl.BlockSpec(shape, lambda b: (0,) * len(shape))
    state = pl.BlockSpec((1, H_DELTA, DK_DELTA, DV_DELTA), lambda b: (b, 0, 0, 0))
    return pl.pallas_call(
        _delta_step_kernel,
        grid=(bs,),
        in_specs=[tok(CONV_CH), pl.BlockSpec((1, CONV_W - 1, CONV_CH), lambda b: (b, 0, 0)), const(conv_w.shape),
                  tok(LANES), const(alog_row.shape), const(dtb_row.shape), tok(W_DELTA), state,
                  const((1, DV_DELTA))],
        out_specs=[tok(W_DELTA), state],
        out_shape=[jax.ShapeDtypeStruct((bs, 1, W_DELTA), BF16),
                   jax.ShapeDtypeStruct((bs, H_DELTA, DK_DELTA, DV_DELTA), F32)],
        compiler_params=pltpu.CompilerParams(dimension_semantics=("arbitrary",)),
        name="delta_step",
    )(cin, conv_prev, conv_w, tail, alog_row, dtb_row, z, s0, norm_w.reshape(1, DV_DELTA))


def _out_proj_kernel(oa_ref, ob_ref, x_ref, w_ref, g_ref, b_ref, y_ref, *, alpha):
    mix = jnp.dot(oa_ref[...], w_ref[:W_DIFF, :], preferred_element_type=F32)
    mix = mix + jnp.dot(ob_ref[...], w_ref[W_DIFF:, :], preferred_element_type=F32)
    r = alpha * x_ref[...] + mix
    mu = jnp.mean(r, axis=-1, keepdims=True)
    var = jnp.mean(jnp.square(r - mu), axis=-1, keepdims=True)
    y_ref[...] = (r - mu) * lax.rsqrt(var + LN_EPS) * g_ref[...] + b_ref[...]


def _out_proj(oa, ob, x, w_out_b, ln_g, ln_b, alpha):
    m, d = x.shape
    tm = min(512, m)
    assert m % tm == 0
    row = lambda width: pl.BlockSpec((tm, width), lambda i: (i, 0))
    const = lambda shape: pl.BlockSpec(shape, lambda i: (0, 0))
    return pl.pallas_call(
        functools.partial(_out_proj_kernel, alpha=alpha),
        grid=(m // tm,),
        in_specs=[row(W_DIFF), row(W_DELTA), row(d), const(w_out_b.shape), const((1, d)), const((1, d))],
        out_specs=row(d),
        out_shape=jax.ShapeDtypeStruct((m, d), F32),
        compiler_params=pltpu.CompilerParams(dimension_semantics=("arbitrary",), vmem_limit_bytes=VMEM_LIMIT),
        name="out_proj",
    )(oa, ob, x, w_out_b, ln_g.reshape(1, d), ln_b.reshape(1, d))


def _pad_lanes(vec, offset):
    return jnp.zeros((1, LANES), F32).at[0, offset:offset + vec.shape[0]].set(vec.astype(F32))


def _gate_rows_param(vec):
    col = jnp.broadcast_to(vec.astype(F32)[:, None], (H_DELTA, LANES))
    return jnp.concatenate([jnp.zeros((H_DELTA, LANES), F32), col], axis=0)


def kernel(x_prompt, x_sample, cache_k, cache_v, page_table, state_delta, state_conv, w_in, conv_w, a_log,
           dt_bias, delta_norm_w, diff_lambda, diff_norm_w, w_out, ln_g, ln_b):
    depth = w_in.shape[0]
    bp, tp, d = x_prompt.shape
    bs, ts, _ = x_sample.shape
    assert ts == 1 and w_in.shape[2] == P_MAIN + 2 * H_DELTA and d == w_out.shape[2]
    past_len = page_table.shape[1] * cache_k.shape[2]
    alpha = (2 * depth) ** 0.25
    tables_p = _rope_tables(jnp.arange(tp, dtype=jnp.int32))
    tables_s = _rope_tables(jnp.full((bs,), past_len, jnp.int32))

    hp, hs = x_prompt, x_sample
    outs = [[] for _ in range(8)]
    for l in range(depth):
        lam_init = _lambda_init(l)
        w_main = w_in[l, :, :P_MAIN].astype(BF16)
        w_tail = jnp.zeros((d, LANES), BF16).at[:, :2 * H_DELTA].set(w_in[l, :, P_MAIN:].astype(BF16))
        w_out_b = w_out[l].astype(BF16)
        alog_row = _pad_lanes(a_log[l], H_DELTA)
        dtb_row = _pad_lanes(dt_bias[l], H_DELTA)
        alog_col = _gate_rows_param(a_log[l])
        dtb_col = _gate_rows_param(dt_bias[l])

        sq, sk, _, sv, _, sga, scin, sz, stail = _in_proj(hs.reshape(1, bs, d), w_main, w_tail, tables_s, False)
        tok = lambda a: a.reshape(bs, 1, -1)
        decode_args = dict(q=tok(sq), k_new=tok(sk), v_new=tok(sv), ga=tok(sga), cache_k=cache_k, cache_v=cache_v,
                           layer=l, page_table=page_table, diff_lambda=diff_lambda[l], norm_w=diff_norm_w[l],
                           lam_init=lam_init)
        rider = _DecodeRider(**decode_args, seq0=0, n_seq=bs, grid=(bp, _delta_prep_grid(tp)[1]))
        if not rider.ok:
            rider = None

        q, k, kb, v, vb, ga, cin, z, _, tail_t = _in_proj(hp, w_main, w_tail, tables_p, True)
        oa = _prompt_attn(q, kb, vb, ga, diff_lambda[l], diff_norm_w[l], lam_init)
        prep, soa = _delta_prep(cin, jnp.zeros((bp, CONV_W - 1, CONV_CH), F32), conv_w[l], tail_t, alog_col,
                                dtb_col, rider)
        ob, sp = _delta_scan(prep, z, jnp.zeros((bp, H_DELTA, DK_DELTA, DV_DELTA), F32), delta_norm_w[l])
        hp = _out_proj(oa.reshape(bp * tp, W_DIFF), ob.reshape(bp * tp, W_DELTA), hp.reshape(bp * tp, d),
                       w_out_b, ln_g[l], ln_b[l], alpha).reshape(bp, tp, d)
        outs[0].append(k.reshape(bp, tp, H_DIFF, 2, DH_DIFF))
        outs[1].append(v.reshape(bp, tp, H_DIFF, DV_DIFF))
        outs[2].append(sp)
        outs[3].append(cin[:, tp - (CONV_W - 1):, :])

        if rider is None:
            soa = _decode_attn(decode_args, bs)
        sob, ss = _delta_step(tok(scin), state_conv[l], conv_w[l], tok(stail), alog_row, dtb_row, tok(sz),
                              state_delta[l], delta_norm_w[l])
        hs = _out_proj(soa.reshape(bs, W_DIFF), sob.reshape(bs, W_DELTA), hs.reshape(bs, d),
                       w_out_b, ln_g[l], ln_b[l], alpha).reshape(bs, 1, d)
        outs[4].append(sk.reshape(bs, 1, H_DIFF, 2, DH_DIFF))
        outs[5].append(sv.reshape(bs, 1, H_DIFF, DV_DIFF))
        outs[6].append(ss)
        outs[7].append(jnp.concatenate([state_conv[l][:, 1:, :], tok(scin)], axis=1))
    return (hp, hs) + tuple(jnp.stack(o) for o in outs)
```

```python
import functools
import math

import jax
import jax.numpy as jnp
from jax import lax
from jax.experimental import pallas as pl
from jax.experimental.pallas import tpu as pltpu

F32 = jnp.float32
BF16 = jnp.bfloat16
HIGHEST = lax.Precision.HIGHEST

H_DIFF = 4
DH_DIFF = 64
DV_DIFF = 2 * DH_DIFF
W_DIFF = H_DIFF * DV_DIFF
H_DELTA = 4
DK_DELTA = 128
DV_DELTA = 128
W_DELTA = H_DELTA * DK_DELTA
ROT_DIM = DH_DIFF // 4
ROPE_THETA = 500000.0
CONV_W = 4
CONV_CH = 3 * W_DELTA
CHUNK = 64
P_MAIN = 4 * W_DIFF + 4 * W_DELTA
LN_EPS = 1e-5
HEAD_NORM_EPS = 1e-5
DELTA_NORM_EPS = 1e-6
L2_EPS = 1e-6
NEG_INF = -1e30

LANES = 128
PAIR = 2 * CHUNK
VMEM_LIMIT = 56 * 1024 * 1024
DECODE_PAGES_PER_STEP = 16

_NT = (((1,), (1,)), ((), ()))


def _sigmoid(x):
    return 0.5 * jnp.tanh(0.5 * x) + 0.5


def _silu(x):
    return x * _sigmoid(x)


def _softplus(x):
    return jnp.maximum(x, 0.0) + jnp.log1p(jnp.exp(-jnp.abs(x)))


def _lambda_init(layer):
    return 0.8 - 0.6 * math.exp(-0.3 * layer)


def _rope_tables(pos):
    half = ROT_DIM // 2
    inv = ROPE_THETA ** (-jnp.arange(half, dtype=F32) / half)
    ang = pos.astype(F32)[:, None] * inv[None, :]
    cos, sin = jnp.cos(ang), jnp.sin(ang)
    t = pos.shape[0]
    rest = DH_DIFF - ROT_DIM
    cos_m = jnp.concatenate([cos, cos, jnp.ones((t, rest), F32)], axis=-1)
    sin_lo = jnp.concatenate([-sin, jnp.zeros((t, half + rest), F32)], axis=-1)
    sin_hi = jnp.concatenate([jnp.zeros((t, half), F32), sin, jnp.zeros((t, rest), F32)], axis=-1)
    reps = W_DIFF // DH_DIFF
    return jnp.tile(cos_m, (1, reps)), jnp.tile(sin_lo, (1, reps)), jnp.tile(sin_hi, (1, reps))


def _in_proj_kernel(x_ref, w_ref, wt_ref, cos_ref, slo_ref, shi_ref,
                    q_ref, k_ref, kb_ref, v_ref, vb_ref, ga_ref, cin_ref, z_ref, tail_ref, *tail_t_ref):
    xb = x_ref[0].astype(BF16)
    half = ROT_DIM // 2
    tm = xb.shape[0]

    def proj(c0, width):
        return jnp.dot(xb, w_ref[:, c0:c0 + width], preferred_element_type=F32)

    def rope(h):
        return (h * cos_ref[...]
                + pltpu.roll(h, W_DIFF - half, 1) * slo_ref[...]
                + pltpu.roll(h, half, 1) * shi_ref[...])

    q_ref[0] = rope(proj(0, W_DIFF)).astype(BF16)
    k = rope(proj(W_DIFF, W_DIFF))
    k_ref[0] = k
    kb_ref[0] = k.astype(BF16)
    v = proj(2 * W_DIFF, W_DIFF)
    for h in range(H_DIFF):
        v_ref[0, pl.ds(h, tm, stride=H_DIFF), :] = v[:, h * DV_DIFF:(h + 1) * DV_DIFF]
    vb_ref[0] = v.astype(BF16)
    ga_ref[0] = proj(3 * W_DIFF, W_DIFF)
    for j in range(3):
        cin_ref[0, :, j * W_DELTA:(j + 1) * W_DELTA] = proj(4 * W_DIFF + j * W_DELTA, W_DELTA)
    z_ref[0] = proj(4 * W_DIFF + 3 * W_DELTA, W_DELTA)
    tail = jnp.dot(xb, wt_ref[...], preferred_element_type=F32)
    tail_ref[0] = tail
    if tail_t_ref:
        for c in range(tm // LANES):
            cs = slice(c * LANES, (c + 1) * LANES)
            tail_t_ref[0][0, :, cs] = tail[cs, :].T[:2 * H_DELTA, :]


def _in_proj(x, w_main, w_tail, tables, gates_time_major):
    b, t, d = x.shape
    tm = min(512, t)
    assert t % tm == 0 and (not gates_time_major or tm % LANES == 0)
    cos_t, slo_t, shi_t = tables
    row = lambda width: pl.BlockSpec((1, tm, width), lambda ti, bi: (bi, ti, 0))
    tab = pl.BlockSpec((tm, W_DIFF), lambda ti, bi: (ti, 0))
    const = lambda shape: pl.BlockSpec(shape, lambda ti, bi: (0, 0))
    outs = [(W_DIFF, BF16), (W_DIFF, F32), (W_DIFF, BF16), None, (W_DIFF, BF16),
            (W_DIFF, F32), (CONV_CH, F32), (W_DELTA, F32), (LANES, F32)]
    out_specs = [row(o[0]) if o else pl.BlockSpec((1, tm * H_DIFF, DV_DIFF), lambda ti, bi: (bi, ti, 0))
                 for o in outs]
    out_shape = [jax.ShapeDtypeStruct((b, t, o[0]), o[1]) if o else
                 jax.ShapeDtypeStruct((b, t * H_DIFF, DV_DIFF), F32) for o in outs]
    if gates_time_major:
        out_specs.append(pl.BlockSpec((1, 2 * H_DELTA, tm), lambda ti, bi: (bi, 0, ti)))
        out_shape.append(jax.ShapeDtypeStruct((b, 2 * H_DELTA, t), F32))
    return pl.pallas_call(
        _in_proj_kernel,
        grid=(t // tm, b),
        in_specs=[row(d), const(w_main.shape), const(w_tail.shape), tab, tab, tab],
        out_specs=out_specs,
        out_shape=out_shape,
        compiler_params=pltpu.CompilerParams(
            dimension_semantics=("arbitrary", "arbitrary"), vmem_limit_bytes=VMEM_LIMIT),
        name="in_proj",
    )(x, w_main, w_tail, cos_t, slo_t, shi_t)


def _diff_lambda_value(dl_ref, lam_init):
    dl = dl_ref[...]
    a = jnp.sum(dl[0:1] * dl[1:2], axis=1, keepdims=True)
    b = jnp.sum(dl[2:3] * dl[3:4], axis=1, keepdims=True)
    return jnp.exp(a) - jnp.exp(b) + lam_init


def _head_norm_gate(o, normw, gate, lam_init):
    ms = jnp.mean(o * o, axis=-1, keepdims=True)
    o = o * lax.rsqrt(ms + HEAD_NORM_EPS) * normw * (1.0 - lam_init)
    return o * _silu(gate)


def _prompt_attn_kernel(q_ref, k_ref, v_ref, ga_ref, dl_ref, nw_ref, o_ref, s_scr, m_scr, l_scr, acc_scr,
                        *, blk, lam_init):
    i = pl.program_id(1)
    lane = lax.broadcasted_iota(jnp.int32, (blk, DV_DIFF), 1)
    heads = [slice(h * DV_DIFF, (h + 1) * DV_DIFF) for h in range(H_DIFF)]
    lane_chunks = [slice(c * LANES, (c + 1) * LANES) for c in range(blk // LANES)]
    q_maps = []
    for hs in heads:
        qs = q_ref[0, :, hs].astype(F32) * DH_DIFF ** -0.5
        q_maps.append(jnp.where(lane < DH_DIFF, qs, 0.0).astype(BF16))
        q_maps.append(jnp.where(lane >= DH_DIFF, qs, 0.0).astype(BF16))

    def fold(x, op):
        r = x[:, lane_chunks[0]]
        for c in lane_chunks[1:]:
            r = op(r, x[:, c])
        return r

    m_scr[...] = jnp.full(m_scr.shape, NEG_INF, F32)

    def score_block(j, masked):
        rows = pl.ds(pl.multiple_of(j * blk, blk), blk)
        for h, hs in enumerate(heads):
            k = k_ref[0, rows, hs]
            for mi in (2 * h, 2 * h + 1):
                s = lax.dot_general(q_maps[mi], k, _NT, preferred_element_type=F32)
                if masked:
                    row = lax.broadcasted_iota(jnp.int32, (blk, blk), 0)
                    col = lax.broadcasted_iota(jnp.int32, (blk, blk), 1)
                    s = jnp.where(col <= row, s, NEG_INF)
                s_scr[j, mi] = s
                m_scr[mi] = jnp.maximum(m_scr[mi], fold(s, jnp.maximum))

    def score_loop(j, carry):
        score_block(j, False)
        return carry

    lax.fori_loop(0, i, score_loop, 0)
    score_block(i, True)
    for mi in range(2 * H_DIFF):
        m_scr[mi] = jnp.broadcast_to(jnp.max(m_scr[mi], axis=-1, keepdims=True), (blk, LANES))

    l_scr[...] = jnp.zeros(l_scr.shape, F32)
    acc_scr[...] = jnp.zeros(acc_scr.shape, F32)

    def prob_block(j, carry):
        rows = pl.ds(pl.multiple_of(j * blk, blk), blk)
        for h, hs in enumerate(heads):
            v = v_ref[0, rows, hs]
            for mi in (2 * h, 2 * h + 1):
                m = m_scr[mi]
                p = jnp.concatenate([jnp.exp(s_scr[j, mi, :, c] - m) for c in lane_chunks], axis=1)
                l_scr[mi] = l_scr[mi] + fold(p, jnp.add)
                acc_scr[mi] = acc_scr[mi] + jnp.dot(p.astype(BF16), v, preferred_element_type=F32)
        return carry

    lax.fori_loop(0, i + 1, prob_block, 0)

    lam = _diff_lambda_value(dl_ref, lam_init)
    for h, hs in enumerate(heads):
        l1 = jnp.sum(l_scr[2 * h], axis=-1, keepdims=True)
        l2 = jnp.sum(l_scr[2 * h + 1], axis=-1, keepdims=True)
        o = acc_scr[2 * h] / l1 - lam * (acc_scr[2 * h + 1] / l2)
        o_ref[0, :, hs] = _head_norm_gate(o, nw_ref[...], ga_ref[0, :, hs], lam_init).astype(BF16)


def _prompt_attn(q, kb, vb, ga, diff_lambda, norm_w, lam_init):
    b, t, _ = q.shape
    blk = min(256, t)
    assert t % blk == 0
    n_maps = 2 * H_DIFF
    qspec = pl.BlockSpec((1, blk, W_DIFF), lambda bi, i: (bi, i, 0))
    kvspec = pl.BlockSpec((1, t, W_DIFF), lambda bi, i: (bi, 0, 0))
    return pl.pallas_call(
        functools.partial(_prompt_attn_kernel, blk=blk, lam_init=lam_init),
        grid=(b, t // blk),
        in_specs=[qspec, kvspec, kvspec, qspec,
                  pl.BlockSpec(diff_lambda.shape, lambda bi, i: (0, 0)),
                  pl.BlockSpec((1, DV_DIFF), lambda bi, i: (0, 0))],
        out_specs=qspec,
        out_shape=jax.ShapeDtypeStruct((b, t, W_DIFF), BF16),
        scratch_shapes=[pltpu.VMEM((t // blk, n_maps, blk, blk), F32),
                        pltpu.VMEM((n_maps, blk, LANES), F32),
                        pltpu.VMEM((n_maps, blk, LANES), F32),
                        pltpu.VMEM((n_maps, blk, DV_DIFF), F32)],
        compiler_params=pltpu.CompilerParams(
            dimension_semantics=("arbitrary", "arbitrary"), vmem_limit_bytes=VMEM_LIMIT),
        name="prompt_attn",
    )(q, kb, vb, ga, diff_lambda, norm_w.reshape(1, DV_DIFF))


class _DecodeStep:
    def __init__(self, step, n_steps, spq, first_page, pages, pt_ref, in_refs, o_ref, scratch_refs, lam_init):
        self.step, self.n_steps, self.pages, self.first_page = step, n_steps, pages, first_page
        self.g, self.n_g, self.lam_init, self.pt_ref = lax.rem(step, spq), spq, lam_init, pt_ref
        (self.q_ref, self.kn_ref, self.vn_ref, self.ga_ref, self.dl_ref, self.nw_ref,
         self.ck_ref, self.cv_ref) = in_refs
        self.o_ref = o_ref
        (self.qcol_ref, self.m_ref, self.l_ref, self.acc_ref,
         self.kbuf_ref, self.vbuf_ref, self.sem_ref) = scratch_refs
        self.slot = lax.rem(step, 2)

    def _query(self):
        return self.q_ref[0].astype(F32) * DH_DIFF ** -0.5

    def _page_copies(self, step, slot):
        base = self.first_page + step * self.pages
        copies = []
        for i in range(self.pages):
            pid = self.pt_ref[base + i]
            copies.append(pltpu.make_async_copy(self.ck_ref.at[pid], self.kbuf_ref.at[slot, i],
                                                self.sem_ref.at[slot, 0]))
            copies.append(pltpu.make_async_copy(self.cv_ref.at[pid], self.vbuf_ref.at[slot, i],
                                                self.sem_ref.at[slot, 1]))
        return copies

    def init(self):
        @pl.when(self.step == 0)
        def _():
            for cp in self._page_copies(0, 0):
                cp.start()

        for cp in self._page_copies(self.step, self.slot):
            cp.wait()

        @pl.when(self.step + 1 < self.n_steps)
        def _():
            for cp in self._page_copies(self.step + 1, 1 - self.slot):
                cp.start()

        @pl.when(self.g == 0)
        def _():
            self.m_ref[...] = jnp.full(self.m_ref.shape, NEG_INF, F32)
            self.l_ref[...] = jnp.zeros(self.l_ref.shape, F32)
            self.acc_ref[...] = jnp.zeros(self.acc_ref.shape, F32)
            qs = self._query()
            for c in range(W_DIFF // LANES):
                cs = slice(c * LANES, (c + 1) * LANES)
                self.qcol_ref[cs, :] = jnp.broadcast_to(qs[:, cs], (LANES, LANES)).T

    def main(self):
        k_pages = [self.kbuf_ref.at[self.slot, i] for i in range(self.pages)]
        v_pages = [self.vbuf_ref.at[self.slot, i] for i in range(self.pages)]
        _decode_pages(k_pages, v_pages, self.qcol_ref, self.m_ref, self.l_ref, self.acc_ref)

    def finalize(self):
        @pl.when(self.g == self.n_g - 1)
        def _():
            _decode_finish(self._query(), self.kn_ref, self.vn_ref, self.ga_ref, self.dl_ref, self.nw_ref,
                           self.o_ref, self.m_ref, self.l_ref, self.acc_ref, self.lam_init)


def _decode_pages(kp_refs, vp_refs, qcol_ref, m_ref, l_ref, acc_ref):
    n_maps = 2 * H_DIFF
    page = kp_refs[0].shape[1]
    qcol = qcol_ref[...]
    s = jnp.concatenate(
        [jnp.sum((kp[...] * qcol).reshape(n_maps, DH_DIFF, page), axis=1) for kp in kp_refs], axis=1)
    m_prev = m_ref[:, 0:1]
    m_new = jnp.maximum(m_prev, jnp.max(s, axis=-1, keepdims=True))
    p = jnp.exp(s - m_new)
    alpha = jnp.exp(m_prev - m_new)
    l_new = alpha * l_ref[:, 0:1] + jnp.sum(p, axis=-1, keepdims=True)
    pb = p.astype(BF16)
    for h in range(H_DIFF):
        pv = jnp.zeros((n_maps, DV_DIFF), F32)
        for i, vp in enumerate(vp_refs):
            v_head = vp[pl.ds(h, page, stride=H_DIFF), :].astype(BF16)
            pv = pv + jnp.dot(pb[:, i * page:(i + 1) * page], v_head, preferred_element_type=F32)
        acc_ref[h] = alpha * acc_ref[h] + pv
    m_ref[...] = jnp.broadcast_to(m_new, m_ref.shape)
    l_ref[...] = jnp.broadcast_to(l_new, l_ref.shape)


def _decode_finish(qs, kn_ref, vn_ref, ga_ref, dl_ref, nw_ref, o_ref, m_ref, l_ref, acc_ref, lam_init):
    n_maps = 2 * H_DIFF
    rows = lax.broadcasted_iota(jnp.int32, (n_maps, W_DIFF), 0)
    lanes = lax.broadcasted_iota(jnp.int32, (n_maps, W_DIFF), 1)
    q_blk = jnp.where((lanes >> 6) == rows, jnp.broadcast_to(qs, (n_maps, W_DIFF)), 0.0).astype(BF16)
    k_self = jnp.broadcast_to(kn_ref[0].astype(BF16), (8, W_DIFF))
    s_self = lax.dot_general(q_blk, k_self, _NT, preferred_element_type=F32)[:, 0:1]
    m_past = m_ref[:, 0:1]
    m_f = jnp.maximum(m_past, s_self)
    a_f = jnp.exp(m_past - m_f)
    p_self = jnp.exp(s_self - m_f)
    l_f = a_f * l_ref[:, 0:1] + p_self
    pv_self = p_self.astype(BF16).astype(F32) * vn_ref[0].astype(BF16).astype(F32)
    lam = _diff_lambda_value(dl_ref, lam_init)
    for h in range(H_DIFF):
        hs = slice(h * DV_DIFF, (h + 1) * DV_DIFF)
        a = (a_f * acc_ref[h] + pv_self[:, hs]) / l_f
        o = a[2 * h:2 * h + 1] - lam * a[2 * h + 1:2 * h + 2]
        o_ref[0, :, hs] = _head_norm_gate(o, nw_ref[...], ga_ref[0, :, hs], lam_init).astype(BF16)


class _DecodeRider:
    def __init__(self, q, k_new, v_new, ga, cache_k, cache_v, layer, page_table, diff_lambda, norm_w, lam_init,
                 seq0, n_seq, grid):
        depth, n_phys, page = cache_k.shape[:3]
        n_pages = page_table.shape[1]
        n_steps = grid[0] * grid[1]
        assert page == LANES
        pages = n_seq * n_pages // n_steps
        self.ok = pages >= 1 and pages * n_steps == n_seq * n_pages and n_pages % pages == 0
        if not self.ok:
            return
        spq = n_pages // pages
        self.pages, self.spq, self.lam_init, self.grid = pages, spq, lam_init, grid
        self.first_page = seq0 * n_pages
        ck = jnp.transpose(cache_k, (0, 1, 3, 4, 5, 2)).reshape(depth * n_phys, W_DIFF, page)
        cv = cache_v.reshape(depth * n_phys, page * H_DIFF, DV_DIFF)
        self.page_ids = page_table.reshape(-1) + layer * n_phys
        step = lambda i0, i1: i0 * grid[1] + i1
        tok = pl.BlockSpec((1, 1, W_DIFF), lambda i0, i1, pt: (seq0 + step(i0, i1) // spq, 0, 0))
        const = lambda shape: pl.BlockSpec(shape, lambda i0, i1, pt: (0, 0))
        hbm = pl.BlockSpec(memory_space=pl.ANY)
        self.inputs = [q, k_new, v_new, ga, diff_lambda, norm_w.reshape(1, DV_DIFF), ck, cv]
        self.in_specs = [tok, tok, tok, tok, const(diff_lambda.shape), const((1, DV_DIFF)), hbm, hbm]
        self.out_spec = pl.BlockSpec((1, 1, W_DIFF), lambda i0, i1, pt: (step(i0, i1) // spq, 0, 0))
        self.out_shape = jax.ShapeDtypeStruct((n_seq, 1, W_DIFF), BF16)
        n_maps = 2 * H_DIFF
        self.scratch_shapes = [pltpu.VMEM((W_DIFF, page), F32),
                               pltpu.VMEM((n_maps, LANES), F32), pltpu.VMEM((n_maps, LANES), F32),
                               pltpu.VMEM((H_DIFF, n_maps, DV_DIFF), F32),
                               pltpu.VMEM((2, pages, W_DIFF, page), F32),
                               pltpu.VMEM((2, pages, page * H_DIFF, DV_DIFF), F32),
                               pltpu.SemaphoreType.DMA((2, 2))]

    def step(self, pt_ref, in_refs, out_ref, scratch_refs):
        step = pl.program_id(0) * self.grid[1] + pl.program_id(1)
        return _DecodeStep(step, self.grid[0] * self.grid[1], self.spq, self.first_page, self.pages, pt_ref,
                           in_refs, out_ref, scratch_refs, self.lam_init)


def _ride(host_kernel, n_in, n_out, n_scratch, rider):
    n_rin = len(rider.inputs)

    def kernel(pt_ref, *refs):
        host_in, refs = refs[:n_in], refs[n_in:]
        rider_in, refs = refs[:n_rin], refs[n_rin:]
        host_out, refs = refs[:n_out], refs[n_out:]
        rider_out, refs = refs[0], refs[1:]
        host_scratch, rider_scratch = refs[:n_scratch], refs[n_scratch:]
        host_kernel(*host_in, *host_out, *host_scratch,
                    decode=rider.step(pt_ref, rider_in, rider_out, rider_scratch))

    return kernel


def _hosted_call(host_kernel, grid, in_specs, out_specs, out_shape, scratch_shapes, inputs, name, rider):
    params = pltpu.CompilerParams(dimension_semantics=("arbitrary", "arbitrary"), vmem_limit_bytes=VMEM_LIMIT)
    if rider is None:
        outs = pl.pallas_call(host_kernel, grid=grid, in_specs=in_specs, out_specs=out_specs, out_shape=out_shape,
                              scratch_shapes=scratch_shapes, compiler_params=params, name=name)(*inputs)
        return outs, None
    grid_spec = pltpu.PrefetchScalarGridSpec(
        num_scalar_prefetch=1, grid=grid,
        in_specs=list(in_specs) + rider.in_specs,
        out_specs=list(out_specs) + [rider.out_spec],
        scratch_shapes=list(scratch_shapes) + rider.scratch_shapes)
    outs = pl.pallas_call(
        _ride(host_kernel, len(in_specs), len(out_specs), len(scratch_shapes), rider),
        grid_spec=grid_spec, out_shape=list(out_shape) + [rider.out_shape],
        compiler_params=params, name=name)(rider.page_ids, *inputs, *rider.inputs)
    return outs[:-1], outs[-1]


def _decode_attn_kernel(pt_ref, *refs, rider):
    n_rin = len(rider.inputs)
    decode = rider.step(pt_ref, refs[:n_rin], refs[n_rin], refs[n_rin + 1:])
    decode.init()
    decode.main()
    decode.finalize()


def _decode_attn(rider_args, n_seq):
    n_pages = rider_args["page_table"].shape[1]
    grid = (n_seq, n_pages // math.gcd(DECODE_PAGES_PER_STEP, n_pages))
    rider = _DecodeRider(**rider_args, seq0=0, n_seq=n_seq, grid=grid)
    grid_spec = pltpu.PrefetchScalarGridSpec(
        num_scalar_prefetch=1, grid=grid, in_specs=rider.in_specs,
        out_specs=rider.out_spec, scratch_shapes=rider.scratch_shapes)
    return pl.pallas_call(
        functools.partial(_decode_attn_kernel, rider=rider), grid_spec=grid_spec, out_shape=rider.out_shape,
        compiler_params=pltpu.CompilerParams(
            dimension_semantics=("arbitrary", "arbitrary"), vmem_limit_bytes=VMEM_LIMIT),
        name="decode_attn")(rider.page_ids, *rider.inputs)


def _gate_rows(tail, alog_row, dtb_row):
    beta = _sigmoid(tail)
    g = -jnp.exp(alog_row) * _softplus(tail + dtb_row)
    return beta, g


def _lane_bcast(x, lane, rows):
    return jnp.broadcast_to(x[:, lane:lane + 1], (rows, LANES))


def _l2norm(x):
    return x * lax.rsqrt(jnp.sum(x * x, axis=-1, keepdims=True) + L2_EPS)


def _split_bf16(x):
    hi = x.astype(BF16)
    return hi, (x - hi.astype(F32)).astype(BF16)


def _dot_split(lhs, rhs):
    d = lambda a, b: jnp.dot(a, b, preferred_element_type=F32)
    return d(lhs[0], rhs[0]) + d(lhs[0], rhs[1]) + d(lhs[1], rhs[0])


def _unit_lower_inverses(a_mats):
    n = a_mats[0].shape[0]
    eye = (lax.broadcasted_iota(jnp.int32, (n, n), 0) == lax.broadcasted_iota(jnp.int32, (n, n), 1)).astype(F32)
    xs = [-a for a in a_mats]
    ps = [eye + x for x in xs]
    splits = [_split_bf16(x) for x in xs]
    xs = [_dot_split(s, s) for s in splits]
    power = 2
    while 2 * power < CHUNK:
        both = [_dot_split(_split_bf16(x), _split_bf16(jnp.concatenate([p, x], axis=1))) for p, x in zip(ps, xs)]
        ps = [p + b[:, :n] for p, b in zip(ps, both)]
        xs = [b[:, n:] for b in both]
        power *= 2
    return [p + _dot_split(_split_bf16(x), _split_bf16(p)) for p, x in zip(ps, xs)]


def _delta_prep_kernel(cin_ref, prev_ref, cw_ref, tail_ref, alog_ref, dtb_ref,
                       u_ref, w_ref, qg_ref, at_ref, kdt_ref, el_ref, ext_ref, *, pairs, decode=None):
    t = pl.program_id(1)
    pad = 8
    keep = CONV_W - 1
    rows = pairs * PAIR

    @pl.when(t == 0)
    def _():
        ext_ref[pad - keep:pad, :] = prev_ref[0]

    @pl.when(t > 0)
    def _():
        ext_ref[pad - keep:pad, :] = ext_ref[pad + rows - keep:pad + rows, :]

    if decode is not None:
        decode.init()
    ext_ref[pad:pad + rows, :] = cin_ref[0]

    r = lax.broadcasted_iota(jnp.int32, (PAIR, PAIR), 0)
    c = lax.broadcasted_iota(jnp.int32, (PAIR, PAIR), 1)
    same = (r >> 6) == (c >> 6)
    incl = same & (c <= r)
    strict = same & (c < r)
    incl_t = same & (r <= c)
    chunk_end = r == (c | (CHUNK - 1))

    chains = []
    for pi in range(pairs):
        base = pad - keep + pi * PAIR
        conv = ext_ref[base:base + PAIR, :] * cw_ref[0:1, :]
        for j in range(1, CONV_W):
            conv = conv + ext_ref[base + j:base + j + PAIR, :] * cw_ref[j:j + 1, :]
        conv = _silu(conv)
        rs = slice(pi * PAIR, (pi + 1) * PAIR)
        beta_t, g_t = _gate_rows(tail_ref[0, :, rs], alog_ref[...], dtb_ref[...])
        gc_t = jnp.dot(g_t, incl_t.astype(F32), precision=HIGHEST, preferred_element_type=F32)
        g_last_t = jnp.dot(gc_t, chunk_end.astype(F32), precision=HIGHEST, preferred_element_type=F32)
        for h in range(H_DELTA):
            hs = slice(h * DK_DELTA, (h + 1) * DK_DELTA)
            qs = _l2norm(conv[:, hs]) * DK_DELTA ** -0.5
            kn = _l2norm(conv[:, W_DELTA + h * DK_DELTA:W_DELTA + (h + 1) * DK_DELTA])
            vh = conv[:, 2 * W_DELTA + h * DV_DELTA:2 * W_DELTA + (h + 1) * DV_DELTA]
            beta = jnp.broadcast_to(beta_t[h:h + 1, :], (PAIR, PAIR)).T
            gc_cols = jnp.broadcast_to(gc_t[H_DELTA + h:H_DELTA + h + 1, :], (PAIR, PAIR))
            g_last_cols = jnp.broadcast_to(g_last_t[H_DELTA + h:H_DELTA + h + 1, :], (PAIR, PAIR))
            chains.append(dict(pi=pi, h=h, rs=rs, qs=qs, kn=kn, vh=vh, beta=beta, gc=gc_cols.T, gc_cols=gc_cols,
                               g_last_cols=g_last_cols, kbeta=kn * beta))

    for ch in chains:
        ch["kn_t"] = ch["kn"].T
        kn_tb = ch["kn_t"].astype(BF16)
        ch["kk"] = jnp.dot(ch["kbeta"].astype(BF16), kn_tb, preferred_element_type=F32)
        ch["qk"] = jnp.dot(ch["qs"].astype(BF16), kn_tb, preferred_element_type=F32)
    for ch in chains:
        ch["decay"] = jnp.exp(jnp.where(incl, ch["gc"] - ch["gc_cols"], -jnp.inf))
    t_mats = _unit_lower_inverses([jnp.where(strict, ch["kk"] * ch["decay"], 0.0) for ch in chains])
    for ch, t_mat in zip(chains, t_mats):
        rhs = jnp.concatenate([ch["vh"] * ch["beta"], ch["kbeta"] * jnp.exp(ch["gc"])], axis=1).astype(BF16)
        ch["uw"] = jnp.dot(t_mat.astype(BF16), rhs, preferred_element_type=F32)
    for ch in chains:
        pi, h, rs, gc = ch["pi"], ch["h"], ch["rs"], ch["gc"]
        u_ref[0, h, rs, :] = ch["uw"][:, :DV_DELTA]
        w_ref[0, h, rs, :] = ch["uw"][:, DV_DELTA:].astype(BF16)
        attn = jnp.where(incl, ch["qk"] * ch["decay"], 0.0)
        at_ref[0, h, rs, :] = jnp.concatenate([attn[:CHUNK, :CHUNK], attn[CHUNK:, CHUNK:]], axis=0).astype(BF16)
        qg_ref[0, h, rs, :] = (ch["qs"] * jnp.exp(gc)).astype(BF16)
        kdt_ref[0, h, pi] = (ch["kn_t"] * jnp.exp(ch["g_last_cols"] - ch["gc_cols"])).astype(BF16)
        el_ref[0, h, pi] = jnp.exp(jnp.concatenate([jnp.broadcast_to(gc[CHUNK - 1:CHUNK], (8, LANES)),
                                                    jnp.broadcast_to(gc[PAIR - 1:PAIR], (8, LANES))], axis=0))

    if decode is not None:
        decode.main()
        decode.finalize()


def _delta_prep_grid(t):
    assert t % PAIR == 0
    npair = t // PAIR
    pairs = 2 if npair % 2 == 0 else 1
    return pairs, npair // pairs


def _delta_prep(cin, conv_prev, conv_w, tail, alog_row, dtb_row, rider):
    b, t, _ = cin.shape
    npair = t // PAIR
    pairs, nt = _delta_prep_grid(t)
    rows = pairs * PAIR
    per_head = lambda width: pl.BlockSpec((1, H_DELTA, rows, width), lambda bi, ti, *_: (bi, 0, ti, 0))
    const = lambda shape: pl.BlockSpec(shape, lambda bi, ti, *_: (0,) * len(shape))
    out_shape = [
        jax.ShapeDtypeStruct((b, H_DELTA, t, DV_DELTA), F32),
        jax.ShapeDtypeStruct((b, H_DELTA, t, DK_DELTA), BF16),
        jax.ShapeDtypeStruct((b, H_DELTA, t, DK_DELTA), BF16),
        jax.ShapeDtypeStruct((b, H_DELTA, t, CHUNK), BF16),
        jax.ShapeDtypeStruct((b, H_DELTA, npair, DK_DELTA, PAIR), BF16),
        jax.ShapeDtypeStruct((b, H_DELTA, npair, 16, LANES), F32),
    ]
    out_specs = [per_head(DV_DELTA), per_head(DK_DELTA), per_head(DK_DELTA), per_head(CHUNK),
                 pl.BlockSpec((1, H_DELTA, pairs, DK_DELTA, PAIR), lambda bi, ti, *_: (bi, 0, ti, 0, 0)),
                 pl.BlockSpec((1, H_DELTA, pairs, 16, LANES), lambda bi, ti, *_: (bi, 0, ti, 0, 0))]
    return _hosted_call(
        functools.partial(_delta_prep_kernel, pairs=pairs),
        grid=(b, nt),
        in_specs=[pl.BlockSpec((1, rows, CONV_CH), lambda bi, ti, *_: (bi, ti, 0)),
                  pl.BlockSpec((1, CONV_W - 1, CONV_CH), lambda bi, ti, *_: (bi, 0, 0)),
                  const(conv_w.shape),
                  pl.BlockSpec((1, 2 * H_DELTA, rows), lambda bi, ti, *_: (bi, 0, ti)),
                  const(alog_row.shape), const(dtb_row.shape)],
        out_specs=out_specs,
        out_shape=out_shape,
        scratch_shapes=[pltpu.VMEM((8 + rows, CONV_CH), F32)],
        inputs=(cin, conv_prev, conv_w, tail, alog_row, dtb_row),
        name="delta_prep", rider=rider)


def _delta_norm_gate(o, normw, z):
    ms = jnp.mean(o * o, axis=-1, keepdims=True)
    return o * lax.rsqrt(ms + DELTA_NORM_EPS) * normw * _silu(z)


def _delta_scan_kernel(u_ref, w_ref, qg_ref, at_ref, kdt_ref, el_ref, z_ref, s0_ref, nw_ref,
                       o_ref, sfin_ref, s_scr, *, nb, nblk):
    t = pl.program_id(1)

    @pl.when(t == 0)
    def _():
        s_scr[...] = s0_ref[...]

    def block(jb, carry):
        r0 = pl.multiple_of(jb * PAIR, PAIR)
        seqs = [(bi, h) for bi in range(nb) for h in range(H_DELTA)]
        for ci in range(2):
            rows = pl.ds(pl.multiple_of(r0 + ci * CHUNK, CHUNK), CHUNK)
            states = [s_scr[bi, h] for bi, h in seqs]
            res = [jnp.dot(jnp.concatenate([w_ref[bi, h, rows, :], qg_ref[bi, h, rows, :]], axis=0),
                           s.astype(BF16), preferred_element_type=F32) for (bi, h), s in zip(seqs, states)]
            v_new = [(u_ref[bi, h, rows, :] - r[:CHUNK]).astype(BF16) for (bi, h), r in zip(seqs, res)]
            upd = [jnp.dot(kdt_ref[bi, h, jb, :, ci * CHUNK:(ci + 1) * CHUNK], v, preferred_element_type=F32)
                   for (bi, h), v in zip(seqs, v_new)]
            intra = [jnp.dot(at_ref[bi, h, rows, :], v, preferred_element_type=F32) for (bi, h), v in zip(seqs, v_new)]
            for (bi, h), s, r, du, oi in zip(seqs, states, res, upd, intra):
                decay = jnp.broadcast_to(el_ref[bi, h, jb, ci * 8:ci * 8 + 1, :], (DK_DELTA, DV_DELTA))
                s_scr[bi, h] = s * decay + du
                hs = slice(h * DV_DELTA, (h + 1) * DV_DELTA)
                o_ref[bi, rows, hs] = _delta_norm_gate(r[CHUNK:] + oi, nw_ref[...], z_ref[bi, rows, hs]).astype(BF16)
        return carry

    lax.fori_loop(0, nblk, block, 0)

    @pl.when(t == pl.num_programs(1) - 1)
    def _():
        sfin_ref[...] = s_scr[...]


def _delta_scan(prep, z, s0, norm_w):
    u, w, qg, at, kdt, el = prep
    b, _, t, _ = u.shape
    nb = 2 if b % 2 == 0 else 1
    ts = min(512, t)
    assert t % ts == 0
    nblk = ts // PAIR
    per_head = lambda width: pl.BlockSpec((nb, H_DELTA, ts, width), lambda bi, ti: (bi, 0, ti, 0))
    state = pl.BlockSpec((nb, H_DELTA, DK_DELTA, DV_DELTA), lambda bi, ti: (bi, 0, 0, 0))
    tok = pl.BlockSpec((nb, ts, W_DELTA), lambda bi, ti: (bi, ti, 0))
    return pl.pallas_call(
        functools.partial(_delta_scan_kernel, nb=nb, nblk=nblk),
        grid=(b // nb, t // ts),
        in_specs=[per_head(DV_DELTA), per_head(DK_DELTA), per_head(DK_DELTA), per_head(CHUNK),
                  pl.BlockSpec((nb, H_DELTA, nblk, DK_DELTA, PAIR), lambda bi, ti: (bi, 0, ti, 0, 0)),
                  pl.BlockSpec((nb, H_DELTA, nblk, 16, LANES), lambda bi, ti: (bi, 0, ti, 0, 0)),
                  tok, state, pl.BlockSpec((1, DV_DELTA), lambda bi, ti: (0, 0))],
        out_specs=[tok, state],
        out_shape=[jax.ShapeDtypeStruct((b, t, W_DELTA), BF16),
                   jax.ShapeDtypeStruct((b, H_DELTA, DK_DELTA, DV_DELTA), F32)],
        scratch_shapes=[pltpu.VMEM((nb, H_DELTA, DK_DELTA, DV_DELTA), F32)],
        compiler_params=pltpu.CompilerParams(
            dimension_semantics=("arbitrary", "arbitrary"), vmem_limit_bytes=VMEM_LIMIT),
        name="delta_scan",
    )(u, w, qg, at, kdt, el, z, s0, norm_w.reshape(1, DV_DELTA))


def _delta_step_kernel(cin_ref, prev_ref, cw_ref, tail_ref, alog_ref, dtb_ref, z_ref, s0_ref, nw_ref,
                       o_ref, s_ref):
    prev = prev_ref[0]
    conv = prev[0:1] * cw_ref[0:1, :]
    for j in range(1, CONV_W - 1):
        conv = conv + prev[j:j + 1] * cw_ref[j:j + 1, :]
    conv = _silu(conv + cin_ref[0] * cw_ref[CONV_W - 1:CONV_W, :])
    beta_all, g_all = _gate_rows(tail_ref[0], alog_ref[...], dtb_ref[...])
    for h in range(H_DELTA):
        hs = slice(h * DK_DELTA, (h + 1) * DK_DELTA)
        qs = _l2norm(conv[:, hs]) * DK_DELTA ** -0.5
        kn = _l2norm(conv[:, W_DELTA + h * DK_DELTA:W_DELTA + (h + 1) * DK_DELTA])
        vh = conv[:, 2 * W_DELTA + h * DV_DELTA:2 * W_DELTA + (h + 1) * DV_DELTA]
        beta = beta_all[:, h:h + 1]
        eg = jnp.exp(g_all[:, H_DELTA + h:H_DELTA + h + 1])
        s = s0_ref[0, h]
        lhs = jnp.concatenate([kn * (beta * eg), qs * eg, jnp.zeros((6, DK_DELTA), F32)], axis=0).astype(BF16)
        res = jnp.dot(lhs, s.astype(BF16), preferred_element_type=F32)
        v_new = vh * beta - res[0:1]
        qk = jnp.sum(qs.astype(BF16).astype(F32) * kn.astype(BF16).astype(F32), axis=-1, keepdims=True)
        o = res[1:2] + qk * v_new
        k_col = jnp.broadcast_to(kn, (DK_DELTA, DK_DELTA)).T
        s_ref[0, h] = s * eg + k_col * v_new
        o_ref[0, :, hs] = _delta_norm_gate(o, nw_ref[...], z_ref[0, :, hs]).astype(BF16)


def _delta_step(cin, conv_prev, conv_w, tail, alog_row, dtb_row, z, s0, norm_w):
    bs = cin.shape[0]
    tok = lambda width: pl.BlockSpec((1, 1, width), lambda b: (b, 0, 0))
    const = lambda shape: pl.BlockSpec(shape, lambda b: (0,) * len(shape))
    state = pl.BlockSpec((1, H_DELTA, DK_DELTA, DV_DELTA), lambda b: (b, 0, 0, 0))
    return pl.pallas_call(
        _delta_step_kernel,
        grid=(bs,),
        in_specs=[tok(CONV_CH), pl.BlockSpec((1, CONV_W - 1, CONV_CH), lambda b: (b, 0, 0)), const(conv_w.shape),
                  tok(LANES), const(alog_row.shape), const(dtb_row.shape), tok(W_DELTA), state,
                  const((1, DV_DELTA))],
        out_specs=[tok(W_DELTA), state],
        out_shape=[jax.ShapeDtypeStruct((bs, 1, W_DELTA), BF16),
                   jax.ShapeDtypeStruct((bs, H_DELTA, DK_DELTA, DV_DELTA), F32)],
        compiler_params=pltpu.CompilerParams(dimension_semantics=("arbitrary",)),
        name="delta_step",
    )(cin, conv_prev, conv_w, tail, alog_row, dtb_row, z, s0, norm_w.reshape(1, DV_DELTA))


def _out_proj_kernel(oa_ref, ob_ref, x_ref, w_ref, g_ref, b_ref, y_ref, *, alpha):
    mix = jnp.dot(oa_ref[...], w_ref[:W_DIFF, :], preferred_element_type=F32)
    mix = mix + jnp.dot(ob_ref[...], w_ref[W_DIFF:, :], preferred_element_type=F32)
    r = alpha * x_ref[...] + mix
    mu = jnp.mean(r, axis=-1, keepdims=True)
    var = jnp.mean(jnp.square(r - mu), axis=-1, keepdims=True)
    y_ref[...] = (r - mu) * lax.rsqrt(var + LN_EPS) * g_ref[...] + b_ref[...]


def _out_proj(oa, ob, x, w_out_b, ln_g, ln_b, alpha):
    m, d = x.shape
    tm = min(512, m)
    assert m % tm == 0
    row = lambda width: pl.BlockSpec((tm, width), lambda i: (i, 0))
    const = lambda shape: pl.BlockSpec(shape, lambda i: (0, 0))
    return pl.pallas_call(
        functools.partial(_out_proj_kernel, alpha=alpha),
        grid=(m // tm,),
        in_specs=[row(W_DIFF), row(W_DELTA), row(d), const(w_out_b.shape), const((1, d)), const((1, d))],
        out_specs=row(d),
        out_shape=jax.ShapeDtypeStruct((m, d), F32),
        compiler_params=pltpu.CompilerParams(dimension_semantics=("arbitrary",), vmem_limit_bytes=VMEM_LIMIT),
        name="out_proj",
    )(oa, ob, x, w_out_b, ln_g.reshape(1, d), ln_b.reshape(1, d))


def _pad_lanes(vec, offset):
    return jnp.zeros((1, LANES), F32).at[0, offset:offset + vec.shape[0]].set(vec.astype(F32))


def _gate_rows_param(vec):
    col = jnp.broadcast_to(vec.astype(F32)[:, None], (H_DELTA, LANES))
    return jnp.concatenate([jnp.zeros((H_DELTA, LANES), F32), col], axis=0)


def kernel(x_prompt, x_sample, cache_k, cache_v, page_table, state_delta, state_conv, w_in, conv_w, a_log,
           dt_bias, delta_norm_w, diff_lambda, diff_norm_w, w_out, ln_g, ln_b):
    depth = w_in.shape[0]
    bp, tp, d = x_prompt.shape
    bs, ts, _ = x_sample.shape
    assert ts == 1 and w_in.shape[2] == P_MAIN + 2 * H_DELTA and d == w_out.shape[2]
    past_len = page_table.shape[1] * cache_k.shape[2]
    alpha = (2 * depth) ** 0.25
    tables_p = _rope_tables(jnp.arange(tp, dtype=jnp.int32))
    tables_s = _rope_tables(jnp.full((bs,), past_len, jnp.int32))

    hp, hs = x_prompt, x_sample
    outs = [[] for _ in range(8)]
    for l in range(depth):
        lam_init = _lambda_init(l)
        w_main = w_in[l, :, :P_MAIN].astype(BF16)
        w_tail = jnp.zeros((d, LANES), BF16).at[:, :2 * H_DELTA].set(w_in[l, :, P_MAIN:].astype(BF16))
        w_out_b = w_out[l].astype(BF16)
        alog_row = _pad_lanes(a_log[l], H_DELTA)
        dtb_row = _pad_lanes(dt_bias[l], H_DELTA)
        alog_col = _gate_rows_param(a_log[l])
        dtb_col = _gate_rows_param(dt_bias[l])

        sq, sk, _, sv, _, sga, scin, sz, stail = _in_proj(hs.reshape(1, bs, d), w_main, w_tail, tables_s, False)
        tok = lambda a: a.reshape(bs, 1, -1)
        decode_args = dict(q=tok(sq), k_new=tok(sk), v_new=tok(sv), ga=tok(sga), cache_k=cache_k, cache_v=cache_v,
                           layer=l, page_table=page_table, diff_lambda=diff_lambda[l], norm_w=diff_norm_w[l],
                           lam_init=lam_init)
        rider = _DecodeRider(**decode_args, seq0=0, n_seq=bs, grid=(bp, _delta_prep_grid(tp)[1]))
        if not rider.ok:
            rider = None

        q, k, kb, v, vb, ga, cin, z, _, tail_t = _in_proj(hp, w_main, w_tail, tables_p, True)
        oa = _prompt_attn(q, kb, vb, ga, diff_lambda[l], diff_norm_w[l], lam_init)
        prep, soa = _delta_prep(cin, jnp.zeros((bp, CONV_W - 1, CONV_CH), F32), conv_w[l], tail_t, alog_col,
                                dtb_col, rider)
        ob, sp = _delta_scan(prep, z, jnp.zeros((bp, H_DELTA, DK_DELTA, DV_DELTA), F32), delta_norm_w[l])
        hp = _out_proj(oa.reshape(bp * tp, W_DIFF), ob.reshape(bp * tp, W_DELTA), hp.reshape(bp * tp, d),
                       w_out_b, ln_g[l], ln_b[l], alpha).reshape(bp, tp, d)
        outs[0].append(k.reshape(bp, tp, H_DIFF, 2, DH_DIFF))
        outs[1].append(v.reshape(bp, tp, H_DIFF, DV_DIFF))
        outs[2].append(sp)
        outs[3].append(cin[:, tp - (CONV_W - 1):, :])

        if rider is None:
            soa = _decode_attn(decode_args, bs)
        sob, ss = _delta_step(tok(scin), state_conv[l], conv_w[l], tok(stail), alog_row, dtb_row, tok(sz),
                              state_delta[l], delta_norm_w[l])
        hs = _out_proj(soa.reshape(bs, W_DIFF), sob.reshape(bs, W_DELTA), hs.reshape(bs, d),
                       w_out_b, ln_g[l], ln_b[l], alpha).reshape(bs, 1, d)
        outs[4].append(sk.reshape(bs, 1, H_DIFF, 2, DH_DIFF))
        outs[5].append(sv.reshape(bs, 1, H_DIFF, DV_DIFF))
        outs[6].append(ss)
        outs[7].append(jnp.concatenate([state_conv[l][:, 1:, :], tok(scin)], axis=1))
    return (hp, hs) + tuple(jnp.stack(o) for o in outs)
```

```python
import functools
import math

import jax
import jax.numpy as jnp
from jax import lax
from jax.experimental import pallas as pl
from jax.experimental.pallas import tpu as pltpu

F32 = jnp.float32
BF16 = jnp.bfloat16
HIGHEST = lax.Precision.HIGHEST

H_DIFF = 4
DH_DIFF = 64
DV_DIFF = 2 * DH_DIFF
W_DIFF = H_DIFF * DV_DIFF
H_DELTA = 4
DK_DELTA = 128
DV_DELTA = 128
W_DELTA = H_DELTA * DK_DELTA
ROT_DIM = DH_DIFF // 4
ROPE_THETA = 500000.0
CONV_W = 4
CONV_CH = 3 * W_DELTA
CHUNK = 64
P_MAIN = 4 * W_DIFF + 4 * W_DELTA
LN_EPS = 1e-5
HEAD_NORM_EPS = 1e-5
DELTA_NORM_EPS = 1e-6
L2_EPS = 1e-6
NEG_INF = -1e30

LANES = 128
SUBLANES = 8
PAIR = 2 * CHUNK
VMEM_LIMIT = 56 * 1024 * 1024
DECODE_PAGES_PER_STEP = 16

_NT = (((1,), (1,)), ((), ()))


def _sigmoid(x):
    return 0.5 * jnp.tanh(0.5 * x) + 0.5


def _silu(x):
    return x * _sigmoid(x)


def _softplus(x):
    return jnp.maximum(x, 0.0) + jnp.log1p(jnp.exp(-jnp.abs(x)))


def _lambda_init(layer):
    return 0.8 - 0.6 * math.exp(-0.3 * layer)


def _rope_tables(pos):
    half = ROT_DIM // 2
    inv = ROPE_THETA ** (-jnp.arange(half, dtype=F32) / half)
    ang = pos.astype(F32)[:, None] * inv[None, :]
    cos, sin = jnp.cos(ang), jnp.sin(ang)
    t = pos.shape[0]
    rest = DH_DIFF - ROT_DIM
    cos_m = jnp.concatenate([cos, cos, jnp.ones((t, rest), F32)], axis=-1)
    sin_lo = jnp.concatenate([-sin, jnp.zeros((t, half + rest), F32)], axis=-1)
    sin_hi = jnp.concatenate([jnp.zeros((t, half), F32), sin, jnp.zeros((t, rest), F32)], axis=-1)
    reps = W_DIFF // DH_DIFF
    return jnp.tile(cos_m, (1, reps)), jnp.tile(sin_lo, (1, reps)), jnp.tile(sin_hi, (1, reps))


def _in_proj_kernel(*refs, chunked):
    if chunked:
        (x_ref, w_ref, wt_ref, cos_ref, slo_ref, shi_ref, prev_ref, cw_ref,
         q_ref, k_ref, kb_ref, v_ref, vb_ref, ga_ref, feat_ref, z_ref, tail_t_ref, ctail_ref,
         halo_ref, ext_ref) = refs
    else:
        (x_ref, w_ref, wt_ref, cos_ref, slo_ref, shi_ref,
         q_ref, k_ref, kb_ref, v_ref, vb_ref, ga_ref, cin_ref, z_ref, tail_ref) = refs
    xb = x_ref[0].astype(BF16)
    half = ROT_DIM // 2
    tm = xb.shape[0]
    keep = CONV_W - 1
    ti, bi = pl.program_id(0), pl.program_id(1)

    if chunked:
        @pl.when(ti == 0)
        def _():
            ext_ref[SUBLANES - keep:SUBLANES, :] = prev_ref[0]

        @pl.when(ti > 0)
        def _():
            ext_ref[SUBLANES - keep:SUBLANES, :] = halo_ref[bi, SUBLANES - keep:SUBLANES, :]

    def proj(c0, width):
        return jnp.dot(xb, w_ref[:, c0:c0 + width], preferred_element_type=F32)

    def rope(h):
        return (h * cos_ref[...]
                + pltpu.roll(h, W_DIFF - half, 1) * slo_ref[...]
                + pltpu.roll(h, half, 1) * shi_ref[...])

    delta_in = cin_ref.at[0] if not chunked else ext_ref.at[SUBLANES:SUBLANES + tm]
    for j in range(3):
        delta_in[:, j * W_DELTA:(j + 1) * W_DELTA] = proj(4 * W_DIFF + j * W_DELTA, W_DELTA)
    tail = jnp.dot(xb, wt_ref[...], preferred_element_type=F32)
    q_ref[0] = rope(proj(0, W_DIFF)).astype(BF16)
    k = rope(proj(W_DIFF, W_DIFF))
    k_ref[0] = k
    kb_ref[0] = k.astype(BF16)
    v = proj(2 * W_DIFF, W_DIFF)
    for h in range(H_DIFF):
        v_ref[0, pl.ds(h, tm, stride=H_DIFF), :] = v[:, h * DV_DIFF:(h + 1) * DV_DIFF]
    vb_ref[0] = v.astype(BF16)
    ga_ref[0] = proj(3 * W_DIFF, W_DIFF)
    z_ref[0] = proj(4 * W_DIFF + 3 * W_DELTA, W_DELTA)
    if not chunked:
        tail_ref[0] = tail
        return

    for c in range(tm // LANES):
        cs = slice(c * LANES, (c + 1) * LANES)
        tail_t_ref[0, :, cs] = tail[cs, :].T[:2 * H_DELTA, :]
    for j in range(3):
        js = slice(j * W_DELTA, (j + 1) * W_DELTA)
        conv = ext_ref[SUBLANES - keep:SUBLANES - keep + tm, js] * cw_ref[0:1, js]
        for k_tap in range(1, CONV_W):
            lo = SUBLANES - keep + k_tap
            conv = conv + ext_ref[lo:lo + tm, js] * cw_ref[k_tap:k_tap + 1, js]
        conv = _silu(conv)
        for h in range(H_DELTA):
            hs = slice(h * DK_DELTA, (h + 1) * DK_DELTA)
            col = slice(j * W_DELTA + h * DK_DELTA, j * W_DELTA + (h + 1) * DK_DELTA)
            if j == 0:
                feat_ref[0, :, col] = _l2norm(conv[:, hs]) * DK_DELTA ** -0.5
            elif j == 1:
                feat_ref[0, :, col] = _l2norm(conv[:, hs])
            else:
                feat_ref[0, :, col] = conv[:, hs]
    last = ext_ref[tm:tm + SUBLANES, :]
    halo_ref[bi] = last
    ctail_ref[0, 0] = last


def _in_proj(x, w_main, w_tail, tables, conv_prev=None, conv_w=None):
    chunked = conv_prev is not None
    b, t, d = x.shape
    tm = min(512, t)
    assert t % tm == 0 and (not chunked or tm % LANES == 0)
    cos_t, slo_t, shi_t = tables
    row = lambda width: pl.BlockSpec((1, tm, width), lambda ti, bi: (bi, ti, 0))
    tab = pl.BlockSpec((tm, W_DIFF), lambda ti, bi: (ti, 0))
    const = lambda shape: pl.BlockSpec(shape, lambda ti, bi: (0, 0))
    per_seq = lambda rows: pl.BlockSpec((1, rows, CONV_CH), lambda ti, bi: (bi, 0, 0))
    v_spec = pl.BlockSpec((1, tm * H_DIFF, DV_DIFF), lambda ti, bi: (bi, ti, 0))
    v_shape = jax.ShapeDtypeStruct((b, t * H_DIFF, DV_DIFF), F32)
    arr = lambda width, dtype: jax.ShapeDtypeStruct((b, t, width), dtype)
    in_specs = [row(d), const(w_main.shape), const(w_tail.shape), tab, tab, tab]
    inputs = [x, w_main, w_tail, cos_t, slo_t, shi_t]
    out_specs = [row(W_DIFF), row(W_DIFF), row(W_DIFF), v_spec, row(W_DIFF), row(W_DIFF), row(CONV_CH), row(W_DELTA)]
    out_shape = [arr(W_DIFF, BF16), arr(W_DIFF, F32), arr(W_DIFF, BF16), v_shape, arr(W_DIFF, BF16),
                 arr(W_DIFF, F32), arr(CONV_CH, F32), arr(W_DELTA, F32)]
    scratch = []
    if chunked:
        in_specs += [per_seq(CONV_W - 1), const(conv_w.shape)]
        inputs += [conv_prev, conv_w]
        out_specs += [pl.BlockSpec((1, 2 * H_DELTA, tm), lambda ti, bi: (bi, 0, ti)),
                      pl.BlockSpec((1, 1, SUBLANES, CONV_CH), lambda ti, bi: (ti, bi, 0, 0))]
        out_shape += [jax.ShapeDtypeStruct((b, 2 * H_DELTA, t), F32),
                      jax.ShapeDtypeStruct((t // tm, b, SUBLANES, CONV_CH), F32)]
        scratch = [pltpu.VMEM((b, SUBLANES, CONV_CH), F32), pltpu.VMEM((SUBLANES + tm, CONV_CH), F32)]
    else:
        out_specs.append(row(LANES))
        out_shape.append(arr(LANES, F32))
    return pl.pallas_call(
        functools.partial(_in_proj_kernel, chunked=chunked),
        grid=(t // tm, b),
        in_specs=in_specs,
        out_specs=out_specs,
        out_shape=out_shape,
        scratch_shapes=scratch,
        compiler_params=pltpu.CompilerParams(
            dimension_semantics=("arbitrary", "arbitrary"), vmem_limit_bytes=VMEM_LIMIT),
        name="in_proj",
    )(*inputs)


def _diff_lambda_value(dl_ref, lam_init):
    dl = dl_ref[...]
    a = jnp.sum(dl[0:1] * dl[1:2], axis=1, keepdims=True)
    b = jnp.sum(dl[2:3] * dl[3:4], axis=1, keepdims=True)
    return jnp.exp(a) - jnp.exp(b) + lam_init


def _head_norm_gate(o, normw, gate, lam_init):
    ms = jnp.mean(o * o, axis=-1, keepdims=True)
    o = o * lax.rsqrt(ms + HEAD_NORM_EPS) * normw * (1.0 - lam_init)
    return o * _silu(gate)


def _prompt_attn_kernel(q_ref, k_ref, v_ref, ga_ref, dl_ref, nw_ref, o_ref, s_scr, m_scr, l_scr, acc_scr,
                        *, blk, lam_init):
    i = pl.program_id(1)
    lane = lax.broadcasted_iota(jnp.int32, (blk, DV_DIFF), 1)
    heads = [slice(h * DV_DIFF, (h + 1) * DV_DIFF) for h in range(H_DIFF)]
    lane_chunks = [slice(c * LANES, (c + 1) * LANES) for c in range(blk // LANES)]
    q_maps = []
    for hs in heads:
        qs = q_ref[0, :, hs].astype(F32) * (DH_DIFF ** -0.5 * math.log2(math.e))
        q_maps.append(jnp.where(lane < DH_DIFF, qs, 0.0).astype(BF16))
        q_maps.append(jnp.where(lane >= DH_DIFF, qs, 0.0).astype(BF16))

    def fold(x, op):
        r = x[:, lane_chunks[0]]
        for c in lane_chunks[1:]:
            r = op(r, x[:, c])
        return r

    m_scr[...] = jnp.full(m_scr.shape, NEG_INF, F32)

    def score_block(j, masked):
        rows = pl.ds(pl.multiple_of(j * blk, blk), blk)
        for h, hs in enumerate(heads):
            k = k_ref[0, rows, hs]
            for mi in (2 * h, 2 * h + 1):
                s = lax.dot_general(q_maps[mi], k, _NT, preferred_element_type=F32)
                if masked:
                    row = lax.broadcasted_iota(jnp.int32, (blk, blk), 0)
                    col = lax.broadcasted_iota(jnp.int32, (blk, blk), 1)
                    s = jnp.where(col <= row, s, NEG_INF)
                s_scr[j, mi] = s
                m_scr[mi] = jnp.maximum(m_scr[mi], fold(s, jnp.maximum))

    def score_loop(j, carry):
        score_block(j, False)
        return carry

    lax.fori_loop(0, i, score_loop, 0)
    score_block(i, True)
    for mi in range(2 * H_DIFF):
        m_scr[mi] = jnp.broadcast_to(jnp.max(m_scr[mi], axis=-1, keepdims=True), (blk, LANES))

    l_scr[...] = jnp.zeros(l_scr.shape, F32)
    acc_scr[...] = jnp.zeros(acc_scr.shape, F32)

    def prob_block(j, carry):
        rows = pl.ds(pl.multiple_of(j * blk, blk), blk)
        for h, hs in enumerate(heads):
            v = v_ref[0, rows, hs]
            for mi in (2 * h, 2 * h + 1):
                m = m_scr[mi]
                p = jnp.concatenate([jnp.exp2(s_scr[j, mi, :, c] - m) for c in lane_chunks], axis=1)
                l_scr[mi] = l_scr[mi] + fold(p, jnp.add)
                acc_scr[mi] = acc_scr[mi] + jnp.dot(p.astype(BF16), v, preferred_element_type=F32)
        return carry

    lax.fori_loop(0, i + 1, prob_block, 0)

    lam = _diff_lambda_value(dl_ref, lam_init)
    for h, hs in enumerate(heads):
        l1 = jnp.sum(l_scr[2 * h], axis=-1, keepdims=True)
        l2 = jnp.sum(l_scr[2 * h + 1], axis=-1, keepdims=True)
        o = acc_scr[2 * h] / l1 - lam * (acc_scr[2 * h + 1] / l2)
        o_ref[0, :, hs] = _head_norm_gate(o, nw_ref[...], ga_ref[0, :, hs], lam_init).astype(BF16)


def _prompt_attn(q, kb, vb, ga, diff_lambda, norm_w, lam_init):
    b, t, _ = q.shape
    blk = min(256, t)
    assert t % blk == 0
    n_maps = 2 * H_DIFF
    qspec = pl.BlockSpec((1, blk, W_DIFF), lambda bi, i: (bi, i, 0))
    kvspec = pl.BlockSpec((1, t, W_DIFF), lambda bi, i: (bi, 0, 0))
    return pl.pallas_call(
        functools.partial(_prompt_attn_kernel, blk=blk, lam_init=lam_init),
        grid=(b, t // blk),
        in_specs=[qspec, kvspec, kvspec, qspec,
                  pl.BlockSpec(diff_lambda.shape, lambda bi, i: (0, 0)),
                  pl.BlockSpec((1, DV_DIFF), lambda bi, i: (0, 0))],
        out_specs=qspec,
        out_shape=jax.ShapeDtypeStruct((b, t, W_DIFF), BF16),
        scratch_shapes=[pltpu.VMEM((t // blk, n_maps, blk, blk), F32),
                        pltpu.VMEM((n_maps, blk, LANES), F32),
                        pltpu.VMEM((n_maps, blk, LANES), F32),
                        pltpu.VMEM((n_maps, blk, DV_DIFF), F32)],
        compiler_params=pltpu.CompilerParams(
            dimension_semantics=("arbitrary", "arbitrary"), vmem_limit_bytes=VMEM_LIMIT),
        name="prompt_attn",
    )(q, kb, vb, ga, diff_lambda, norm_w.reshape(1, DV_DIFF))


class _DecodeStep:
    def __init__(self, step, n_steps, spq, first_page, pages, pt_ref, in_refs, o_ref, scratch_refs, lam_init):
        self.step, self.n_steps, self.pages, self.first_page = step, n_steps, pages, first_page
        self.g, self.n_g, self.lam_init, self.pt_ref = lax.rem(step, spq), spq, lam_init, pt_ref
        (self.q_ref, self.kn_ref, self.vn_ref, self.ga_ref, self.dl_ref, self.nw_ref,
         self.ck_ref, self.cv_ref) = in_refs
        self.o_ref = o_ref
        (self.qcol_ref, self.m_ref, self.l_ref, self.acc_ref,
         self.kbuf_ref, self.vbuf_ref, self.sem_ref) = scratch_refs
        self.slot = lax.rem(step, 2)

    def _query(self):
        return self.q_ref[0].astype(F32) * DH_DIFF ** -0.5

    def _page_copies(self, step, slot):
        base = self.first_page + step * self.pages
        copies = []
        for i in range(self.pages):
            pid = self.pt_ref[base + i]
            copies.append(pltpu.make_async_copy(self.ck_ref.at[pid], self.kbuf_ref.at[slot, i],
                                                self.sem_ref.at[slot, 0]))
            copies.append(pltpu.make_async_copy(self.cv_ref.at[pid], self.vbuf_ref.at[slot, i],
                                                self.sem_ref.at[slot, 1]))
        return copies

    def init(self):
        @pl.when(self.step == 0)
        def _():
            for cp in self._page_copies(0, 0):
                cp.start()

        for cp in self._page_copies(self.step, self.slot):
            cp.wait()

        @pl.when(self.step + 1 < self.n_steps)
        def _():
            for cp in self._page_copies(self.step + 1, 1 - self.slot):
                cp.start()

        @pl.when(self.g == 0)
        def _():
            self.m_ref[...] = jnp.full(self.m_ref.shape, NEG_INF, F32)
            self.l_ref[...] = jnp.zeros(self.l_ref.shape, F32)
            self.acc_ref[...] = jnp.zeros(self.acc_ref.shape, F32)
            qs = self._query()
            for c in range(W_DIFF // LANES):
                cs = slice(c * LANES, (c + 1) * LANES)
                self.qcol_ref[cs, :] = jnp.broadcast_to(qs[:, cs], (LANES, LANES)).T

    def main(self):
        k_pages = [self.kbuf_ref.at[self.slot, i] for i in range(self.pages)]
        v_pages = [self.vbuf_ref.at[self.slot, i] for i in range(self.pages)]
        _decode_pages(k_pages, v_pages, self.qcol_ref, self.m_ref, self.l_ref, self.acc_ref)

    def finalize(self):
        @pl.when(self.g == self.n_g - 1)
        def _():
            _decode_finish(self._query(), self.kn_ref, self.vn_ref, self.ga_ref, self.dl_ref, self.nw_ref,
                           self.o_ref, self.m_ref, self.l_ref, self.acc_ref, self.lam_init)


def _decode_pages(kp_refs, vp_refs, qcol_ref, m_ref, l_ref, acc_ref):
    n_maps = 2 * H_DIFF
    page = kp_refs[0].shape[1]
    qcol = qcol_ref[...]
    s = jnp.concatenate(
        [jnp.sum((kp[...] * qcol).reshape(n_maps, DH_DIFF, page), axis=1) for kp in kp_refs], axis=1)
    m_prev = m_ref[:, 0:1]
    m_new = jnp.maximum(m_prev, jnp.max(s, axis=-1, keepdims=True))
    p = jnp.exp(s - m_new)
    alpha = jnp.exp(m_prev - m_new)
    l_new = alpha * l_ref[:, 0:1] + jnp.sum(p, axis=-1, keepdims=True)
    pb = p.astype(BF16)
    for h in range(H_DIFF):
        pv = jnp.zeros((n_maps, DV_DIFF), F32)
        for i, vp in enumerate(vp_refs):
            v_head = vp[pl.ds(h, page, stride=H_DIFF), :].astype(BF16)
            pv = pv + jnp.dot(pb[:, i * page:(i + 1) * page], v_head, preferred_element_type=F32)
        acc_ref[h] = alpha * acc_ref[h] + pv
    m_ref[...] = jnp.broadcast_to(m_new, m_ref.shape)
    l_ref[...] = jnp.broadcast_to(l_new, l_ref.shape)


def _decode_finish(qs, kn_ref, vn_ref, ga_ref, dl_ref, nw_ref, o_ref, m_ref, l_ref, acc_ref, lam_init):
    n_maps = 2 * H_DIFF
    rows = lax.broadcasted_iota(jnp.int32, (n_maps, W_DIFF), 0)
    lanes = lax.broadcasted_iota(jnp.int32, (n_maps, W_DIFF), 1)
    q_blk = jnp.where((lanes >> 6) == rows, jnp.broadcast_to(qs, (n_maps, W_DIFF)), 0.0).astype(BF16)
    k_self = jnp.broadcast_to(kn_ref[0].astype(BF16), (8, W_DIFF))
    s_self = lax.dot_general(q_blk, k_self, _NT, preferred_element_type=F32)[:, 0:1]
    m_past = m_ref[:, 0:1]
    m_f = jnp.maximum(m_past, s_self)
    a_f = jnp.exp(m_past - m_f)
    p_self = jnp.exp(s_self - m_f)
    l_f = a_f * l_ref[:, 0:1] + p_self
    pv_self = p_self.astype(BF16).astype(F32) * vn_ref[0].astype(BF16).astype(F32)
    lam = _diff_lambda_value(dl_ref, lam_init)
    for h in range(H_DIFF):
        hs = slice(h * DV_DIFF, (h + 1) * DV_DIFF)
        a = (a_f * acc_ref[h] + pv_self[:, hs]) / l_f
        o = a[2 * h:2 * h + 1] - lam * a[2 * h + 1:2 * h + 2]
        o_ref[0, :, hs] = _head_norm_gate(o, nw_ref[...], ga_ref[0, :, hs], lam_init).astype(BF16)


class _DecodeRider:
    def __init__(self, q, k_new, v_new, ga, cache_k, cache_v, layer, page_table, diff_lambda, norm_w, lam_init,
                 seq0, n_seq, grid):
        depth, n_phys, page = cache_k.shape[:3]
        n_pages = page_table.shape[1]
        n_steps = grid[0] * grid[1]
        assert page == LANES
        pages = n_seq * n_pages // n_steps
        self.ok = pages >= 1 and pages * n_steps == n_seq * n_pages and n_pages % pages == 0
        if not self.ok:
            return
        spq = n_pages // pages
        self.pages, self.spq, self.lam_init, self.grid = pages, spq, lam_init, grid
        self.first_page = seq0 * n_pages
        ck = jnp.transpose(cache_k, (0, 1, 3, 4, 5, 2)).reshape(depth * n_phys, W_DIFF, page)
        cv = cache_v.reshape(depth * n_phys, page * H_DIFF, DV_DIFF)
        self.page_ids = page_table.reshape(-1) + layer * n_phys
        step = lambda i0, i1: i0 * grid[1] + i1
        tok = pl.BlockSpec((1, 1, W_DIFF), lambda i0, i1, pt: (seq0 + step(i0, i1) // spq, 0, 0))
        const = lambda shape: pl.BlockSpec(shape, lambda i0, i1, pt: (0, 0))
        hbm = pl.BlockSpec(memory_space=pl.ANY)
        self.inputs = [q, k_new, v_new, ga, diff_lambda, norm_w.reshape(1, DV_DIFF), ck, cv]
        self.in_specs = [tok, tok, tok, tok, const(diff_lambda.shape), const((1, DV_DIFF)), hbm, hbm]
        self.out_spec = pl.BlockSpec((1, 1, W_DIFF), lambda i0, i1, pt: (step(i0, i1) // spq, 0, 0))
        self.out_shape = jax.ShapeDtypeStruct((n_seq, 1, W_DIFF), BF16)
        n_maps = 2 * H_DIFF
        self.scratch_shapes = [pltpu.VMEM((W_DIFF, page), F32),
                               pltpu.VMEM((n_maps, LANES), F32), pltpu.VMEM((n_maps, LANES), F32),
                               pltpu.VMEM((H_DIFF, n_maps, DV_DIFF), F32),
                               pltpu.VMEM((2, pages, W_DIFF, page), F32),
                               pltpu.VMEM((2, pages, page * H_DIFF, DV_DIFF), F32),
                               pltpu.SemaphoreType.DMA((2, 2))]

    def step(self, pt_ref, in_refs, out_ref, scratch_refs):
        step = pl.program_id(0) * self.grid[1] + pl.program_id(1)
        return _DecodeStep(step, self.grid[0] * self.grid[1], self.spq, self.first_page, self.pages, pt_ref,
                           in_refs, out_ref, scratch_refs, self.lam_init)


def _ride(host_kernel, n_in, n_out, n_scratch, rider):
    n_rin = len(rider.inputs)

    def kernel(pt_ref, *refs):
        host_in, refs = refs[:n_in], refs[n_in:]
        rider_in, refs = refs[:n_rin], refs[n_rin:]
        host_out, refs = refs[:n_out], refs[n_out:]
        rider_out, refs = refs[0], refs[1:]
        host_scratch, rider_scratch = refs[:n_scratch], refs[n_scratch:]
        host_kernel(*host_in, *host_out, *host_scratch,
                    decode=rider.step(pt_ref, rider_in, rider_out, rider_scratch))

    return kernel


def _hosted_call(host_kernel, grid, in_specs, out_specs, out_shape, scratch_shapes, inputs, name, rider):
    params = pltpu.CompilerParams(dimension_semantics=("arbitrary", "arbitrary"), vmem_limit_bytes=VMEM_LIMIT)
    if rider is None:
        outs = pl.pallas_call(host_kernel, grid=grid, in_specs=in_specs, out_specs=out_specs, out_shape=out_shape,
                              scratch_shapes=scratch_shapes, compiler_params=params, name=name)(*inputs)
        return outs, None
    grid_spec = pltpu.PrefetchScalarGridSpec(
        num_scalar_prefetch=1, grid=grid,
        in_specs=list(in_specs) + rider.in_specs,
        out_specs=list(out_specs) + [rider.out_spec],
        scratch_shapes=list(scratch_shapes) + rider.scratch_shapes)
    outs = pl.pallas_call(
        _ride(host_kernel, len(in_specs), len(out_specs), len(scratch_shapes), rider),
        grid_spec=grid_spec, out_shape=list(out_shape) + [rider.out_shape],
        compiler_params=params, name=name)(rider.page_ids, *inputs, *rider.inputs)
    return outs[:-1], outs[-1]


def _decode_attn_kernel(pt_ref, *refs, rider):
    n_rin = len(rider.inputs)
    decode = rider.step(pt_ref, refs[:n_rin], refs[n_rin], refs[n_rin + 1:])
    decode.init()
    decode.main()
    decode.finalize()


def _decode_attn(rider_args, n_seq):
    n_pages = rider_args["page_table"].shape[1]
    grid = (n_seq, n_pages // math.gcd(DECODE_PAGES_PER_STEP, n_pages))
    rider = _DecodeRider(**rider_args, seq0=0, n_seq=n_seq, grid=grid)
    grid_spec = pltpu.PrefetchScalarGridSpec(
        num_scalar_prefetch=1, grid=grid, in_specs=rider.in_specs,
        out_specs=rider.out_spec, scratch_shapes=rider.scratch_shapes)
    return pl.pallas_call(
        functools.partial(_decode_attn_kernel, rider=rider), grid_spec=grid_spec, out_shape=rider.out_shape,
        compiler_params=pltpu.CompilerParams(
            dimension_semantics=("arbitrary", "arbitrary"), vmem_limit_bytes=VMEM_LIMIT),
        name="decode_attn")(rider.page_ids, *rider.inputs)


def _gate_rows(tail, alog_row, dtb_row):
    beta = _sigmoid(tail)
    g = -jnp.exp(alog_row) * _softplus(tail + dtb_row)
    return beta, g


def _lane_bcast(x, lane, rows):
    return jnp.broadcast_to(x[:, lane:lane + 1], (rows, LANES))


def _l2norm(x):
    return x * lax.rsqrt(jnp.sum(x * x, axis=-1, keepdims=True) + L2_EPS)


def _split_bf16(x):
    hi = x.astype(BF16)
    return hi, (x - hi.astype(F32)).astype(BF16)


def _dot_split(lhs, rhs):
    d = lambda a, b: jnp.dot(a, b, preferred_element_type=F32)
    return d(lhs[0], rhs[0]) + d(lhs[0], rhs[1]) + d(lhs[1], rhs[0])


def _unit_lower_inverses(a_mats):
    n = a_mats[0].shape[0]
    eye = (lax.broadcasted_iota(jnp.int32, (n, n), 0) == lax.broadcasted_iota(jnp.int32, (n, n), 1)).astype(F32)
    xs = [-a for a in a_mats]
    ps = [eye + x for x in xs]
    splits = [_split_bf16(x) for x in xs]
    xs = [_dot_split(s, s) for s in splits]
    power = 2
    while 2 * power < CHUNK:
        both = [_dot_split(_split_bf16(x), _split_bf16(jnp.concatenate([p, x], axis=1))) for p, x in zip(ps, xs)]
        ps = [p + b[:, :n] for p, b in zip(ps, both)]
        xs = [b[:, n:] for b in both]
        power *= 2
    return [p + _dot_split(_split_bf16(x), _split_bf16(p)) for p, x in zip(ps, xs)]


def _delta_prep_kernel(feat_ref, tail_ref, alog_ref, dtb_ref,
                       u_ref, w_ref, qg_ref, at_ref, kdt_ref, el_ref, *, pairs, decode=None):
    if decode is not None:
        decode.init()

    r = lax.broadcasted_iota(jnp.int32, (PAIR, PAIR), 0)
    c = lax.broadcasted_iota(jnp.int32, (PAIR, PAIR), 1)
    same = (r >> 6) == (c >> 6)
    incl = same & (c <= r)
    strict = same & (c < r)
    incl_t = same & (r <= c)
    chunk_end = r == (c | (CHUNK - 1))

    chains = []
    for pi in range(pairs):
        rs = slice(pi * PAIR, (pi + 1) * PAIR)
        beta_t, g_t = _gate_rows(tail_ref[0, :, rs], alog_ref[...], dtb_ref[...])
        gc_t = jnp.dot(g_t, incl_t.astype(F32), precision=HIGHEST, preferred_element_type=F32)
        g_last_t = jnp.dot(gc_t, chunk_end.astype(F32), precision=HIGHEST, preferred_element_type=F32)
        for h in range(H_DELTA):
            qs = feat_ref[0, rs, h * DK_DELTA:(h + 1) * DK_DELTA]
            kn = feat_ref[0, rs, W_DELTA + h * DK_DELTA:W_DELTA + (h + 1) * DK_DELTA]
            vh = feat_ref[0, rs, 2 * W_DELTA + h * DV_DELTA:2 * W_DELTA + (h + 1) * DV_DELTA]
            beta = jnp.broadcast_to(beta_t[h:h + 1, :], (PAIR, PAIR)).T
            gc_cols = jnp.broadcast_to(gc_t[H_DELTA + h:H_DELTA + h + 1, :], (PAIR, PAIR))
            g_last_cols = jnp.broadcast_to(g_last_t[H_DELTA + h:H_DELTA + h + 1, :], (PAIR, PAIR))
            chains.append(dict(pi=pi, h=h, rs=rs, qs=qs, kn=kn, vh=vh, beta=beta, gc=gc_cols.T, gc_cols=gc_cols,
                               g_last_cols=g_last_cols, kbeta=kn * beta))

    for ch in chains:
        ch["kn_t"] = ch["kn"].T
        kn_tb = ch["kn_t"].astype(BF16)
        ch["kk"] = jnp.dot(ch["kbeta"].astype(BF16), kn_tb, preferred_element_type=F32)
        ch["qk"] = jnp.dot(ch["qs"].astype(BF16), kn_tb, preferred_element_type=F32)
    for ch in chains:
        ch["decay"] = jnp.exp(jnp.where(incl, ch["gc"] - ch["gc_cols"], -jnp.inf))
    t_mats = _unit_lower_inverses([jnp.where(strict, ch["kk"] * ch["decay"], 0.0) for ch in chains])
    for ch, t_mat in zip(chains, t_mats):
        rhs = jnp.concatenate([ch["vh"] * ch["beta"], ch["kbeta"] * jnp.exp(ch["gc"])], axis=1).astype(BF16)
        ch["uw"] = jnp.dot(t_mat.astype(BF16), rhs, preferred_element_type=F32)
    for ch in chains:
        pi, h, rs, gc = ch["pi"], ch["h"], ch["rs"], ch["gc"]
        u_ref[0, h, rs, :] = ch["uw"][:, :DV_DELTA]
        w_ref[0, h, rs, :] = ch["uw"][:, DV_DELTA:].astype(BF16)
        attn = jnp.where(incl, ch["qk"] * ch["decay"], 0.0)
        at_ref[0, h, rs, :] = jnp.concatenate([attn[:CHUNK, :CHUNK], attn[CHUNK:, CHUNK:]], axis=0).astype(BF16)
        qg_ref[0, h, rs, :] = (ch["qs"] * jnp.exp(gc)).astype(BF16)
        kdt_ref[0, h, pi] = (ch["kn_t"] * jnp.exp(ch["g_last_cols"] - ch["gc_cols"])).astype(BF16)
        el_ref[0, h, pi] = jnp.exp(jnp.concatenate([jnp.broadcast_to(gc[CHUNK - 1:CHUNK], (8, LANES)),
                                                    jnp.broadcast_to(gc[PAIR - 1:PAIR], (8, LANES))], axis=0))

    if decode is not None:
        decode.main()
        decode.finalize()


def _delta_prep_grid(t):
    assert t % PAIR == 0
    npair = t // PAIR
    pairs = 2 if npair % 2 == 0 else 1
    return pairs, npair // pairs


def _delta_prep(feat, tail, alog_row, dtb_row, rider):
    b, t, _ = feat.shape
    npair = t // PAIR
    pairs, nt = _delta_prep_grid(t)
    rows = pairs * PAIR
    per_head = lambda width: pl.BlockSpec((1, H_DELTA, rows, width), lambda bi, ti, *_: (bi, 0, ti, 0))
    const = lambda shape: pl.BlockSpec(shape, lambda bi, ti, *_: (0,) * len(shape))
    out_shape = [
        jax.ShapeDtypeStruct((b, H_DELTA, t, DV_DELTA), F32),
        jax.ShapeDtypeStruct((b, H_DELTA, t, DK_DELTA), BF16),
        jax.ShapeDtypeStruct((b, H_DELTA, t, DK_DELTA), BF16),
        jax.ShapeDtypeStruct((b, H_DELTA, t, CHUNK), BF16),
        jax.ShapeDtypeStruct((b, H_DELTA, npair, DK_DELTA, PAIR), BF16),
        jax.ShapeDtypeStruct((b, H_DELTA, npair, 16, LANES), F32),
    ]
    out_specs = [per_head(DV_DELTA), per_head(DK_DELTA), per_head(DK_DELTA), per_head(CHUNK),
                 pl.BlockSpec((1, H_DELTA, pairs, DK_DELTA, PAIR), lambda bi, ti, *_: (bi, 0, ti, 0, 0)),
                 pl.BlockSpec((1, H_DELTA, pairs, 16, LANES), lambda bi, ti, *_: (bi, 0, ti, 0, 0))]
    return _hosted_call(
        functools.partial(_delta_prep_kernel, pairs=pairs),
        grid=(b, nt),
        in_specs=[pl.BlockSpec((1, rows, CONV_CH), lambda bi, ti, *_: (bi, ti, 0)),
                  pl.BlockSpec((1, 2 * H_DELTA, rows), lambda bi, ti, *_: (bi, 0, ti)),
                  const(alog_row.shape), const(dtb_row.shape)],
        out_specs=out_specs,
        out_shape=out_shape,
        scratch_shapes=[],
        inputs=(feat, tail, alog_row, dtb_row),
        name="delta_prep", rider=rider)


def _delta_norm_gate(o, normw, z):
    ms = jnp.mean(o * o, axis=-1, keepdims=True)
    return o * lax.rsqrt(ms + DELTA_NORM_EPS) * normw * _silu(z)


def _delta_scan_kernel(u_ref, w_ref, qg_ref, at_ref, kdt_ref, el_ref, z_ref, s0_ref, nw_ref,
                       o_ref, sfin_ref, s_scr, *, nb, nblk):
    t = pl.program_id(1)

    @pl.when(t == 0)
    def _():
        s_scr[...] = s0_ref[...]

    def block(jb, carry):
        r0 = pl.multiple_of(jb * PAIR, PAIR)
        seqs = [(bi, h) for bi in range(nb) for h in range(H_DELTA)]
        for ci in range(2):
            rows = pl.ds(pl.multiple_of(r0 + ci * CHUNK, CHUNK), CHUNK)
            states = [s_scr[bi, h] for bi, h in seqs]
            res = [jnp.dot(jnp.concatenate([w_ref[bi, h, rows, :], qg_ref[bi, h, rows, :]], axis=0),
                           s.astype(BF16), preferred_element_type=F32) for (bi, h), s in zip(seqs, states)]
            v_new = [(u_ref[bi, h, rows, :] - r[:CHUNK]).astype(BF16) for (bi, h), r in zip(seqs, res)]
            upd = [jnp.dot(kdt_ref[bi, h, jb, :, ci * CHUNK:(ci + 1) * CHUNK], v, preferred_element_type=F32)
                   for (bi, h), v in zip(seqs, v_new)]
            intra = [jnp.dot(at_ref[bi, h, rows, :], v, preferred_element_type=F32) for (bi, h), v in zip(seqs, v_new)]
            for (bi, h), s, r, du, oi in zip(seqs, states, res, upd, intra):
                decay = jnp.broadcast_to(el_ref[bi, h, jb, ci * 8:ci * 8 + 1, :], (DK_DELTA, DV_DELTA))
                s_scr[bi, h] = s * decay + du
                hs = slice(h * DV_DELTA, (h + 1) * DV_DELTA)
                o_ref[bi, rows, hs] = _delta_norm_gate(r[CHUNK:] + oi, nw_ref[...], z_ref[bi, rows, hs]).astype(BF16)
        return carry

    lax.fori_loop(0, nblk, block, 0)

    @pl.when(t == pl.num_programs(1) - 1)
    def _():
        sfin_ref[...] = s_scr[...]


def _delta_scan(prep, z, s0, norm_w):
    u, w, qg, at, kdt, el = prep
    b, _, t, _ = u.shape
    nb = 2 if b % 2 == 0 else 1
    ts = min(512, t)
    assert t % ts == 0
    nblk = ts // PAIR
    per_head = lambda width: pl.BlockSpec((nb, H_DELTA, ts, width), lambda bi, ti: (bi, 0, ti, 0))
    state = pl.BlockSpec((nb, H_DELTA, DK_DELTA, DV_DELTA), lambda bi, ti: (bi, 0, 0, 0))
    tok = pl.BlockSpec((nb, ts, W_DELTA), lambda bi, ti: (bi, ti, 0))
    return pl.pallas_call(
        functools.partial(_delta_scan_kernel, nb=nb, nblk=nblk),
        grid=(b // nb, t // ts),
        in_specs=[per_head(DV_DELTA), per_head(DK_DELTA), per_head(DK_DELTA), per_head(CHUNK),
                  pl.BlockSpec((nb, H_DELTA, nblk, DK_DELTA, PAIR), lambda bi, ti: (bi, 0, ti, 0, 0)),
                  pl.BlockSpec((nb, H_DELTA, nblk, 16, LANES), lambda bi, ti: (bi, 0, ti, 0, 0)),
                  tok, state, pl.BlockSpec((1, DV_DELTA), lambda bi, ti: (0, 0))],
        out_specs=[tok, state],
        out_shape=[jax.ShapeDtypeStruct((b, t, W_DELTA), BF16),
                   jax.ShapeDtypeStruct((b, H_DELTA, DK_DELTA, DV_DELTA), F32)],
        scratch_shapes=[pltpu.VMEM((nb, H_DELTA, DK_DELTA, DV_DELTA), F32)],
        compiler_params=pltpu.CompilerParams(
            dimension_semantics=("arbitrary", "arbitrary"), vmem_limit_bytes=VMEM_LIMIT),
        name="delta_scan",
    )(u, w, qg, at, kdt, el, z, s0, norm_w.reshape(1, DV_DELTA))


def _delta_step_kernel(cin_ref, prev_ref, cw_ref, tail_ref, alog_ref, dtb_ref, z_ref, s0_ref, nw_ref,
                       o_ref, s_ref):
    prev = prev_ref[0]
    conv = prev[0:1] * cw_ref[0:1, :]
    for j in range(1, CONV_W - 1):
        conv = conv + prev[j:j + 1] * cw_ref[j:j + 1, :]
    conv = _silu(conv + cin_ref[0] * cw_ref[CONV_W - 1:CONV_W, :])
    beta_all, g_all = _gate_rows(tail_ref[0], alog_ref[...], dtb_ref[...])
    for h in range(H_DELTA):
        hs = slice(h * DK_DELTA, (h + 1) * DK_DELTA)
        qs = _l2norm(conv[:, hs]) * DK_DELTA ** -0.5
        kn = _l2norm(conv[:, W_DELTA + h * DK_DELTA:W_DELTA + (h + 1) * DK_DELTA])
        vh = conv[:, 2 * W_DELTA + h * DV_DELTA:2 * W_DELTA + (h + 1) * DV_DELTA]
        beta = beta_all[:, h:h + 1]
        eg = jnp.exp(g_all[:, H_DELTA + h:H_DELTA + h + 1])
        s = s0_ref[0, h]
        lhs = jnp.concatenate([kn * (beta * eg), qs * eg, jnp.zeros((6, DK_DELTA), F32)], axis=0).astype(BF16)
        res = jnp.dot(lhs, s.astype(BF16), preferred_element_type=F32)
        v_new = vh * beta - res[0:1]
        qk = jnp.sum(qs.astype(BF16).astype(F32) * kn.astype(BF16).astype(F32), axis=-1, keepdims=True)
        o = res[1:2] + qk * v_new
        k_col = jnp.broadcast_to(kn, (DK_DELTA, DK_DELTA)).T
        s_ref[0, h] = s * eg + k_col * v_new
        o_ref[0, :, hs] = _delta_norm_gate(o, nw_ref[...], z_ref[0, :, hs]).astype(BF16)


def _delta_step(cin, conv_prev, conv_w, tail, alog_row, dtb_row, z, s0, norm_w):
    bs = cin.shape[0]
    tok = lambda width: pl.BlockSpec((1, 1, width), lambda b: (b, 0, 0))
    const = lambda shape: pl.BlockSpec(shape, lambda b: (0,) * len(shape))
    state = pl.BlockSpec((1, H_DELTA, DK_DELTA, DV_DELTA), lambda b: (b, 0, 0, 0))
    return pl.pallas_call(
        _delta_step_kernel,
        grid=(bs,),
        in_specs=[tok(CONV_CH), pl.BlockSpec((1, CONV_W - 1, CONV_CH), lambda b: (b, 0, 0)), const(conv_w.shape),
                  tok(LANES), const(alog_row.shape), const(dtb_row.shape), tok(W_DELTA), state,
                  const((1, DV_DELTA))],
        out_specs=[tok(W_DELTA), state],
        out_shape=[jax.ShapeDtypeStruct((bs, 1, W_DELTA), BF16),
                   jax.ShapeDtypeStruct((bs, H_DELTA, DK_DELTA, DV_DELTA), F32)],
        compiler_params=pltpu.CompilerParams(dimension_semantics=("arbitrary",)),
        name="delta_step",
    )(cin, conv_prev, conv_w, tail, alog_row, dtb_row, z, s0, norm_w.reshape(1, DV_DELTA))


def _out_proj_kernel(oa_ref, ob_ref, x_ref, w_ref, g_ref, b_ref, y_ref, *, alpha):
    mix = jnp.dot(oa_ref[...], w_ref[:W_DIFF, :], preferred_element_type=F32)
    mix = mix + jnp.dot(ob_ref[...], w_ref[W_DIFF:, :], preferred_element_type=F32)
    r = alpha * x_ref[...] + mix
    mu = jnp.mean(r, axis=-1, keepdims=True)
    var = jnp.mean(jnp.square(r - mu), axis=-1, keepdims=True)
    y_ref[...] = (r - mu) * lax.rsqrt(var + LN_EPS) * g_ref[...] + b_ref[...]


def _out_proj(oa, ob, x, w_out_b, ln_g, ln_b, alpha):
    m, d = x.shape
    tm = min(512, m)
    assert m % tm == 0
    row = lambda width: pl.BlockSpec((tm, width), lambda i: (i, 0))
    const = lambda shape: pl.BlockSpec(shape, lambda i: (0, 0))
    return pl.pallas_call(
        functools.partial(_out_proj_kernel, alpha=alpha),
        grid=(m // tm,),
        in_specs=[row(W_DIFF), row(W_DELTA), row(d), const(w_out_b.shape), const((1, d)), const((1, d))],
        out_specs=row(d),
        out_shape=jax.ShapeDtypeStruct((m, d), F32),
        compiler_params=pltpu.CompilerParams(dimension_semantics=("arbitrary",), vmem_limit_bytes=VMEM_LIMIT),
        name="out_proj",
    )(oa, ob, x, w_out_b, ln_g.reshape(1, d), ln_b.reshape(1, d))


def _pad_lanes(vec, offset):
    return jnp.zeros((1, LANES), F32).at[0, offset:offset + vec.shape[0]].set(vec.astype(F32))


def _gate_rows_param(vec):
    col = jnp.broadcast_to(vec.astype(F32)[:, None], (H_DELTA, LANES))
    return jnp.concatenate([jnp.zeros((H_DELTA, LANES), F32), col], axis=0)


def kernel(x_prompt, x_sample, cache_k, cache_v, page_table, state_delta, state_conv, w_in, conv_w, a_log,
           dt_bias, delta_norm_w, diff_lambda, diff_norm_w, w_out, ln_g, ln_b):
    depth = w_in.shape[0]
    bp, tp, d = x_prompt.shape
    bs, ts, _ = x_sample.shape
    assert ts == 1 and w_in.shape[2] == P_MAIN + 2 * H_DELTA and d == w_out.shape[2]
    past_len = page_table.shape[1] * cache_k.shape[2]
    alpha = (2 * depth) ** 0.25
    tables_p = _rope_tables(jnp.arange(tp, dtype=jnp.int32))
    tables_s = _rope_tables(jnp.full((bs,), past_len, jnp.int32))

    hp, hs = x_prompt, x_sample
    outs = [[] for _ in range(8)]
    for l in range(depth):
        lam_init = _lambda_init(l)
        w_main = w_in[l, :, :P_MAIN].astype(BF16)
        w_tail = jnp.zeros((d, LANES), BF16).at[:, :2 * H_DELTA].set(w_in[l, :, P_MAIN:].astype(BF16))
        w_out_b = w_out[l].astype(BF16)
        alog_row = _pad_lanes(a_log[l], H_DELTA)
        dtb_row = _pad_lanes(dt_bias[l], H_DELTA)
        alog_col = _gate_rows_param(a_log[l])
        dtb_col = _gate_rows_param(dt_bias[l])

        sq, sk, _, sv, _, sga, scin, sz, stail = _in_proj(hs.reshape(1, bs, d), w_main, w_tail, tables_s)
        tok = lambda a: a.reshape(bs, 1, -1)
        decode_args = dict(q=tok(sq), k_new=tok(sk), v_new=tok(sv), ga=tok(sga), cache_k=cache_k, cache_v=cache_v,
                           layer=l, page_table=page_table, diff_lambda=diff_lambda[l], norm_w=diff_norm_w[l],
                           lam_init=lam_init)
        rider = _DecodeRider(**decode_args, seq0=0, n_seq=bs, grid=(bp, _delta_prep_grid(tp)[1]))
        if not rider.ok:
            rider = None

        q, k, kb, v, vb, ga, feat, z, tail_t, conv_tail = _in_proj(
            hp, w_main, w_tail, tables_p, conv_prev=jnp.zeros((bp, CONV_W - 1, CONV_CH), F32), conv_w=conv_w[l])
        oa = _prompt_attn(q, kb, vb, ga, diff_lambda[l], diff_norm_w[l], lam_init)
        prep, soa = _delta_prep(feat, tail_t, alog_col, dtb_col, rider)
        ob, sp = _delta_scan(prep, z, jnp.zeros((bp, H_DELTA, DK_DELTA, DV_DELTA), F32), delta_norm_w[l])
        hp = _out_proj(oa.reshape(bp * tp, W_DIFF), ob.reshape(bp * tp, W_DELTA), hp.reshape(bp * tp, d),
                       w_out_b, ln_g[l], ln_b[l], alpha).reshape(bp, tp, d)
        outs[0].append(k.reshape(bp, tp, H_DIFF, 2, DH_DIFF))
        outs[1].append(v.reshape(bp, tp, H_DIFF, DV_DIFF))
        outs[2].append(sp)
        outs[3].append(conv_tail[-1, :, SUBLANES - (CONV_W - 1):, :])

        if rider is None:
            soa = _decode_attn(decode_args, bs)
        sob, ss = _delta_step(tok(scin), state_conv[l], conv_w[l], tok(stail), alog_row, dtb_row, tok(sz),
                              state_delta[l], delta_norm_w[l])
        hs = _out_proj(soa.reshape(bs, W_DIFF), sob.reshape(bs, W_DELTA), hs.reshape(bs, d),
                       w_out_b, ln_g[l], ln_b[l], alpha).reshape(bs, 1, d)
        outs[4].append(sk.reshape(bs, 1, H_DIFF, 2, DH_DIFF))
        outs[5].append(sv.reshape(bs, 1, H_DIFF, DV_DIFF))
        outs[6].append(ss)
        outs[7].append(jnp.concatenate([state_conv[l][:, 1:, :], tok(scin)], axis=1))
    return (hp, hs) + tuple(jnp.stack(o) for o in outs)
```

```python
import functools
import math

import jax
import jax.numpy as jnp
from jax import lax
from jax.experimental import pallas as pl
from jax.experimental.pallas import tpu as pltpu

F32 = jnp.float32
BF16 = jnp.bfloat16
HIGHEST = lax.Precision.HIGHEST

H_DIFF = 4
DH_DIFF = 64
DV_DIFF = 2 * DH_DIFF
W_DIFF = H_DIFF * DV_DIFF
H_DELTA = 4
DK_DELTA = 128
DV_DELTA = 128
W_DELTA = H_DELTA * DK_DELTA
ROT_DIM = DH_DIFF // 4
ROPE_THETA = 500000.0
CONV_W = 4
CONV_CH = 3 * W_DELTA
CHUNK = 64
P_MAIN = 4 * W_DIFF + 4 * W_DELTA
LN_EPS = 1e-5
HEAD_NORM_EPS = 1e-5
DELTA_NORM_EPS = 1e-6
L2_EPS = 1e-6
NEG_INF = -1e30

LANES = 128
SUBLANES = 8
PAIR = 2 * CHUNK
VMEM_LIMIT = 56 * 1024 * 1024
DECODE_PAGES_PER_STEP = 16

_NT = (((1,), (1,)), ((), ()))


def _sigmoid(x):
    return 0.5 * jnp.tanh(0.5 * x) + 0.5


def _silu(x):
    return x * _sigmoid(x)


def _softplus(x):
    return jnp.maximum(x, 0.0) + jnp.log1p(jnp.exp(-jnp.abs(x)))


def _lambda_init(layer):
    return 0.8 - 0.6 * math.exp(-0.3 * layer)


def _rope_tables(pos):
    half = ROT_DIM // 2
    inv = ROPE_THETA ** (-jnp.arange(half, dtype=F32) / half)
    ang = pos.astype(F32)[:, None] * inv[None, :]
    cos, sin = jnp.cos(ang), jnp.sin(ang)
    t = pos.shape[0]
    rest = DH_DIFF - ROT_DIM
    cos_m = jnp.concatenate([cos, cos, jnp.ones((t, rest), F32)], axis=-1)
    sin_lo = jnp.concatenate([-sin, jnp.zeros((t, half + rest), F32)], axis=-1)
    sin_hi = jnp.concatenate([jnp.zeros((t, half), F32), sin, jnp.zeros((t, rest), F32)], axis=-1)
    reps = LANES // DH_DIFF
    return jnp.tile(cos_m, (1, reps)), jnp.tile(sin_lo, (1, reps)), jnp.tile(sin_hi, (1, reps))


def _in_proj_kernel(x_ref, w_ref, wt_ref, cos_ref, slo_ref, shi_ref,
                    q_ref, k_ref, kb_ref, v_ref, vb_ref, ga_ref, cin_ref, z_ref, tail_ref, *tail_t_ref):
    xb = x_ref[0].astype(BF16)
    half = ROT_DIM // 2
    tm = xb.shape[0]

    def proj(c0, width):
        return jnp.dot(xb, w_ref[:, c0:c0 + width], preferred_element_type=F32)

    def rope(h):
        blocks = []
        for c in range(W_DIFF // LANES):
            hb = h[:, c * LANES:(c + 1) * LANES]
            blocks.append(hb * cos_ref[...]
                          + pltpu.roll(hb, LANES - half, 1) * slo_ref[...]
                          + pltpu.roll(hb, half, 1) * shi_ref[...])
        return jnp.concatenate(blocks, axis=1)

    q_ref[0] = rope(proj(0, W_DIFF)).astype(BF16)
    k = rope(proj(W_DIFF, W_DIFF))
    k_ref[0] = k
    kb_ref[0] = k.astype(BF16)
    v = proj(2 * W_DIFF, W_DIFF)
    for h in range(H_DIFF):
        v_ref[0, pl.ds(h, tm, stride=H_DIFF), :] = v[:, h * DV_DIFF:(h + 1) * DV_DIFF]
    vb_ref[0] = v.astype(BF16)
    ga_ref[0] = proj(3 * W_DIFF, W_DIFF)
    for j in range(3):
        cin_ref[0, :, j * W_DELTA:(j + 1) * W_DELTA] = proj(4 * W_DIFF + j * W_DELTA, W_DELTA)
    z_ref[0] = proj(4 * W_DIFF + 3 * W_DELTA, W_DELTA)
    tail = jnp.dot(xb, wt_ref[...], preferred_element_type=F32)
    tail_ref[0] = tail
    if tail_t_ref:
        for c in range(tm // LANES):
            cs = slice(c * LANES, (c + 1) * LANES)
            tail_t_ref[0][0, :, cs] = tail[cs, :].T[:2 * H_DELTA, :]


def _in_proj(x, w_main, w_tail, tables, gates_time_major):
    b, t, d = x.shape
    tm = min(512, t)
    assert t % tm == 0 and (not gates_time_major or tm % LANES == 0)
    cos_t, slo_t, shi_t = tables
    row = lambda width: pl.BlockSpec((1, tm, width), lambda ti, bi: (bi, ti, 0))
    tab = pl.BlockSpec((tm, LANES), lambda ti, bi: (ti, 0))
    const = lambda shape: pl.BlockSpec(shape, lambda ti, bi: (0, 0))
    outs = [(W_DIFF, BF16), (W_DIFF, F32), (W_DIFF, BF16), None, (W_DIFF, BF16),
            (W_DIFF, F32), (CONV_CH, F32), (W_DELTA, F32), (LANES, F32)]
    out_specs = [row(o[0]) if o else pl.BlockSpec((1, tm * H_DIFF, DV_DIFF), lambda ti, bi: (bi, ti, 0))
                 for o in outs]
    out_shape = [jax.ShapeDtypeStruct((b, t, o[0]), o[1]) if o else
                 jax.ShapeDtypeStruct((b, t * H_DIFF, DV_DIFF), F32) for o in outs]
    if gates_time_major:
        out_specs.append(pl.BlockSpec((1, 2 * H_DELTA, tm), lambda ti, bi: (bi, 0, ti)))
        out_shape.append(jax.ShapeDtypeStruct((b, 2 * H_DELTA, t), F32))
    return pl.pallas_call(
        _in_proj_kernel,
        grid=(t // tm, b),
        in_specs=[row(d), const(w_main.shape), const(w_tail.shape), tab, tab, tab],
        out_specs=out_specs,
        out_shape=out_shape,
        compiler_params=pltpu.CompilerParams(
            dimension_semantics=("arbitrary", "arbitrary"), vmem_limit_bytes=VMEM_LIMIT),
        name="in_proj",
    )(x, w_main, w_tail, cos_t, slo_t, shi_t)


def _diff_lambda_value(dl_ref, lam_init):
    dl = dl_ref[...]
    a = jnp.sum(dl[0:1] * dl[1:2], axis=1, keepdims=True)
    b = jnp.sum(dl[2:3] * dl[3:4], axis=1, keepdims=True)
    return jnp.exp(a) - jnp.exp(b) + lam_init


def _head_norm_gate(o, normw, gate, lam_init):
    ms = jnp.mean(o * o, axis=-1, keepdims=True)
    o = o * lax.rsqrt(ms + HEAD_NORM_EPS) * normw * (1.0 - lam_init)
    return o * _silu(gate)


def _prompt_attn_kernel(q0_ref, qn_ref, k_ref, v_ref, ga_ref, dl_ref, nw_ref, o_ref, s_scr, m_scr, l_scr, acc_scr,
                        *, blk, lam_init):
    i = pl.program_id(1)
    last = pl.num_programs(1) - 1
    slot = lax.rem(i, 2)
    lane = lax.broadcasted_iota(jnp.int32, (blk, DV_DIFF), 1)
    heads = [slice(h * DV_DIFF, (h + 1) * DV_DIFF) for h in range(H_DIFF)]
    lane_chunks = [slice(c * LANES, (c + 1) * LANES) for c in range(blk // LANES)]
    n_maps = 2 * H_DIFF

    def query_maps(q_ref):
        maps = []
        for hs in heads:
            qs = q_ref[0, :, hs].astype(F32) * (DH_DIFF ** -0.5 * math.log2(math.e))
            maps.append(jnp.where(lane < DH_DIFF, qs, 0.0).astype(BF16))
            maps.append(jnp.where(lane >= DH_DIFF, qs, 0.0).astype(BF16))
        return maps

    def fold(x, op):
        r = x[:, lane_chunks[0]]
        for c in lane_chunks[1:]:
            r = op(r, x[:, c])
        return r

    def score_block(q_maps, dst, j, masked):
        rows = pl.ds(pl.multiple_of(j * blk, blk), blk)
        for h, hs in enumerate(heads):
            k = k_ref[0, rows, hs]
            for mi in (2 * h, 2 * h + 1):
                s = lax.dot_general(q_maps[mi], k, _NT, preferred_element_type=F32)
                if masked:
                    row = lax.broadcasted_iota(jnp.int32, (blk, blk), 0)
                    col = lax.broadcasted_iota(jnp.int32, (blk, blk), 1)
                    s = jnp.where(col <= row, s, NEG_INF)
                s_scr[dst, j, mi] = s
                m_scr[dst, mi] = jnp.maximum(m_scr[dst, mi], fold(s, jnp.maximum))

    def finish_max(dst):
        for mi in range(n_maps):
            m_scr[dst, mi] = jnp.broadcast_to(jnp.max(m_scr[dst, mi], axis=-1, keepdims=True), (blk, LANES))

    def prob_block(j):
        rows = pl.ds(pl.multiple_of(j * blk, blk), blk)
        for h, hs in enumerate(heads):
            v = v_ref[0, rows, hs]
            for mi in (2 * h, 2 * h + 1):
                m = m_scr[slot, mi]
                p = jnp.concatenate([jnp.exp2(s_scr[slot, j, mi, :, c] - m) for c in lane_chunks], axis=1)
                l_scr[mi] = l_scr[mi] + fold(p, jnp.add)
                acc_scr[mi] = acc_scr[mi] + jnp.dot(p.astype(BF16), v, preferred_element_type=F32)

    @pl.when(i == 0)
    def _():
        m_scr[0] = jnp.full(m_scr.shape[1:], NEG_INF, F32)
        score_block(query_maps(q0_ref), 0, 0, True)
        finish_max(0)

    l_scr[...] = jnp.zeros(l_scr.shape, F32)
    acc_scr[...] = jnp.zeros(acc_scr.shape, F32)

    @pl.when(i < last)
    def _():
        nxt = 1 - slot
        q_next = query_maps(qn_ref)
        m_scr[nxt] = jnp.full(m_scr.shape[1:], NEG_INF, F32)

        def both(j, carry):
            prob_block(j)
            score_block(q_next, nxt, j, False)
            return carry

        lax.fori_loop(0, i + 1, both, 0)
        score_block(q_next, nxt, i + 1, True)
        finish_max(nxt)

    @pl.when(i == last)
    def _():
        def only_probs(j, carry):
            prob_block(j)
            return carry

        lax.fori_loop(0, i + 1, only_probs, 0)

    lam = _diff_lambda_value(dl_ref, lam_init)
    for h, hs in enumerate(heads):
        l1 = jnp.sum(l_scr[2 * h], axis=-1, keepdims=True)
        l2 = jnp.sum(l_scr[2 * h + 1], axis=-1, keepdims=True)
        o = acc_scr[2 * h] / l1 - lam * (acc_scr[2 * h + 1] / l2)
        o_ref[0, :, hs] = _head_norm_gate(o, nw_ref[...], ga_ref[0, :, hs], lam_init).astype(BF16)


def _prompt_attn(q, kb, vb, ga, diff_lambda, norm_w, lam_init):
    b, t, _ = q.shape
    blk = min(256, t)
    assert t % blk == 0
    n_maps = 2 * H_DIFF
    nq = t // blk
    qspec = pl.BlockSpec((1, blk, W_DIFF), lambda bi, i: (bi, i, 0))
    q_first = pl.BlockSpec((1, blk, W_DIFF), lambda bi, i: (bi, 0, 0))
    q_next = pl.BlockSpec((1, blk, W_DIFF), lambda bi, i: (bi, jnp.minimum(i + 1, nq - 1), 0))
    kvspec = pl.BlockSpec((1, t, W_DIFF), lambda bi, i: (bi, 0, 0))
    return pl.pallas_call(
        functools.partial(_prompt_attn_kernel, blk=blk, lam_init=lam_init),
        grid=(b, nq),
        in_specs=[q_first, q_next, kvspec, kvspec, qspec,
                  pl.BlockSpec(diff_lambda.shape, lambda bi, i: (0, 0)),
                  pl.BlockSpec((1, DV_DIFF), lambda bi, i: (0, 0))],
        out_specs=qspec,
        out_shape=jax.ShapeDtypeStruct((b, t, W_DIFF), BF16),
        scratch_shapes=[pltpu.VMEM((2, nq, n_maps, blk, blk), F32),
                        pltpu.VMEM((2, n_maps, blk, LANES), F32),
                        pltpu.VMEM((n_maps, blk, LANES), F32),
                        pltpu.VMEM((n_maps, blk, DV_DIFF), F32)],
        compiler_params=pltpu.CompilerParams(
            dimension_semantics=("arbitrary", "arbitrary"), vmem_limit_bytes=VMEM_LIMIT),
        name="prompt_attn",
    )(q, q, kb, vb, ga, diff_lambda, norm_w.reshape(1, DV_DIFF))


class _DecodeStep:
    def __init__(self, step, n_steps, spq, first_page, pages, pt_ref, in_refs, o_ref, scratch_refs, lam_init):
        self.step, self.n_steps, self.pages, self.first_page = step, n_steps, pages, first_page
        self.g, self.n_g, self.lam_init, self.pt_ref = lax.rem(step, spq), spq, lam_init, pt_ref
        (self.q_ref, self.kn_ref, self.vn_ref, self.ga_ref, self.dl_ref, self.nw_ref,
         self.ck_ref, self.cv_ref) = in_refs
        self.o_ref = o_ref
        (self.qcol_ref, self.m_ref, self.l_ref, self.acc_ref,
         self.kbuf_ref, self.vbuf_ref, self.sem_ref) = scratch_refs
        self.slot = lax.rem(step, 2)

    def _query(self):
        return self.q_ref[0].astype(F32) * DH_DIFF ** -0.5

    def _page_copies(self, step, slot):
        base = self.first_page + step * self.pages
        copies = []
        for i in range(self.pages):
            pid = self.pt_ref[base + i]
            copies.append(pltpu.make_async_copy(self.ck_ref.at[pid], self.kbuf_ref.at[slot, i],
                                                self.sem_ref.at[slot, 0]))
            copies.append(pltpu.make_async_copy(self.cv_ref.at[pid], self.vbuf_ref.at[slot, i],
                                                self.sem_ref.at[slot, 1]))
        return copies

    def init(self):
        @pl.when(self.step == 0)
        def _():
            for cp in self._page_copies(0, 0):
                cp.start()

        for cp in self._page_copies(self.step, self.slot):
            cp.wait()

        @pl.when(self.step + 1 < self.n_steps)
        def _():
            for cp in self._page_copies(self.step + 1, 1 - self.slot):
                cp.start()

        @pl.when(self.g == 0)
        def _():
            self.m_ref[...] = jnp.full(self.m_ref.shape, NEG_INF, F32)
            self.l_ref[...] = jnp.zeros(self.l_ref.shape, F32)
            self.acc_ref[...] = jnp.zeros(self.acc_ref.shape, F32)
            qs = self._query()
            for c in range(W_DIFF // LANES):
                cs = slice(c * LANES, (c + 1) * LANES)
                self.qcol_ref[cs, :] = jnp.broadcast_to(qs[:, cs], (LANES, LANES)).T

    def main(self):
        k_pages = [self.kbuf_ref.at[self.slot, i] for i in range(self.pages)]
        v_pages = [self.vbuf_ref.at[self.slot, i] for i in range(self.pages)]
        _decode_pages(k_pages, v_pages, self.qcol_ref, self.m_ref, self.l_ref, self.acc_ref)

    def finalize(self):
        @pl.when(self.g == self.n_g - 1)
        def _():
            _decode_finish(self._query(), self.kn_ref, self.vn_ref, self.ga_ref, self.dl_ref, self.nw_ref,
                           self.o_ref, self.m_ref, self.l_ref, self.acc_ref, self.lam_init)


def _decode_pages(kp_refs, vp_refs, qcol_ref, m_ref, l_ref, acc_ref):
    n_maps = 2 * H_DIFF
    page = kp_refs[0].shape[1]
    qcol = qcol_ref[...]
    s = jnp.concatenate(
        [jnp.sum((kp[...] * qcol).reshape(n_maps, DH_DIFF, page), axis=1) for kp in kp_refs], axis=1)
    m_prev = m_ref[:, 0:1]
    m_new = jnp.maximum(m_prev, jnp.max(s, axis=-1, keepdims=True))
    p = jnp.exp(s - m_new)
    alpha = jnp.exp(m_prev - m_new)
    l_new = alpha * l_ref[:, 0:1] + jnp.sum(p, axis=-1, keepdims=True)
    pb = p.astype(BF16)
    for h in range(H_DIFF):
        pv = jnp.zeros((n_maps, DV_DIFF), F32)
        for i, vp in enumerate(vp_refs):
            v_head = vp[pl.ds(h, page, stride=H_DIFF), :].astype(BF16)
            pv = pv + jnp.dot(pb[:, i * page:(i + 1) * page], v_head, preferred_element_type=F32)
        acc_ref[h] = alpha * acc_ref[h] + pv
    m_ref[...] = jnp.broadcast_to(m_new, m_ref.shape)
    l_ref[...] = jnp.broadcast_to(l_new, l_ref.shape)


def _decode_finish(qs, kn_ref, vn_ref, ga_ref, dl_ref, nw_ref, o_ref, m_ref, l_ref, acc_ref, lam_init):
    n_maps = 2 * H_DIFF
    rows = lax.broadcasted_iota(jnp.int32, (n_maps, W_DIFF), 0)
    lanes = lax.broadcasted_iota(jnp.int32, (n_maps, W_DIFF), 1)
    q_blk = jnp.where((lanes >> 6) == rows, jnp.broadcast_to(qs, (n_maps, W_DIFF)), 0.0).astype(BF16)
    k_self = jnp.broadcast_to(kn_ref[0].astype(BF16), (8, W_DIFF))
    s_self = lax.dot_general(q_blk, k_self, _NT, preferred_element_type=F32)[:, 0:1]
    m_past = m_ref[:, 0:1]
    m_f = jnp.maximum(m_past, s_self)
    a_f = jnp.exp(m_past - m_f)
    p_self = jnp.exp(s_self - m_f)
    l_f = a_f * l_ref[:, 0:1] + p_self
    pv_self = p_self.astype(BF16).astype(F32) * vn_ref[0].astype(BF16).astype(F32)
    lam = _diff_lambda_value(dl_ref, lam_init)
    for h in range(H_DIFF):
        hs = slice(h * DV_DIFF, (h + 1) * DV_DIFF)
        a = (a_f * acc_ref[h] + pv_self[:, hs]) / l_f
        o = a[2 * h:2 * h + 1] - lam * a[2 * h + 1:2 * h + 2]
        o_ref[0, :, hs] = _head_norm_gate(o, nw_ref[...], ga_ref[0, :, hs], lam_init).astype(BF16)


class _DecodeRider:
    def __init__(self, q, k_new, v_new, ga, cache_k, cache_v, layer, page_table, diff_lambda, norm_w, lam_init,
                 seq0, n_seq, grid):
        depth, n_phys, page = cache_k.shape[:3]
        n_pages = page_table.shape[1]
        n_steps = grid[0] * grid[1]
        assert page == LANES
        pages = n_seq * n_pages // n_steps
        self.ok = pages >= 1 and pages * n_steps == n_seq * n_pages and n_pages % pages == 0
        if not self.ok:
            return
        spq = n_pages // pages
        self.pages, self.spq, self.lam_init, self.grid = pages, spq, lam_init, grid
        self.first_page = seq0 * n_pages
        ck = jnp.transpose(cache_k, (0, 1, 3, 4, 5, 2)).reshape(depth * n_phys, W_DIFF, page)
        cv = cache_v.reshape(depth * n_phys, page * H_DIFF, DV_DIFF)
        self.page_ids = page_table.reshape(-1) + layer * n_phys
        step = lambda i0, i1: i0 * grid[1] + i1
        tok = pl.BlockSpec((1, 1, W_DIFF), lambda i0, i1, pt: (seq0 + step(i0, i1) // spq, 0, 0))
        const = lambda shape: pl.BlockSpec(shape, lambda i0, i1, pt: (0, 0))
        hbm = pl.BlockSpec(memory_space=pl.ANY)
        self.inputs = [q, k_new, v_new, ga, diff_lambda, norm_w.reshape(1, DV_DIFF), ck, cv]
        self.in_specs = [tok, tok, tok, tok, const(diff_lambda.shape), const((1, DV_DIFF)), hbm, hbm]
        self.out_spec = pl.BlockSpec((1, 1, W_DIFF), lambda i0, i1, pt: (step(i0, i1) // spq, 0, 0))
        self.out_shape = jax.ShapeDtypeStruct((n_seq, 1, W_DIFF), BF16)
        n_maps = 2 * H_DIFF
        self.scratch_shapes = [pltpu.VMEM((W_DIFF, page), F32),
                               pltpu.VMEM((n_maps, LANES), F32), pltpu.VMEM((n_maps, LANES), F32),
                               pltpu.VMEM((H_DIFF, n_maps, DV_DIFF), F32),
                               pltpu.VMEM((2, pages, W_DIFF, page), F32),
                               pltpu.VMEM((2, pages, page * H_DIFF, DV_DIFF), F32),
                               pltpu.SemaphoreType.DMA((2, 2))]

    def step(self, pt_ref, in_refs, out_ref, scratch_refs):
        step = pl.program_id(0) * self.grid[1] + pl.program_id(1)
        return _DecodeStep(step, self.grid[0] * self.grid[1], self.spq, self.first_page, self.pages, pt_ref,
                           in_refs, out_ref, scratch_refs, self.lam_init)


def _ride(host_kernel, n_in, n_out, n_scratch, rider):
    n_rin = len(rider.inputs)

    def kernel(pt_ref, *refs):
        host_in, refs = refs[:n_in], refs[n_in:]
        rider_in, refs = refs[:n_rin], refs[n_rin:]
        host_out, refs = refs[:n_out], refs[n_out:]
        rider_out, refs = refs[0], refs[1:]
        host_scratch, rider_scratch = refs[:n_scratch], refs[n_scratch:]
        host_kernel(*host_in, *host_out, *host_scratch,
                    decode=rider.step(pt_ref, rider_in, rider_out, rider_scratch))

    return kernel


def _hosted_call(host_kernel, grid, in_specs, out_specs, out_shape, scratch_shapes, inputs, name, rider):
    params = pltpu.CompilerParams(dimension_semantics=("arbitrary", "arbitrary"), vmem_limit_bytes=VMEM_LIMIT)
    if rider is None:
        outs = pl.pallas_call(host_kernel, grid=grid, in_specs=in_specs, out_specs=out_specs, out_shape=out_shape,
                              scratch_shapes=scratch_shapes, compiler_params=params, name=name)(*inputs)
        return outs, None
    grid_spec = pltpu.PrefetchScalarGridSpec(
        num_scalar_prefetch=1, grid=grid,
        in_specs=list(in_specs) + rider.in_specs,
        out_specs=list(out_specs) + [rider.out_spec],
        scratch_shapes=list(scratch_shapes) + rider.scratch_shapes)
    outs = pl.pallas_call(
        _ride(host_kernel, len(in_specs), len(out_specs), len(scratch_shapes), rider),
        grid_spec=grid_spec, out_shape=list(out_shape) + [rider.out_shape],
        compiler_params=params, name=name)(rider.page_ids, *inputs, *rider.inputs)
    return outs[:-1], outs[-1]


def _decode_attn_kernel(pt_ref, *refs, rider):
    n_rin = len(rider.inputs)
    decode = rider.step(pt_ref, refs[:n_rin], refs[n_rin], refs[n_rin + 1:])
    decode.init()
    decode.main()
    decode.finalize()


def _decode_attn(rider_args, n_seq):
    n_pages = rider_args["page_table"].shape[1]
    grid = (n_seq, n_pages // math.gcd(DECODE_PAGES_PER_STEP, n_pages))
    rider = _DecodeRider(**rider_args, seq0=0, n_seq=n_seq, grid=grid)
    grid_spec = pltpu.PrefetchScalarGridSpec(
        num_scalar_prefetch=1, grid=grid, in_specs=rider.in_specs,
        out_specs=rider.out_spec, scratch_shapes=rider.scratch_shapes)
    return pl.pallas_call(
        functools.partial(_decode_attn_kernel, rider=rider), grid_spec=grid_spec, out_shape=rider.out_shape,
        compiler_params=pltpu.CompilerParams(
            dimension_semantics=("arbitrary", "arbitrary"), vmem_limit_bytes=VMEM_LIMIT),
        name="decode_attn")(rider.page_ids, *rider.inputs)


def _gate_rows(tail, alog_row, dtb_row):
    beta = _sigmoid(tail)
    g = -jnp.exp(alog_row) * _softplus(tail + dtb_row)
    return beta, g


def _lane_bcast(x, lane, rows):
    return jnp.broadcast_to(x[:, lane:lane + 1], (rows, LANES))


def _l2norm(x):
    return x * lax.rsqrt(jnp.sum(x * x, axis=-1, keepdims=True) + L2_EPS)


def _split_bf16(x):
    hi = x.astype(BF16)
    return hi, (x - hi.astype(F32)).astype(BF16)


def _dot_split(lhs, rhs):
    d = lambda a, b: jnp.dot(a, b, preferred_element_type=F32)
    return d(lhs[0], rhs[0]) + d(lhs[0], rhs[1]) + d(lhs[1], rhs[0])


def _unit_lower_inverses(a_mats):
    n = a_mats[0].shape[0]
    eye = (lax.broadcasted_iota(jnp.int32, (n, n), 0) == lax.broadcasted_iota(jnp.int32, (n, n), 1)).astype(F32)
    xs = [-a for a in a_mats]
    ps = [eye + x for x in xs]
    splits = [_split_bf16(x) for x in xs]
    xs = [_dot_split(s, s) for s in splits]
    power = 2
    while 2 * power < CHUNK:
        both = [_dot_split(_split_bf16(x), _split_bf16(jnp.concatenate([p, x], axis=1))) for p, x in zip(ps, xs)]
        ps = [p + b[:, :n] for p, b in zip(ps, both)]
        xs = [b[:, n:] for b in both]
        power *= 2
    return [p + _dot_split(_split_bf16(x), _split_bf16(p)) for p, x in zip(ps, xs)]


def _delta_prep_kernel(cin_ref, prev_ref, cw_ref, tail_ref, alog_ref, dtb_ref,
                       u_ref, w_ref, qg_ref, at_ref, kdt_ref, el_ref, ext_ref, *, pairs, decode=None):
    t = pl.program_id(1)
    pad = SUBLANES
    keep = CONV_W - 1
    rows = pairs * PAIR

    @pl.when(t == 0)
    def _():
        ext_ref[pad - keep:pad, :] = prev_ref[0]

    @pl.when(t > 0)
    def _():
        ext_ref[pad - keep:pad, :] = ext_ref[pad + rows - keep:pad + rows, :]

    if decode is not None:
        decode.init()

    ext_ref[pad:pad + rows, :] = cin_ref[0]

    r = lax.broadcasted_iota(jnp.int32, (PAIR, PAIR), 0)
    c = lax.broadcasted_iota(jnp.int32, (PAIR, PAIR), 1)
    same = (r >> 6) == (c >> 6)
    incl = same & (c <= r)
    strict = same & (c < r)
    incl_t = same & (r <= c)
    chunk_end = r == (c | (CHUNK - 1))

    chains = []
    for pi in range(pairs):
        base = pad - keep + pi * PAIR
        conv = ext_ref[base:base + PAIR, :] * cw_ref[0:1, :]
        for j in range(1, CONV_W):
            conv = conv + ext_ref[base + j:base + j + PAIR, :] * cw_ref[j:j + 1, :]
        conv = _silu(conv)
        rs = slice(pi * PAIR, (pi + 1) * PAIR)
        beta_t, g_t = _gate_rows(tail_ref[0, :, rs], alog_ref[...], dtb_ref[...])
        gc_t = jnp.dot(g_t, incl_t.astype(F32), precision=HIGHEST, preferred_element_type=F32)
        g_last_t = jnp.dot(gc_t, chunk_end.astype(F32), precision=HIGHEST, preferred_element_type=F32)
        for h in range(H_DELTA):
            hs = slice(h * DK_DELTA, (h + 1) * DK_DELTA)
            qs = _l2norm(conv[:, hs]) * DK_DELTA ** -0.5
            kn = _l2norm(conv[:, W_DELTA + h * DK_DELTA:W_DELTA + (h + 1) * DK_DELTA])
            vh = conv[:, 2 * W_DELTA + h * DV_DELTA:2 * W_DELTA + (h + 1) * DV_DELTA]
            beta = jnp.broadcast_to(beta_t[h:h + 1, :], (PAIR, PAIR)).T
            gc_cols = jnp.broadcast_to(gc_t[H_DELTA + h:H_DELTA + h + 1, :], (PAIR, PAIR))
            g_last_cols = jnp.broadcast_to(g_last_t[H_DELTA + h:H_DELTA + h + 1, :], (PAIR, PAIR))
            chains.append(dict(pi=pi, h=h, rs=rs, qs=qs, kn=kn, vh=vh, beta=beta, gc=gc_cols.T, gc_cols=gc_cols,
                               g_last_cols=g_last_cols, kbeta=kn * beta))

    for ch in chains:
        ch["kn_t"] = ch["kn"].T
        kn_tb = ch["kn_t"].astype(BF16)
        ch["kk"] = jnp.dot(ch["kbeta"].astype(BF16), kn_tb, preferred_element_type=F32)
        ch["qk"] = jnp.dot(ch["qs"].astype(BF16), kn_tb, preferred_element_type=F32)
    for ch in chains:
        ch["decay"] = jnp.exp(jnp.where(incl, ch["gc"] - ch["gc_cols"], -jnp.inf))
    t_mats = _unit_lower_inverses([jnp.where(strict, ch["kk"] * ch["decay"], 0.0) for ch in chains])
    for ch, t_mat in zip(chains, t_mats):
        rhs = jnp.concatenate([ch["vh"] * ch["beta"], ch["kbeta"] * jnp.exp(ch["gc"])], axis=1).astype(BF16)
        ch["uw"] = jnp.dot(t_mat.astype(BF16), rhs, preferred_element_type=F32)
    for ch in chains:
        pi, h, rs, gc = ch["pi"], ch["h"], ch["rs"], ch["gc"]
        u_ref[0, h, rs, :] = ch["uw"][:, :DV_DELTA]
        w_ref[0, h, rs, :] = ch["uw"][:, DV_DELTA:].astype(BF16)
        attn = jnp.where(incl, ch["qk"] * ch["decay"], 0.0)
        at_ref[0, h, rs, :] = jnp.concatenate([attn[:CHUNK, :CHUNK], attn[CHUNK:, CHUNK:]], axis=0).astype(BF16)
        qg_ref[0, h, rs, :] = (ch["qs"] * jnp.exp(gc)).astype(BF16)
        kdt_ref[0, h, pi] = (ch["kn_t"] * jnp.exp(ch["g_last_cols"] - ch["gc_cols"])).astype(BF16)
        el_ref[0, h, pi] = jnp.exp(jnp.concatenate([jnp.broadcast_to(gc[CHUNK - 1:CHUNK], (8, LANES)),
                                                    jnp.broadcast_to(gc[PAIR - 1:PAIR], (8, LANES))], axis=0))

    if decode is not None:
        decode.main()
        decode.finalize()


def _delta_prep_grid(t):
    assert t % PAIR == 0
    npair = t // PAIR
    pairs = 2 if npair % 2 == 0 else 1
    return pairs, npair // pairs


def _delta_prep(cin, conv_prev, conv_w, tail, alog_row, dtb_row, rider):
    b, t, _ = cin.shape
    npair = t // PAIR
    pairs, nt = _delta_prep_grid(t)
    rows = pairs * PAIR
    per_head = lambda width: pl.BlockSpec((1, H_DELTA, rows, width), lambda bi, ti, *_: (bi, 0, ti, 0))
    const = lambda shape: pl.BlockSpec(shape, lambda bi, ti, *_: (0,) * len(shape))
    out_shape = [
        jax.ShapeDtypeStruct((b, H_DELTA, t, DV_DELTA), F32),
        jax.ShapeDtypeStruct((b, H_DELTA, t, DK_DELTA), BF16),
        jax.ShapeDtypeStruct((b, H_DELTA, t, DK_DELTA), BF16),
        jax.ShapeDtypeStruct((b, H_DELTA, t, CHUNK), BF16),
        jax.ShapeDtypeStruct((b, H_DELTA, npair, DK_DELTA, PAIR), BF16),
        jax.ShapeDtypeStruct((b, H_DELTA, npair, 16, LANES), F32),
    ]
    out_specs = [per_head(DV_DELTA), per_head(DK_DELTA), per_head(DK_DELTA), per_head(CHUNK),
                 pl.BlockSpec((1, H_DELTA, pairs, DK_DELTA, PAIR), lambda bi, ti, *_: (bi, 0, ti, 0, 0)),
                 pl.BlockSpec((1, H_DELTA, pairs, 16, LANES), lambda bi, ti, *_: (bi, 0, ti, 0, 0))]
    return _hosted_call(
        functools.partial(_delta_prep_kernel, pairs=pairs),
        grid=(b, nt),
        in_specs=[pl.BlockSpec((1, rows, CONV_CH), lambda bi, ti, *_: (bi, ti, 0)),
                  pl.BlockSpec((1, CONV_W - 1, CONV_CH), lambda bi, ti, *_: (bi, 0, 0)),
                  const(conv_w.shape),
                  pl.BlockSpec((1, 2 * H_DELTA, rows), lambda bi, ti, *_: (bi, 0, ti)),
                  const(alog_row.shape), const(dtb_row.shape)],
        out_specs=out_specs,
        out_shape=out_shape,
        scratch_shapes=[pltpu.VMEM((SUBLANES + rows, CONV_CH), F32)],
        inputs=(cin, conv_prev, conv_w, tail, alog_row, dtb_row),
        name="delta_prep", rider=rider)


def _delta_norm_gate(o, normw, z):
    ms = jnp.mean(o * o, axis=-1, keepdims=True)
    return o * lax.rsqrt(ms + DELTA_NORM_EPS) * normw * _silu(z)


def _delta_scan_kernel(u_ref, w_ref, qg_ref, at_ref, kdt_ref, el_ref, z_ref, s0_ref, nw_ref,
                       o_ref, sfin_ref, s_scr, *, nb, nblk):
    t = pl.program_id(1)

    @pl.when(t == 0)
    def _():
        s_scr[...] = s0_ref[...]

    def block(jb, carry):
        r0 = pl.multiple_of(jb * PAIR, PAIR)
        seqs = [(bi, h) for bi in range(nb) for h in range(H_DELTA)]
        for ci in range(2):
            rows = pl.ds(pl.multiple_of(r0 + ci * CHUNK, CHUNK), CHUNK)
            states = [s_scr[bi, h] for bi, h in seqs]
            res = [jnp.dot(jnp.concatenate([w_ref[bi, h, rows, :], qg_ref[bi, h, rows, :]], axis=0),
                           s.astype(BF16), preferred_element_type=F32) for (bi, h), s in zip(seqs, states)]
            v_new = [(u_ref[bi, h, rows, :] - r[:CHUNK]).astype(BF16) for (bi, h), r in zip(seqs, res)]
            upd = [jnp.dot(kdt_ref[bi, h, jb, :, ci * CHUNK:(ci + 1) * CHUNK], v, preferred_element_type=F32)
                   for (bi, h), v in zip(seqs, v_new)]
            intra = [jnp.dot(at_ref[bi, h, rows, :], v, preferred_element_type=F32) for (bi, h), v in zip(seqs, v_new)]
            for (bi, h), s, r, du, oi in zip(seqs, states, res, upd, intra):
                decay = jnp.broadcast_to(el_ref[bi, h, jb, ci * 8:ci * 8 + 1, :], (DK_DELTA, DV_DELTA))
                s_scr[bi, h] = s * decay + du
                hs = slice(h * DV_DELTA, (h + 1) * DV_DELTA)
                o_ref[bi, rows, hs] = _delta_norm_gate(r[CHUNK:] + oi, nw_ref[...], z_ref[bi, rows, hs]).astype(BF16)
        return carry

    lax.fori_loop(0, nblk, block, 0)

    @pl.when(t == pl.num_programs(1) - 1)
    def _():
        sfin_ref[...] = s_scr[...]


def _delta_scan(prep, z, s0, norm_w):
    u, w, qg, at, kdt, el = prep
    b, _, t, _ = u.shape
    nb = 2 if b % 2 == 0 else 1
    ts = min(512, t)
    assert t % ts == 0
    nblk = ts // PAIR
    per_head = lambda width: pl.BlockSpec((nb, H_DELTA, ts, width), lambda bi, ti: (bi, 0, ti, 0))
    state = pl.BlockSpec((nb, H_DELTA, DK_DELTA, DV_DELTA), lambda bi, ti: (bi, 0, 0, 0))
    tok = pl.BlockSpec((nb, ts, W_DELTA), lambda bi, ti: (bi, ti, 0))
    return pl.pallas_call(
        functools.partial(_delta_scan_kernel, nb=nb, nblk=nblk),
        grid=(b // nb, t // ts),
        in_specs=[per_head(DV_DELTA), per_head(DK_DELTA), per_head(DK_DELTA), per_head(CHUNK),
                  pl.BlockSpec((nb, H_DELTA, nblk, DK_DELTA, PAIR), lambda bi, ti: (bi, 0, ti, 0, 0)),
                  pl.BlockSpec((nb, H_DELTA, nblk, 16, LANES), lambda bi, ti: (bi, 0, ti, 0, 0)),
                  tok, state, pl.BlockSpec((1, DV_DELTA), lambda bi, ti: (0, 0))],
        out_specs=[tok, state],
        out_shape=[jax.ShapeDtypeStruct((b, t, W_DELTA), BF16),
                   jax.ShapeDtypeStruct((b, H_DELTA, DK_DELTA, DV_DELTA), F32)],
        scratch_shapes=[pltpu.VMEM((nb, H_DELTA, DK_DELTA, DV_DELTA), F32)],
        compiler_params=pltpu.CompilerParams(
            dimension_semantics=("arbitrary", "arbitrary"), vmem_limit_bytes=VMEM_LIMIT),
        name="delta_scan",
    )(u, w, qg, at, kdt, el, z, s0, norm_w.reshape(1, DV_DELTA))


def _delta_step_kernel(cin_ref, prev_ref, cw_ref, tail_ref, alog_ref, dtb_ref, z_ref, s0_ref, nw_ref,
                       o_ref, s_ref):
    prev = prev_ref[0]
    conv = prev[0:1] * cw_ref[0:1, :]
    for j in range(1, CONV_W - 1):
        conv = conv + prev[j:j + 1] * cw_ref[j:j + 1, :]
    conv = _silu(conv + cin_ref[0] * cw_ref[CONV_W - 1:CONV_W, :])
    beta_all, g_all = _gate_rows(tail_ref[0], alog_ref[...], dtb_ref[...])
    for h in range(H_DELTA):
        hs = slice(h * DK_DELTA, (h + 1) * DK_DELTA)
        qs = _l2norm(conv[:, hs]) * DK_DELTA ** -0.5
        kn = _l2norm(conv[:, W_DELTA + h * DK_DELTA:W_DELTA + (h + 1) * DK_DELTA])
        vh = conv[:, 2 * W_DELTA + h * DV_DELTA:2 * W_DELTA + (h + 1) * DV_DELTA]
        beta = beta_all[:, h:h + 1]
        eg = jnp.exp(g_all[:, H_DELTA + h:H_DELTA + h + 1])
        s = s0_ref[0, h]
        lhs = jnp.concatenate([kn * (beta * eg), qs * eg, jnp.zeros((6, DK_DELTA), F32)], axis=0).astype(BF16)
        res = jnp.dot(lhs, s.astype(BF16), preferred_element_type=F32)
        v_new = vh * beta - res[0:1]
        qk = jnp.sum(qs.astype(BF16).astype(F32) * kn.astype(BF16).astype(F32), axis=-1, keepdims=True)
        o = res[1:2] + qk * v_new
        k_col = jnp.broadcast_to(kn, (DK_DELTA, DK_DELTA)).T
        s_ref[0, h] = s * eg + k_col * v_new
        o_ref[0, :, hs] = _delta_norm_gate(o, nw_ref[...], z_ref[0, :, hs]).astype(BF16)


def _delta_step(cin, conv_prev, conv_w, tail, alog_row, dtb_row, z, s0, norm_w):
    bs = cin.shape[0]
    tok = lambda width: pl.BlockSpec((1, 1, width), lambda b: (b, 0, 0))
    const = lambda shape: pl.BlockSpec(shape, lambda b: (0,) * len(shape))
    state = pl.BlockSpec((1, H_DELTA, DK_DELTA, DV_DELTA), lambda b: (b, 0, 0, 0))
    return pl.pallas_call(
        _delta_step_kernel,
        grid=(bs,),
        in_specs=[tok(CONV_CH), pl.BlockSpec((1, CONV_W - 1, CONV_CH), lambda b: (b, 0, 0)), const(conv_w.shape),
                  tok(LANES), const(alog_row.shape), const(dtb_row.shape), tok(W_DELTA), state,
                  const((1, DV_DELTA))],
        out_specs=[tok(W_DELTA), state],
        out_shape=[jax.ShapeDtypeStruct((bs, 1, W_DELTA), BF16),
                   jax.ShapeDtypeStruct((bs, H_DELTA, DK_DELTA, DV_DELTA), F32)],
        compiler_params=pltpu.CompilerParams(dimension_semantics=("arbitrary",)),
        name="delta_step",
    )(cin, conv_prev, conv_w, tail, alog_row, dtb_row, z, s0, norm_w.reshape(1, DV_DELTA))


def _out_proj_kernel(oa_ref, ob_ref, x_ref, w_ref, g_ref, b_ref, y_ref, *, alpha):
    mix = jnp.dot(oa_ref[...], w_ref[:W_DIFF, :], preferred_element_type=F32)
    mix = mix + jnp.dot(ob_ref[...], w_ref[W_DIFF:, :], preferred_element_type=F32)
    r = alpha * x_ref[...] + mix
    mu = jnp.mean(r, axis=-1, keepdims=True)
    var = jnp.mean(jnp.square(r - mu), axis=-1, keepdims=True)
    y_ref[...] = (r - mu) * lax.rsqrt(var + LN_EPS) * g_ref[...] + b_ref[...]


def _out_proj(oa, ob, x, w_out_b, ln_g, ln_b, alpha):
    m, d = x.shape
    tm = min(512, m)
    assert m % tm == 0
    row = lambda width: pl.BlockSpec((tm, width), lambda i: (i, 0))
    const = lambda shape: pl.BlockSpec(shape, lambda i: (0, 0))
    return pl.pallas_call(
        functools.partial(_out_proj_kernel, alpha=alpha),
        grid=(m // tm,),
        in_specs=[row(W_DIFF), row(W_DELTA), row(d), const(w_out_b.shape), const((1, d)), const((1, d))],
        out_specs=row(d),
        out_shape=jax.ShapeDtypeStruct((m, d), F32),
        compiler_params=pltpu.CompilerParams(dimension_semantics=("arbitrary",), vmem_limit_bytes=VMEM_LIMIT),
        name="out_proj",
    )(oa, ob, x, w_out_b, ln_g.reshape(1, d), ln_b.reshape(1, d))


def _pad_lanes(vec, offset):
    return jnp.zeros((1, LANES), F32).at[0, offset:offset + vec.shape[0]].set(vec.astype(F32))


def _gate_rows_param(vec):
    col = jnp.broadcast_to(vec.astype(F32)[:, None], (H_DELTA, LANES))
    return jnp.concatenate([jnp.zeros((H_DELTA, LANES), F32), col], axis=0)


def kernel(x_prompt, x_sample, cache_k, cache_v, page_table, state_delta, state_conv, w_in, conv_w, a_log,
           dt_bias, delta_norm_w, diff_lambda, diff_norm_w, w_out, ln_g, ln_b):
    depth = w_in.shape[0]
    bp, tp, d = x_prompt.shape
    bs, ts, _ = x_sample.shape
    assert ts == 1 and w_in.shape[2] == P_MAIN + 2 * H_DELTA and d == w_out.shape[2]
    past_len = page_table.shape[1] * cache_k.shape[2]
    alpha = (2 * depth) ** 0.25
    tables_p = _rope_tables(jnp.arange(tp, dtype=jnp.int32))
    tables_s = _rope_tables(jnp.full((bs,), past_len, jnp.int32))

    hp, hs = x_prompt, x_sample
    outs = [[] for _ in range(8)]
    for l in range(depth):
        lam_init = _lambda_init(l)
        w_main = w_in[l, :, :P_MAIN].astype(BF16)
        w_tail = jnp.zeros((d, LANES), BF16).at[:, :2 * H_DELTA].set(w_in[l, :, P_MAIN:].astype(BF16))
        w_out_b = w_out[l].astype(BF16)
        alog_row = _pad_lanes(a_log[l], H_DELTA)
        dtb_row = _pad_lanes(dt_bias[l], H_DELTA)
        alog_col = _gate_rows_param(a_log[l])
        dtb_col = _gate_rows_param(dt_bias[l])

        sq, sk, _, sv, _, sga, scin, sz, stail = _in_proj(hs.reshape(1, bs, d), w_main, w_tail, tables_s, False)
        tok = lambda a: a.reshape(bs, 1, -1)
        decode_args = dict(q=tok(sq), k_new=tok(sk), v_new=tok(sv), ga=tok(sga), cache_k=cache_k, cache_v=cache_v,
                           layer=l, page_table=page_table, diff_lambda=diff_lambda[l], norm_w=diff_norm_w[l],
                           lam_init=lam_init)
        rider = _DecodeRider(**decode_args, seq0=0, n_seq=bs, grid=(bp, _delta_prep_grid(tp)[1]))
        if not rider.ok:
            rider = None

        q, k, kb, v, vb, ga, cin, z, _, tail_t = _in_proj(hp, w_main, w_tail, tables_p, True)
        oa = _prompt_attn(q, kb, vb, ga, diff_lambda[l], diff_norm_w[l], lam_init)
        prep, soa = _delta_prep(cin, jnp.zeros((bp, CONV_W - 1, CONV_CH), F32), conv_w[l], tail_t, alog_col,
                                dtb_col, rider)
        ob, sp = _delta_scan(prep, z, jnp.zeros((bp, H_DELTA, DK_DELTA, DV_DELTA), F32), delta_norm_w[l])
        hp = _out_proj(oa.reshape(bp * tp, W_DIFF), ob.reshape(bp * tp, W_DELTA), hp.reshape(bp * tp, d),
                       w_out_b, ln_g[l], ln_b[l], alpha).reshape(bp, tp, d)
        outs[0].append(k.reshape(bp, tp, H_DIFF, 2, DH_DIFF))
        outs[1].append(v.reshape(bp, tp, H_DIFF, DV_DIFF))
        outs[2].append(sp)
        outs[3].append(cin[:, tp - (CONV_W - 1):, :])

        if rider is None:
            soa = _decode_attn(decode_args, bs)
        sob, ss = _delta_step(tok(scin), state_conv[l], conv_w[l], tok(stail), alog_row, dtb_row, tok(sz),
                              state_delta[l], delta_norm_w[l])
        hs = _out_proj(soa.reshape(bs, W_DIFF), sob.reshape(bs, W_DELTA), hs.reshape(bs, d),
                       w_out_b, ln_g[l], ln_b[l], alpha).reshape(bs, 1, d)
        outs[4].append(sk.reshape(bs, 1, H_DIFF, 2, DH_DIFF))
        outs[5].append(sv.reshape(bs, 1, H_DIFF, DV_DIFF))
        outs[6].append(ss)
        outs[7].append(jnp.concatenate([state_conv[l][:, 1:, :], tok(scin)], axis=1))
    return (hp, hs) + tuple(jnp.stack(o) for o in outs)
```

```python
import functools
import math

import jax
import jax.numpy as jnp
from jax import lax
from jax.experimental import pallas as pl
from jax.experimental.pallas import tpu as pltpu

F32 = jnp.float32
BF16 = jnp.bfloat16
HIGHEST = lax.Precision.HIGHEST

H_DIFF = 4
DH_DIFF = 64
DV_DIFF = 2 * DH_DIFF
W_DIFF = H_DIFF * DV_DIFF
H_DELTA = 4
DK_DELTA = 128
DV_DELTA = 128
W_DELTA = H_DELTA * DK_DELTA
ROT_DIM = DH_DIFF // 4
ROPE_THETA = 500000.0
CONV_W = 4
CONV_CH = 3 * W_DELTA
CHUNK = 64
P_MAIN = 4 * W_DIFF + 4 * W_DELTA
LN_EPS = 1e-5
HEAD_NORM_EPS = 1e-5
DELTA_NORM_EPS = 1e-6
L2_EPS = 1e-6
NEG_INF = -1e30

LANES = 128
SUBLANES = 8
PAIR = 2 * CHUNK
CHUNK_BITS = CHUNK.bit_length() - 1
MAP_BITS = DH_DIFF.bit_length() - 1
VMEM_LIMIT = 56 * 1024 * 1024
DECODE_PAGES_PER_STEP = 16

_NT = (((1,), (1,)), ((), ()))


def _sigmoid(x):
    return 0.5 * jnp.tanh(0.5 * x) + 0.5


def _silu(x):
    return x * _sigmoid(x)


def _softplus(x):
    return jnp.maximum(x, 0.0) + jnp.log1p(jnp.exp(-jnp.abs(x)))


def _lambda_init(layer):
    return 0.8 - 0.6 * math.exp(-0.3 * layer)


def _rope_tables(pos):
    half = ROT_DIM // 2
    inv = ROPE_THETA ** (-jnp.arange(half, dtype=F32) / half)
    ang = pos.astype(F32)[:, None] * inv[None, :]
    cos, sin = jnp.cos(ang), jnp.sin(ang)
    t = pos.shape[0]
    rest = DH_DIFF - ROT_DIM
    cos_m = jnp.concatenate([cos, cos, jnp.ones((t, rest), F32)], axis=-1)
    sin_lo = jnp.concatenate([-sin, jnp.zeros((t, half + rest), F32)], axis=-1)
    sin_hi = jnp.concatenate([jnp.zeros((t, half), F32), sin, jnp.zeros((t, rest), F32)], axis=-1)
    reps = LANES // DH_DIFF
    return jnp.tile(cos_m, (1, reps)), jnp.tile(sin_lo, (1, reps)), jnp.tile(sin_hi, (1, reps))


def _in_proj_kernel(x_ref, w_ref, wt_ref, cos_ref, slo_ref, shi_ref,
                    q_ref, k_ref, kb_ref, v_ref, vb_ref, ga_ref, cin_ref, z_ref, tail_ref, *tail_t_ref):
    xb = x_ref[0].astype(BF16)
    half = ROT_DIM // 2
    tm = xb.shape[0]

    def proj(c0, width):
        return jnp.dot(xb, w_ref[:, c0:c0 + width], preferred_element_type=F32)

    def rope(h):
        blocks = []
        for c in range(W_DIFF // LANES):
            hb = h[:, c * LANES:(c + 1) * LANES]
            blocks.append(hb * cos_ref[...]
                          + pltpu.roll(hb, LANES - half, 1) * slo_ref[...]
                          + pltpu.roll(hb, half, 1) * shi_ref[...])
        return jnp.concatenate(blocks, axis=1)

    q_ref[0] = rope(proj(0, W_DIFF)).astype(BF16)
    k = rope(proj(W_DIFF, W_DIFF))
    k_ref[0] = k
    kb_ref[0] = k.astype(BF16)
    v = proj(2 * W_DIFF, W_DIFF)
    for h in range(H_DIFF):
        v_ref[0, pl.ds(h, tm, stride=H_DIFF), :] = v[:, h * DV_DIFF:(h + 1) * DV_DIFF]
    vb_ref[0] = v.astype(BF16)
    ga_ref[0] = proj(3 * W_DIFF, W_DIFF)
    for j in range(3):
        cin_ref[0, :, j * W_DELTA:(j + 1) * W_DELTA] = proj(4 * W_DIFF + j * W_DELTA, W_DELTA)
    z_ref[0] = proj(4 * W_DIFF + 3 * W_DELTA, W_DELTA)
    tail = jnp.dot(xb, wt_ref[...], preferred_element_type=F32)
    tail_ref[0] = tail
    if tail_t_ref:
        for c in range(tm // LANES):
            cs = slice(c * LANES, (c + 1) * LANES)
            tail_t_ref[0][0, :, cs] = tail[cs, :].T[:2 * H_DELTA, :]


def _in_proj(x, w_main, w_tail, tables, gates_time_major):
    b, t, d = x.shape
    tm = min(512, t)
    assert t % tm == 0 and (not gates_time_major or tm % LANES == 0)
    cos_t, slo_t, shi_t = tables
    row = lambda width: pl.BlockSpec((1, tm, width), lambda ti, bi: (bi, ti, 0))
    tab = pl.BlockSpec((tm, LANES), lambda ti, bi: (ti, 0))
    const = lambda shape: pl.BlockSpec(shape, lambda ti, bi: (0, 0))
    outs = [(W_DIFF, BF16), (W_DIFF, F32), (W_DIFF, BF16), None, (W_DIFF, BF16),
            (W_DIFF, F32), (CONV_CH, F32), (W_DELTA, F32), (LANES, F32)]
    out_specs = [row(o[0]) if o else pl.BlockSpec((1, tm * H_DIFF, DV_DIFF), lambda ti, bi: (bi, ti, 0))
                 for o in outs]
    out_shape = [jax.ShapeDtypeStruct((b, t, o[0]), o[1]) if o else
                 jax.ShapeDtypeStruct((b, t * H_DIFF, DV_DIFF), F32) for o in outs]
    if gates_time_major:
        out_specs.append(pl.BlockSpec((1, 2 * H_DELTA, tm), lambda ti, bi: (bi, 0, ti)))
        out_shape.append(jax.ShapeDtypeStruct((b, 2 * H_DELTA, t), F32))
    return pl.pallas_call(
        _in_proj_kernel,
        grid=(t // tm, b),
        in_specs=[row(d), const((d, P_MAIN)), const(w_tail.shape), tab, tab, tab],
        out_specs=out_specs,
        out_shape=out_shape,
        compiler_params=pltpu.CompilerParams(
            dimension_semantics=("arbitrary", "arbitrary"), vmem_limit_bytes=VMEM_LIMIT),
        name="in_proj",
    )(x, w_main, w_tail, cos_t, slo_t, shi_t)


def _diff_lambda_value(dl_ref, lam_init):
    dl = dl_ref[...]
    a = jnp.sum(dl[0:1] * dl[1:2], axis=1, keepdims=True)
    b = jnp.sum(dl[2:3] * dl[3:4], axis=1, keepdims=True)
    return jnp.exp(a) - jnp.exp(b) + lam_init


def _head_norm_gate(o, normw, gate, lam_init):
    ms = jnp.mean(o * o, axis=-1, keepdims=True)
    o = o * lax.rsqrt(ms + HEAD_NORM_EPS) * normw * (1.0 - lam_init)
    return o * _silu(gate)


def _prompt_attn_kernel(q0_ref, qn_ref, k_ref, v_ref, ga_ref, dl_ref, nw_ref, o_ref, s_scr, m_scr, l_scr, acc_scr,
                        *, blk, lam_init):
    i = pl.program_id(1)
    last = pl.num_programs(1) - 1
    slot = lax.rem(i, 2)
    lane = lax.broadcasted_iota(jnp.int32, (blk, DV_DIFF), 1)
    heads = [slice(h * DV_DIFF, (h + 1) * DV_DIFF) for h in range(H_DIFF)]
    lane_chunks = [slice(c * LANES, (c + 1) * LANES) for c in range(blk // LANES)]
    n_maps = 2 * H_DIFF

    def query_maps(q_ref):
        maps = []
        for hs in heads:
            qs = q_ref[0, :, hs].astype(F32) * (DH_DIFF ** -0.5 * math.log2(math.e))
            maps.append(jnp.where(lane < DH_DIFF, qs, 0.0).astype(BF16))
            maps.append(jnp.where(lane >= DH_DIFF, qs, 0.0).astype(BF16))
        return maps

    def fold(x, op):
        r = x[:, lane_chunks[0]]
        for c in lane_chunks[1:]:
            r = op(r, x[:, c])
        return r

    def score_block(q_maps, dst, j, masked):
        rows = pl.ds(pl.multiple_of(j * blk, blk), blk)
        for h, hs in enumerate(heads):
            k = k_ref[0, rows, hs]
            for mi in (2 * h, 2 * h + 1):
                s = lax.dot_general(q_maps[mi], k, _NT, preferred_element_type=F32)
                if masked:
                    row = lax.broadcasted_iota(jnp.int32, (blk, blk), 0)
                    col = lax.broadcasted_iota(jnp.int32, (blk, blk), 1)
                    s = jnp.where(col <= row, s, NEG_INF)
                s_scr[dst, j, mi] = s
                m_scr[dst, mi] = jnp.maximum(m_scr[dst, mi], fold(s, jnp.maximum))

    def finish_max(dst):
        for mi in range(n_maps):
            m_scr[dst, mi] = jnp.broadcast_to(jnp.max(m_scr[dst, mi], axis=-1, keepdims=True), (blk, LANES))

    def prob_block(j):
        rows = pl.ds(pl.multiple_of(j * blk, blk), blk)
        for h, hs in enumerate(heads):
            v = v_ref[0, rows, hs]
            for mi in (2 * h, 2 * h + 1):
                m = m_scr[slot, mi]
                p = jnp.concatenate([jnp.exp2(s_scr[slot, j, mi, :, c] - m) for c in lane_chunks], axis=1)
                l_scr[mi] = l_scr[mi] + fold(p, jnp.add)
                acc_scr[mi] = acc_scr[mi] + jnp.dot(p.astype(BF16), v, preferred_element_type=F32)

    def write_output():
        lam = _diff_lambda_value(dl_ref, lam_init)
        for h, hs in enumerate(heads):
            l1 = jnp.sum(l_scr[2 * h], axis=-1, keepdims=True)
            l2 = jnp.sum(l_scr[2 * h + 1], axis=-1, keepdims=True)
            o = acc_scr[2 * h] / l1 - lam * (acc_scr[2 * h + 1] / l2)
            o_ref[0, :, hs] = _head_norm_gate(o, nw_ref[...], ga_ref[0, :, hs], lam_init).astype(BF16)

    @pl.when(i == 0)
    def _():
        m_scr[0] = jnp.full(m_scr.shape[1:], NEG_INF, F32)
        score_block(query_maps(q0_ref), 0, 0, True)
        finish_max(0)

    l_scr[...] = jnp.zeros(l_scr.shape, F32)
    acc_scr[...] = jnp.zeros(acc_scr.shape, F32)

    @pl.when(i < last)
    def _():
        nxt = 1 - slot
        q_next = query_maps(qn_ref)
        m_scr[nxt] = jnp.full(m_scr.shape[1:], NEG_INF, F32)

        def both(j, carry):
            prob_block(j)
            score_block(q_next, nxt, j, False)
            return carry

        lax.fori_loop(0, i + 1, both, 0)
        write_output()
        score_block(q_next, nxt, i + 1, True)
        finish_max(nxt)

    @pl.when(i == last)
    def _():
        def only_probs(j, carry):
            prob_block(j)
            return carry

        lax.fori_loop(0, i + 1, only_probs, 0)
        write_output()


def _prompt_attn(q, kb, vb, ga, diff_lambda, norm_w, lam_init):
    b, t, _ = q.shape
    blk = min(256, t)
    assert t % blk == 0
    n_maps = 2 * H_DIFF
    nq = t // blk
    qspec = pl.BlockSpec((1, blk, W_DIFF), lambda bi, i: (bi, i, 0))
    q_first = pl.BlockSpec((1, blk, W_DIFF), lambda bi, i: (bi, 0, 0))
    q_next = pl.BlockSpec((1, blk, W_DIFF), lambda bi, i: (bi, jnp.minimum(i + 1, nq - 1), 0))
    kvspec = pl.BlockSpec((1, t, W_DIFF), lambda bi, i: (bi, 0, 0))
    return pl.pallas_call(
        functools.partial(_prompt_attn_kernel, blk=blk, lam_init=lam_init),
        grid=(b, nq),
        in_specs=[q_first, q_next, kvspec, kvspec, qspec,
                  pl.BlockSpec(diff_lambda.shape, lambda bi, i: (0, 0)),
                  pl.BlockSpec((1, DV_DIFF), lambda bi, i: (0, 0))],
        out_specs=qspec,
        out_shape=jax.ShapeDtypeStruct((b, t, W_DIFF), BF16),
        scratch_shapes=[pltpu.VMEM((2, nq, n_maps, blk, blk), F32),
                        pltpu.VMEM((2, n_maps, blk, LANES), F32),
                        pltpu.VMEM((n_maps, blk, LANES), F32),
                        pltpu.VMEM((n_maps, blk, DV_DIFF), F32)],
        compiler_params=pltpu.CompilerParams(
            dimension_semantics=("arbitrary", "arbitrary"), vmem_limit_bytes=VMEM_LIMIT),
        name="prompt_attn",
    )(q, q, kb, vb, ga, diff_lambda, norm_w.reshape(1, DV_DIFF))


class _DecodeStep:
    def __init__(self, step, n_steps, spq, first_page, pages, pt_ref, in_refs, o_ref, scratch_refs, lam_init):
        self.step, self.n_steps, self.pages, self.first_page = step, n_steps, pages, first_page
        self.g, self.n_g, self.lam_init, self.pt_ref = lax.rem(step, spq), spq, lam_init, pt_ref
        (self.q_ref, self.kn_ref, self.vn_ref, self.ga_ref, self.dl_ref, self.nw_ref,
         self.ck_ref, self.cv_ref) = in_refs
        self.o_ref = o_ref
        (self.qcol_ref, self.m_ref, self.l_ref, self.acc_ref,
         self.kbuf_ref, self.vbuf_ref, self.sem_ref) = scratch_refs
        self.slot = lax.rem(step, 2)

    def _query(self):
        return self.q_ref[0].astype(F32) * DH_DIFF ** -0.5

    def _page_copies(self, step, slot):
        base = self.first_page + step * self.pages
        copies = []
        for i in range(self.pages):
            pid = self.pt_ref[base + i]
            copies.append(pltpu.make_async_copy(self.ck_ref.at[pid], self.kbuf_ref.at[slot, i],
                                                self.sem_ref.at[slot, 0]))
            copies.append(pltpu.make_async_copy(self.cv_ref.at[pid], self.vbuf_ref.at[slot, i],
                                                self.sem_ref.at[slot, 1]))
        return copies

    def init(self):
        @pl.when(self.step == 0)
        def _():
            for cp in self._page_copies(0, 0):
                cp.start()

        for cp in self._page_copies(self.step, self.slot):
            cp.wait()

        @pl.when(self.step + 1 < self.n_steps)
        def _():
            for cp in self._page_copies(self.step + 1, 1 - self.slot):
                cp.start()

        @pl.when(self.g == 0)
        def _():
            self.m_ref[...] = jnp.full(self.m_ref.shape, NEG_INF, F32)
            self.l_ref[...] = jnp.zeros(self.l_ref.shape, F32)
            self.acc_ref[...] = jnp.zeros(self.acc_ref.shape, F32)
            qs = self._query()
            for c in range(W_DIFF // LANES):
                cs = slice(c * LANES, (c + 1) * LANES)
                self.qcol_ref[cs, :] = jnp.broadcast_to(qs[:, cs], (LANES, LANES)).T

    def main(self):
        k_pages = [self.kbuf_ref.at[self.slot, i] for i in range(self.pages)]
        v_pages = [self.vbuf_ref.at[self.slot, i] for i in range(self.pages)]
        _decode_pages(k_pages, v_pages, self.qcol_ref, self.m_ref, self.l_ref, self.acc_ref)

    def finalize(self):
        @pl.when(self.g == self.n_g - 1)
        def _():
            _decode_finish(self._query(), self.kn_ref, self.vn_ref, self.ga_ref, self.dl_ref, self.nw_ref,
                           self.o_ref, self.m_ref, self.l_ref, self.acc_ref, self.lam_init)


def _decode_pages(kp_refs, vp_refs, qcol_ref, m_ref, l_ref, acc_ref):
    n_maps = 2 * H_DIFF
    page = kp_refs[0].shape[1]
    qcol = qcol_ref[...]
    s = jnp.concatenate(
        [jnp.sum((kp[...] * qcol).reshape(n_maps, DH_DIFF, page), axis=1) for kp in kp_refs], axis=1)
    m_prev = m_ref[:, 0:1]
    m_new = jnp.maximum(m_prev, jnp.max(s, axis=-1, keepdims=True))
    p = jnp.exp(s - m_new)
    alpha = jnp.exp(m_prev - m_new)
    l_new = alpha * l_ref[:, 0:1] + jnp.sum(p, axis=-1, keepdims=True)
    pb = p.astype(BF16)
    for h in range(H_DIFF):
        pv = jnp.zeros((n_maps, DV_DIFF), F32)
        for i, vp in enumerate(vp_refs):
            v_head = vp[pl.ds(h, page, stride=H_DIFF), :].astype(BF16)
            pv = pv + jnp.dot(pb[:, i * page:(i + 1) * page], v_head, preferred_element_type=F32)
        acc_ref[h] = alpha * acc_ref[h] + pv
    m_ref[...] = jnp.broadcast_to(m_new, m_ref.shape)
    l_ref[...] = jnp.broadcast_to(l_new, l_ref.shape)


def _decode_finish(qs, kn_ref, vn_ref, ga_ref, dl_ref, nw_ref, o_ref, m_ref, l_ref, acc_ref, lam_init):
    n_maps = 2 * H_DIFF
    rows = lax.broadcasted_iota(jnp.int32, (n_maps, W_DIFF), 0)
    lanes = lax.broadcasted_iota(jnp.int32, (n_maps, W_DIFF), 1)
    q_blk = jnp.where((lanes >> MAP_BITS) == rows, jnp.broadcast_to(qs, (n_maps, W_DIFF)), 0.0).astype(BF16)
    k_self = jnp.broadcast_to(kn_ref[0].astype(BF16), (SUBLANES, W_DIFF))
    s_self = lax.dot_general(q_blk, k_self, _NT, preferred_element_type=F32)[:, 0:1]
    m_past = m_ref[:, 0:1]
    m_f = jnp.maximum(m_past, s_self)
    a_f = jnp.exp(m_past - m_f)
    p_self = jnp.exp(s_self - m_f)
    l_f = a_f * l_ref[:, 0:1] + p_self
    pv_self = p_self.astype(BF16).astype(F32) * vn_ref[0].astype(BF16).astype(F32)
    lam = _diff_lambda_value(dl_ref, lam_init)
    for h in range(H_DIFF):
        hs = slice(h * DV_DIFF, (h + 1) * DV_DIFF)
        a = (a_f * acc_ref[h] + pv_self[:, hs]) / l_f
        o = a[2 * h:2 * h + 1] - lam * a[2 * h + 1:2 * h + 2]
        o_ref[0, :, hs] = _head_norm_gate(o, nw_ref[...], ga_ref[0, :, hs], lam_init).astype(BF16)


class _DecodeRider:
    def __init__(self, q, k_new, v_new, ga, cache_k, cache_v, layer, page_table, diff_lambda, norm_w, lam_init,
                 seq0, n_seq, grid):
        depth, n_phys, page = cache_k.shape[:3]
        n_pages = page_table.shape[1]
        n_steps = grid[0] * grid[1]
        assert page == LANES
        pages = n_seq * n_pages // n_steps
        self.ok = pages >= 1 and pages * n_steps == n_seq * n_pages and n_pages % pages == 0
        if not self.ok:
            return
        spq = n_pages // pages
        self.pages, self.spq, self.lam_init, self.grid = pages, spq, lam_init, grid
        self.first_page = seq0 * n_pages
        ck = jnp.transpose(cache_k, (0, 1, 3, 4, 5, 2)).reshape(depth * n_phys, W_DIFF, page)
        cv = cache_v.reshape(depth * n_phys, page * H_DIFF, DV_DIFF)
        self.page_ids = page_table.reshape(-1) + layer * n_phys
        step = lambda i0, i1: i0 * grid[1] + i1
        tok = pl.BlockSpec((1, 1, W_DIFF), lambda i0, i1, pt: (seq0 + step(i0, i1) // spq, 0, 0))
        const = lambda shape: pl.BlockSpec(shape, lambda i0, i1, pt: (0, 0))
        hbm = pl.BlockSpec(memory_space=pl.ANY)
        self.inputs = [q, k_new, v_new, ga, diff_lambda, norm_w.reshape(1, DV_DIFF), ck, cv]
        self.in_specs = [tok, tok, tok, tok, const(diff_lambda.shape), const((1, DV_DIFF)), hbm, hbm]
        self.out_spec = pl.BlockSpec((1, 1, W_DIFF), lambda i0, i1, pt: (step(i0, i1) // spq, 0, 0))
        self.out_shape = jax.ShapeDtypeStruct((n_seq, 1, W_DIFF), BF16)
        n_maps = 2 * H_DIFF
        self.scratch_shapes = [pltpu.VMEM((W_DIFF, page), F32),
                               pltpu.VMEM((n_maps, LANES), F32), pltpu.VMEM((n_maps, LANES), F32),
                               pltpu.VMEM((H_DIFF, n_maps, DV_DIFF), F32),
                               pltpu.VMEM((2, pages, W_DIFF, page), F32),
                               pltpu.VMEM((2, pages, page * H_DIFF, DV_DIFF), F32),
                               pltpu.SemaphoreType.DMA((2, 2))]

    def step(self, pt_ref, in_refs, out_ref, scratch_refs):
        step = pl.program_id(0) * self.grid[1] + pl.program_id(1)
        return _DecodeStep(step, self.grid[0] * self.grid[1], self.spq, self.first_page, self.pages, pt_ref,
                           in_refs, out_ref, scratch_refs, self.lam_init)


def _ride(host_kernel, n_in, n_out, n_scratch, rider):
    n_rin = len(rider.inputs)

    def kernel(pt_ref, *refs):
        host_in, refs = refs[:n_in], refs[n_in:]
        rider_in, refs = refs[:n_rin], refs[n_rin:]
        host_out, refs = refs[:n_out], refs[n_out:]
        rider_out, refs = refs[0], refs[1:]
        host_scratch, rider_scratch = refs[:n_scratch], refs[n_scratch:]
        host_kernel(*host_in, *host_out, *host_scratch,
                    decode=rider.step(pt_ref, rider_in, rider_out, rider_scratch))

    return kernel


def _hosted_call(host_kernel, grid, in_specs, out_specs, out_shape, scratch_shapes, inputs, name, rider):
    params = pltpu.CompilerParams(dimension_semantics=("arbitrary", "arbitrary"), vmem_limit_bytes=VMEM_LIMIT)
    if rider is None:
        outs = pl.pallas_call(host_kernel, grid=grid, in_specs=in_specs, out_specs=out_specs, out_shape=out_shape,
                              scratch_shapes=scratch_shapes, compiler_params=params, name=name)(*inputs)
        return outs, None
    grid_spec = pltpu.PrefetchScalarGridSpec(
        num_scalar_prefetch=1, grid=grid,
        in_specs=list(in_specs) + rider.in_specs,
        out_specs=list(out_specs) + [rider.out_spec],
        scratch_shapes=list(scratch_shapes) + rider.scratch_shapes)
    outs = pl.pallas_call(
        _ride(host_kernel, len(in_specs), len(out_specs), len(scratch_shapes), rider),
        grid_spec=grid_spec, out_shape=list(out_shape) + [rider.out_shape],
        compiler_params=params, name=name)(rider.page_ids, *inputs, *rider.inputs)
    return outs[:-1], outs[-1]


def _decode_attn_kernel(pt_ref, *refs, rider):
    n_rin = len(rider.inputs)
    decode = rider.step(pt_ref, refs[:n_rin], refs[n_rin], refs[n_rin + 1:])
    decode.init()
    decode.main()
    decode.finalize()


def _decode_attn(rider_args, n_seq):
    n_pages = rider_args["page_table"].shape[1]
    grid = (n_seq, n_pages // math.gcd(DECODE_PAGES_PER_STEP, n_pages))
    rider = _DecodeRider(**rider_args, seq0=0, n_seq=n_seq, grid=grid)
    grid_spec = pltpu.PrefetchScalarGridSpec(
        num_scalar_prefetch=1, grid=grid, in_specs=rider.in_specs,
        out_specs=rider.out_spec, scratch_shapes=rider.scratch_shapes)
    return pl.pallas_call(
        functools.partial(_decode_attn_kernel, rider=rider), grid_spec=grid_spec, out_shape=rider.out_shape,
        compiler_params=pltpu.CompilerParams(
            dimension_semantics=("arbitrary", "arbitrary"), vmem_limit_bytes=VMEM_LIMIT),
        name="decode_attn")(rider.page_ids, *rider.inputs)


def _gate_rows(tail, alog_row, dtb_row):
    beta = _sigmoid(tail)
    g = -jnp.exp(alog_row) * _softplus(tail + dtb_row)
    return beta, g


def _lane_bcast(x, lane, rows):
    return jnp.broadcast_to(x[:, lane:lane + 1], (rows, LANES))


def _l2norm(x):
    return x * lax.rsqrt(jnp.sum(x * x, axis=-1, keepdims=True) + L2_EPS)


def _split_bf16(x):
    hi = x.astype(BF16)
    return hi, (x - hi.astype(F32)).astype(BF16)


def _dot_split(lhs, rhs):
    d = lambda a, b: jnp.dot(a, b, preferred_element_type=F32)
    return d(lhs[0], rhs[0]) + d(lhs[0], rhs[1]) + d(lhs[1], rhs[0])


def _unit_lower_inverses(a_mats):
    n = a_mats[0].shape[0]
    eye = (lax.broadcasted_iota(jnp.int32, (n, n), 0) == lax.broadcasted_iota(jnp.int32, (n, n), 1)).astype(F32)
    xs = [-a for a in a_mats]
    ps = [eye + x for x in xs]
    splits = [_split_bf16(x) for x in xs]
    xs = [_dot_split(s, s) for s in splits]
    power = 2
    while 2 * power < CHUNK:
        both = [_dot_split(_split_bf16(x), _split_bf16(jnp.concatenate([p, x], axis=1))) for p, x in zip(ps, xs)]
        ps = [p + b[:, :n] for p, b in zip(ps, both)]
        xs = [b[:, n:] for b in both]
        power *= 2
    return [p + _dot_split(_split_bf16(x), _split_bf16(p)) for p, x in zip(ps, xs)]


def _delta_prep_kernel(cin_ref, prev_ref, cw_ref, tail_ref, alog_ref, dtb_ref,
                       u_ref, w_ref, qg_ref, at_ref, kdt_ref, el_ref, ext_ref, *, pairs, decode=None):
    t = pl.program_id(1)
    pad = SUBLANES
    keep = CONV_W - 1
    rows = pairs * PAIR

    @pl.when(t == 0)
    def _():
        ext_ref[pad - keep:pad, :] = prev_ref[0]

    @pl.when(t > 0)
    def _():
        ext_ref[pad - keep:pad, :] = ext_ref[pad + rows - keep:pad + rows, :]

    if decode is not None:
        decode.init()

    ext_ref[pad:pad + rows, :] = cin_ref[0]

    r = lax.broadcasted_iota(jnp.int32, (PAIR, PAIR), 0)
    c = lax.broadcasted_iota(jnp.int32, (PAIR, PAIR), 1)
    same = (r >> CHUNK_BITS) == (c >> CHUNK_BITS)
    incl = same & (c <= r)
    strict = same & (c < r)
    incl_t = same & (r <= c)
    chunk_end = r == (c | (CHUNK - 1))

    chains = []
    for pi in range(pairs):
        base = pad - keep + pi * PAIR
        conv = ext_ref[base:base + PAIR, :] * cw_ref[0:1, :]
        for j in range(1, CONV_W):
            conv = conv + ext_ref[base + j:base + j + PAIR, :] * cw_ref[j:j + 1, :]
        conv = _silu(conv)
        rs = slice(pi * PAIR, (pi + 1) * PAIR)
        beta_t, g_t = _gate_rows(tail_ref[0, :, rs], alog_ref[...], dtb_ref[...])
        gc_t = jnp.dot(g_t, incl_t.astype(F32), precision=HIGHEST, preferred_element_type=F32)
        g_last_t = jnp.dot(gc_t, chunk_end.astype(F32), precision=HIGHEST, preferred_element_type=F32)
        for h in range(H_DELTA):
            hs = slice(h * DK_DELTA, (h + 1) * DK_DELTA)
            qs = _l2norm(conv[:, hs]) * DK_DELTA ** -0.5
            kn = _l2norm(conv[:, W_DELTA + h * DK_DELTA:W_DELTA + (h + 1) * DK_DELTA])
            vh = conv[:, 2 * W_DELTA + h * DV_DELTA:2 * W_DELTA + (h + 1) * DV_DELTA]
            beta = jnp.broadcast_to(beta_t[h:h + 1, :], (PAIR, PAIR)).T
            gc_cols = jnp.broadcast_to(gc_t[H_DELTA + h:H_DELTA + h + 1, :], (PAIR, PAIR))
            g_last_cols = jnp.broadcast_to(g_last_t[H_DELTA + h:H_DELTA + h + 1, :], (PAIR, PAIR))
            chains.append(dict(pi=pi, h=h, rs=rs, qs=qs, kn=kn, vh=vh, beta=beta, gc=gc_cols.T, gc_cols=gc_cols,
                               g_last_cols=g_last_cols, kbeta=kn * beta))

    for ch in chains:
        ch["kn_t"] = ch["kn"].T
        kn_tb = ch["kn_t"].astype(BF16)
        ch["kk"] = jnp.dot(ch["kbeta"].astype(BF16), kn_tb, preferred_element_type=F32)
        ch["qk"] = jnp.dot(ch["qs"].astype(BF16), kn_tb, preferred_element_type=F32)
    for ch in chains:
        ch["decay"] = jnp.exp(jnp.where(incl, ch["gc"] - ch["gc_cols"], -jnp.inf))
    t_mats = _unit_lower_inverses([jnp.where(strict, ch["kk"] * ch["decay"], 0.0) for ch in chains])
    for ch, t_mat in zip(chains, t_mats):
        rhs = jnp.concatenate([ch["vh"] * ch["beta"], ch["kbeta"] * jnp.exp(ch["gc"])], axis=1).astype(BF16)
        ch["uw"] = jnp.dot(t_mat.astype(BF16), rhs, preferred_element_type=F32)
    for ch in chains:
        pi, h, rs, gc = ch["pi"], ch["h"], ch["rs"], ch["gc"]
        u_ref[0, h, rs, :] = ch["uw"][:, :DV_DELTA]
        w_ref[0, h, rs, :] = ch["uw"][:, DV_DELTA:].astype(BF16)
        attn = jnp.where(incl, ch["qk"] * ch["decay"], 0.0)
        at_ref[0, h, rs, :] = jnp.concatenate([attn[:CHUNK, :CHUNK], attn[CHUNK:, CHUNK:]], axis=0).astype(BF16)
        qg_ref[0, h, rs, :] = (ch["qs"] * jnp.exp(gc)).astype(BF16)
        kdt_ref[0, h, pi] = (ch["kn_t"] * jnp.exp(ch["g_last_cols"] - ch["gc_cols"])).astype(BF16)
        el_ref[0, h, pi] = jnp.exp(jnp.concatenate([jnp.broadcast_to(gc[CHUNK - 1:CHUNK], (SUBLANES, LANES)),
                                                    jnp.broadcast_to(gc[PAIR - 1:PAIR], (SUBLANES, LANES))], axis=0))

    if decode is not None:
        decode.main()
        decode.finalize()


def _delta_prep_grid(t):
    assert t % PAIR == 0
    npair = t // PAIR
    pairs = 2 if npair % 2 == 0 else 1
    return pairs, npair // pairs


def _delta_prep(cin, conv_prev, conv_w, tail, alog_row, dtb_row, rider):
    b, t, _ = cin.shape
    npair = t // PAIR
    pairs, nt = _delta_prep_grid(t)
    rows = pairs * PAIR
    per_head = lambda width: pl.BlockSpec((1, H_DELTA, rows, width), lambda bi, ti, *_: (bi, 0, ti, 0))
    const = lambda shape: pl.BlockSpec(shape, lambda bi, ti, *_: (0,) * len(shape))
    out_shape = [
        jax.ShapeDtypeStruct((b, H_DELTA, t, DV_DELTA), F32),
        jax.ShapeDtypeStruct((b, H_DELTA, t, DK_DELTA), BF16),
        jax.ShapeDtypeStruct((b, H_DELTA, t, DK_DELTA), BF16),
        jax.ShapeDtypeStruct((b, H_DELTA, t, CHUNK), BF16),
        jax.ShapeDtypeStruct((b, H_DELTA, npair, DK_DELTA, PAIR), BF16),
        jax.ShapeDtypeStruct((b, H_DELTA, npair, 2 * SUBLANES, LANES), F32),
    ]
    out_specs = [per_head(DV_DELTA), per_head(DK_DELTA), per_head(DK_DELTA), per_head(CHUNK),
                 pl.BlockSpec((1, H_DELTA, pairs, DK_DELTA, PAIR), lambda bi, ti, *_: (bi, 0, ti, 0, 0)),
                 pl.BlockSpec((1, H_DELTA, pairs, 2 * SUBLANES, LANES), lambda bi, ti, *_: (bi, 0, ti, 0, 0))]
    return _hosted_call(
        functools.partial(_delta_prep_kernel, pairs=pairs),
        grid=(b, nt),
        in_specs=[pl.BlockSpec((1, rows, CONV_CH), lambda bi, ti, *_: (bi, ti, 0)),
                  pl.BlockSpec((1, CONV_W - 1, CONV_CH), lambda bi, ti, *_: (bi, 0, 0)),
                  const(conv_w.shape),
                  pl.BlockSpec((1, 2 * H_DELTA, rows), lambda bi, ti, *_: (bi, 0, ti)),
                  const(alog_row.shape), const(dtb_row.shape)],
        out_specs=out_specs,
        out_shape=out_shape,
        scratch_shapes=[pltpu.VMEM((SUBLANES + rows, CONV_CH), F32)],
        inputs=(cin, conv_prev, conv_w, tail, alog_row, dtb_row),
        name="delta_prep", rider=rider)


def _delta_norm_gate(o, normw, z):
    ms = jnp.mean(o * o, axis=-1, keepdims=True)
    return o * lax.rsqrt(ms + DELTA_NORM_EPS) * normw * _silu(z)


def _delta_scan_kernel(u_ref, w_ref, qg_ref, at_ref, kdt_ref, el_ref, z_ref, s0_ref, nw_ref,
                       o_ref, sfin_ref, s_scr, *, nb, nblk):
    t = pl.program_id(1)

    @pl.when(t == 0)
    def _():
        s_scr[...] = s0_ref[...]

    def block(jb, carry):
        r0 = pl.multiple_of(jb * PAIR, PAIR)
        seqs = [(bi, h) for bi in range(nb) for h in range(H_DELTA)]
        for ci in range(2):
            rows = pl.ds(pl.multiple_of(r0 + ci * CHUNK, CHUNK), CHUNK)
            states = [s_scr[bi, h] for bi, h in seqs]
            res = [jnp.dot(jnp.concatenate([w_ref[bi, h, rows, :], qg_ref[bi, h, rows, :]], axis=0),
                           s.astype(BF16), preferred_element_type=F32) for (bi, h), s in zip(seqs, states)]
            v_new = [(u_ref[bi, h, rows, :] - r[:CHUNK]).astype(BF16) for (bi, h), r in zip(seqs, res)]
            upd = [jnp.dot(kdt_ref[bi, h, jb, :, ci * CHUNK:(ci + 1) * CHUNK], v, preferred_element_type=F32)
                   for (bi, h), v in zip(seqs, v_new)]
            intra = [jnp.dot(at_ref[bi, h, rows, :], v, preferred_element_type=F32) for (bi, h), v in zip(seqs, v_new)]
            for (bi, h), s, r, du, oi in zip(seqs, states, res, upd, intra):
                decay = jnp.broadcast_to(el_ref[bi, h, jb, ci * SUBLANES:ci * SUBLANES + 1, :], (DK_DELTA, DV_DELTA))
                s_scr[bi, h] = s * decay + du
                hs = slice(h * DV_DELTA, (h + 1) * DV_DELTA)
                o_ref[bi, rows, hs] = _delta_norm_gate(r[CHUNK:] + oi, nw_ref[...], z_ref[bi, rows, hs]).astype(BF16)
        return carry

    lax.fori_loop(0, nblk, block, 0)

    @pl.when(t == pl.num_programs(1) - 1)
    def _():
        sfin_ref[...] = s_scr[...]


def _delta_scan(prep, z, s0, norm_w):
    u, w, qg, at, kdt, el = prep
    b, _, t, _ = u.shape
    nb = 2 if b % 2 == 0 else 1
    ts = min(512, t)
    assert t % ts == 0
    nblk = ts // PAIR
    per_head = lambda width: pl.BlockSpec((nb, H_DELTA, ts, width), lambda bi, ti: (bi, 0, ti, 0))
    state = pl.BlockSpec((nb, H_DELTA, DK_DELTA, DV_DELTA), lambda bi, ti: (bi, 0, 0, 0))
    tok = pl.BlockSpec((nb, ts, W_DELTA), lambda bi, ti: (bi, ti, 0))
    return pl.pallas_call(
        functools.partial(_delta_scan_kernel, nb=nb, nblk=nblk),
        grid=(b // nb, t // ts),
        in_specs=[per_head(DV_DELTA), per_head(DK_DELTA), per_head(DK_DELTA), per_head(CHUNK),
                  pl.BlockSpec((nb, H_DELTA, nblk, DK_DELTA, PAIR), lambda bi, ti: (bi, 0, ti, 0, 0)),
                  pl.BlockSpec((nb, H_DELTA, nblk, 2 * SUBLANES, LANES), lambda bi, ti: (bi, 0, ti, 0, 0)),
                  tok, state, pl.BlockSpec((1, DV_DELTA), lambda bi, ti: (0, 0))],
        out_specs=[tok, state],
        out_shape=[jax.ShapeDtypeStruct((b, t, W_DELTA), BF16),
                   jax.ShapeDtypeStruct((b, H_DELTA, DK_DELTA, DV_DELTA), F32)],
        scratch_shapes=[pltpu.VMEM((nb, H_DELTA, DK_DELTA, DV_DELTA), F32)],
        compiler_params=pltpu.CompilerParams(
            dimension_semantics=("arbitrary", "arbitrary"), vmem_limit_bytes=VMEM_LIMIT),
        name="delta_scan",
    )(u, w, qg, at, kdt, el, z, s0, norm_w.reshape(1, DV_DELTA))


def _delta_step_kernel(cin_ref, prev_ref, cw_ref, tail_ref, alog_ref, dtb_ref, z_ref, s0_ref, nw_ref,
                       o_ref, s_ref):
    prev = prev_ref[0]
    conv = prev[0:1] * cw_ref[0:1, :]
    for j in range(1, CONV_W - 1):
        conv = conv + prev[j:j + 1] * cw_ref[j:j + 1, :]
    conv = _silu(conv + cin_ref[0] * cw_ref[CONV_W - 1:CONV_W, :])
    beta_all, g_all = _gate_rows(tail_ref[0], alog_ref[...], dtb_ref[...])
    for h in range(H_DELTA):
        hs = slice(h * DK_DELTA, (h + 1) * DK_DELTA)
        qs = _l2norm(conv[:, hs]) * DK_DELTA ** -0.5
        kn = _l2norm(conv[:, W_DELTA + h * DK_DELTA:W_DELTA + (h + 1) * DK_DELTA])
        vh = conv[:, 2 * W_DELTA + h * DV_DELTA:2 * W_DELTA + (h + 1) * DV_DELTA]
        beta = beta_all[:, h:h + 1]
        eg = jnp.exp(g_all[:, H_DELTA + h:H_DELTA + h + 1])
        s = s0_ref[0, h]
        lhs = jnp.concatenate([kn * (beta * eg), qs * eg, jnp.zeros((SUBLANES - 2, DK_DELTA), F32)],
                              axis=0).astype(BF16)
        res = jnp.dot(lhs, s.astype(BF16), preferred_element_type=F32)
        v_new = vh * beta - res[0:1]
        qk = jnp.sum(qs.astype(BF16).astype(F32) * kn.astype(BF16).astype(F32), axis=-1, keepdims=True)
        o = res[1:2] + qk * v_new
        k_col = jnp.broadcast_to(kn, (DK_DELTA, DK_DELTA)).T
        s_ref[0, h] = s * eg + k_col * v_new
        o_ref[0, :, hs] = _delta_norm_gate(o, nw_ref[...], z_ref[0, :, hs]).astype(BF16)


def _delta_step(cin, conv_prev, conv_w, tail, alog_row, dtb_row, z, s0, norm_w):
    bs = cin.shape[0]
    tok = lambda width: pl.BlockSpec((1, 1, width), lambda b: (b, 0, 0))
    const = lambda shape: pl.BlockSpec(shape, lambda b: (0,) * len(shape))
    state = pl.BlockSpec((1, H_DELTA, DK_DELTA, DV_DELTA), lambda b: (b, 0, 0, 0))
    return pl.pallas_call(
        _delta_step_kernel,
        grid=(bs,),
        in_specs=[tok(CONV_CH), pl.BlockSpec((1, CONV_W - 1, CONV_CH), lambda b: (b, 0, 0)), const(conv_w.shape),
                  tok(LANES), const(alog_row.shape), const(dtb_row.shape), tok(W_DELTA), state,
                  const((1, DV_DELTA))],
        out_specs=[tok(W_DELTA), state],
        out_shape=[jax.ShapeDtypeStruct((bs, 1, W_DELTA), BF16),
                   jax.ShapeDtypeStruct((bs, H_DELTA, DK_DELTA, DV_DELTA), F32)],
        compiler_params=pltpu.CompilerParams(dimension_semantics=("arbitrary",)),
        name="delta_step",
    )(cin, conv_prev, conv_w, tail, alog_row, dtb_row, z, s0, norm_w.reshape(1, DV_DELTA))


def _out_proj_kernel(oa_ref, ob_ref, x_ref, w_ref, g_ref, b_ref, y_ref, *, alpha):
    mix = jnp.dot(oa_ref[...], w_ref[:W_DIFF, :], preferred_element_type=F32)
    mix = mix + jnp.dot(ob_ref[...], w_ref[W_DIFF:, :], preferred_element_type=F32)
    r = alpha * x_ref[...] + mix
    mu = jnp.mean(r, axis=-1, keepdims=True)
    var = jnp.mean(jnp.square(r - mu), axis=-1, keepdims=True)
    y_ref[...] = (r - mu) * lax.rsqrt(var + LN_EPS) * g_ref[...] + b_ref[...]


def _out_proj(oa, ob, x, w_out_b, ln_g, ln_b, alpha):
    m, d = x.shape
    tm = min(1024, m)
    assert m % tm == 0
    row = lambda width: pl.BlockSpec((tm, width), lambda i: (i, 0))
    const = lambda shape: pl.BlockSpec(shape, lambda i: (0, 0))
    return pl.pallas_call(
        functools.partial(_out_proj_kernel, alpha=alpha),
        grid=(m // tm,),
        in_specs=[row(W_DIFF), row(W_DELTA), row(d), const(w_out_b.shape), const((1, d)), const((1, d))],
        out_specs=row(d),
        out_shape=jax.ShapeDtypeStruct((m, d), F32),
        compiler_params=pltpu.CompilerParams(dimension_semantics=("arbitrary",), vmem_limit_bytes=VMEM_LIMIT),
        name="out_proj",
    )(oa, ob, x, w_out_b, ln_g.reshape(1, d), ln_b.reshape(1, d))


def _pad_lanes(vec, offset):
    return jnp.zeros((1, LANES), F32).at[0, offset:offset + vec.shape[0]].set(vec.astype(F32))


def _gate_rows_param(vec):
    col = jnp.broadcast_to(vec.astype(F32)[:, None], (H_DELTA, LANES))
    return jnp.concatenate([jnp.zeros((H_DELTA, LANES), F32), col], axis=0)


def kernel(x_prompt, x_sample, cache_k, cache_v, page_table, state_delta, state_conv, w_in, conv_w, a_log,
           dt_bias, delta_norm_w, diff_lambda, diff_norm_w, w_out, ln_g, ln_b):
    depth = w_in.shape[0]
    bp, tp, d = x_prompt.shape
    bs, ts, _ = x_sample.shape
    assert ts == 1 and w_in.shape[2] == P_MAIN + 2 * H_DELTA and d == w_out.shape[2]
    past_len = page_table.shape[1] * cache_k.shape[2]
    alpha = (2 * depth) ** 0.25
    tables_p = _rope_tables(jnp.arange(tp, dtype=jnp.int32))
    tables_s = _rope_tables(jnp.full((bs,), past_len, jnp.int32))

    hp, hs = x_prompt, x_sample
    outs = [[] for _ in range(8)]
    for l in range(depth):
        lam_init = _lambda_init(l)
        w_main = w_in[l].astype(BF16)
        w_tail = jnp.zeros((d, LANES), BF16).at[:, :2 * H_DELTA].set(w_in[l, :, P_MAIN:].astype(BF16))
        w_out_b = w_out[l].astype(BF16)
        alog_row = _pad_lanes(a_log[l], H_DELTA)
        dtb_row = _pad_lanes(dt_bias[l], H_DELTA)
        alog_col = _gate_rows_param(a_log[l])
        dtb_col = _gate_rows_param(dt_bias[l])

        sq, sk, _, sv, _, sga, scin, sz, stail = _in_proj(hs.reshape(1, bs, d), w_main, w_tail, tables_s, False)
        tok = lambda a: a.reshape(bs, 1, -1)
        decode_args = dict(q=tok(sq), k_new=tok(sk), v_new=tok(sv), ga=tok(sga), cache_k=cache_k, cache_v=cache_v,
                           layer=l, page_table=page_table, diff_lambda=diff_lambda[l], norm_w=diff_norm_w[l],
                           lam_init=lam_init)
        rider = _DecodeRider(**decode_args, seq0=0, n_seq=bs, grid=(bp, _delta_prep_grid(tp)[1]))
        if not rider.ok:
            rider = None

        q, k, kb, v, vb, ga, cin, z, _, tail_t = _in_proj(hp, w_main, w_tail, tables_p, True)
        oa = _prompt_attn(q, kb, vb, ga, diff_lambda[l], diff_norm_w[l], lam_init)
        prep, soa = _delta_prep(cin, jnp.zeros((bp, CONV_W - 1, CONV_CH), F32), conv_w[l], tail_t, alog_col,
                                dtb_col, rider)
        ob, sp = _delta_scan(prep, z, jnp.zeros((bp, H_DELTA, DK_DELTA, DV_DELTA), F32), delta_norm_w[l])
        hp = _out_proj(oa.reshape(bp * tp, W_DIFF), ob.reshape(bp * tp, W_DELTA), hp.reshape(bp * tp, d),
                       w_out_b, ln_g[l], ln_b[l], alpha).reshape(bp, tp, d)
        outs[0].append(k.reshape(bp, tp, H_DIFF, 2, DH_DIFF))
        outs[1].append(v.reshape(bp, tp, H_DIFF, DV_DIFF))
        outs[2].append(sp)
        outs[3].append(cin[:, tp - (CONV_W - 1):, :])

        if rider is None:
            soa = _decode_attn(decode_args, bs)
        sob, ss = _delta_step(tok(scin), state_conv[l], conv_w[l], tok(stail), alog_row, dtb_row, tok(sz),
                              state_delta[l], delta_norm_w[l])
        hs = _out_proj(soa.reshape(bs, W_DIFF), sob.reshape(bs, W_DELTA), hs.reshape(bs, d),
                       w_out_b, ln_g[l], ln_b[l], alpha).reshape(bs, 1, d)
        outs[4].append(sk.reshape(bs, 1, H_DIFF, 2, DH_DIFF))
        outs[5].append(sv.reshape(bs, 1, H_DIFF, DV_DIFF))
        outs[6].append(ss)
        outs[7].append(jnp.concatenate([state_conv[l][:, 1:, :], tok(scin)], axis=1))
    return (hp, hs) + tuple(jnp.stack(o) for o in outs)
```

```python
import functools
import math

import jax
import jax.numpy as jnp
from jax import lax
from jax.experimental import pallas as pl
from jax.experimental.pallas import tpu as pltpu

F32 = jnp.float32
BF16 = jnp.bfloat16
HIGHEST = lax.Precision.HIGHEST

H_DIFF = 4
DH_DIFF = 64
DV_DIFF = 2 * DH_DIFF
W_DIFF = H_DIFF * DV_DIFF
H_DELTA = 4
DK_DELTA = 128
DV_DELTA = 128
W_DELTA = H_DELTA * DK_DELTA
ROT_DIM = DH_DIFF // 4
ROPE_THETA = 500000.0
CONV_W = 4
CONV_CH = 3 * W_DELTA
CHUNK = 64
P_MAIN = 4 * W_DIFF + 4 * W_DELTA
LN_EPS = 1e-5
HEAD_NORM_EPS = 1e-5
DELTA_NORM_EPS = 1e-6
L2_EPS = 1e-6
NEG_INF = -1e30

LANES = 128
SUBLANES = 8
PAIR = 2 * CHUNK
CHUNK_BITS = CHUNK.bit_length() - 1
MAP_BITS = DH_DIFF.bit_length() - 1
VMEM_LIMIT = 56 * 1024 * 1024
DECODE_PAGES_PER_STEP = 16

_NT = (((1,), (1,)), ((), ()))


def _sigmoid(x):
    return 0.5 * jnp.tanh(0.5 * x) + 0.5


def _silu(x):
    return x * _sigmoid(x)


def _softplus(x):
    return jnp.maximum(x, 0.0) + jnp.log1p(jnp.exp(-jnp.abs(x)))


def _lambda_init(layer):
    return 0.8 - 0.6 * math.exp(-0.3 * layer)


def _rope_tables(pos):
    half = ROT_DIM // 2
    inv = ROPE_THETA ** (-jnp.arange(half, dtype=F32) / half)
    ang = pos.astype(F32)[:, None] * inv[None, :]
    cos, sin = jnp.cos(ang), jnp.sin(ang)
    t = pos.shape[0]
    rest = DH_DIFF - ROT_DIM
    cos_m = jnp.concatenate([cos, cos, jnp.ones((t, rest), F32)], axis=-1)
    sin_lo = jnp.concatenate([-sin, jnp.zeros((t, half + rest), F32)], axis=-1)
    sin_hi = jnp.concatenate([jnp.zeros((t, half), F32), sin, jnp.zeros((t, rest), F32)], axis=-1)
    reps = LANES // DH_DIFF
    return jnp.tile(cos_m, (1, reps)), jnp.tile(sin_lo, (1, reps)), jnp.tile(sin_hi, (1, reps))


def _in_proj_kernel(x_ref, w_ref, wt_ref, cos_ref, slo_ref, shi_ref,
                    q_ref, k_ref, kb_ref, v_ref, vb_ref, ga_ref, cin_ref, z_ref, tail_ref, *tail_t_ref):
    xb = x_ref[0].astype(BF16)
    half = ROT_DIM // 2
    tm = xb.shape[0]

    def proj(c0, width):
        return jnp.dot(xb, w_ref[:, c0:c0 + width], preferred_element_type=F32)

    def rope(h):
        blocks = []
        for c in range(W_DIFF // LANES):
            hb = h[:, c * LANES:(c + 1) * LANES]
            blocks.append(hb * cos_ref[...]
                          + pltpu.roll(hb, LANES - half, 1) * slo_ref[...]
                          + pltpu.roll(hb, half, 1) * shi_ref[...])
        return jnp.concatenate(blocks, axis=1)

    q_ref[0] = rope(proj(0, W_DIFF)).astype(BF16)
    k = rope(proj(W_DIFF, W_DIFF))
    k_ref[0] = k
    kb_ref[0] = k.astype(BF16)
    v = proj(2 * W_DIFF, W_DIFF)
    for h in range(H_DIFF):
        v_ref[0, pl.ds(h, tm, stride=H_DIFF), :] = v[:, h * DV_DIFF:(h + 1) * DV_DIFF]
    vb_ref[0] = v.astype(BF16)
    ga_ref[0] = proj(3 * W_DIFF, W_DIFF)
    for j in range(3):
        cin_ref[0, :, j * W_DELTA:(j + 1) * W_DELTA] = proj(4 * W_DIFF + j * W_DELTA, W_DELTA)
    z_ref[0] = proj(4 * W_DIFF + 3 * W_DELTA, W_DELTA)
    tail = jnp.dot(xb, wt_ref[...], preferred_element_type=F32)
    tail_ref[0] = tail
    if tail_t_ref:
        for c in range(tm // LANES):
            cs = slice(c * LANES, (c + 1) * LANES)
            tail_t_ref[0][0, :, cs] = tail[cs, :].T[:2 * H_DELTA, :]


def _in_proj(x, w_main, w_tail, tables, gates_time_major):
    b, t, d = x.shape
    tm = min(512, t)
    assert t % tm == 0 and (not gates_time_major or tm % LANES == 0)
    cos_t, slo_t, shi_t = tables
    row = lambda width: pl.BlockSpec((1, tm, width), lambda ti, bi: (bi, ti, 0))
    tab = pl.BlockSpec((tm, LANES), lambda ti, bi: (ti, 0))
    const = lambda shape: pl.BlockSpec(shape, lambda ti, bi: (0, 0))
    outs = [(W_DIFF, BF16), (W_DIFF, F32), (W_DIFF, BF16), None, (W_DIFF, BF16),
            (W_DIFF, F32), (CONV_CH, F32), (W_DELTA, F32), (LANES, F32)]
    out_specs = [row(o[0]) if o else pl.BlockSpec((1, tm * H_DIFF, DV_DIFF), lambda ti, bi: (bi, ti, 0))
                 for o in outs]
    out_shape = [jax.ShapeDtypeStruct((b, t, o[0]), o[1]) if o else
                 jax.ShapeDtypeStruct((b, t * H_DIFF, DV_DIFF), F32) for o in outs]
    if gates_time_major:
        out_specs.append(pl.BlockSpec((1, 2 * H_DELTA, tm), lambda ti, bi: (bi, 0, ti)))
        out_shape.append(jax.ShapeDtypeStruct((b, 2 * H_DELTA, t), F32))
    return pl.pallas_call(
        _in_proj_kernel,
        grid=(t // tm, b),
        in_specs=[row(d), const((d, P_MAIN)), const(w_tail.shape), tab, tab, tab],
        out_specs=out_specs,
        out_shape=out_shape,
        compiler_params=pltpu.CompilerParams(
            dimension_semantics=("arbitrary", "arbitrary"), vmem_limit_bytes=VMEM_LIMIT),
        name="in_proj",
    )(x, w_main, w_tail, cos_t, slo_t, shi_t)


def _diff_lambda_value(dl_ref, lam_init):
    dl = dl_ref[...]
    a = jnp.sum(dl[0:1] * dl[1:2], axis=1, keepdims=True)
    b = jnp.sum(dl[2:3] * dl[3:4], axis=1, keepdims=True)
    return jnp.exp(a) - jnp.exp(b) + lam_init


def _head_norm_gate(o, normw, gate, lam_init):
    ms = jnp.mean(o * o, axis=-1, keepdims=True)
    o = o * lax.rsqrt(ms + HEAD_NORM_EPS) * normw * (1.0 - lam_init)
    return o * _silu(gate)


def _prompt_attn_kernel(q0_ref, qn_ref, k_ref, v_ref, ga_ref, dl_ref, nw_ref, o_ref, s_scr, m_scr, l_scr, acc_scr,
                        *, blk, lam_init):
    i = pl.program_id(1)
    last = pl.num_programs(1) - 1
    slot = lax.rem(i, 2)
    lane = lax.broadcasted_iota(jnp.int32, (blk, DV_DIFF), 1)
    heads = [slice(h * DV_DIFF, (h + 1) * DV_DIFF) for h in range(H_DIFF)]
    lane_chunks = [slice(c * LANES, (c + 1) * LANES) for c in range(blk // LANES)]
    n_maps = 2 * H_DIFF

    def query_maps(q_ref):
        maps = []
        for hs in heads:
            qs = q_ref[0, :, hs].astype(F32) * (DH_DIFF ** -0.5 * math.log2(math.e))
            maps.append(jnp.where(lane < DH_DIFF, qs, 0.0).astype(BF16))
            maps.append(jnp.where(lane >= DH_DIFF, qs, 0.0).astype(BF16))
        return maps

    def fold(x, op):
        r = x[:, lane_chunks[0]]
        for c in lane_chunks[1:]:
            r = op(r, x[:, c])
        return r

    def score_block(q_maps, dst, j, masked):
        rows = pl.ds(pl.multiple_of(j * blk, blk), blk)
        for h, hs in enumerate(heads):
            k = k_ref[0, rows, hs]
            for mi in (2 * h, 2 * h + 1):
                s = lax.dot_general(q_maps[mi], k, _NT, preferred_element_type=F32)
                if masked:
                    row = lax.broadcasted_iota(jnp.int32, (blk, blk), 0)
                    col = lax.broadcasted_iota(jnp.int32, (blk, blk), 1)
                    s = jnp.where(col <= row, s, NEG_INF)
                s_scr[dst, j, mi] = s
                m_scr[dst, mi] = jnp.maximum(m_scr[dst, mi], fold(s, jnp.maximum))

    def finish_max(dst):
        for mi in range(n_maps):
            m_scr[dst, mi] = jnp.broadcast_to(jnp.max(m_scr[dst, mi], axis=-1, keepdims=True), (blk, LANES))

    def prob_block(j):
        rows = pl.ds(pl.multiple_of(j * blk, blk), blk)
        for h, hs in enumerate(heads):
            v = v_ref[0, rows, hs]
            for mi in (2 * h, 2 * h + 1):
                m = m_scr[slot, mi]
                p = jnp.concatenate([jnp.exp2(s_scr[slot, j, mi, :, c] - m) for c in lane_chunks], axis=1)
                l_scr[mi] = l_scr[mi] + fold(p, jnp.add)
                acc_scr[mi] = acc_scr[mi] + jnp.dot(p.astype(BF16), v, preferred_element_type=F32)

    def write_output():
        lam = _diff_lambda_value(dl_ref, lam_init)
        for h, hs in enumerate(heads):
            l1 = jnp.sum(l_scr[2 * h], axis=-1, keepdims=True)
            l2 = jnp.sum(l_scr[2 * h + 1], axis=-1, keepdims=True)
            o = acc_scr[2 * h] / l1 - lam * (acc_scr[2 * h + 1] / l2)
            o_ref[0, :, hs] = _head_norm_gate(o, nw_ref[...], ga_ref[0, :, hs], lam_init).astype(BF16)

    @pl.when(i == 0)
    def _():
        m_scr[0] = jnp.full(m_scr.shape[1:], NEG_INF, F32)
        score_block(query_maps(q0_ref), 0, 0, True)
        finish_max(0)

    l_scr[...] = jnp.zeros(l_scr.shape, F32)
    acc_scr[...] = jnp.zeros(acc_scr.shape, F32)

    @pl.when(i < last)
    def _():
        nxt = 1 - slot
        q_next = query_maps(qn_ref)
        m_scr[nxt] = jnp.full(m_scr.shape[1:], NEG_INF, F32)

        def both(j, carry):
            prob_block(j)
            score_block(q_next, nxt, j, False)
            return carry

        lax.fori_loop(0, i + 1, both, 0)
        write_output()
        score_block(q_next, nxt, i + 1, True)
        finish_max(nxt)

    @pl.when(i == last)
    def _():
        def only_probs(j, carry):
            prob_block(j)
            return carry

        lax.fori_loop(0, i + 1, only_probs, 0)
        write_output()


def _prompt_attn(q, kb, vb, ga, diff_lambda, norm_w, lam_init):
    b, t, _ = q.shape
    blk = min(256, t)
    assert t % blk == 0
    n_maps = 2 * H_DIFF
    nq = t // blk
    qspec = pl.BlockSpec((1, blk, W_DIFF), lambda bi, i: (bi, i, 0))
    q_first = pl.BlockSpec((1, blk, W_DIFF), lambda bi, i: (bi, 0, 0))
    q_next = pl.BlockSpec((1, blk, W_DIFF), lambda bi, i: (bi, jnp.minimum(i + 1, nq - 1), 0))
    kvspec = pl.BlockSpec((1, t, W_DIFF), lambda bi, i: (bi, 0, 0))
    return pl.pallas_call(
        functools.partial(_prompt_attn_kernel, blk=blk, lam_init=lam_init),
        grid=(b, nq),
        in_specs=[q_first, q_next, kvspec, kvspec, qspec,
                  pl.BlockSpec(diff_lambda.shape, lambda bi, i: (0, 0)),
                  pl.BlockSpec((1, DV_DIFF), lambda bi, i: (0, 0))],
        out_specs=qspec,
        out_shape=jax.ShapeDtypeStruct((b, t, W_DIFF), BF16),
        scratch_shapes=[pltpu.VMEM((2, nq, n_maps, blk, blk), F32),
                        pltpu.VMEM((2, n_maps, blk, LANES), F32),
                        pltpu.VMEM((n_maps, blk, LANES), F32),
                        pltpu.VMEM((n_maps, blk, DV_DIFF), F32)],
        compiler_params=pltpu.CompilerParams(
            dimension_semantics=("arbitrary", "arbitrary"), vmem_limit_bytes=VMEM_LIMIT),
        name="prompt_attn",
    )(q, q, kb, vb, ga, diff_lambda, norm_w.reshape(1, DV_DIFF))


class _DecodeStep:
    def __init__(self, step, n_steps, spq, first_page, pages, pt_ref, in_refs, o_ref, scratch_refs, lam_init):
        self.step, self.n_steps, self.pages, self.first_page = step, n_steps, pages, first_page
        self.g, self.n_g, self.lam_init, self.pt_ref = lax.rem(step, spq), spq, lam_init, pt_ref
        (self.q_ref, self.kn_ref, self.vn_ref, self.ga_ref, self.dl_ref, self.nw_ref,
         self.ck_ref, self.cv_ref) = in_refs
        self.o_ref = o_ref
        (self.qcol_ref, self.m_ref, self.l_ref, self.acc_ref,
         self.kbuf_ref, self.vbuf_ref, self.sem_ref) = scratch_refs
        self.slot = lax.rem(step, 2)

    def _query(self):
        return self.q_ref[0].astype(F32) * DH_DIFF ** -0.5

    def _page_copies(self, step, slot):
        base = self.first_page + step * self.pages
        copies = []
        for i in range(self.pages):
            pid = self.pt_ref[base + i]
            copies.append(pltpu.make_async_copy(self.ck_ref.at[pid], self.kbuf_ref.at[slot, i],
                                                self.sem_ref.at[slot, 0]))
            copies.append(pltpu.make_async_copy(self.cv_ref.at[pid], self.vbuf_ref.at[slot, i],
                                                self.sem_ref.at[slot, 1]))
        return copies

    def init(self):
        @pl.when(self.step == 0)
        def _():
            for cp in self._page_copies(0, 0):
                cp.start()

        for cp in self._page_copies(self.step, self.slot):
            cp.wait()

        @pl.when(self.step + 1 < self.n_steps)
        def _():
            for cp in self._page_copies(self.step + 1, 1 - self.slot):
                cp.start()

        @pl.when(self.g == 0)
        def _():
            self.m_ref[...] = jnp.full(self.m_ref.shape, NEG_INF, F32)
            self.l_ref[...] = jnp.zeros(self.l_ref.shape, F32)
            self.acc_ref[...] = jnp.zeros(self.acc_ref.shape, F32)
            qs = self._query()
            for c in range(W_DIFF // LANES):
                cs = slice(c * LANES, (c + 1) * LANES)
                self.qcol_ref[cs, :] = jnp.broadcast_to(qs[:, cs], (LANES, LANES)).T

    def main(self):
        k_pages = [self.kbuf_ref.at[self.slot, i] for i in range(self.pages)]
        v_pages = [self.vbuf_ref.at[self.slot, i] for i in range(self.pages)]
        _decode_pages(k_pages, v_pages, self.qcol_ref, self.m_ref, self.l_ref, self.acc_ref)

    def finalize(self):
        @pl.when(self.g == self.n_g - 1)
        def _():
            _decode_finish(self._query(), self.kn_ref, self.vn_ref, self.ga_ref, self.dl_ref, self.nw_ref,
                           self.o_ref, self.m_ref, self.l_ref, self.acc_ref, self.lam_init)


def _decode_pages(kp_refs, vp_refs, qcol_ref, m_ref, l_ref, acc_ref):
    n_maps = 2 * H_DIFF
    page = kp_refs[0].shape[1]
    qcol = qcol_ref[...]
    s = jnp.concatenate(
        [jnp.sum((kp[...] * qcol).reshape(n_maps, DH_DIFF, page), axis=1) for kp in kp_refs], axis=1)
    m_prev = m_ref[:, 0:1]
    m_new = jnp.maximum(m_prev, jnp.max(s, axis=-1, keepdims=True))
    p = jnp.exp(s - m_new)
    alpha = jnp.exp(m_prev - m_new)
    l_new = alpha * l_ref[:, 0:1] + jnp.sum(p, axis=-1, keepdims=True)
    pb = p.astype(BF16)
    for h in range(H_DIFF):
        pv = jnp.zeros((n_maps, DV_DIFF), F32)
        for i, vp in enumerate(vp_refs):
            v_head = vp[pl.ds(h, page, stride=H_DIFF), :].astype(BF16)
            pv = pv + jnp.dot(pb[:, i * page:(i + 1) * page], v_head, preferred_element_type=F32)
        acc_ref[h] = alpha * acc_ref[h] + pv
    m_ref[...] = jnp.broadcast_to(m_new, m_ref.shape)
    l_ref[...] = jnp.broadcast_to(l_new, l_ref.shape)


def _decode_finish(qs, kn_ref, vn_ref, ga_ref, dl_ref, nw_ref, o_ref, m_ref, l_ref, acc_ref, lam_init):
    n_maps = 2 * H_DIFF
    rows = lax.broadcasted_iota(jnp.int32, (n_maps, W_DIFF), 0)
    lanes = lax.broadcasted_iota(jnp.int32, (n_maps, W_DIFF), 1)
    q_blk = jnp.where((lanes >> MAP_BITS) == rows, jnp.broadcast_to(qs, (n_maps, W_DIFF)), 0.0).astype(BF16)
    k_self = jnp.broadcast_to(kn_ref[0].astype(BF16), (SUBLANES, W_DIFF))
    s_self = lax.dot_general(q_blk, k_self, _NT, preferred_element_type=F32)[:, 0:1]
    m_past = m_ref[:, 0:1]
    m_f = jnp.maximum(m_past, s_self)
    a_f = jnp.exp(m_past - m_f)
    p_self = jnp.exp(s_self - m_f)
    l_f = a_f * l_ref[:, 0:1] + p_self
    pv_self = p_self.astype(BF16).astype(F32) * vn_ref[0].astype(BF16).astype(F32)
    lam = _diff_lambda_value(dl_ref, lam_init)
    for h in range(H_DIFF):
        hs = slice(h * DV_DIFF, (h + 1) * DV_DIFF)
        a = (a_f * acc_ref[h] + pv_self[:, hs]) / l_f
        o = a[2 * h:2 * h + 1] - lam * a[2 * h + 1:2 * h + 2]
        o_ref[0, :, hs] = _head_norm_gate(o, nw_ref[...], ga_ref[0, :, hs], lam_init).astype(BF16)


class _DecodeRider:
    def __init__(self, q, k_new, v_new, ga, cache_k, cache_v, layer, page_table, diff_lambda, norm_w, lam_init,
                 seq0, n_seq, grid):
        depth, n_phys, page = cache_k.shape[:3]
        n_pages = page_table.shape[1]
        n_steps = grid[0] * grid[1]
        assert page == LANES
        pages = n_seq * n_pages // n_steps
        self.ok = pages >= 1 and pages * n_steps == n_seq * n_pages and n_pages % pages == 0
        if not self.ok:
            return
        spq = n_pages // pages
        self.pages, self.spq, self.lam_init, self.grid = pages, spq, lam_init, grid
        self.first_page = seq0 * n_pages
        ck = jnp.transpose(cache_k, (0, 1, 3, 4, 5, 2)).reshape(depth * n_phys, W_DIFF, page)
        cv = cache_v.reshape(depth * n_phys, page * H_DIFF, DV_DIFF)
        self.page_ids = page_table.reshape(-1) + layer * n_phys
        step = lambda i0, i1: i0 * grid[1] + i1
        tok = pl.BlockSpec((1, 1, W_DIFF), lambda i0, i1, pt: (seq0 + step(i0, i1) // spq, 0, 0))
        const = lambda shape: pl.BlockSpec(shape, lambda i0, i1, pt: (0, 0))
        hbm = pl.BlockSpec(memory_space=pl.ANY)
        self.inputs = [q, k_new, v_new, ga, diff_lambda, norm_w.reshape(1, DV_DIFF), ck, cv]
        self.in_specs = [tok, tok, tok, tok, const(diff_lambda.shape), const((1, DV_DIFF)), hbm, hbm]
        self.out_spec = pl.BlockSpec((1, 1, W_DIFF), lambda i0, i1, pt: (step(i0, i1) // spq, 0, 0))
        self.out_shape = jax.ShapeDtypeStruct((n_seq, 1, W_DIFF), BF16)
        n_maps = 2 * H_DIFF
        self.scratch_shapes = [pltpu.VMEM((W_DIFF, page), F32),
                               pltpu.VMEM((n_maps, LANES), F32), pltpu.VMEM((n_maps, LANES), F32),
                               pltpu.VMEM((H_DIFF, n_maps, DV_DIFF), F32),
                               pltpu.VMEM((2, pages, W_DIFF, page), F32),
                               pltpu.VMEM((2, pages, page * H_DIFF, DV_DIFF), F32),
                               pltpu.SemaphoreType.DMA((2, 2))]

    def step(self, pt_ref, in_refs, out_ref, scratch_refs):
        step = pl.program_id(0) * self.grid[1] + pl.program_id(1)
        return _DecodeStep(step, self.grid[0] * self.grid[1], self.spq, self.first_page, self.pages, pt_ref,
                           in_refs, out_ref, scratch_refs, self.lam_init)


def _ride(host_kernel, n_in, n_out, n_scratch, rider):
    n_rin = len(rider.inputs)

    def kernel(pt_ref, *refs):
        host_in, refs = refs[:n_in], refs[n_in:]
        rider_in, refs = refs[:n_rin], refs[n_rin:]
        host_out, refs = refs[:n_out], refs[n_out:]
        rider_out, refs = refs[0], refs[1:]
        host_scratch, rider_scratch = refs[:n_scratch], refs[n_scratch:]
        host_kernel(*host_in, *host_out, *host_scratch,
                    decode=rider.step(pt_ref, rider_in, rider_out, rider_scratch))

    return kernel


def _hosted_call(host_kernel, grid, in_specs, out_specs, out_shape, scratch_shapes, inputs, name, rider):
    params = pltpu.CompilerParams(dimension_semantics=("arbitrary", "arbitrary"), vmem_limit_bytes=VMEM_LIMIT)
    if rider is None:
        outs = pl.pallas_call(host_kernel, grid=grid, in_specs=in_specs, out_specs=out_specs, out_shape=out_shape,
                              scratch_shapes=scratch_shapes, compiler_params=params, name=name)(*inputs)
        return outs, None
    grid_spec = pltpu.PrefetchScalarGridSpec(
        num_scalar_prefetch=1, grid=grid,
        in_specs=list(in_specs) + rider.in_specs,
        out_specs=list(out_specs) + [rider.out_spec],
        scratch_shapes=list(scratch_shapes) + rider.scratch_shapes)
    outs = pl.pallas_call(
        _ride(host_kernel, len(in_specs), len(out_specs), len(scratch_shapes), rider),
        grid_spec=grid_spec, out_shape=list(out_shape) + [rider.out_shape],
        compiler_params=params, name=name)(rider.page_ids, *inputs, *rider.inputs)
    return outs[:-1], outs[-1]


def _decode_attn_kernel(pt_ref, *refs, rider):
    n_rin = len(rider.inputs)
    decode = rider.step(pt_ref, refs[:n_rin], refs[n_rin], refs[n_rin + 1:])
    decode.init()
    decode.main()
    decode.finalize()


def _decode_attn(rider_args, n_seq):
    n_pages = rider_args["page_table"].shape[1]
    grid = (n_seq, n_pages // math.gcd(DECODE_PAGES_PER_STEP, n_pages))
    rider = _DecodeRider(**rider_args, seq0=0, n_seq=n_seq, grid=grid)
    grid_spec = pltpu.PrefetchScalarGridSpec(
        num_scalar_prefetch=1, grid=grid, in_specs=rider.in_specs,
        out_specs=rider.out_spec, scratch_shapes=rider.scratch_shapes)
    return pl.pallas_call(
        functools.partial(_decode_attn_kernel, rider=rider), grid_spec=grid_spec, out_shape=rider.out_shape,
        compiler_params=pltpu.CompilerParams(
            dimension_semantics=("arbitrary", "arbitrary"), vmem_limit_bytes=VMEM_LIMIT),
        name="decode_attn")(rider.page_ids, *rider.inputs)


def _gate_rows(tail, alog_row, dtb_row):
    beta = _sigmoid(tail)
    g = -jnp.exp(alog_row) * _softplus(tail + dtb_row)
    return beta, g


def _lane_bcast(x, lane, rows):
    return jnp.broadcast_to(x[:, lane:lane + 1], (rows, LANES))


def _l2norm(x):
    return x * lax.rsqrt(jnp.sum(x * x, axis=-1, keepdims=True) + L2_EPS)


def _split_bf16(x):
    hi = x.astype(BF16)
    return hi, (x - hi.astype(F32)).astype(BF16)


def _dot_split(lhs, rhs):
    d = lambda a, b: jnp.dot(a, b, preferred_element_type=F32)
    return d(lhs[0], rhs[0]) + d(lhs[0], rhs[1]) + d(lhs[1], rhs[0])


def _unit_lower_inverses(a_mats):
    n = a_mats[0].shape[0]
    eye = (lax.broadcasted_iota(jnp.int32, (n, n), 0) == lax.broadcasted_iota(jnp.int32, (n, n), 1)).astype(F32)
    xs = [-a for a in a_mats]
    ps = [eye + x for x in xs]
    splits = [_split_bf16(x) for x in xs]
    xs = [_dot_split(s, s) for s in splits]
    power = 2
    while 2 * power < CHUNK:
        both = [_dot_split(_split_bf16(x), _split_bf16(jnp.concatenate([p, x], axis=1))) for p, x in zip(ps, xs)]
        ps = [p + b[:, :n] for p, b in zip(ps, both)]
        xs = [b[:, n:] for b in both]
        power *= 2
    return [p + _dot_split(_split_bf16(x), _split_bf16(p)) for p, x in zip(ps, xs)]


def _delta_prep_kernel(cin_ref, prev_ref, cw_ref, tail_ref, alog_ref, dtb_ref,
                       u_ref, w_ref, qg_ref, at_ref, kdt_ref, el_ref, ext_ref, *, pairs, decode=None):
    t = pl.program_id(1)
    pad = SUBLANES
    keep = CONV_W - 1
    rows = pairs * PAIR

    @pl.when(t == 0)
    def _():
        ext_ref[pad - keep:pad, :] = prev_ref[0]

    @pl.when(t > 0)
    def _():
        ext_ref[pad - keep:pad, :] = ext_ref[pad + rows - keep:pad + rows, :]

    if decode is not None:
        decode.init()

    ext_ref[pad:pad + rows, :] = cin_ref[0]

    r = lax.broadcasted_iota(jnp.int32, (PAIR, PAIR), 0)
    c = lax.broadcasted_iota(jnp.int32, (PAIR, PAIR), 1)
    same = (r >> CHUNK_BITS) == (c >> CHUNK_BITS)
    incl = same & (c <= r)
    strict = same & (c < r)
    incl_t = same & (r <= c)
    chunk_end = r == (c | (CHUNK - 1))

    chains = []
    for pi in range(pairs):
        base = pad - keep + pi * PAIR
        conv = ext_ref[base:base + PAIR, :] * cw_ref[0:1, :]
        for j in range(1, CONV_W):
            conv = conv + ext_ref[base + j:base + j + PAIR, :] * cw_ref[j:j + 1, :]
        conv = _silu(conv)
        rs = slice(pi * PAIR, (pi + 1) * PAIR)
        beta_t, g_t = _gate_rows(tail_ref[0, :, rs], alog_ref[...], dtb_ref[...])
        gc_t = jnp.dot(g_t, incl_t.astype(F32), precision=HIGHEST, preferred_element_type=F32)
        g_last_t = jnp.dot(gc_t, chunk_end.astype(F32), precision=HIGHEST, preferred_element_type=F32)
        for h in range(H_DELTA):
            hs = slice(h * DK_DELTA, (h + 1) * DK_DELTA)
            qs = _l2norm(conv[:, hs]) * DK_DELTA ** -0.5
            kn = _l2norm(conv[:, W_DELTA + h * DK_DELTA:W_DELTA + (h + 1) * DK_DELTA])
            vh = conv[:, 2 * W_DELTA + h * DV_DELTA:2 * W_DELTA + (h + 1) * DV_DELTA]
            beta = jnp.broadcast_to(beta_t[h:h + 1, :], (PAIR, PAIR)).T
            gc_cols = jnp.broadcast_to(gc_t[H_DELTA + h:H_DELTA + h + 1, :], (PAIR, PAIR))
            g_last_cols = jnp.broadcast_to(g_last_t[H_DELTA + h:H_DELTA + h + 1, :], (PAIR, PAIR))
            chains.append(dict(pi=pi, h=h, rs=rs, qs=qs, kn=kn, vh=vh, beta=beta, gc=gc_cols.T, gc_cols=gc_cols,
                               g_last_cols=g_last_cols, kbeta=kn * beta))

    for ch in chains:
        ch["kn_t"] = ch["kn"].T
        kn_tb = ch["kn_t"].astype(BF16)
        ch["kk"] = jnp.dot(ch["kbeta"].astype(BF16), kn_tb, preferred_element_type=F32)
        ch["qk"] = jnp.dot(ch["qs"].astype(BF16), kn_tb, preferred_element_type=F32)
    for ch in chains:
        ch["decay"] = jnp.exp(jnp.where(incl, ch["gc"] - ch["gc_cols"], -jnp.inf))
    t_mats = _unit_lower_inverses([jnp.where(strict, ch["kk"] * ch["decay"], 0.0) for ch in chains])
    for ch, t_mat in zip(chains, t_mats):
        rhs = jnp.concatenate([ch["vh"] * ch["beta"], ch["kbeta"] * jnp.exp(ch["gc"])], axis=1).astype(BF16)
        ch["uw"] = jnp.dot(t_mat.astype(BF16), rhs, preferred_element_type=F32)
    for ch in chains:
        pi, h, rs, gc = ch["pi"], ch["h"], ch["rs"], ch["gc"]
        u_ref[0, h, rs, :] = ch["uw"][:, :DV_DELTA]
        w_ref[0, h, rs, :] = ch["uw"][:, DV_DELTA:].astype(BF16)
        attn = jnp.where(incl, ch["qk"] * ch["decay"], 0.0)
        at_ref[0, h, rs, :] = jnp.concatenate([attn[:CHUNK, :CHUNK], attn[CHUNK:, CHUNK:]], axis=0).astype(BF16)
        qg_ref[0, h, rs, :] = (ch["qs"] * jnp.exp(gc)).astype(BF16)
        kdt_ref[0, h, pi] = (ch["kn_t"] * jnp.exp(ch["g_last_cols"] - ch["gc_cols"])).astype(BF16)
        el_ref[0, h, pi] = jnp.exp(jnp.concatenate([jnp.broadcast_to(gc[CHUNK - 1:CHUNK], (SUBLANES, LANES)),
                                                    jnp.broadcast_to(gc[PAIR - 1:PAIR], (SUBLANES, LANES))], axis=0))

    if decode is not None:
        decode.main()
        decode.finalize()


def _delta_prep_grid(t):
    assert t % PAIR == 0
    npair = t // PAIR
    pairs = 2 if npair % 2 == 0 else 1
    return pairs, npair // pairs


def _delta_prep(cin, conv_prev, conv_w, tail, alog_row, dtb_row, rider):
    b, t, _ = cin.shape
    npair = t // PAIR
    pairs, nt = _delta_prep_grid(t)
    rows = pairs * PAIR
    per_head = lambda width: pl.BlockSpec((1, H_DELTA, rows, width), lambda bi, ti, *_: (bi, 0, ti, 0))
    const = lambda shape: pl.BlockSpec(shape, lambda bi, ti, *_: (0,) * len(shape))
    out_shape = [
        jax.ShapeDtypeStruct((b, H_DELTA, t, DV_DELTA), F32),
        jax.ShapeDtypeStruct((b, H_DELTA, t, DK_DELTA), BF16),
        jax.ShapeDtypeStruct((b, H_DELTA, t, DK_DELTA), BF16),
        jax.ShapeDtypeStruct((b, H_DELTA, t, CHUNK), BF16),
        jax.ShapeDtypeStruct((b, H_DELTA, npair, DK_DELTA, PAIR), BF16),
        jax.ShapeDtypeStruct((b, H_DELTA, npair, 2 * SUBLANES, LANES), F32),
    ]
    out_specs = [per_head(DV_DELTA), per_head(DK_DELTA), per_head(DK_DELTA), per_head(CHUNK),
                 pl.BlockSpec((1, H_DELTA, pairs, DK_DELTA, PAIR), lambda bi, ti, *_: (bi, 0, ti, 0, 0)),
                 pl.BlockSpec((1, H_DELTA, pairs, 2 * SUBLANES, LANES), lambda bi, ti, *_: (bi, 0, ti, 0, 0))]
    return _hosted_call(
        functools.partial(_delta_prep_kernel, pairs=pairs),
        grid=(b, nt),
        in_specs=[pl.BlockSpec((1, rows, CONV_CH), lambda bi, ti, *_: (bi, ti, 0)),
                  pl.BlockSpec((1, CONV_W - 1, CONV_CH), lambda bi, ti, *_: (bi, 0, 0)),
                  const(conv_w.shape),
                  pl.BlockSpec((1, 2 * H_DELTA, rows), lambda bi, ti, *_: (bi, 0, ti)),
                  const(alog_row.shape), const(dtb_row.shape)],
        out_specs=out_specs,
        out_shape=out_shape,
        scratch_shapes=[pltpu.VMEM((SUBLANES + rows, CONV_CH), F32)],
        inputs=(cin, conv_prev, conv_w, tail, alog_row, dtb_row),
        name="delta_prep", rider=rider)


def _delta_norm_gate(o, normw, z):
    ms = jnp.mean(o * o, axis=-1, keepdims=True)
    return o * lax.rsqrt(ms + DELTA_NORM_EPS) * normw * _silu(z)


def _delta_scan_kernel(u_ref, w_ref, qg_ref, at_ref, kdt_ref, el_ref, z_ref, s0_ref, nw_ref,
                       o_ref, sfin_ref, s_scr, *, nb, nblk):
    t = pl.program_id(1)

    @pl.when(t == 0)
    def _():
        s_scr[...] = s0_ref[...]

    def block(jb, carry):
        r0 = pl.multiple_of(jb * PAIR, PAIR)
        seqs = [(bi, h) for bi in range(nb) for h in range(H_DELTA)]
        for ci in range(2):
            rows = pl.ds(pl.multiple_of(r0 + ci * CHUNK, CHUNK), CHUNK)
            states = [s_scr[bi, h] for bi, h in seqs]
            res = [jnp.dot(jnp.concatenate([w_ref[bi, h, rows, :], qg_ref[bi, h, rows, :]], axis=0),
                           s.astype(BF16), preferred_element_type=F32) for (bi, h), s in zip(seqs, states)]
            v_new = [(u_ref[bi, h, rows, :] - r[:CHUNK]).astype(BF16) for (bi, h), r in zip(seqs, res)]
            upd = [jnp.dot(kdt_ref[bi, h, jb, :, ci * CHUNK:(ci + 1) * CHUNK], v, preferred_element_type=F32)
                   for (bi, h), v in zip(seqs, v_new)]
            intra = [jnp.dot(at_ref[bi, h, rows, :], v, preferred_element_type=F32) for (bi, h), v in zip(seqs, v_new)]
            for (bi, h), s, r, du, oi in zip(seqs, states, res, upd, intra):
                decay = jnp.broadcast_to(el_ref[bi, h, jb, ci * SUBLANES:ci * SUBLANES + 1, :], (DK_DELTA, DV_DELTA))
                s_scr[bi, h] = s * decay + du
                hs = slice(h * DV_DELTA, (h + 1) * DV_DELTA)
                o_ref[bi, rows, hs] = _delta_norm_gate(r[CHUNK:] + oi, nw_ref[...], z_ref[bi, rows, hs]).astype(BF16)
        return carry

    lax.fori_loop(0, nblk, block, 0)

    @pl.when(t == pl.num_programs(1) - 1)
    def _():
        sfin_ref[...] = s_scr[...]


def _delta_scan(prep, z, s0, norm_w):
    u, w, qg, at, kdt, el = prep
    b, _, t, _ = u.shape
    nb = 2 if b % 2 == 0 else 1
    ts = min(512, t)
    assert t % ts == 0
    nblk = ts // PAIR
    per_head = lambda width: pl.BlockSpec((nb, H_DELTA, ts, width), lambda bi, ti: (bi, 0, ti, 0))
    state = pl.BlockSpec((nb, H_DELTA, DK_DELTA, DV_DELTA), lambda bi, ti: (bi, 0, 0, 0))
    tok = pl.BlockSpec((nb, ts, W_DELTA), lambda bi, ti: (bi, ti, 0))
    return pl.pallas_call(
        functools.partial(_delta_scan_kernel, nb=nb, nblk=nblk),
        grid=(b // nb, t // ts),
        in_specs=[per_head(DV_DELTA), per_head(DK_DELTA), per_head(DK_DELTA), per_head(CHUNK),
                  pl.BlockSpec((nb, H_DELTA, nblk, DK_DELTA, PAIR), lambda bi, ti: (bi, 0, ti, 0, 0)),
                  pl.BlockSpec((nb, H_DELTA, nblk, 2 * SUBLANES, LANES), lambda bi, ti: (bi, 0, ti, 0, 0)),
                  tok, state, pl.BlockSpec((1, DV_DELTA), lambda bi, ti: (0, 0))],
        out_specs=[tok, state],
        out_shape=[jax.ShapeDtypeStruct((b, t, W_DELTA), BF16),
                   jax.ShapeDtypeStruct((b, H_DELTA, DK_DELTA, DV_DELTA), F32)],
        scratch_shapes=[pltpu.VMEM((nb, H_DELTA, DK_DELTA, DV_DELTA), F32)],
        compiler_params=pltpu.CompilerParams(
            dimension_semantics=("arbitrary", "arbitrary"), vmem_limit_bytes=VMEM_LIMIT),
        name="delta_scan",
    )(u, w, qg, at, kdt, el, z, s0, norm_w.reshape(1, DV_DELTA))


def _delta_step_kernel(cin_ref, prev_ref, cw_ref, tail_ref, alog_ref, dtb_ref, z_ref, s0_ref, nw_ref,
                       o_ref, s_ref, *, seqs):
    items = []
    for b in range(seqs):
        prev = prev_ref[b]
        conv = prev[0:1] * cw_ref[0:1, :]
        for j in range(1, CONV_W - 1):
            conv = conv + prev[j:j + 1] * cw_ref[j:j + 1, :]
        conv = _silu(conv + cin_ref[b] * cw_ref[CONV_W - 1:CONV_W, :])
        beta_all, g_all = _gate_rows(tail_ref[b], alog_ref[...], dtb_ref[...])
        for h in range(H_DELTA):
            hs = slice(h * DK_DELTA, (h + 1) * DK_DELTA)
            qs = _l2norm(conv[:, hs]) * DK_DELTA ** -0.5
            kn = _l2norm(conv[:, W_DELTA + h * DK_DELTA:W_DELTA + (h + 1) * DK_DELTA])
            vh = conv[:, 2 * W_DELTA + h * DV_DELTA:2 * W_DELTA + (h + 1) * DV_DELTA]
            beta = beta_all[:, h:h + 1]
            eg = jnp.exp(g_all[:, H_DELTA + h:H_DELTA + h + 1])
            items.append((b, h, hs, qs, kn, vh, beta, eg, s0_ref[b, h]))
    res = [jnp.dot(jnp.concatenate([kn * (beta * eg), qs * eg, jnp.zeros((SUBLANES - 2, DK_DELTA), F32)],
                                   axis=0).astype(BF16),
                   s.astype(BF16), preferred_element_type=F32)
           for (b, h, hs, qs, kn, vh, beta, eg, s) in items]
    for (b, h, hs, qs, kn, vh, beta, eg, s), r in zip(items, res):
        v_new = vh * beta - r[0:1]
        qk = jnp.sum(qs.astype(BF16).astype(F32) * kn.astype(BF16).astype(F32), axis=-1, keepdims=True)
        o = r[1:2] + qk * v_new
        k_col = jnp.broadcast_to(kn, (DK_DELTA, DK_DELTA)).T
        s_ref[b, h] = s * eg + k_col * v_new
        o_ref[b, :, hs] = _delta_norm_gate(o, nw_ref[...], z_ref[b, :, hs]).astype(BF16)


def _delta_step(cin, conv_prev, conv_w, tail, alog_row, dtb_row, z, s0, norm_w):
    bs = cin.shape[0]
    seqs = math.gcd(bs, 4)
    tok = lambda width: pl.BlockSpec((seqs, 1, width), lambda b: (b, 0, 0))
    const = lambda shape: pl.BlockSpec(shape, lambda b: (0,) * len(shape))
    state = pl.BlockSpec((seqs, H_DELTA, DK_DELTA, DV_DELTA), lambda b: (b, 0, 0, 0))
    return pl.pallas_call(
        functools.partial(_delta_step_kernel, seqs=seqs),
        grid=(bs // seqs,),
        in_specs=[tok(CONV_CH), pl.BlockSpec((seqs, CONV_W - 1, CONV_CH), lambda b: (b, 0, 0)), const(conv_w.shape),
                  tok(LANES), const(alog_row.shape), const(dtb_row.shape), tok(W_DELTA), state,
                  const((1, DV_DELTA))],
        out_specs=[tok(W_DELTA), state],
        out_shape=[jax.ShapeDtypeStruct((bs, 1, W_DELTA), BF16),
                   jax.ShapeDtypeStruct((bs, H_DELTA, DK_DELTA, DV_DELTA), F32)],
        compiler_params=pltpu.CompilerParams(dimension_semantics=("arbitrary",)),
        name="delta_step",
    )(cin, conv_prev, conv_w, tail, alog_row, dtb_row, z, s0, norm_w.reshape(1, DV_DELTA))


def _out_proj_kernel(oa_ref, ob_ref, x_ref, w_ref, g_ref, b_ref, y_ref, *, alpha):
    mix = jnp.dot(oa_ref[...], w_ref[:W_DIFF, :], preferred_element_type=F32)
    mix = mix + jnp.dot(ob_ref[...], w_ref[W_DIFF:, :], preferred_element_type=F32)
    r = alpha * x_ref[...] + mix
    mu = jnp.mean(r, axis=-1, keepdims=True)
    var = jnp.mean(jnp.square(r - mu), axis=-1, keepdims=True)
    y_ref[...] = (r - mu) * lax.rsqrt(var + LN_EPS) * g_ref[...] + b_ref[...]


def _out_proj(oa, ob, x, w_out_b, ln_g, ln_b, alpha):
    m, d = x.shape
    tm = min(1024, m)
    assert m % tm == 0
    row = lambda width: pl.BlockSpec((tm, width), lambda i: (i, 0))
    const = lambda shape: pl.BlockSpec(shape, lambda i: (0, 0))
    return pl.pallas_call(
        functools.partial(_out_proj_kernel, alpha=alpha),
        grid=(m // tm,),
        in_specs=[row(W_DIFF), row(W_DELTA), row(d), const(w_out_b.shape), const((1, d)), const((1, d))],
        out_specs=row(d),
        out_shape=jax.ShapeDtypeStruct((m, d), F32),
        compiler_params=pltpu.CompilerParams(dimension_semantics=("arbitrary",), vmem_limit_bytes=VMEM_LIMIT),
        name="out_proj",
    )(oa, ob, x, w_out_b, ln_g.reshape(1, d), ln_b.reshape(1, d))


def _pad_lanes(vec, offset):
    return jnp.zeros((1, LANES), F32).at[0, offset:offset + vec.shape[0]].set(vec.astype(F32))


def _gate_rows_param(vec):
    col = jnp.broadcast_to(vec.astype(F32)[:, None], (H_DELTA, LANES))
    return jnp.concatenate([jnp.zeros((H_DELTA, LANES), F32), col], axis=0)


def kernel(x_prompt, x_sample, cache_k, cache_v, page_table, state_delta, state_conv, w_in, conv_w, a_log,
           dt_bias, delta_norm_w, diff_lambda, diff_norm_w, w_out, ln_g, ln_b):
    depth = w_in.shape[0]
    bp, tp, d = x_prompt.shape
    bs, ts, _ = x_sample.shape
    assert ts == 1 and w_in.shape[2] == P_MAIN + 2 * H_DELTA and d == w_out.shape[2]
    past_len = page_table.shape[1] * cache_k.shape[2]
    alpha = (2 * depth) ** 0.25
    tables_p = _rope_tables(jnp.arange(tp, dtype=jnp.int32))
    tables_s = _rope_tables(jnp.full((bs,), past_len, jnp.int32))

    hp, hs = x_prompt, x_sample
    outs = [[] for _ in range(8)]
    for l in range(depth):
        lam_init = _lambda_init(l)
        w_main = w_in[l].astype(BF16)
        w_tail = jnp.zeros((d, LANES), BF16).at[:, :2 * H_DELTA].set(w_in[l, :, P_MAIN:].astype(BF16))
        w_out_b = w_out[l].astype(BF16)
        alog_row = _pad_lanes(a_log[l], H_DELTA)
        dtb_row = _pad_lanes(dt_bias[l], H_DELTA)
        alog_col = _gate_rows_param(a_log[l])
        dtb_col = _gate_rows_param(dt_bias[l])

        sq, sk, _, sv, _, sga, scin, sz, stail = _in_proj(hs.reshape(1, bs, d), w_main, w_tail, tables_s, False)
        tok = lambda a: a.reshape(bs, 1, -1)
        decode_args = dict(q=tok(sq), k_new=tok(sk), v_new=tok(sv), ga=tok(sga), cache_k=cache_k, cache_v=cache_v,
                           layer=l, page_table=page_table, diff_lambda=diff_lambda[l], norm_w=diff_norm_w[l],
                           lam_init=lam_init)
        rider = _DecodeRider(**decode_args, seq0=0, n_seq=bs, grid=(bp, _delta_prep_grid(tp)[1]))
        if not rider.ok:
            rider = None

        q, k, kb, v, vb, ga, cin, z, _, tail_t = _in_proj(hp, w_main, w_tail, tables_p, True)
        oa = _prompt_attn(q, kb, vb, ga, diff_lambda[l], diff_norm_w[l], lam_init)
        prep, soa = _delta_prep(cin, jnp.zeros((bp, CONV_W - 1, CONV_CH), F32), conv_w[l], tail_t, alog_col,
                                dtb_col, rider)
        ob, sp = _delta_scan(prep, z, jnp.zeros((bp, H_DELTA, DK_DELTA, DV_DELTA), F32), delta_norm_w[l])
        hp = _out_proj(oa.reshape(bp * tp, W_DIFF), ob.reshape(bp * tp, W_DELTA), hp.reshape(bp * tp, d),
                       w_out_b, ln_g[l], ln_b[l], alpha).reshape(bp, tp, d)
        outs[0].append(k.reshape(bp, tp, H_DIFF, 2, DH_DIFF))
        outs[1].append(v.reshape(bp, tp, H_DIFF, DV_DIFF))
        outs[2].append(sp)
        outs[3].append(cin[:, tp - (CONV_W - 1):, :])

        if rider is None:
            soa = _decode_attn(decode_args, bs)
        sob, ss = _delta_step(tok(scin), state_conv[l], conv_w[l], tok(stail), alog_row, dtb_row, tok(sz),
                              state_delta[l], delta_norm_w[l])
        hs = _out_proj(soa.reshape(bs, W_DIFF), sob.reshape(bs, W_DELTA), hs.reshape(bs, d),
                       w_out_b, ln_g[l], ln_b[l], alpha).reshape(bs, 1, d)
        outs[4].append(sk.reshape(bs, 1, H_DIFF, 2, DH_DIFF))
        outs[5].append(sv.reshape(bs, 1, H_DIFF, DV_DIFF))
        outs[6].append(ss)
        outs[7].append(jnp.concatenate([state_conv[l][:, 1:, :], tok(scin)], axis=1))
    return (hp, hs) + tuple(jnp.stack(o) for o in outs)
```

```python
import functools
import math

import jax
import jax.numpy as jnp
from jax import lax
from jax.experimental import pallas as pl
from jax.experimental.pallas import tpu as pltpu

F32 = jnp.float32
BF16 = jnp.bfloat16
HIGHEST = lax.Precision.HIGHEST

H_DIFF = 4
DH_DIFF = 64
DV_DIFF = 2 * DH_DIFF
W_DIFF = H_DIFF * DV_DIFF
H_DELTA = 4
DK_DELTA = 128
DV_DELTA = 128
W_DELTA = H_DELTA * DK_DELTA
ROT_DIM = DH_DIFF // 4
ROPE_THETA = 500000.0
CONV_W = 4
CONV_CH = 3 * W_DELTA
CHUNK = 64
P_MAIN = 4 * W_DIFF + 4 * W_DELTA
LN_EPS = 1e-5
HEAD_NORM_EPS = 1e-5
DELTA_NORM_EPS = 1e-6
L2_EPS = 1e-6
NEG_INF = -1e30

LANES = 128
SUBLANES = 8
PAIR = 2 * CHUNK
CHUNK_BITS = CHUNK.bit_length() - 1
MAP_BITS = DH_DIFF.bit_length() - 1
VMEM_LIMIT = 56 * 1024 * 1024
DECODE_PAGES_PER_STEP = 16

_NT = (((1,), (1,)), ((), ()))


def _sigmoid(x):
    return 0.5 * jnp.tanh(0.5 * x) + 0.5


def _silu(x):
    return x * _sigmoid(x)


def _softplus(x):
    return jnp.maximum(x, 0.0) + jnp.log1p(jnp.exp(-jnp.abs(x)))


def _lambda_init(layer):
    return 0.8 - 0.6 * math.exp(-0.3 * layer)


def _rope_tables(pos):
    half = ROT_DIM // 2
    inv = ROPE_THETA ** (-jnp.arange(half, dtype=F32) / half)
    ang = pos.astype(F32)[:, None] * inv[None, :]
    cos, sin = jnp.cos(ang), jnp.sin(ang)
    t = pos.shape[0]
    rest = DH_DIFF - ROT_DIM
    cos_m = jnp.concatenate([cos, cos, jnp.ones((t, rest), F32)], axis=-1)
    sin_lo = jnp.concatenate([-sin, jnp.zeros((t, half + rest), F32)], axis=-1)
    sin_hi = jnp.concatenate([jnp.zeros((t, half), F32), sin, jnp.zeros((t, rest), F32)], axis=-1)
    reps = LANES // DH_DIFF
    return jnp.tile(cos_m, (1, reps)), jnp.tile(sin_lo, (1, reps)), jnp.tile(sin_hi, (1, reps))


def _in_proj_kernel(x_ref, w_ref, wt_ref, cos_ref, slo_ref, shi_ref,
                    q_ref, k_ref, kb_ref, v_ref, vb_ref, ga_ref, cin_ref, z_ref, tail_ref, *tail_t_ref):
    xb = x_ref[0].astype(BF16)
    half = ROT_DIM // 2
    tm = xb.shape[0]

    def proj(c0, width):
        return jnp.dot(xb, w_ref[:, c0:c0 + width], preferred_element_type=F32)

    def rope(h):
        blocks = []
        for c in range(W_DIFF // LANES):
            hb = h[:, c * LANES:(c + 1) * LANES]
            blocks.append(hb * cos_ref[...]
                          + pltpu.roll(hb, LANES - half, 1) * slo_ref[...]
                          + pltpu.roll(hb, half, 1) * shi_ref[...])
        return jnp.concatenate(blocks, axis=1)

    q_ref[0] = rope(proj(0, W_DIFF)).astype(BF16)
    k = rope(proj(W_DIFF, W_DIFF))
    k_ref[0] = k
    kb_ref[0] = k.astype(BF16)
    v = proj(2 * W_DIFF, W_DIFF)
    for h in range(H_DIFF):
        v_ref[0, pl.ds(h, tm, stride=H_DIFF), :] = v[:, h * DV_DIFF:(h + 1) * DV_DIFF]
    vb_ref[0] = v.astype(BF16)
    ga_ref[0] = proj(3 * W_DIFF, W_DIFF)
    for j in range(3):
        cin_ref[0, :, j * W_DELTA:(j + 1) * W_DELTA] = proj(4 * W_DIFF + j * W_DELTA, W_DELTA)
    z_ref[0] = proj(4 * W_DIFF + 3 * W_DELTA, W_DELTA)
    tail = jnp.dot(xb, wt_ref[...], preferred_element_type=F32)
    tail_ref[0] = tail
    if tail_t_ref:
        for c in range(tm // LANES):
            cs = slice(c * LANES, (c + 1) * LANES)
            tail_t_ref[0][0, :, cs] = tail[cs, :].T[:2 * H_DELTA, :]


def _in_proj(x, w_main, w_tail, tables, gates_time_major):
    b, t, d = x.shape
    tm = min(512, t)
    assert t % tm == 0 and (not gates_time_major or tm % LANES == 0)
    cos_t, slo_t, shi_t = tables
    row = lambda width: pl.BlockSpec((1, tm, width), lambda ti, bi: (bi, ti, 0))
    tab = pl.BlockSpec((tm, LANES), lambda ti, bi: (ti, 0))
    const = lambda shape: pl.BlockSpec(shape, lambda ti, bi: (0, 0))
    outs = [(W_DIFF, BF16), (W_DIFF, F32), (W_DIFF, BF16), None, (W_DIFF, BF16),
            (W_DIFF, F32), (CONV_CH, F32), (W_DELTA, F32), (LANES, F32)]
    out_specs = [row(o[0]) if o else pl.BlockSpec((1, tm * H_DIFF, DV_DIFF), lambda ti, bi: (bi, ti, 0))
                 for o in outs]
    out_shape = [jax.ShapeDtypeStruct((b, t, o[0]), o[1]) if o else
                 jax.ShapeDtypeStruct((b, t * H_DIFF, DV_DIFF), F32) for o in outs]
    if gates_time_major:
        out_specs.append(pl.BlockSpec((1, 2 * H_DELTA, tm), lambda ti, bi: (bi, 0, ti)))
        out_shape.append(jax.ShapeDtypeStruct((b, 2 * H_DELTA, t), F32))
    return pl.pallas_call(
        _in_proj_kernel,
        grid=(t // tm, b),
        in_specs=[row(d), const((d, P_MAIN)), const(w_tail.shape), tab, tab, tab],
        out_specs=out_specs,
        out_shape=out_shape,
        compiler_params=pltpu.CompilerParams(
            dimension_semantics=("arbitrary", "arbitrary"), vmem_limit_bytes=VMEM_LIMIT),
        name="in_proj",
    )(x, w_main, w_tail, cos_t, slo_t, shi_t)


def _diff_lambda_value(dl_ref, lam_init):
    dl = dl_ref[...]
    a = jnp.sum(dl[0:1] * dl[1:2], axis=1, keepdims=True)
    b = jnp.sum(dl[2:3] * dl[3:4], axis=1, keepdims=True)
    return jnp.exp(a) - jnp.exp(b) + lam_init


def _head_norm_gate(o, normw, gate, lam_init):
    ms = jnp.mean(o * o, axis=-1, keepdims=True)
    o = o * lax.rsqrt(ms + HEAD_NORM_EPS) * normw * (1.0 - lam_init)
    return o * _silu(gate)


def _prompt_attn_kernel(q0_ref, qn_ref, k_ref, v_ref, ga_ref, dl_ref, nw_ref, o_ref, s_scr, m_scr, l_scr, acc_scr,
                        *, blk, lam_init):
    i = pl.program_id(1)
    last = pl.num_programs(1) - 1
    slot = lax.rem(i, 2)
    lane = lax.broadcasted_iota(jnp.int32, (blk, DV_DIFF), 1)
    heads = [slice(h * DV_DIFF, (h + 1) * DV_DIFF) for h in range(H_DIFF)]
    lane_chunks = [slice(c * LANES, (c + 1) * LANES) for c in range(blk // LANES)]
    n_maps = 2 * H_DIFF

    def query_maps(q_ref):
        maps = []
        for hs in heads:
            qs = q_ref[0, :, hs].astype(F32) * (DH_DIFF ** -0.5 * math.log2(math.e))
            maps.append(jnp.where(lane < DH_DIFF, qs, 0.0).astype(BF16))
            maps.append(jnp.where(lane >= DH_DIFF, qs, 0.0).astype(BF16))
        return maps

    def fold(x, op):
        r = x[:, lane_chunks[0]]
        for c in lane_chunks[1:]:
            r = op(r, x[:, c])
        return r

    def score_block(q_maps, dst, j, masked):
        rows = pl.ds(pl.multiple_of(j * blk, blk), blk)
        for h, hs in enumerate(heads):
            k = k_ref[0, rows, hs]
            for mi in (2 * h, 2 * h + 1):
                s = lax.dot_general(q_maps[mi], k, _NT, preferred_element_type=F32)
                if masked:
                    row = lax.broadcasted_iota(jnp.int32, (blk, blk), 0)
                    col = lax.broadcasted_iota(jnp.int32, (blk, blk), 1)
                    s = jnp.where(col <= row, s, NEG_INF)
                s_scr[dst, j, mi] = s
                m_scr[dst, mi] = jnp.maximum(m_scr[dst, mi], fold(s, jnp.maximum))

    def finish_max(dst):
        for mi in range(n_maps):
            m_scr[dst, mi] = jnp.broadcast_to(jnp.max(m_scr[dst, mi], axis=-1, keepdims=True), (blk, LANES))

    def prob_block(j):
        rows = pl.ds(pl.multiple_of(j * blk, blk), blk)
        for h, hs in enumerate(heads):
            v = v_ref[0, rows, hs]
            for mi in (2 * h, 2 * h + 1):
                m = m_scr[slot, mi]
                p = jnp.concatenate([jnp.exp2(s_scr[slot, j, mi, :, c] - m) for c in lane_chunks], axis=1)
                l_scr[mi] = l_scr[mi] + fold(p, jnp.add)
                acc_scr[mi] = acc_scr[mi] + jnp.dot(p.astype(BF16), v, preferred_element_type=F32)

    def write_output():
        lam = _diff_lambda_value(dl_ref, lam_init)
        for h, hs in enumerate(heads):
            l1 = jnp.sum(l_scr[2 * h], axis=-1, keepdims=True)
            l2 = jnp.sum(l_scr[2 * h + 1], axis=-1, keepdims=True)
            o = acc_scr[2 * h] / l1 - lam * (acc_scr[2 * h + 1] / l2)
            o_ref[0, :, hs] = _head_norm_gate(o, nw_ref[...], ga_ref[0, :, hs], lam_init).astype(BF16)

    @pl.when(i == 0)
    def _():
        m_scr[0] = jnp.full(m_scr.shape[1:], NEG_INF, F32)
        score_block(query_maps(q0_ref), 0, 0, True)
        finish_max(0)

    l_scr[...] = jnp.zeros(l_scr.shape, F32)
    acc_scr[...] = jnp.zeros(acc_scr.shape, F32)

    @pl.when(i < last)
    def _():
        nxt = 1 - slot
        q_next = query_maps(qn_ref)
        m_scr[nxt] = jnp.full(m_scr.shape[1:], NEG_INF, F32)

        def both(j, carry):
            prob_block(j)
            score_block(q_next, nxt, j, False)
            return carry

        lax.fori_loop(0, i + 1, both, 0)
        write_output()
        score_block(q_next, nxt, i + 1, True)
        finish_max(nxt)

    @pl.when(i == last)
    def _():
        def only_probs(j, carry):
            prob_block(j)
            return carry

        lax.fori_loop(0, i + 1, only_probs, 0)
        write_output()


def _prompt_attn(q, kb, vb, ga, diff_lambda, norm_w, lam_init):
    b, t, _ = q.shape
    blk = min(256, t)
    assert t % blk == 0
    n_maps = 2 * H_DIFF
    nq = t // blk
    qspec = pl.BlockSpec((1, blk, W_DIFF), lambda bi, i: (bi, i, 0))
    q_first = pl.BlockSpec((1, blk, W_DIFF), lambda bi, i: (bi, 0, 0))
    q_next = pl.BlockSpec((1, blk, W_DIFF), lambda bi, i: (bi, jnp.minimum(i + 1, nq - 1), 0))
    kvspec = pl.BlockSpec((1, t, W_DIFF), lambda bi, i: (bi, 0, 0))
    return pl.pallas_call(
        functools.partial(_prompt_attn_kernel, blk=blk, lam_init=lam_init),
        grid=(b, nq),
        in_specs=[q_first, q_next, kvspec, kvspec, qspec,
                  pl.BlockSpec(diff_lambda.shape, lambda bi, i: (0, 0)),
                  pl.BlockSpec((1, DV_DIFF), lambda bi, i: (0, 0))],
        out_specs=qspec,
        out_shape=jax.ShapeDtypeStruct((b, t, W_DIFF), BF16),
        scratch_shapes=[pltpu.VMEM((2, nq, n_maps, blk, blk), F32),
                        pltpu.VMEM((2, n_maps, blk, LANES), F32),
                        pltpu.VMEM((n_maps, blk, LANES), F32),
                        pltpu.VMEM((n_maps, blk, DV_DIFF), F32)],
        compiler_params=pltpu.CompilerParams(
            dimension_semantics=("arbitrary", "arbitrary"), vmem_limit_bytes=VMEM_LIMIT),
        name="prompt_attn",
    )(q, q, kb, vb, ga, diff_lambda, norm_w.reshape(1, DV_DIFF))


class _DecodeStep:
    def __init__(self, step, n_steps, spq, first_page, pages, pt_ref, in_refs, o_ref, scratch_refs, lam_init):
        self.step, self.n_steps, self.pages, self.first_page = step, n_steps, pages, first_page
        self.g, self.n_g, self.lam_init, self.pt_ref = lax.rem(step, spq), spq, lam_init, pt_ref
        (self.q_ref, self.kn_ref, self.vn_ref, self.ga_ref, self.dl_ref, self.nw_ref,
         self.ck_ref, self.cv_ref) = in_refs
        self.o_ref = o_ref
        (self.qcol_ref, self.m_ref, self.l_ref, self.acc_ref,
         self.kbuf_ref, self.vbuf_ref, self.sem_ref) = scratch_refs
        self.slot = lax.rem(step, 2)

    def _query(self):
        return self.q_ref[0].astype(F32) * DH_DIFF ** -0.5

    def _page_copies(self, step, slot):
        base = self.first_page + step * self.pages
        copies = []
        for i in range(self.pages):
            pid = self.pt_ref[base + i]
            copies.append(pltpu.make_async_copy(self.ck_ref.at[pid], self.kbuf_ref.at[slot, i],
                                                self.sem_ref.at[slot, 0]))
            copies.append(pltpu.make_async_copy(self.cv_ref.at[pid], self.vbuf_ref.at[slot, i],
                                                self.sem_ref.at[slot, 1]))
        return copies

    def init(self):
        @pl.when(self.step == 0)
        def _():
            for cp in self._page_copies(0, 0):
                cp.start()

        for cp in self._page_copies(self.step, self.slot):
            cp.wait()

        @pl.when(self.step + 1 < self.n_steps)
        def _():
            for cp in self._page_copies(self.step + 1, 1 - self.slot):
                cp.start()

        @pl.when(self.g == 0)
        def _():
            self.m_ref[...] = jnp.full(self.m_ref.shape, NEG_INF, F32)
            self.l_ref[...] = jnp.zeros(self.l_ref.shape, F32)
            self.acc_ref[...] = jnp.zeros(self.acc_ref.shape, F32)
            qs = self._query()
            for c in range(W_DIFF // LANES):
                cs = slice(c * LANES, (c + 1) * LANES)
                self.qcol_ref[cs, :] = jnp.broadcast_to(qs[:, cs], (LANES, LANES)).T

    def main(self):
        k_pages = [self.kbuf_ref.at[self.slot, i] for i in range(self.pages)]
        v_pages = [self.vbuf_ref.at[self.slot, i] for i in range(self.pages)]
        _decode_pages(k_pages, v_pages, self.qcol_ref, self.m_ref, self.l_ref, self.acc_ref)

    def finalize(self):
        @pl.when(self.g == self.n_g - 1)
        def _():
            _decode_finish(self._query(), self.kn_ref, self.vn_ref, self.ga_ref, self.dl_ref, self.nw_ref,
                           self.o_ref, self.m_ref, self.l_ref, self.acc_ref, self.lam_init)


def _decode_pages(kp_refs, vp_refs, qcol_ref, m_ref, l_ref, acc_ref):
    n_maps = 2 * H_DIFF
    page = kp_refs[0].shape[1]
    qcol = qcol_ref[...]
    s = jnp.concatenate(
        [jnp.sum((kp[...] * qcol).reshape(n_maps, DH_DIFF, page), axis=1) for kp in kp_refs], axis=1)
    m_prev = m_ref[:, 0:1]
    m_new = jnp.maximum(m_prev, jnp.max(s, axis=-1, keepdims=True))
    p = jnp.exp(s - m_new)
    alpha = jnp.exp(m_prev - m_new)
    l_new = alpha * l_ref[:, 0:1] + jnp.sum(p, axis=-1, keepdims=True)
    pb = p.astype(BF16)
    for h in range(H_DIFF):
        pv = jnp.zeros((n_maps, DV_DIFF), F32)
        for i, vp in enumerate(vp_refs):
            v_head = vp[pl.ds(h, page, stride=H_DIFF), :].astype(BF16)
            pv = pv + jnp.dot(pb[:, i * page:(i + 1) * page], v_head, preferred_element_type=F32)
        acc_ref[h] = alpha * acc_ref[h] + pv
    m_ref[...] = jnp.broadcast_to(m_new, m_ref.shape)
    l_ref[...] = jnp.broadcast_to(l_new, l_ref.shape)


def _decode_finish(qs, kn_ref, vn_ref, ga_ref, dl_ref, nw_ref, o_ref, m_ref, l_ref, acc_ref, lam_init):
    n_maps = 2 * H_DIFF
    rows = lax.broadcasted_iota(jnp.int32, (n_maps, W_DIFF), 0)
    lanes = lax.broadcasted_iota(jnp.int32, (n_maps, W_DIFF), 1)
    q_blk = jnp.where((lanes >> MAP_BITS) == rows, jnp.broadcast_to(qs, (n_maps, W_DIFF)), 0.0).astype(BF16)
    k_self = jnp.broadcast_to(kn_ref[0].astype(BF16), (SUBLANES, W_DIFF))
    s_self = lax.dot_general(q_blk, k_self, _NT, preferred_element_type=F32)[:, 0:1]
    m_past = m_ref[:, 0:1]
    m_f = jnp.maximum(m_past, s_self)
    a_f = jnp.exp(m_past - m_f)
    p_self = jnp.exp(s_self - m_f)
    l_f = a_f * l_ref[:, 0:1] + p_self
    pv_self = p_self.astype(BF16).astype(F32) * vn_ref[0].astype(BF16).astype(F32)
    lam = _diff_lambda_value(dl_ref, lam_init)
    for h in range(H_DIFF):
        hs = slice(h * DV_DIFF, (h + 1) * DV_DIFF)
        a = (a_f * acc_ref[h] + pv_self[:, hs]) / l_f
        o = a[2 * h:2 * h + 1] - lam * a[2 * h + 1:2 * h + 2]
        o_ref[0, :, hs] = _head_norm_gate(o, nw_ref[...], ga_ref[0, :, hs], lam_init).astype(BF16)


class _DecodeRider:
    def __init__(self, q, k_new, v_new, ga, cache_k, cache_v, layer, page_table, diff_lambda, norm_w, lam_init,
                 seq0, n_seq, grid):
        depth, n_phys, page = cache_k.shape[:3]
        n_pages = page_table.shape[1]
        n_steps = grid[0] * grid[1]
        assert page == LANES
        pages = n_seq * n_pages // n_steps
        self.ok = pages >= 1 and pages * n_steps == n_seq * n_pages and n_pages % pages == 0
        if not self.ok:
            return
        spq = n_pages // pages
        self.pages, self.spq, self.lam_init, self.grid = pages, spq, lam_init, grid
        self.first_page = seq0 * n_pages
        ck = jnp.transpose(cache_k, (0, 1, 3, 4, 5, 2)).reshape(depth * n_phys, W_DIFF, page)
        cv = cache_v.reshape(depth * n_phys, page * H_DIFF, DV_DIFF)
        self.page_ids = page_table.reshape(-1) + layer * n_phys
        step = lambda i0, i1: i0 * grid[1] + i1
        tok = pl.BlockSpec((1, 1, W_DIFF), lambda i0, i1, pt: (seq0 + step(i0, i1) // spq, 0, 0))
        const = lambda shape: pl.BlockSpec(shape, lambda i0, i1, pt: (0, 0))
        hbm = pl.BlockSpec(memory_space=pl.ANY)
        self.inputs = [q, k_new, v_new, ga, diff_lambda, norm_w.reshape(1, DV_DIFF), ck, cv]
        self.in_specs = [tok, tok, tok, tok, const(diff_lambda.shape), const((1, DV_DIFF)), hbm, hbm]
        self.out_spec = pl.BlockSpec((1, 1, W_DIFF), lambda i0, i1, pt: (step(i0, i1) // spq, 0, 0))
        self.out_shape = jax.ShapeDtypeStruct((n_seq, 1, W_DIFF), BF16)
        n_maps = 2 * H_DIFF
        self.scratch_shapes = [pltpu.VMEM((W_DIFF, page), F32),
                               pltpu.VMEM((n_maps, LANES), F32), pltpu.VMEM((n_maps, LANES), F32),
                               pltpu.VMEM((H_DIFF, n_maps, DV_DIFF), F32),
                               pltpu.VMEM((2, pages, W_DIFF, page), F32),
                               pltpu.VMEM((2, pages, page * H_DIFF, DV_DIFF), F32),
                               pltpu.SemaphoreType.DMA((2, 2))]

    def step(self, pt_ref, in_refs, out_ref, scratch_refs):
        step = pl.program_id(0) * self.grid[1] + pl.program_id(1)
        return _DecodeStep(step, self.grid[0] * self.grid[1], self.spq, self.first_page, self.pages, pt_ref,
                           in_refs, out_ref, scratch_refs, self.lam_init)


def _ride(host_kernel, n_in, n_out, n_scratch, rider):
    n_rin = len(rider.inputs)

    def kernel(pt_ref, *refs):
        host_in, refs = refs[:n_in], refs[n_in:]
        rider_in, refs = refs[:n_rin], refs[n_rin:]
        host_out, refs = refs[:n_out], refs[n_out:]
        rider_out, refs = refs[0], refs[1:]
        host_scratch, rider_scratch = refs[:n_scratch], refs[n_scratch:]
        host_kernel(*host_in, *host_out, *host_scratch,
                    decode=rider.step(pt_ref, rider_in, rider_out, rider_scratch))

    return kernel


def _hosted_call(host_kernel, grid, in_specs, out_specs, out_shape, scratch_shapes, inputs, name, rider):
    params = pltpu.CompilerParams(dimension_semantics=("arbitrary", "arbitrary"), vmem_limit_bytes=VMEM_LIMIT)
    if rider is None:
        outs = pl.pallas_call(host_kernel, grid=grid, in_specs=in_specs, out_specs=out_specs, out_shape=out_shape,
                              scratch_shapes=scratch_shapes, compiler_params=params, name=name)(*inputs)
        return outs, None
    grid_spec = pltpu.PrefetchScalarGridSpec(
        num_scalar_prefetch=1, grid=grid,
        in_specs=list(in_specs) + rider.in_specs,
        out_specs=list(out_specs) + [rider.out_spec],
        scratch_shapes=list(scratch_shapes) + rider.scratch_shapes)
    outs = pl.pallas_call(
        _ride(host_kernel, len(in_specs), len(out_specs), len(scratch_shapes), rider),
        grid_spec=grid_spec, out_shape=list(out_shape) + [rider.out_shape],
        compiler_params=params, name=name)(rider.page_ids, *inputs, *rider.inputs)
    return outs[:-1], outs[-1]


def _decode_attn_kernel(pt_ref, *refs, rider):
    n_rin = len(rider.inputs)
    decode = rider.step(pt_ref, refs[:n_rin], refs[n_rin], refs[n_rin + 1:])
    decode.init()
    decode.main()
    decode.finalize()


def _decode_attn(rider_args, n_seq):
    n_pages = rider_args["page_table"].shape[1]
    grid = (n_seq, n_pages // math.gcd(DECODE_PAGES_PER_STEP, n_pages))
    rider = _DecodeRider(**rider_args, seq0=0, n_seq=n_seq, grid=grid)
    grid_spec = pltpu.PrefetchScalarGridSpec(
        num_scalar_prefetch=1, grid=grid, in_specs=rider.in_specs,
        out_specs=rider.out_spec, scratch_shapes=rider.scratch_shapes)
    return pl.pallas_call(
        functools.partial(_decode_attn_kernel, rider=rider), grid_spec=grid_spec, out_shape=rider.out_shape,
        compiler_params=pltpu.CompilerParams(
            dimension_semantics=("arbitrary", "arbitrary"), vmem_limit_bytes=VMEM_LIMIT),
        name="decode_attn")(rider.page_ids, *rider.inputs)


def _gate_rows(tail, alog_row, dtb_row):
    beta = _sigmoid(tail)
    g = -jnp.exp(alog_row) * _softplus(tail + dtb_row)
    return beta, g


def _lane_bcast(x, lane, rows):
    return jnp.broadcast_to(x[:, lane:lane + 1], (rows, LANES))


def _l2norm(x):
    return x * lax.rsqrt(jnp.sum(x * x, axis=-1, keepdims=True) + L2_EPS)


def _split_bf16(x):
    hi = x.astype(BF16)
    return hi, (x - hi.astype(F32)).astype(BF16)


def _dot_split(lhs, rhs):
    d = lambda a, b: jnp.dot(a, b, preferred_element_type=F32)
    return d(lhs[0], rhs[0]) + d(lhs[0], rhs[1]) + d(lhs[1], rhs[0])


def _unit_lower_inverses(a_mats):
    n = a_mats[0].shape[0]
    eye = (lax.broadcasted_iota(jnp.int32, (n, n), 0) == lax.broadcasted_iota(jnp.int32, (n, n), 1)).astype(F32)
    xs = [-a for a in a_mats]
    ps = [eye + x for x in xs]
    splits = [_split_bf16(x) for x in xs]
    xs = [_dot_split(s, s) for s in splits]
    power = 2
    while 2 * power < CHUNK:
        both = [_dot_split(_split_bf16(x), _split_bf16(jnp.concatenate([p, x], axis=1))) for p, x in zip(ps, xs)]
        ps = [p + b[:, :n] for p, b in zip(ps, both)]
        xs = [b[:, n:] for b in both]
        power *= 2
    return [p + _dot_split(_split_bf16(x), _split_bf16(p)) for p, x in zip(ps, xs)]


def _delta_prep_kernel(cin_ref, prev_ref, cw_ref, tail_ref, alog_ref, dtb_ref,
                       u_ref, w_ref, qg_ref, at_ref, kdt_ref, el_ref, ext_ref, *, pairs, decode=None):
    t = pl.program_id(1)
    pad = SUBLANES
    keep = CONV_W - 1
    rows = pairs * PAIR

    @pl.when(t == 0)
    def _():
        ext_ref[pad - keep:pad, :] = prev_ref[0]

    @pl.when(t > 0)
    def _():
        ext_ref[pad - keep:pad, :] = ext_ref[pad + rows - keep:pad + rows, :]

    if decode is not None:
        decode.init()

    ext_ref[pad:pad + rows, :] = cin_ref[0]

    r = lax.broadcasted_iota(jnp.int32, (PAIR, PAIR), 0)
    c = lax.broadcasted_iota(jnp.int32, (PAIR, PAIR), 1)
    same = (r >> CHUNK_BITS) == (c >> CHUNK_BITS)
    incl = same & (c <= r)
    strict = same & (c < r)
    incl_t = same & (r <= c)
    chunk_end = r == (c | (CHUNK - 1))

    chains = []
    for pi in range(pairs):
        base = pad - keep + pi * PAIR
        conv = ext_ref[base:base + PAIR, :] * cw_ref[0:1, :]
        for j in range(1, CONV_W):
            conv = conv + ext_ref[base + j:base + j + PAIR, :] * cw_ref[j:j + 1, :]
        conv = _silu(conv)
        rs = slice(pi * PAIR, (pi + 1) * PAIR)
        beta_t, g_t = _gate_rows(tail_ref[0, :, rs], alog_ref[...], dtb_ref[...])
        gc_t = jnp.dot(g_t, incl_t.astype(F32), precision=HIGHEST, preferred_element_type=F32)
        g_last_t = jnp.dot(gc_t, chunk_end.astype(F32), precision=HIGHEST, preferred_element_type=F32)
        for h in range(H_DELTA):
            hs = slice(h * DK_DELTA, (h + 1) * DK_DELTA)
            qs = _l2norm(conv[:, hs]) * DK_DELTA ** -0.5
            kn = _l2norm(conv[:, W_DELTA + h * DK_DELTA:W_DELTA + (h + 1) * DK_DELTA])
            vh = conv[:, 2 * W_DELTA + h * DV_DELTA:2 * W_DELTA + (h + 1) * DV_DELTA]
            beta = jnp.broadcast_to(beta_t[h:h + 1, :], (PAIR, PAIR)).T
            gc_cols = jnp.broadcast_to(gc_t[H_DELTA + h:H_DELTA + h + 1, :], (PAIR, PAIR))
            g_last_cols = jnp.broadcast_to(g_last_t[H_DELTA + h:H_DELTA + h + 1, :], (PAIR, PAIR))
            chains.append(dict(pi=pi, h=h, rs=rs, qs=qs, kn=kn, vh=vh, beta=beta, gc=gc_cols.T, gc_cols=gc_cols,
                               g_last_cols=g_last_cols, kbeta=kn * beta))

    for ch in chains:
        ch["kn_t"] = ch["kn"].T
        kn_tb = ch["kn_t"].astype(BF16)
        ch["kk"] = jnp.dot(ch["kbeta"].astype(BF16), kn_tb, preferred_element_type=F32)
        ch["qk"] = jnp.dot(ch["qs"].astype(BF16), kn_tb, preferred_element_type=F32)
    for ch in chains:
        ch["decay"] = jnp.exp(jnp.where(incl, ch["gc"] - ch["gc_cols"], -jnp.inf))
    t_mats = _unit_lower_inverses([jnp.where(strict, ch["kk"] * ch["decay"], 0.0) for ch in chains])
    for ch, t_mat in zip(chains, t_mats):
        rhs = jnp.concatenate([ch["vh"] * ch["beta"], ch["kbeta"] * jnp.exp(ch["gc"])], axis=1).astype(BF16)
        ch["uw"] = jnp.dot(t_mat.astype(BF16), rhs, preferred_element_type=F32)
    for ch in chains:
        pi, h, rs, gc = ch["pi"], ch["h"], ch["rs"], ch["gc"]
        u_ref[0, h, rs, :] = ch["uw"][:, :DV_DELTA]
        w_ref[0, h, rs, :] = ch["uw"][:, DV_DELTA:].astype(BF16)
        attn = jnp.where(incl, ch["qk"] * ch["decay"], 0.0)
        at_ref[0, h, rs, :] = jnp.concatenate([attn[:CHUNK, :CHUNK], attn[CHUNK:, CHUNK:]], axis=0).astype(BF16)
        qg_ref[0, h, rs, :] = (ch["qs"] * jnp.exp(gc)).astype(BF16)
        kdt_ref[0, h, pi] = (ch["kn_t"] * jnp.exp(ch["g_last_cols"] - ch["gc_cols"])).astype(BF16)
        el_ref[0, h, pi] = jnp.exp(jnp.concatenate([jnp.broadcast_to(gc[CHUNK - 1:CHUNK], (SUBLANES, LANES)),
                                                    jnp.broadcast_to(gc[PAIR - 1:PAIR], (SUBLANES, LANES))], axis=0))

    if decode is not None:
        decode.main()
        decode.finalize()


def _delta_prep_grid(t):
    assert t % PAIR == 0
    npair = t // PAIR
    pairs = 2 if npair % 2 == 0 else 1
    return pairs, npair // pairs


def _delta_prep(cin, conv_prev, conv_w, tail, alog_row, dtb_row, rider):
    b, t, _ = cin.shape
    npair = t // PAIR
    pairs, nt = _delta_prep_grid(t)
    rows = pairs * PAIR
    per_head = lambda width: pl.BlockSpec((1, H_DELTA, rows, width), lambda bi, ti, *_: (bi, 0, ti, 0))
    const = lambda shape: pl.BlockSpec(shape, lambda bi, ti, *_: (0,) * len(shape))
    out_shape = [
        jax.ShapeDtypeStruct((b, H_DELTA, t, DV_DELTA), F32),
        jax.ShapeDtypeStruct((b, H_DELTA, t, DK_DELTA), BF16),
        jax.ShapeDtypeStruct((b, H_DELTA, t, DK_DELTA), BF16),
        jax.ShapeDtypeStruct((b, H_DELTA, t, CHUNK), BF16),
        jax.ShapeDtypeStruct((b, H_DELTA, npair, DK_DELTA, PAIR), BF16),
        jax.ShapeDtypeStruct((b, H_DELTA, npair, 2 * SUBLANES, LANES), F32),
    ]
    out_specs = [per_head(DV_DELTA), per_head(DK_DELTA), per_head(DK_DELTA), per_head(CHUNK),
                 pl.BlockSpec((1, H_DELTA, pairs, DK_DELTA, PAIR), lambda bi, ti, *_: (bi, 0, ti, 0, 0)),
                 pl.BlockSpec((1, H_DELTA, pairs, 2 * SUBLANES, LANES), lambda bi, ti, *_: (bi, 0, ti, 0, 0))]
    return _hosted_call(
        functools.partial(_delta_prep_kernel, pairs=pairs),
        grid=(b, nt),
        in_specs=[pl.BlockSpec((1, rows, CONV_CH), lambda bi, ti, *_: (bi, ti, 0)),
                  pl.BlockSpec((1, CONV_W - 1, CONV_CH), lambda bi, ti, *_: (bi, 0, 0)),
                  const(conv_w.shape),
                  pl.BlockSpec((1, 2 * H_DELTA, rows), lambda bi, ti, *_: (bi, 0, ti)),
                  const(alog_row.shape), const(dtb_row.shape)],
        out_specs=out_specs,
        out_shape=out_shape,
        scratch_shapes=[pltpu.VMEM((SUBLANES + rows, CONV_CH), F32)],
        inputs=(cin, conv_prev, conv_w, tail, alog_row, dtb_row),
        name="delta_prep", rider=rider)


def _delta_norm_gate(o, normw, z):
    ms = jnp.mean(o * o, axis=-1, keepdims=True)
    return o * lax.rsqrt(ms + DELTA_NORM_EPS) * normw * _silu(z)


def _delta_scan_kernel(u_ref, w_ref, qg_ref, at_ref, kdt_ref, el_ref, z_ref, s0_ref, nw_ref,
                       o_ref, sfin_ref, s_scr, *, nb, nblk):
    t = pl.program_id(1)

    @pl.when(t == 0)
    def _():
        s_scr[...] = s0_ref[...]

    def block(jb, carry):
        r0 = pl.multiple_of(jb * PAIR, PAIR)
        seqs = [(bi, h) for bi in range(nb) for h in range(H_DELTA)]
        for ci in range(2):
            rows = pl.ds(pl.multiple_of(r0 + ci * CHUNK, CHUNK), CHUNK)
            states = [s_scr[bi, h] for bi, h in seqs]
            res = [jnp.dot(jnp.concatenate([w_ref[bi, h, rows, :], qg_ref[bi, h, rows, :]], axis=0),
                           s.astype(BF16), preferred_element_type=F32) for (bi, h), s in zip(seqs, states)]
            v_new = [(u_ref[bi, h, rows, :] - r[:CHUNK]).astype(BF16) for (bi, h), r in zip(seqs, res)]
            upd = [jnp.dot(kdt_ref[bi, h, jb, :, ci * CHUNK:(ci + 1) * CHUNK], v, preferred_element_type=F32)
                   for (bi, h), v in zip(seqs, v_new)]
            intra = [jnp.dot(at_ref[bi, h, rows, :], v, preferred_element_type=F32) for (bi, h), v in zip(seqs, v_new)]
            for (bi, h), s, r, du, oi in zip(seqs, states, res, upd, intra):
                decay = jnp.broadcast_to(el_ref[bi, h, jb, ci * SUBLANES:ci * SUBLANES + 1, :], (DK_DELTA, DV_DELTA))
                s_scr[bi, h] = s * decay + du
                hs = slice(h * DV_DELTA, (h + 1) * DV_DELTA)
                o_ref[bi, rows, hs] = _delta_norm_gate(r[CHUNK:] + oi, nw_ref[...], z_ref[bi, rows, hs]).astype(BF16)
        return carry

    lax.fori_loop(0, nblk, block, 0)

    @pl.when(t == pl.num_programs(1) - 1)
    def _():
        sfin_ref[...] = s_scr[...]


def _delta_scan(prep, z, s0, norm_w):
    u, w, qg, at, kdt, el = prep
    b, _, t, _ = u.shape
    nb = math.gcd(b, 4)
    ts = min(512, t)
    assert t % ts == 0
    nblk = ts // PAIR
    per_head = lambda width: pl.BlockSpec((nb, H_DELTA, ts, width), lambda bi, ti: (bi, 0, ti, 0))
    state = pl.BlockSpec((nb, H_DELTA, DK_DELTA, DV_DELTA), lambda bi, ti: (bi, 0, 0, 0))
    tok = pl.BlockSpec((nb, ts, W_DELTA), lambda bi, ti: (bi, ti, 0))
    return pl.pallas_call(
        functools.partial(_delta_scan_kernel, nb=nb, nblk=nblk),
        grid=(b // nb, t // ts),
        in_specs=[per_head(DV_DELTA), per_head(DK_DELTA), per_head(DK_DELTA), per_head(CHUNK),
                  pl.BlockSpec((nb, H_DELTA, nblk, DK_DELTA, PAIR), lambda bi, ti: (bi, 0, ti, 0, 0)),
                  pl.BlockSpec((nb, H_DELTA, nblk, 2 * SUBLANES, LANES), lambda bi, ti: (bi, 0, ti, 0, 0)),
                  tok, state, pl.BlockSpec((1, DV_DELTA), lambda bi, ti: (0, 0))],
        out_specs=[tok, state],
        out_shape=[jax.ShapeDtypeStruct((b, t, W_DELTA), BF16),
                   jax.ShapeDtypeStruct((b, H_DELTA, DK_DELTA, DV_DELTA), F32)],
        scratch_shapes=[pltpu.VMEM((nb, H_DELTA, DK_DELTA, DV_DELTA), F32)],
        compiler_params=pltpu.CompilerParams(
            dimension_semantics=("arbitrary", "arbitrary"), vmem_limit_bytes=VMEM_LIMIT),
        name="delta_scan",
    )(u, w, qg, at, kdt, el, z, s0, norm_w.reshape(1, DV_DELTA))


def _delta_step_kernel(cin_ref, prev_ref, cw_ref, tail_ref, alog_ref, dtb_ref, z_ref, s0_ref, nw_ref,
                       o_ref, s_ref, *, seqs):
    items = []
    for b in range(seqs):
        prev = prev_ref[b]
        conv = prev[0:1] * cw_ref[0:1, :]
        for j in range(1, CONV_W - 1):
            conv = conv + prev[j:j + 1] * cw_ref[j:j + 1, :]
        conv = _silu(conv + cin_ref[b] * cw_ref[CONV_W - 1:CONV_W, :])
        beta_all, g_all = _gate_rows(tail_ref[b], alog_ref[...], dtb_ref[...])
        for h in range(H_DELTA):
            hs = slice(h * DK_DELTA, (h + 1) * DK_DELTA)
            qs = _l2norm(conv[:, hs]) * DK_DELTA ** -0.5
            kn = _l2norm(conv[:, W_DELTA + h * DK_DELTA:W_DELTA + (h + 1) * DK_DELTA])
            vh = conv[:, 2 * W_DELTA + h * DV_DELTA:2 * W_DELTA + (h + 1) * DV_DELTA]
            beta = beta_all[:, h:h + 1]
            eg = jnp.exp(g_all[:, H_DELTA + h:H_DELTA + h + 1])
            items.append((b, h, hs, qs, kn, vh, beta, eg, s0_ref[b, h]))
    res = [jnp.dot(jnp.concatenate([kn * (beta * eg), qs * eg, jnp.zeros((SUBLANES - 2, DK_DELTA), F32)],
                                   axis=0).astype(BF16),
                   s.astype(BF16), preferred_element_type=F32)
           for (b, h, hs, qs, kn, vh, beta, eg, s) in items]
    for (b, h, hs, qs, kn, vh, beta, eg, s), r in zip(items, res):
        v_new = vh * beta - r[0:1]
        qk = jnp.sum(qs.astype(BF16).astype(F32) * kn.astype(BF16).astype(F32), axis=-1, keepdims=True)
        o = r[1:2] + qk * v_new
        k_col = jnp.broadcast_to(kn, (DK_DELTA, DK_DELTA)).T
        s_ref[b, h] = s * eg + k_col * v_new
        o_ref[b, :, hs] = _delta_norm_gate(o, nw_ref[...], z_ref[b, :, hs]).astype(BF16)


def _delta_step(cin, conv_prev, conv_w, tail, alog_row, dtb_row, z, s0, norm_w):
    bs = cin.shape[0]
    seqs = math.gcd(bs, 4)
    tok = lambda width: pl.BlockSpec((seqs, 1, width), lambda b: (b, 0, 0))
    const = lambda shape: pl.BlockSpec(shape, lambda b: (0,) * len(shape))
    state = pl.BlockSpec((seqs, H_DELTA, DK_DELTA, DV_DELTA), lambda b: (b, 0, 0, 0))
    return pl.pallas_call(
        functools.partial(_delta_step_kernel, seqs=seqs),
        grid=(bs // seqs,),
        in_specs=[tok(CONV_CH), pl.BlockSpec((seqs, CONV_W - 1, CONV_CH), lambda b: (b, 0, 0)), const(conv_w.shape),
                  tok(LANES), const(alog_row.shape), const(dtb_row.shape), tok(W_DELTA), state,
                  const((1, DV_DELTA))],
        out_specs=[tok(W_DELTA), state],
        out_shape=[jax.ShapeDtypeStruct((bs, 1, W_DELTA), BF16),
                   jax.ShapeDtypeStruct((bs, H_DELTA, DK_DELTA, DV_DELTA), F32)],
        compiler_params=pltpu.CompilerParams(dimension_semantics=("arbitrary",)),
        name="delta_step",
    )(cin, conv_prev, conv_w, tail, alog_row, dtb_row, z, s0, norm_w.reshape(1, DV_DELTA))


def _out_proj_kernel(oa_ref, ob_ref, x_ref, w_ref, g_ref, b_ref, y_ref, *, alpha):
    mix = jnp.dot(oa_ref[...], w_ref[:W_DIFF, :], preferred_element_type=F32)
    mix = mix + jnp.dot(ob_ref[...], w_ref[W_DIFF:, :], preferred_element_type=F32)
    r = alpha * x_ref[...] + mix
    mu = jnp.mean(r, axis=-1, keepdims=True)
    var = jnp.mean(jnp.square(r - mu), axis=-1, keepdims=True)
    y_ref[...] = (r - mu) * lax.rsqrt(var + LN_EPS) * g_ref[...] + b_ref[...]


def _out_proj(oa, ob, x, w_out_b, ln_g, ln_b, alpha):
    m, d = x.shape
    tm = min(1024, m)
    assert m % tm == 0
    row = lambda width: pl.BlockSpec((tm, width), lambda i: (i, 0))
    const = lambda shape: pl.BlockSpec(shape, lambda i: (0, 0))
    return pl.pallas_call(
        functools.partial(_out_proj_kernel, alpha=alpha),
        grid=(m // tm,),
        in_specs=[row(W_DIFF), row(W_DELTA), row(d), const(w_out_b.shape), const((1, d)), const((1, d))],
        out_specs=row(d),
        out_shape=jax.ShapeDtypeStruct((m, d), F32),
        compiler_params=pltpu.CompilerParams(dimension_semantics=("arbitrary",), vmem_limit_bytes=VMEM_LIMIT),
        name="out_proj",
    )(oa, ob, x, w_out_b, ln_g.reshape(1, d), ln_b.reshape(1, d))


def _pad_lanes(vec, offset):
    return jnp.zeros((1, LANES), F32).at[0, offset:offset + vec.shape[0]].set(vec.astype(F32))


def _gate_rows_param(vec):
    col = jnp.broadcast_to(vec.astype(F32)[:, None], (H_DELTA, LANES))
    return jnp.concatenate([jnp.zeros((H_DELTA, LANES), F32), col], axis=0)


def kernel(x_prompt, x_sample, cache_k, cache_v, page_table, state_delta, state_conv, w_in, conv_w, a_log,
           dt_bias, delta_norm_w, diff_lambda, diff_norm_w, w_out, ln_g, ln_b):
    depth = w_in.shape[0]
    bp, tp, d = x_prompt.shape
    bs, ts, _ = x_sample.shape
    assert ts == 1 and w_in.shape[2] == P_MAIN + 2 * H_DELTA and d == w_out.shape[2]
    past_len = page_table.shape[1] * cache_k.shape[2]
    alpha = (2 * depth) ** 0.25
    tables_p = _rope_tables(jnp.arange(tp, dtype=jnp.int32))
    tables_s = _rope_tables(jnp.full((bs,), past_len, jnp.int32))

    hp, hs = x_prompt, x_sample
    outs = [[] for _ in range(8)]
    for l in range(depth):
        lam_init = _lambda_init(l)
        w_main = w_in[l].astype(BF16)
        w_tail = jnp.zeros((d, LANES), BF16).at[:, :2 * H_DELTA].set(w_in[l, :, P_MAIN:].astype(BF16))
        w_out_b = w_out[l].astype(BF16)
        alog_row = _pad_lanes(a_log[l], H_DELTA)
        dtb_row = _pad_lanes(dt_bias[l], H_DELTA)
        alog_col = _gate_rows_param(a_log[l])
        dtb_col = _gate_rows_param(dt_bias[l])

        sq, sk, _, sv, _, sga, scin, sz, stail = _in_proj(hs.reshape(1, bs, d), w_main, w_tail, tables_s, False)
        tok = lambda a: a.reshape(bs, 1, -1)
        decode_args = dict(q=tok(sq), k_new=tok(sk), v_new=tok(sv), ga=tok(sga), cache_k=cache_k, cache_v=cache_v,
                           layer=l, page_table=page_table, diff_lambda=diff_lambda[l], norm_w=diff_norm_w[l],
                           lam_init=lam_init)
        rider = _DecodeRider(**decode_args, seq0=0, n_seq=bs, grid=(bp, _delta_prep_grid(tp)[1]))
        if not rider.ok:
            rider = None

        q, k, kb, v, vb, ga, cin, z, _, tail_t = _in_proj(hp, w_main, w_tail, tables_p, True)
        oa = _prompt_attn(q, kb, vb, ga, diff_lambda[l], diff_norm_w[l], lam_init)
        prep, soa = _delta_prep(cin, jnp.zeros((bp, CONV_W - 1, CONV_CH), F32), conv_w[l], tail_t, alog_col,
                                dtb_col, rider)
        ob, sp = _delta_scan(prep, z, jnp.zeros((bp, H_DELTA, DK_DELTA, DV_DELTA), F32), delta_norm_w[l])
        hp = _out_proj(oa.reshape(bp * tp, W_DIFF), ob.reshape(bp * tp, W_DELTA), hp.reshape(bp * tp, d),
                       w_out_b, ln_g[l], ln_b[l], alpha).reshape(bp, tp, d)
        outs[0].append(k.reshape(bp, tp, H_DIFF, 2, DH_DIFF))
        outs[1].append(v.reshape(bp, tp, H_DIFF, DV_DIFF))
        outs[2].append(sp)
        outs[3].append(cin[:, tp - (CONV_W - 1):, :])

        if rider is None:
            soa = _decode_attn(decode_args, bs)
        sob, ss = _delta_step(tok(scin), state_conv[l], conv_w[l], tok(stail), alog_row, dtb_row, tok(sz),
                              state_delta[l], delta_norm_w[l])
        hs = _out_proj(soa.reshape(bs, W_DIFF), sob.reshape(bs, W_DELTA), hs.reshape(bs, d),
                       w_out_b, ln_g[l], ln_b[l], alpha).reshape(bs, 1, d)
        outs[4].append(sk.reshape(bs, 1, H_DIFF, 2, DH_DIFF))
        outs[5].append(sv.reshape(bs, 1, H_DIFF, DV_DIFF))
        outs[6].append(ss)
        outs[7].append(jnp.concatenate([state_conv[l][:, 1:, :], tok(scin)], axis=1))
    return (hp, hs) + tuple(jnp.stack(o) for o in outs)
```

```python
import functools
import math

import jax
import jax.numpy as jnp
from jax import lax
from jax.experimental import pallas as pl
from jax.experimental.pallas import tpu as pltpu

F32 = jnp.float32
BF16 = jnp.bfloat16
HIGHEST = lax.Precision.HIGHEST

H_DIFF = 4
DH_DIFF = 64
DV_DIFF = 2 * DH_DIFF
W_DIFF = H_DIFF * DV_DIFF
H_DELTA = 4
DK_DELTA = 128
DV_DELTA = 128
W_DELTA = H_DELTA * DK_DELTA
ROT_DIM = DH_DIFF // 4
ROPE_THETA = 500000.0
CONV_W = 4
CONV_CH = 3 * W_DELTA
CHUNK = 64
P_MAIN = 4 * W_DIFF + 4 * W_DELTA
LN_EPS = 1e-5
HEAD_NORM_EPS = 1e-5
DELTA_NORM_EPS = 1e-6
L2_EPS = 1e-6
NEG_INF = -1e30

LANES = 128
SUBLANES = 8
PAIR = 2 * CHUNK
CHUNK_BITS = CHUNK.bit_length() - 1
MAP_BITS = DH_DIFF.bit_length() - 1
VMEM_LIMIT = 56 * 1024 * 1024
DECODE_PAGES_PER_STEP = 16

_NT = (((1,), (1,)), ((), ()))


def _sigmoid(x):
    return 0.5 * jnp.tanh(0.5 * x) + 0.5


def _silu(x):
    return x * _sigmoid(x)


def _softplus(x):
    return jnp.maximum(x, 0.0) + jnp.log1p(jnp.exp(-jnp.abs(x)))


def _lambda_init(layer):
    return 0.8 - 0.6 * math.exp(-0.3 * layer)


def _rope_tables(pos):
    half = ROT_DIM // 2
    inv = ROPE_THETA ** (-jnp.arange(half, dtype=F32) / half)
    ang = pos.astype(F32)[:, None] * inv[None, :]
    cos, sin = jnp.cos(ang), jnp.sin(ang)
    t = pos.shape[0]
    rest = DH_DIFF - ROT_DIM
    cos_m = jnp.concatenate([cos, cos, jnp.ones((t, rest), F32)], axis=-1)
    sin_lo = jnp.concatenate([-sin, jnp.zeros((t, half + rest), F32)], axis=-1)
    sin_hi = jnp.concatenate([jnp.zeros((t, half), F32), sin, jnp.zeros((t, rest), F32)], axis=-1)
    reps = LANES // DH_DIFF
    return jnp.tile(cos_m, (1, reps)), jnp.tile(sin_lo, (1, reps)), jnp.tile(sin_hi, (1, reps))


def _in_proj_kernel(x_ref, w_ref, wt_ref, cos_ref, slo_ref, shi_ref,
                    q_ref, k_ref, kb_ref, v_ref, vb_ref, ga_ref, cin_ref, z_ref, tail_ref, *tail_t_ref):
    xb = x_ref[0].astype(BF16)
    half = ROT_DIM // 2
    tm = xb.shape[0]

    def proj(c0, width):
        return jnp.dot(xb, w_ref[:, c0:c0 + width], preferred_element_type=F32)

    def rope(h):
        blocks = []
        for c in range(W_DIFF // LANES):
            hb = h[:, c * LANES:(c + 1) * LANES]
            blocks.append(hb * cos_ref[...]
                          + pltpu.roll(hb, LANES - half, 1) * slo_ref[...]
                          + pltpu.roll(hb, half, 1) * shi_ref[...])
        return jnp.concatenate(blocks, axis=1)

    q_ref[0] = rope(proj(0, W_DIFF)).astype(BF16)
    k = rope(proj(W_DIFF, W_DIFF))
    k_ref[0] = k
    kb_ref[0] = k.astype(BF16)
    v = proj(2 * W_DIFF, W_DIFF)
    for h in range(H_DIFF):
        v_ref[0, pl.ds(h, tm, stride=H_DIFF), :] = v[:, h * DV_DIFF:(h + 1) * DV_DIFF]
    vb_ref[0] = v.astype(BF16)
    ga_ref[0] = proj(3 * W_DIFF, W_DIFF)
    for j in range(3):
        cin_ref[0, :, j * W_DELTA:(j + 1) * W_DELTA] = proj(4 * W_DIFF + j * W_DELTA, W_DELTA)
    z_ref[0] = proj(4 * W_DIFF + 3 * W_DELTA, W_DELTA)
    tail = jnp.dot(xb, wt_ref[...], preferred_element_type=F32)
    tail_ref[0] = tail
    if tail_t_ref:
        for c in range(tm // LANES):
            cs = slice(c * LANES, (c + 1) * LANES)
            tail_t_ref[0][0, :, cs] = tail[cs, :].T[:2 * H_DELTA, :]


def _in_proj(x, w_main, w_tail, tables, gates_time_major):
    b, t, d = x.shape
    tm = min(512, t)
    assert t % tm == 0 and (not gates_time_major or tm % LANES == 0)
    cos_t, slo_t, shi_t = tables
    row = lambda width: pl.BlockSpec((1, tm, width), lambda ti, bi: (bi, ti, 0))
    tab = pl.BlockSpec((tm, LANES), lambda ti, bi: (ti, 0))
    const = lambda shape: pl.BlockSpec(shape, lambda ti, bi: (0, 0))
    outs = [(W_DIFF, BF16), (W_DIFF, F32), (W_DIFF, BF16), None, (W_DIFF, BF16),
            (W_DIFF, F32), (CONV_CH, F32), (W_DELTA, F32), (LANES, F32)]
    out_specs = [row(o[0]) if o else pl.BlockSpec((1, tm * H_DIFF, DV_DIFF), lambda ti, bi: (bi, ti, 0))
                 for o in outs]
    out_shape = [jax.ShapeDtypeStruct((b, t, o[0]), o[1]) if o else
                 jax.ShapeDtypeStruct((b, t * H_DIFF, DV_DIFF), F32) for o in outs]
    if gates_time_major:
        out_specs.append(pl.BlockSpec((1, 2 * H_DELTA, tm), lambda ti, bi: (bi, 0, ti)))
        out_shape.append(jax.ShapeDtypeStruct((b, 2 * H_DELTA, t), F32))
    return pl.pallas_call(
        _in_proj_kernel,
        grid=(t // tm, b),
        in_specs=[row(d), const((d, P_MAIN)), const(w_tail.shape), tab, tab, tab],
        out_specs=out_specs,
        out_shape=out_shape,
        compiler_params=pltpu.CompilerParams(
            dimension_semantics=("arbitrary", "arbitrary"), vmem_limit_bytes=VMEM_LIMIT),
        name="in_proj",
    )(x, w_main, w_tail, cos_t, slo_t, shi_t)


def _diff_lambda_value(dl_ref, lam_init):
    dl = dl_ref[...]
    a = jnp.sum(dl[0:1] * dl[1:2], axis=1, keepdims=True)
    b = jnp.sum(dl[2:3] * dl[3:4], axis=1, keepdims=True)
    return jnp.exp(a) - jnp.exp(b) + lam_init


def _head_norm_gate(o, normw, gate, lam_init):
    ms = jnp.mean(o * o, axis=-1, keepdims=True)
    o = o * lax.rsqrt(ms + HEAD_NORM_EPS) * normw * (1.0 - lam_init)
    return o * _silu(gate)


def _prompt_attn_kernel(q0_ref, qn_ref, k_ref, v_ref, ga_ref, dl_ref, nw_ref, o_ref, s_scr, m_scr, l_scr, acc_scr,
                        *, blk, lam_init):
    i = pl.program_id(1)
    last = pl.num_programs(1) - 1
    slot = lax.rem(i, 2)
    lane = lax.broadcasted_iota(jnp.int32, (blk, DV_DIFF), 1)
    heads = [slice(h * DV_DIFF, (h + 1) * DV_DIFF) for h in range(H_DIFF)]
    lane_chunks = [slice(c * LANES, (c + 1) * LANES) for c in range(blk // LANES)]
    n_maps = 2 * H_DIFF

    def query_maps(q_ref):
        maps = []
        for hs in heads:
            qs = q_ref[0, :, hs].astype(F32) * (DH_DIFF ** -0.5 * math.log2(math.e))
            maps.append(jnp.where(lane < DH_DIFF, qs, 0.0).astype(BF16))
            maps.append(jnp.where(lane >= DH_DIFF, qs, 0.0).astype(BF16))
        return maps

    def fold(x, op):
        r = x[:, lane_chunks[0]]
        for c in lane_chunks[1:]:
            r = op(r, x[:, c])
        return r

    def score_block(q_maps, dst, j, masked):
        rows = pl.ds(pl.multiple_of(j * blk, blk), blk)
        for h, hs in enumerate(heads):
            k = k_ref[0, rows, hs]
            for mi in (2 * h, 2 * h + 1):
                s = lax.dot_general(q_maps[mi], k, _NT, preferred_element_type=F32)
                if masked:
                    row = lax.broadcasted_iota(jnp.int32, (blk, blk), 0)
                    col = lax.broadcasted_iota(jnp.int32, (blk, blk), 1)
                    s = jnp.where(col <= row, s, NEG_INF)
                s_scr[dst, j, mi] = s
                m_scr[dst, mi] = jnp.maximum(m_scr[dst, mi], fold(s, jnp.maximum))

    def finish_max(dst):
        for mi in range(n_maps):
            m_scr[dst, mi] = jnp.broadcast_to(jnp.max(m_scr[dst, mi], axis=-1, keepdims=True), (blk, LANES))

    def prob_block(j):
        rows = pl.ds(pl.multiple_of(j * blk, blk), blk)
        for h, hs in enumerate(heads):
            v = v_ref[0, rows, hs]
            for mi in (2 * h, 2 * h + 1):
                m = m_scr[slot, mi]
                p = jnp.concatenate([jnp.exp2(s_scr[slot, j, mi, :, c] - m) for c in lane_chunks], axis=1)
                l_scr[mi] = l_scr[mi] + fold(p, jnp.add)
                acc_scr[mi] = acc_scr[mi] + jnp.dot(p.astype(BF16), v, preferred_element_type=F32)

    def write_output():
        lam = _diff_lambda_value(dl_ref, lam_init)
        for h, hs in enumerate(heads):
            l1 = jnp.sum(l_scr[2 * h], axis=-1, keepdims=True)
            l2 = jnp.sum(l_scr[2 * h + 1], axis=-1, keepdims=True)
            o = acc_scr[2 * h] / l1 - lam * (acc_scr[2 * h + 1] / l2)
            o_ref[0, :, hs] = _head_norm_gate(o, nw_ref[...], ga_ref[0, :, hs], lam_init).astype(BF16)

    @pl.when(i == 0)
    def _():
        m_scr[0] = jnp.full(m_scr.shape[1:], NEG_INF, F32)
        score_block(query_maps(q0_ref), 0, 0, True)
        finish_max(0)

    l_scr[...] = jnp.zeros(l_scr.shape, F32)
    acc_scr[...] = jnp.zeros(acc_scr.shape, F32)

    @pl.when(i < last)
    def _():
        nxt = 1 - slot
        q_next = query_maps(qn_ref)
        m_scr[nxt] = jnp.full(m_scr.shape[1:], NEG_INF, F32)

        def both(j, carry):
            prob_block(j)
            score_block(q_next, nxt, j, False)
            return carry

        lax.fori_loop(0, i + 1, both, 0)
        write_output()
        score_block(q_next, nxt, i + 1, True)
        finish_max(nxt)

    @pl.when(i == last)
    def _():
        def only_probs(j, carry):
            prob_block(j)
            return carry

        lax.fori_loop(0, i + 1, only_probs, 0)
        write_output()


def _prompt_attn(q, kb, vb, ga, diff_lambda, norm_w, lam_init):
    b, t, _ = q.shape
    blk = min(256, t)
    assert t % blk == 0
    n_maps = 2 * H_DIFF
    nq = t // blk
    qspec = pl.BlockSpec((1, blk, W_DIFF), lambda bi, i: (bi, i, 0))
    q_first = pl.BlockSpec((1, blk, W_DIFF), lambda bi, i: (bi, 0, 0))
    q_next = pl.BlockSpec((1, blk, W_DIFF), lambda bi, i: (bi, jnp.minimum(i + 1, nq - 1), 0))
    kvspec = pl.BlockSpec((1, t, W_DIFF), lambda bi, i: (bi, 0, 0))
    return pl.pallas_call(
        functools.partial(_prompt_attn_kernel, blk=blk, lam_init=lam_init),
        grid=(b, nq),
        in_specs=[q_first, q_next, kvspec, kvspec, qspec,
                  pl.BlockSpec(diff_lambda.shape, lambda bi, i: (0, 0)),
                  pl.BlockSpec((1, DV_DIFF), lambda bi, i: (0, 0))],
        out_specs=qspec,
        out_shape=jax.ShapeDtypeStruct((b, t, W_DIFF), BF16),
        scratch_shapes=[pltpu.VMEM((2, nq, n_maps, blk, blk), F32),
                        pltpu.VMEM((2, n_maps, blk, LANES), F32),
                        pltpu.VMEM((n_maps, blk, LANES), F32),
                        pltpu.VMEM((n_maps, blk, DV_DIFF), F32)],
        compiler_params=pltpu.CompilerParams(
            dimension_semantics=("arbitrary", "arbitrary"), vmem_limit_bytes=VMEM_LIMIT),
        name="prompt_attn",
    )(q, q, kb, vb, ga, diff_lambda, norm_w.reshape(1, DV_DIFF))


class _DecodeStep:
    def __init__(self, step, n_steps, spq, first_page, pages, pt_ref, in_refs, o_ref, scratch_refs, lam_init):
        self.step, self.n_steps, self.pages, self.first_page = step, n_steps, pages, first_page
        self.g, self.n_g, self.lam_init, self.pt_ref = lax.rem(step, spq), spq, lam_init, pt_ref
        (self.q_ref, self.kn_ref, self.vn_ref, self.ga_ref, self.dl_ref, self.nw_ref,
         self.ck_ref, self.cv_ref) = in_refs
        self.o_ref = o_ref
        (self.qcol_ref, self.m_ref, self.l_ref, self.acc_ref,
         self.kbuf_ref, self.vbuf_ref, self.sem_ref) = scratch_refs
        self.slot = lax.rem(step, 2)

    def _query(self):
        return self.q_ref[0].astype(F32) * DH_DIFF ** -0.5

    def _page_copies(self, step, slot):
        base = self.first_page + step * self.pages
        copies = []
        for i in range(self.pages):
            pid = self.pt_ref[base + i]
            copies.append(pltpu.make_async_copy(self.ck_ref.at[pid], self.kbuf_ref.at[slot, i],
                                                self.sem_ref.at[slot, 0]))
            copies.append(pltpu.make_async_copy(self.cv_ref.at[pid], self.vbuf_ref.at[slot, i],
                                                self.sem_ref.at[slot, 1]))
        return copies

    def init(self):
        @pl.when(self.step == 0)
        def _():
            for cp in self._page_copies(0, 0):
                cp.start()

        for cp in self._page_copies(self.step, self.slot):
            cp.wait()

        @pl.when(self.step + 1 < self.n_steps)
        def _():
            for cp in self._page_copies(self.step + 1, 1 - self.slot):
                cp.start()

        @pl.when(self.g == 0)
        def _():
            self.m_ref[...] = jnp.full(self.m_ref.shape, NEG_INF, F32)
            self.l_ref[...] = jnp.zeros(self.l_ref.shape, F32)
            self.acc_ref[...] = jnp.zeros(self.acc_ref.shape, F32)
            qs = self._query()
            for c in range(W_DIFF // LANES):
                cs = slice(c * LANES, (c + 1) * LANES)
                self.qcol_ref[cs, :] = jnp.broadcast_to(qs[:, cs], (LANES, LANES)).T

    def main(self):
        k_pages = [self.kbuf_ref.at[self.slot, i] for i in range(self.pages)]
        v_pages = [self.vbuf_ref.at[self.slot, i] for i in range(self.pages)]
        _decode_pages(k_pages, v_pages, self.qcol_ref, self.m_ref, self.l_ref, self.acc_ref)

    def finalize(self):
        @pl.when(self.g == self.n_g - 1)
        def _():
            _decode_finish(self._query(), self.kn_ref, self.vn_ref, self.ga_ref, self.dl_ref, self.nw_ref,
                           self.o_ref, self.m_ref, self.l_ref, self.acc_ref, self.lam_init)


def _decode_pages(kp_refs, vp_refs, qcol_ref, m_ref, l_ref, acc_ref):
    n_maps = 2 * H_DIFF
    page = kp_refs[0].shape[1]
    qcol = qcol_ref[...]
    s = jnp.concatenate(
        [jnp.sum((kp[...] * qcol).reshape(n_maps, DH_DIFF, page), axis=1) for kp in kp_refs], axis=1)
    m_prev = m_ref[:, 0:1]
    m_new = jnp.maximum(m_prev, jnp.max(s, axis=-1, keepdims=True))
    p = jnp.exp(s - m_new)
    alpha = jnp.exp(m_prev - m_new)
    l_new = alpha * l_ref[:, 0:1] + jnp.sum(p, axis=-1, keepdims=True)
    pb = p.astype(BF16)
    for h in range(H_DIFF):
        pv = jnp.zeros((n_maps, DV_DIFF), F32)
        for i, vp in enumerate(vp_refs):
            v_head = vp[pl.ds(h, page, stride=H_DIFF), :].astype(BF16)
            pv = pv + jnp.dot(pb[:, i * page:(i + 1) * page], v_head, preferred_element_type=F32)
        acc_ref[h] = alpha * acc_ref[h] + pv
    m_ref[...] = jnp.broadcast_to(m_new, m_ref.shape)
    l_ref[...] = jnp.broadcast_to(l_new, l_ref.shape)


def _decode_finish(qs, kn_ref, vn_ref, ga_ref, dl_ref, nw_ref, o_ref, m_ref, l_ref, acc_ref, lam_init):
    n_maps = 2 * H_DIFF
    rows = lax.broadcasted_iota(jnp.int32, (n_maps, W_DIFF), 0)
    lanes = lax.broadcasted_iota(jnp.int32, (n_maps, W_DIFF), 1)
    q_blk = jnp.where((lanes >> MAP_BITS) == rows, jnp.broadcast_to(qs, (n_maps, W_DIFF)), 0.0).astype(BF16)
    k_self = jnp.broadcast_to(kn_ref[0].astype(BF16), (SUBLANES, W_DIFF))
    s_self = lax.dot_general(q_blk, k_self, _NT, preferred_element_type=F32)[:, 0:1]
    m_past = m_ref[:, 0:1]
    m_f = jnp.maximum(m_past, s_self)
    a_f = jnp.exp(m_past - m_f)
    p_self = jnp.exp(s_self - m_f)
    l_f = a_f * l_ref[:, 0:1] + p_self
    pv_self = p_self.astype(BF16).astype(F32) * vn_ref[0].astype(BF16).astype(F32)
    lam = _diff_lambda_value(dl_ref, lam_init)
    for h in range(H_DIFF):
        hs = slice(h * DV_DIFF, (h + 1) * DV_DIFF)
        a = (a_f * acc_ref[h] + pv_self[:, hs]) / l_f
        o = a[2 * h:2 * h + 1] - lam * a[2 * h + 1:2 * h + 2]
        o_ref[0, :, hs] = _head_norm_gate(o, nw_ref[...], ga_ref[0, :, hs], lam_init).astype(BF16)


class _DecodeRider:
    def __init__(self, q, k_new, v_new, ga, cache_k, cache_v, layer, page_table, diff_lambda, norm_w, lam_init,
                 seq0, n_seq, grid):
        depth, n_phys, page = cache_k.shape[:3]
        n_pages = page_table.shape[1]
        n_steps = grid[0] * grid[1]
        assert page == LANES
        pages = n_seq * n_pages // n_steps
        self.ok = pages >= 1 and pages * n_steps == n_seq * n_pages and n_pages % pages == 0
        if not self.ok:
            return
        spq = n_pages // pages
        self.pages, self.spq, self.lam_init, self.grid = pages, spq, lam_init, grid
        self.first_page = seq0 * n_pages
        ck = jnp.transpose(cache_k, (0, 1, 3, 4, 5, 2)).reshape(depth * n_phys, W_DIFF, page)
        cv = cache_v.reshape(depth * n_phys, page * H_DIFF, DV_DIFF)
        self.page_ids = page_table.reshape(-1) + layer * n_phys
        step = lambda i0, i1: i0 * grid[1] + i1
        tok = pl.BlockSpec((1, 1, W_DIFF), lambda i0, i1, pt: (seq0 + step(i0, i1) // spq, 0, 0))
        const = lambda shape: pl.BlockSpec(shape, lambda i0, i1, pt: (0, 0))
        hbm = pl.BlockSpec(memory_space=pl.ANY)
        self.inputs = [q, k_new, v_new, ga, diff_lambda, norm_w.reshape(1, DV_DIFF), ck, cv]
        self.in_specs = [tok, tok, tok, tok, const(diff_lambda.shape), const((1, DV_DIFF)), hbm, hbm]
        self.out_spec = pl.BlockSpec((1, 1, W_DIFF), lambda i0, i1, pt: (step(i0, i1) // spq, 0, 0))
        self.out_shape = jax.ShapeDtypeStruct((n_seq, 1, W_DIFF), BF16)
        n_maps = 2 * H_DIFF
        self.scratch_shapes = [pltpu.VMEM((W_DIFF, page), F32),
                               pltpu.VMEM((n_maps, LANES), F32), pltpu.VMEM((n_maps, LANES), F32),
                               pltpu.VMEM((H_DIFF, n_maps, DV_DIFF), F32),
                               pltpu.VMEM((2, pages, W_DIFF, page), F32),
                               pltpu.VMEM((2, pages, page * H_DIFF, DV_DIFF), F32),
                               pltpu.SemaphoreType.DMA((2, 2))]

    def step(self, pt_ref, in_refs, out_ref, scratch_refs):
        step = pl.program_id(0) * self.grid[1] + pl.program_id(1)
        return _DecodeStep(step, self.grid[0] * self.grid[1], self.spq, self.first_page, self.pages, pt_ref,
                           in_refs, out_ref, scratch_refs, self.lam_init)


def _ride(host_kernel, n_in, n_out, n_scratch, rider):
    n_rin = len(rider.inputs)

    def kernel(pt_ref, *refs):
        host_in, refs = refs[:n_in], refs[n_in:]
        rider_in, refs = refs[:n_rin], refs[n_rin:]
        host_out, refs = refs[:n_out], refs[n_out:]
        rider_out, refs = refs[0], refs[1:]
        host_scratch, rider_scratch = refs[:n_scratch], refs[n_scratch:]
        host_kernel(*host_in, *host_out, *host_scratch,
                    decode=rider.step(pt_ref, rider_in, rider_out, rider_scratch))

    return kernel


def _hosted_call(host_kernel, grid, in_specs, out_specs, out_shape, scratch_shapes, inputs, name, rider):
    params = pltpu.CompilerParams(dimension_semantics=("arbitrary", "arbitrary"), vmem_limit_bytes=VMEM_LIMIT)
    if rider is None:
        outs = pl.pallas_call(host_kernel, grid=grid, in_specs=in_specs, out_specs=out_specs, out_shape=out_shape,
                              scratch_shapes=scratch_shapes, compiler_params=params, name=name)(*inputs)
        return outs, None
    grid_spec = pltpu.PrefetchScalarGridSpec(
        num_scalar_prefetch=1, grid=grid,
        in_specs=list(in_specs) + rider.in_specs,
        out_specs=list(out_specs) + [rider.out_spec],
        scratch_shapes=list(scratch_shapes) + rider.scratch_shapes)
    outs = pl.pallas_call(
        _ride(host_kernel, len(in_specs), len(out_specs), len(scratch_shapes), rider),
        grid_spec=grid_spec, out_shape=list(out_shape) + [rider.out_shape],
        compiler_params=params, name=name)(rider.page_ids, *inputs, *rider.inputs)
    return outs[:-1], outs[-1]


def _decode_attn_kernel(pt_ref, *refs, rider):
    n_rin = len(rider.inputs)
    decode = rider.step(pt_ref, refs[:n_rin], refs[n_rin], refs[n_rin + 1:])
    decode.init()
    decode.main()
    decode.finalize()


def _decode_attn(rider_args, n_seq):
    n_pages = rider_args["page_table"].shape[1]
    grid = (n_seq, n_pages // math.gcd(DECODE_PAGES_PER_STEP, n_pages))
    rider = _DecodeRider(**rider_args, seq0=0, n_seq=n_seq, grid=grid)
    grid_spec = pltpu.PrefetchScalarGridSpec(
        num_scalar_prefetch=1, grid=grid, in_specs=rider.in_specs,
        out_specs=rider.out_spec, scratch_shapes=rider.scratch_shapes)
    return pl.pallas_call(
        functools.partial(_decode_attn_kernel, rider=rider), grid_spec=grid_spec, out_shape=rider.out_shape,
        compiler_params=pltpu.CompilerParams(
            dimension_semantics=("arbitrary", "arbitrary"), vmem_limit_bytes=VMEM_LIMIT),
        name="decode_attn")(rider.page_ids, *rider.inputs)


def _gate_rows(tail, alog_row, dtb_row):
    beta = _sigmoid(tail)
    g = -jnp.exp(alog_row) * _softplus(tail + dtb_row)
    return beta, g


def _lane_bcast(x, lane, rows):
    return jnp.broadcast_to(x[:, lane:lane + 1], (rows, LANES))


def _l2norm(x):
    return x * lax.rsqrt(jnp.sum(x * x, axis=-1, keepdims=True) + L2_EPS)


def _split_bf16(x):
    hi = x.astype(BF16)
    return hi, (x - hi.astype(F32)).astype(BF16)


def _dot_split(lhs, rhs):
    d = lambda a, b: jnp.dot(a, b, preferred_element_type=F32)
    return d(lhs[0], rhs[0]) + d(lhs[0], rhs[1]) + d(lhs[1], rhs[0])


def _unit_lower_inverses(a_mats):
    n = a_mats[0].shape[0]
    eye = (lax.broadcasted_iota(jnp.int32, (n, n), 0) == lax.broadcasted_iota(jnp.int32, (n, n), 1)).astype(F32)
    xs = [-a for a in a_mats]
    ps = [eye + x for x in xs]
    splits = [_split_bf16(x) for x in xs]
    xs = [_dot_split(s, s) for s in splits]
    power = 2
    while 2 * power < CHUNK:
        both = [_dot_split(_split_bf16(x), _split_bf16(jnp.concatenate([p, x], axis=1))) for p, x in zip(ps, xs)]
        ps = [p + b[:, :n] for p, b in zip(ps, both)]
        xs = [b[:, n:] for b in both]
        power *= 2
    return [p + _dot_split(_split_bf16(x), _split_bf16(p)) for p, x in zip(ps, xs)]


def _delta_prep_kernel(cin_ref, prev_ref, cw_ref, tail_ref, alog_ref, dtb_ref,
                       u_ref, w_ref, qg_ref, at_ref, kdt_ref, el_ref, ext_ref, *, pairs, decode=None):
    t = pl.program_id(1)
    pad = SUBLANES
    keep = CONV_W - 1
    rows = pairs * PAIR

    @pl.when(t == 0)
    def _():
        ext_ref[pad - keep:pad, :] = prev_ref[0]

    @pl.when(t > 0)
    def _():
        ext_ref[pad - keep:pad, :] = ext_ref[pad + rows - keep:pad + rows, :]

    if decode is not None:
        decode.init()

    ext_ref[pad:pad + rows, :] = cin_ref[0]

    r = lax.broadcasted_iota(jnp.int32, (PAIR, PAIR), 0)
    c = lax.broadcasted_iota(jnp.int32, (PAIR, PAIR), 1)
    same = (r >> CHUNK_BITS) == (c >> CHUNK_BITS)
    incl = same & (c <= r)
    strict = same & (c < r)
    incl_t = same & (r <= c)
    chunk_end = r == (c | (CHUNK - 1))

    chains = []
    for pi in range(pairs):
        base = pad - keep + pi * PAIR
        conv = ext_ref[base + keep:base + keep + PAIR, :] * cw_ref[keep:keep + 1, :]
        for j in reversed(range(keep)):
            conv = conv + ext_ref[base + j:base + j + PAIR, :] * cw_ref[j:j + 1, :]
        conv = _silu(conv)
        rs = slice(pi * PAIR, (pi + 1) * PAIR)
        beta_t, g_t = _gate_rows(tail_ref[0, :, rs], alog_ref[...], dtb_ref[...])
        gc_t = jnp.dot(g_t, incl_t.astype(F32), precision=HIGHEST, preferred_element_type=F32)
        g_last_t = jnp.dot(gc_t, chunk_end.astype(F32), precision=HIGHEST, preferred_element_type=F32)
        for h in range(H_DELTA):
            hs = slice(h * DK_DELTA, (h + 1) * DK_DELTA)
            qs = _l2norm(conv[:, hs]) * DK_DELTA ** -0.5
            kn = _l2norm(conv[:, W_DELTA + h * DK_DELTA:W_DELTA + (h + 1) * DK_DELTA])
            vh = conv[:, 2 * W_DELTA + h * DV_DELTA:2 * W_DELTA + (h + 1) * DV_DELTA]
            beta = jnp.broadcast_to(beta_t[h:h + 1, :], (PAIR, PAIR)).T
            gc_cols = jnp.broadcast_to(gc_t[H_DELTA + h:H_DELTA + h + 1, :], (PAIR, PAIR))
            g_last_cols = jnp.broadcast_to(g_last_t[H_DELTA + h:H_DELTA + h + 1, :], (PAIR, PAIR))
            chains.append(dict(pi=pi, h=h, rs=rs, qs=qs, kn=kn, vh=vh, beta=beta, gc=gc_cols.T, gc_cols=gc_cols,
                               g_last_cols=g_last_cols, kbeta=kn * beta))

    for ch in chains:
        ch["kn_t"] = ch["kn"].T
        kn_tb = ch["kn_t"].astype(BF16)
        ch["kk"] = jnp.dot(ch["kbeta"].astype(BF16), kn_tb, preferred_element_type=F32)
        ch["qk"] = jnp.dot(ch["qs"].astype(BF16), kn_tb, preferred_element_type=F32)
    for ch in chains:
        ch["decay"] = jnp.exp(jnp.where(incl, ch["gc"] - ch["gc_cols"], -jnp.inf))
    t_mats = _unit_lower_inverses([jnp.where(strict, ch["kk"] * ch["decay"], 0.0) for ch in chains])
    for ch, t_mat in zip(chains, t_mats):
        rhs = jnp.concatenate([ch["vh"] * ch["beta"], ch["kbeta"] * jnp.exp(ch["gc"])], axis=1).astype(BF16)
        ch["uw"] = jnp.dot(t_mat.astype(BF16), rhs, preferred_element_type=F32)
    for ch in chains:
        pi, h, rs, gc = ch["pi"], ch["h"], ch["rs"], ch["gc"]
        u_ref[0, h, rs, :] = ch["uw"][:, :DV_DELTA]
        w_ref[0, h, rs, :] = ch["uw"][:, DV_DELTA:].astype(BF16)
        attn = jnp.where(incl, ch["qk"] * ch["decay"], 0.0)
        at_ref[0, h, rs, :] = jnp.concatenate([attn[:CHUNK, :CHUNK], attn[CHUNK:, CHUNK:]], axis=0).astype(BF16)
        qg_ref[0, h, rs, :] = (ch["qs"] * jnp.exp(gc)).astype(BF16)
        kdt_ref[0, h, pi] = (ch["kn_t"] * jnp.exp(ch["g_last_cols"] - ch["gc_cols"])).astype(BF16)
        el_ref[0, h, pi] = jnp.exp(jnp.concatenate([jnp.broadcast_to(gc[CHUNK - 1:CHUNK], (SUBLANES, LANES)),
                                                    jnp.broadcast_to(gc[PAIR - 1:PAIR], (SUBLANES, LANES))], axis=0))

    if decode is not None:
        decode.main()
        decode.finalize()


def _delta_prep_grid(t):
    assert t % PAIR == 0
    npair = t // PAIR
    pairs = 2 if npair % 2 == 0 else 1
    return pairs, npair // pairs


def _delta_prep(cin, conv_prev, conv_w, tail, alog_row, dtb_row, rider):
    b, t, _ = cin.shape
    npair = t // PAIR
    pairs, nt = _delta_prep_grid(t)
    rows = pairs * PAIR
    per_head = lambda width: pl.BlockSpec((1, H_DELTA, rows, width), lambda bi, ti, *_: (bi, 0, ti, 0))
    const = lambda shape: pl.BlockSpec(shape, lambda bi, ti, *_: (0,) * len(shape))
    out_shape = [
        jax.ShapeDtypeStruct((b, H_DELTA, t, DV_DELTA), F32),
        jax.ShapeDtypeStruct((b, H_DELTA, t, DK_DELTA), BF16),
        jax.ShapeDtypeStruct((b, H_DELTA, t, DK_DELTA), BF16),
        jax.ShapeDtypeStruct((b, H_DELTA, t, CHUNK), BF16),
        jax.ShapeDtypeStruct((b, H_DELTA, npair, DK_DELTA, PAIR), BF16),
        jax.ShapeDtypeStruct((b, H_DELTA, npair, 2 * SUBLANES, LANES), F32),
    ]
    out_specs = [per_head(DV_DELTA), per_head(DK_DELTA), per_head(DK_DELTA), per_head(CHUNK),
                 pl.BlockSpec((1, H_DELTA, pairs, DK_DELTA, PAIR), lambda bi, ti, *_: (bi, 0, ti, 0, 0)),
                 pl.BlockSpec((1, H_DELTA, pairs, 2 * SUBLANES, LANES), lambda bi, ti, *_: (bi, 0, ti, 0, 0))]
    return _hosted_call(
        functools.partial(_delta_prep_kernel, pairs=pairs),
        grid=(b, nt),
        in_specs=[pl.BlockSpec((1, rows, CONV_CH), lambda bi, ti, *_: (bi, ti, 0)),
                  pl.BlockSpec((1, CONV_W - 1, CONV_CH), lambda bi, ti, *_: (bi, 0, 0)),
                  const(conv_w.shape),
                  pl.BlockSpec((1, 2 * H_DELTA, rows), lambda bi, ti, *_: (bi, 0, ti)),
                  const(alog_row.shape), const(dtb_row.shape)],
        out_specs=out_specs,
        out_shape=out_shape,
        scratch_shapes=[pltpu.VMEM((SUBLANES + rows, CONV_CH), F32)],
        inputs=(cin, conv_prev, conv_w, tail, alog_row, dtb_row),
        name="delta_prep", rider=rider)


def _delta_norm_gate(o, normw, z):
    ms = jnp.mean(o * o, axis=-1, keepdims=True)
    return o * lax.rsqrt(ms + DELTA_NORM_EPS) * normw * _silu(z)


def _delta_scan_kernel(u_ref, w_ref, qg_ref, at_ref, kdt_ref, el_ref, z_ref, s0_ref, nw_ref,
                       o_ref, sfin_ref, s_scr, *, nb, nblk):
    t = pl.program_id(1)

    @pl.when(t == 0)
    def _():
        s_scr[...] = s0_ref[...]

    def block(jb, carry):
        r0 = pl.multiple_of(jb * PAIR, PAIR)
        seqs = [(bi, h) for bi in range(nb) for h in range(H_DELTA)]
        for ci in range(2):
            rows = pl.ds(pl.multiple_of(r0 + ci * CHUNK, CHUNK), CHUNK)
            states = [s_scr[bi, h] for bi, h in seqs]
            res = [jnp.dot(jnp.concatenate([w_ref[bi, h, rows, :], qg_ref[bi, h, rows, :]], axis=0),
                           s.astype(BF16), preferred_element_type=F32) for (bi, h), s in zip(seqs, states)]
            v_new = [(u_ref[bi, h, rows, :] - r[:CHUNK]).astype(BF16) for (bi, h), r in zip(seqs, res)]
            upd = [jnp.dot(kdt_ref[bi, h, jb, :, ci * CHUNK:(ci + 1) * CHUNK], v, preferred_element_type=F32)
                   for (bi, h), v in zip(seqs, v_new)]
            intra = [jnp.dot(at_ref[bi, h, rows, :], v, preferred_element_type=F32) for (bi, h), v in zip(seqs, v_new)]
            for (bi, h), s, r, du, oi in zip(seqs, states, res, upd, intra):
                decay = jnp.broadcast_to(el_ref[bi, h, jb, ci * SUBLANES:ci * SUBLANES + 1, :], (DK_DELTA, DV_DELTA))
                s_scr[bi, h] = s * decay + du
                hs = slice(h * DV_DELTA, (h + 1) * DV_DELTA)
                o_ref[bi, rows, hs] = _delta_norm_gate(r[CHUNK:] + oi, nw_ref[...], z_ref[bi, rows, hs]).astype(BF16)
        return carry

    lax.fori_loop(0, nblk, block, 0)

    @pl.when(t == pl.num_programs(1) - 1)
    def _():
        sfin_ref[...] = s_scr[...]


def _delta_scan(prep, z, s0, norm_w):
    u, w, qg, at, kdt, el = prep
    b, _, t, _ = u.shape
    nb = math.gcd(b, 4)
    ts = min(512, t)
    assert t % ts == 0
    nblk = ts // PAIR
    per_head = lambda width: pl.BlockSpec((nb, H_DELTA, ts, width), lambda bi, ti: (bi, 0, ti, 0))
    state = pl.BlockSpec((nb, H_DELTA, DK_DELTA, DV_DELTA), lambda bi, ti: (bi, 0, 0, 0))
    tok = pl.BlockSpec((nb, ts, W_DELTA), lambda bi, ti: (bi, ti, 0))
    return pl.pallas_call(
        functools.partial(_delta_scan_kernel, nb=nb, nblk=nblk),
        grid=(b // nb, t // ts),
        in_specs=[per_head(DV_DELTA), per_head(DK_DELTA), per_head(DK_DELTA), per_head(CHUNK),
                  pl.BlockSpec((nb, H_DELTA, nblk, DK_DELTA, PAIR), lambda bi, ti: (bi, 0, ti, 0, 0)),
                  pl.BlockSpec((nb, H_DELTA, nblk, 2 * SUBLANES, LANES), lambda bi, ti: (bi, 0, ti, 0, 0)),
                  tok, state, pl.BlockSpec((1, DV_DELTA), lambda bi, ti: (0, 0))],
        out_specs=[tok, state],
        out_shape=[jax.ShapeDtypeStruct((b, t, W_DELTA), BF16),
                   jax.ShapeDtypeStruct((b, H_DELTA, DK_DELTA, DV_DELTA), F32)],
        scratch_shapes=[pltpu.VMEM((nb, H_DELTA, DK_DELTA, DV_DELTA), F32)],
        compiler_params=pltpu.CompilerParams(
            dimension_semantics=("arbitrary", "arbitrary"), vmem_limit_bytes=VMEM_LIMIT),
        name="delta_scan",
    )(u, w, qg, at, kdt, el, z, s0, norm_w.reshape(1, DV_DELTA))


def _delta_step_kernel(cin_ref, prev_ref, cw_ref, tail_ref, alog_ref, dtb_ref, z_ref, s0_ref, nw_ref,
                       o_ref, s_ref, *, seqs):
    items = []
    for b in range(seqs):
        prev = prev_ref[b]
        conv = prev[0:1] * cw_ref[0:1, :]
        for j in range(1, CONV_W - 1):
            conv = conv + prev[j:j + 1] * cw_ref[j:j + 1, :]
        conv = _silu(conv + cin_ref[b] * cw_ref[CONV_W - 1:CONV_W, :])
        beta_all, g_all = _gate_rows(tail_ref[b], alog_ref[...], dtb_ref[...])
        for h in range(H_DELTA):
            hs = slice(h * DK_DELTA, (h + 1) * DK_DELTA)
            qs = _l2norm(conv[:, hs]) * DK_DELTA ** -0.5
            kn = _l2norm(conv[:, W_DELTA + h * DK_DELTA:W_DELTA + (h + 1) * DK_DELTA])
            vh = conv[:, 2 * W_DELTA + h * DV_DELTA:2 * W_DELTA + (h + 1) * DV_DELTA]
            beta = beta_all[:, h:h + 1]
            eg = jnp.exp(g_all[:, H_DELTA + h:H_DELTA + h + 1])
            items.append((b, h, hs, qs, kn, vh, beta, eg, s0_ref[b, h]))
    res = [jnp.dot(jnp.concatenate([kn * (beta * eg), qs * eg, jnp.zeros((SUBLANES - 2, DK_DELTA), F32)],
                                   axis=0).astype(BF16),
                   s.astype(BF16), preferred_element_type=F32)
           for (b, h, hs, qs, kn, vh, beta, eg, s) in items]
    for (b, h, hs, qs, kn, vh, beta, eg, s), r in zip(items, res):
        v_new = vh * beta - r[0:1]
        qk = jnp.sum(qs.astype(BF16).astype(F32) * kn.astype(BF16).astype(F32), axis=-1, keepdims=True)
        o = r[1:2] + qk * v_new
        k_col = jnp.broadcast_to(kn, (DK_DELTA, DK_DELTA)).T
        s_ref[b, h] = s * eg + k_col * v_new
        o_ref[b, :, hs] = _delta_norm_gate(o, nw_ref[...], z_ref[b, :, hs]).astype(BF16)


def _delta_step(cin, conv_prev, conv_w, tail, alog_row, dtb_row, z, s0, norm_w):
    bs = cin.shape[0]
    seqs = math.gcd(bs, 4)
    tok = lambda width: pl.BlockSpec((seqs, 1, width), lambda b: (b, 0, 0))
    const = lambda shape: pl.BlockSpec(shape, lambda b: (0,) * len(shape))
    state = pl.BlockSpec((seqs, H_DELTA, DK_DELTA, DV_DELTA), lambda b: (b, 0, 0, 0))
    return pl.pallas_call(
        functools.partial(_delta_step_kernel, seqs=seqs),
        grid=(bs // seqs,),
        in_specs=[tok(CONV_CH), pl.BlockSpec((seqs, CONV_W - 1, CONV_CH), lambda b: (b, 0, 0)), const(conv_w.shape),
                  tok(LANES), const(alog_row.shape), const(dtb_row.shape), tok(W_DELTA), state,
                  const((1, DV_DELTA))],
        out_specs=[tok(W_DELTA), state],
        out_shape=[jax.ShapeDtypeStruct((bs, 1, W_DELTA), BF16),
                   jax.ShapeDtypeStruct((bs, H_DELTA, DK_DELTA, DV_DELTA), F32)],
        compiler_params=pltpu.CompilerParams(dimension_semantics=("arbitrary",)),
        name="delta_step",
    )(cin, conv_prev, conv_w, tail, alog_row, dtb_row, z, s0, norm_w.reshape(1, DV_DELTA))


def _out_proj_kernel(oa_ref, ob_ref, x_ref, w_ref, g_ref, b_ref, y_ref, *, alpha):
    mix = jnp.dot(oa_ref[...], w_ref[:W_DIFF, :], preferred_element_type=F32)
    mix = mix + jnp.dot(ob_ref[...], w_ref[W_DIFF:, :], preferred_element_type=F32)
    r = alpha * x_ref[...] + mix
    mu = jnp.mean(r, axis=-1, keepdims=True)
    var = jnp.mean(jnp.square(r - mu), axis=-1, keepdims=True)
    y_ref[...] = (r - mu) * lax.rsqrt(var + LN_EPS) * g_ref[...] + b_ref[...]


def _out_proj(oa, ob, x, w_out_b, ln_g, ln_b, alpha):
    m, d = x.shape
    tm = min(1024, m)
    assert m % tm == 0
    row = lambda width: pl.BlockSpec((tm, width), lambda i: (i, 0))
    const = lambda shape: pl.BlockSpec(shape, lambda i: (0, 0))
    return pl.pallas_call(
        functools.partial(_out_proj_kernel, alpha=alpha),
        grid=(m // tm,),
        in_specs=[row(W_DIFF), row(W_DELTA), row(d), const(w_out_b.shape), const((1, d)), const((1, d))],
        out_specs=row(d),
        out_shape=jax.ShapeDtypeStruct((m, d), F32),
        compiler_params=pltpu.CompilerParams(dimension_semantics=("arbitrary",), vmem_limit_bytes=VMEM_LIMIT),
        name="out_proj",
    )(oa, ob, x, w_out_b, ln_g.reshape(1, d), ln_b.reshape(1, d))


def _pad_lanes(vec, offset):
    return jnp.zeros((1, LANES), F32).at[0, offset:offset + vec.shape[0]].set(vec.astype(F32))


def _gate_rows_param(vec):
    col = jnp.broadcast_to(vec.astype(F32)[:, None], (H_DELTA, LANES))
    return jnp.concatenate([jnp.zeros((H_DELTA, LANES), F32), col], axis=0)


def kernel(x_prompt, x_sample, cache_k, cache_v, page_table, state_delta, state_conv, w_in, conv_w, a_log,
           dt_bias, delta_norm_w, diff_lambda, diff_norm_w, w_out, ln_g, ln_b):
    depth = w_in.shape[0]
    bp, tp, d = x_prompt.shape
    bs, ts, _ = x_sample.shape
    assert ts == 1 and w_in.shape[2] == P_MAIN + 2 * H_DELTA and d == w_out.shape[2]
    past_len = page_table.shape[1] * cache_k.shape[2]
    alpha = (2 * depth) ** 0.25
    tables_p = _rope_tables(jnp.arange(tp, dtype=jnp.int32))
    tables_s = _rope_tables(jnp.full((bs,), past_len, jnp.int32))

    hp, hs = x_prompt, x_sample
    outs = [[] for _ in range(8)]
    for l in range(depth):
        lam_init = _lambda_init(l)
        w_main = w_in[l].astype(BF16)
        w_tail = jnp.zeros((d, LANES), BF16).at[:, :2 * H_DELTA].set(w_in[l, :, P_MAIN:].astype(BF16))
        w_out_b = w_out[l].astype(BF16)
        alog_row = _pad_lanes(a_log[l], H_DELTA)
        dtb_row = _pad_lanes(dt_bias[l], H_DELTA)
        alog_col = _gate_rows_param(a_log[l])
        dtb_col = _gate_rows_param(dt_bias[l])

        sq, sk, _, sv, _, sga, scin, sz, stail = _in_proj(hs.reshape(1, bs, d), w_main, w_tail, tables_s, False)
        tok = lambda a: a.reshape(bs, 1, -1)
        decode_args = dict(q=tok(sq), k_new=tok(sk), v_new=tok(sv), ga=tok(sga), cache_k=cache_k, cache_v=cache_v,
                           layer=l, page_table=page_table, diff_lambda=diff_lambda[l], norm_w=diff_norm_w[l],
                           lam_init=lam_init)
        rider = _DecodeRider(**decode_args, seq0=0, n_seq=bs, grid=(bp, _delta_prep_grid(tp)[1]))
        if not rider.ok:
            rider = None

        q, k, kb, v, vb, ga, cin, z, _, tail_t = _in_proj(hp, w_main, w_tail, tables_p, True)
        oa = _prompt_attn(q, kb, vb, ga, diff_lambda[l], diff_norm_w[l], lam_init)
        prep, soa = _delta_prep(cin, jnp.zeros((bp, CONV_W - 1, CONV_CH), F32), conv_w[l], tail_t, alog_col,
                                dtb_col, rider)
        ob, sp = _delta_scan(prep, z, jnp.zeros((bp, H_DELTA, DK_DELTA, DV_DELTA), F32), delta_norm_w[l])
        hp = _out_proj(oa.reshape(bp * tp, W_DIFF), ob.reshape(bp * tp, W_DELTA), hp.reshape(bp * tp, d),
                       w_out_b, ln_g[l], ln_b[l], alpha).reshape(bp, tp, d)
        outs[0].append(k.reshape(bp, tp, H_DIFF, 2, DH_DIFF))
        outs[1].append(v.reshape(bp, tp, H_DIFF, DV_DIFF))
        outs[2].append(sp)
        outs[3].append(cin[:, tp - (CONV_W - 1):, :])

        if rider is None:
            soa = _decode_attn(decode_args, bs)
        sob, ss = _delta_step(tok(scin), state_conv[l], conv_w[l], tok(stail), alog_row, dtb_row, tok(sz),
                              state_delta[l], delta_norm_w[l])
        hs = _out_proj(soa.reshape(bs, W_DIFF), sob.reshape(bs, W_DELTA), hs.reshape(bs, d),
                       w_out_b, ln_g[l], ln_b[l], alpha).reshape(bs, 1, d)
        outs[4].append(sk.reshape(bs, 1, H_DIFF, 2, DH_DIFF))
        outs[5].append(sv.reshape(bs, 1, H_DIFF, DV_DIFF))
        outs[6].append(ss)
        outs[7].append(jnp.concatenate([state_conv[l][:, 1:, :], tok(scin)], axis=1))
    return (hp, hs) + tuple(jnp.stack(o) for o in outs)
```

```python
import functools
import math

import jax
import jax.numpy as jnp
from jax import lax
from jax.experimental import pallas as pl
from jax.experimental.pallas import tpu as pltpu

F32 = jnp.float32
BF16 = jnp.bfloat16
HIGHEST = lax.Precision.HIGHEST

H_DIFF = 4
DH_DIFF = 64
DV_DIFF = 2 * DH_DIFF
W_DIFF = H_DIFF * DV_DIFF
H_DELTA = 4
DK_DELTA = 128
DV_DELTA = 128
W_DELTA = H_DELTA * DK_DELTA
ROT_DIM = DH_DIFF // 4
ROPE_THETA = 500000.0
CONV_W = 4
CONV_CH = 3 * W_DELTA
CHUNK = 64
P_MAIN = 4 * W_DIFF + 4 * W_DELTA
LN_EPS = 1e-5
HEAD_NORM_EPS = 1e-5
DELTA_NORM_EPS = 1e-6
L2_EPS = 1e-6
NEG_INF = -1e30

LANES = 128
SUBLANES = 8
PAIR = 2 * CHUNK
CHUNK_BITS = CHUNK.bit_length() - 1
MAP_BITS = DH_DIFF.bit_length() - 1
VMEM_LIMIT = 56 * 1024 * 1024
DECODE_PAGES_PER_STEP = 16

_NT = (((1,), (1,)), ((), ()))


def _sigmoid(x):
    return 0.5 * jnp.tanh(0.5 * x) + 0.5


def _silu(x):
    return x * _sigmoid(x)


def _softplus(x):
    return jnp.maximum(x, 0.0) + jnp.log1p(jnp.exp(-jnp.abs(x)))


def _lambda_init(layer):
    return 0.8 - 0.6 * math.exp(-0.3 * layer)


def _rope_tables(pos):
    half = ROT_DIM // 2
    inv = ROPE_THETA ** (-jnp.arange(half, dtype=F32) / half)
    ang = pos.astype(F32)[:, None] * inv[None, :]
    cos, sin = jnp.cos(ang), jnp.sin(ang)
    t = pos.shape[0]
    rest = DH_DIFF - ROT_DIM
    cos_m = jnp.concatenate([cos, cos, jnp.ones((t, rest), F32)], axis=-1)
    sin_lo = jnp.concatenate([-sin, jnp.zeros((t, half + rest), F32)], axis=-1)
    sin_hi = jnp.concatenate([jnp.zeros((t, half), F32), sin, jnp.zeros((t, rest), F32)], axis=-1)
    reps = LANES // DH_DIFF
    return jnp.tile(cos_m, (1, reps)), jnp.tile(sin_lo, (1, reps)), jnp.tile(sin_hi, (1, reps))


def _in_proj_kernel(x_ref, w_ref, wt_ref, cos_ref, slo_ref, shi_ref,
                    q_ref, k_ref, kb_ref, v_ref, vb_ref, ga_ref, cin_ref, z_ref, tail_ref, *tail_t_ref):
    xb = x_ref[0].astype(BF16)
    half = ROT_DIM // 2
    tm = xb.shape[0]

    def proj(c0, width):
        return jnp.dot(xb, w_ref[:, c0:c0 + width], preferred_element_type=F32)

    def rope(h):
        blocks = []
        for c in range(W_DIFF // LANES):
            hb = h[:, c * LANES:(c + 1) * LANES]
            blocks.append(hb * cos_ref[...]
                          + pltpu.roll(hb, LANES - half, 1) * slo_ref[...]
                          + pltpu.roll(hb, half, 1) * shi_ref[...])
        return jnp.concatenate(blocks, axis=1)

    q_ref[0] = rope(proj(0, W_DIFF)).astype(BF16)
    k = rope(proj(W_DIFF, W_DIFF))
    k_ref[0] = k
    kb_ref[0] = k.astype(BF16)
    v = proj(2 * W_DIFF, W_DIFF)
    for h in range(H_DIFF):
        v_ref[0, pl.ds(h, tm, stride=H_DIFF), :] = v[:, h * DV_DIFF:(h + 1) * DV_DIFF]
    vb_ref[0] = v.astype(BF16)
    ga_ref[0] = proj(3 * W_DIFF, W_DIFF)
    for j in range(3):
        cin_ref[0, :, j * W_DELTA:(j + 1) * W_DELTA] = proj(4 * W_DIFF + j * W_DELTA, W_DELTA)
    z_ref[0] = proj(4 * W_DIFF + 3 * W_DELTA, W_DELTA)
    tail = jnp.dot(xb, wt_ref[...], preferred_element_type=F32)
    tail_ref[0] = tail
    if tail_t_ref:
        for c in range(tm // LANES):
            cs = slice(c * LANES, (c + 1) * LANES)
            tail_t_ref[0][0, :, cs] = tail[cs, :].T[:2 * H_DELTA, :]


def _in_proj(x, w_main, w_tail, tables, gates_time_major):
    b, t, d = x.shape
    tm = min(512, t)
    assert t % tm == 0 and (not gates_time_major or tm % LANES == 0)
    cos_t, slo_t, shi_t = tables
    row = lambda width: pl.BlockSpec((1, tm, width), lambda ti, bi: (bi, ti, 0))
    tab = pl.BlockSpec((tm, LANES), lambda ti, bi: (ti, 0))
    const = lambda shape: pl.BlockSpec(shape, lambda ti, bi: (0, 0))
    outs = [(W_DIFF, BF16), (W_DIFF, F32), (W_DIFF, BF16), None, (W_DIFF, BF16),
            (W_DIFF, F32), (CONV_CH, F32), (W_DELTA, F32), (LANES, F32)]
    out_specs = [row(o[0]) if o else pl.BlockSpec((1, tm * H_DIFF, DV_DIFF), lambda ti, bi: (bi, ti, 0))
                 for o in outs]
    out_shape = [jax.ShapeDtypeStruct((b, t, o[0]), o[1]) if o else
                 jax.ShapeDtypeStruct((b, t * H_DIFF, DV_DIFF), F32) for o in outs]
    if gates_time_major:
        out_specs.append(pl.BlockSpec((1, 2 * H_DELTA, tm), lambda ti, bi: (bi, 0, ti)))
        out_shape.append(jax.ShapeDtypeStruct((b, 2 * H_DELTA, t), F32))
    return pl.pallas_call(
        _in_proj_kernel,
        grid=(t // tm, b),
        in_specs=[row(d), const((d, P_MAIN)), const(w_tail.shape), tab, tab, tab],
        out_specs=out_specs,
        out_shape=out_shape,
        compiler_params=pltpu.CompilerParams(
            dimension_semantics=("arbitrary", "arbitrary"), vmem_limit_bytes=VMEM_LIMIT),
        name="in_proj",
    )(x, w_main, w_tail, cos_t, slo_t, shi_t)


def _diff_lambda_value(dl_ref, lam_init):
    dl = dl_ref[...]
    a = jnp.sum(dl[0:1] * dl[1:2], axis=1, keepdims=True)
    b = jnp.sum(dl[2:3] * dl[3:4], axis=1, keepdims=True)
    return jnp.exp(a) - jnp.exp(b) + lam_init


def _head_norm_gate(o, normw, gate, lam_init):
    ms = jnp.mean(o * o, axis=-1, keepdims=True)
    o = o * lax.rsqrt(ms + HEAD_NORM_EPS) * normw * (1.0 - lam_init)
    return o * _silu(gate)


def _prompt_attn_kernel(q0_ref, qn_ref, k_ref, v_ref, ga_ref, dl_ref, nw_ref, o_ref, s_scr, m_scr, l_scr, acc_scr,
                        *, blk, lam_init):
    i = pl.program_id(1)
    last = pl.num_programs(1) - 1
    slot = lax.rem(i, 2)
    lane = lax.broadcasted_iota(jnp.int32, (blk, DV_DIFF), 1)
    heads = [slice(h * DV_DIFF, (h + 1) * DV_DIFF) for h in range(H_DIFF)]
    lane_chunks = [slice(c * LANES, (c + 1) * LANES) for c in range(blk // LANES)]
    n_maps = 2 * H_DIFF

    def query_maps(q_ref):
        maps = []
        for hs in heads:
            qs = q_ref[0, :, hs].astype(F32) * (DH_DIFF ** -0.5 * math.log2(math.e))
            maps.append(jnp.where(lane < DH_DIFF, qs, 0.0).astype(BF16))
            maps.append(jnp.where(lane >= DH_DIFF, qs, 0.0).astype(BF16))
        return maps

    def fold(x, op):
        r = x[:, lane_chunks[0]]
        for c in lane_chunks[1:]:
            r = op(r, x[:, c])
        return r

    def score_block(q_maps, dst, j, masked):
        rows = pl.ds(pl.multiple_of(j * blk, blk), blk)
        for h, hs in enumerate(heads):
            k = k_ref[0, rows, hs]
            for mi in (2 * h, 2 * h + 1):
                s = lax.dot_general(q_maps[mi], k, _NT, preferred_element_type=F32)
                if masked:
                    row = lax.broadcasted_iota(jnp.int32, (blk, blk), 0)
                    col = lax.broadcasted_iota(jnp.int32, (blk, blk), 1)
                    s = jnp.where(col <= row, s, NEG_INF)
                s_scr[dst, j, mi] = s
                m_scr[dst, mi] = jnp.maximum(m_scr[dst, mi], fold(s, jnp.maximum))

    def finish_max(dst):
        for mi in range(n_maps):
            m_scr[dst, mi] = jnp.broadcast_to(jnp.max(m_scr[dst, mi], axis=-1, keepdims=True), (blk, LANES))

    def prob_block(j):
        rows = pl.ds(pl.multiple_of(j * blk, blk), blk)
        for h, hs in enumerate(heads):
            v = v_ref[0, rows, hs]
            for mi in (2 * h, 2 * h + 1):
                m = m_scr[slot, mi]
                p = jnp.concatenate([jnp.exp2(s_scr[slot, j, mi, :, c] - m) for c in lane_chunks], axis=1)
                l_scr[mi] = l_scr[mi] + fold(p, jnp.add)
                acc_scr[mi] = acc_scr[mi] + jnp.dot(p.astype(BF16), v, preferred_element_type=F32)

    def write_output():
        lam = _diff_lambda_value(dl_ref, lam_init)
        for h, hs in enumerate(heads):
            l1 = jnp.sum(l_scr[2 * h], axis=-1, keepdims=True)
            l2 = jnp.sum(l_scr[2 * h + 1], axis=-1, keepdims=True)
            o = acc_scr[2 * h] / l1 - lam * (acc_scr[2 * h + 1] / l2)
            o_ref[0, :, hs] = _head_norm_gate(o, nw_ref[...], ga_ref[0, :, hs], lam_init).astype(BF16)

    @pl.when(i == 0)
    def _():
        m_scr[0] = jnp.full(m_scr.shape[1:], NEG_INF, F32)
        score_block(query_maps(q0_ref), 0, 0, True)
        finish_max(0)

    l_scr[...] = jnp.zeros(l_scr.shape, F32)
    acc_scr[...] = jnp.zeros(acc_scr.shape, F32)

    @pl.when(i < last)
    def _():
        nxt = 1 - slot
        q_next = query_maps(qn_ref)
        m_scr[nxt] = jnp.full(m_scr.shape[1:], NEG_INF, F32)

        def both(j, carry):
            prob_block(j)
            score_block(q_next, nxt, j, False)
            return carry

        lax.fori_loop(0, i + 1, both, 0)
        write_output()
        score_block(q_next, nxt, i + 1, True)
        finish_max(nxt)

    @pl.when(i == last)
    def _():
        def only_probs(j, carry):
            prob_block(j)
            return carry

        lax.fori_loop(0, i + 1, only_probs, 0)
        write_output()


def _prompt_attn(q, kb, vb, ga, diff_lambda, norm_w, lam_init):
    b, t, _ = q.shape
    blk = min(256, t)
    assert t % blk == 0
    n_maps = 2 * H_DIFF
    nq = t // blk
    qspec = pl.BlockSpec((1, blk, W_DIFF), lambda bi, i: (bi, i, 0))
    q_first = pl.BlockSpec((1, blk, W_DIFF), lambda bi, i: (bi, 0, 0))
    q_next = pl.BlockSpec((1, blk, W_DIFF), lambda bi, i: (bi, jnp.minimum(i + 1, nq - 1), 0))
    kvspec = pl.BlockSpec((1, t, W_DIFF), lambda bi, i: (bi, 0, 0))
    return pl.pallas_call(
        functools.partial(_prompt_attn_kernel, blk=blk, lam_init=lam_init),
        grid=(b, nq),
        in_specs=[q_first, q_next, kvspec, kvspec, qspec,
                  pl.BlockSpec(diff_lambda.shape, lambda bi, i: (0, 0)),
                  pl.BlockSpec((1, DV_DIFF), lambda bi, i: (0, 0))],
        out_specs=qspec,
        out_shape=jax.ShapeDtypeStruct((b, t, W_DIFF), BF16),
        scratch_shapes=[pltpu.VMEM((2, nq, n_maps, blk, blk), F32),
                        pltpu.VMEM((2, n_maps, blk, LANES), F32),
                        pltpu.VMEM((n_maps, blk, LANES), F32),
                        pltpu.VMEM((n_maps, blk, DV_DIFF), F32)],
        compiler_params=pltpu.CompilerParams(
            dimension_semantics=("arbitrary", "arbitrary"), vmem_limit_bytes=VMEM_LIMIT),
        name="prompt_attn",
    )(q, q, kb, vb, ga, diff_lambda, norm_w.reshape(1, DV_DIFF))


class _DecodeStep:
    def __init__(self, step, n_steps, spq, first_page, pages, pt_ref, in_refs, o_ref, scratch_refs, lam_init):
        self.step, self.n_steps, self.pages, self.first_page = step, n_steps, pages, first_page
        self.g, self.n_g, self.lam_init, self.pt_ref = lax.rem(step, spq), spq, lam_init, pt_ref
        (self.q_ref, self.kn_ref, self.vn_ref, self.ga_ref, self.dl_ref, self.nw_ref,
         self.ck_ref, self.cv_ref) = in_refs
        self.o_ref = o_ref
        (self.qblk_ref, self.m_ref, self.l_ref, self.acc_ref,
         self.kbuf_ref, self.vbuf_ref, self.sem_ref) = scratch_refs
        self.slot = lax.rem(step, 2)

    def _query(self):
        return self.q_ref[0].astype(F32) * DH_DIFF ** -0.5

    def _page_copies(self, step, slot):
        base = self.first_page + step * self.pages
        copies = []
        for i in range(self.pages):
            pid = self.pt_ref[base + i]
            copies.append(pltpu.make_async_copy(self.ck_ref.at[pid], self.kbuf_ref.at[slot, i],
                                                self.sem_ref.at[slot, 0]))
            copies.append(pltpu.make_async_copy(self.cv_ref.at[pid], self.vbuf_ref.at[slot, i],
                                                self.sem_ref.at[slot, 1]))
        return copies

    def init(self):
        @pl.when(self.step == 0)
        def _():
            for cp in self._page_copies(0, 0):
                cp.start()

        for cp in self._page_copies(self.step, self.slot):
            cp.wait()

        @pl.when(self.step + 1 < self.n_steps)
        def _():
            for cp in self._page_copies(self.step + 1, 1 - self.slot):
                cp.start()

        @pl.when(self.g == 0)
        def _():
            self.m_ref[...] = jnp.full(self.m_ref.shape, NEG_INF, F32)
            self.l_ref[...] = jnp.zeros(self.l_ref.shape, F32)
            self.acc_ref[...] = jnp.zeros(self.acc_ref.shape, F32)
            n_maps = 2 * H_DIFF
            rows = lax.broadcasted_iota(jnp.int32, (n_maps, W_DIFF), 0)
            lanes = lax.broadcasted_iota(jnp.int32, (n_maps, W_DIFF), 1)
            self.qblk_ref[...] = jnp.where((lanes >> MAP_BITS) == rows,
                                           jnp.broadcast_to(self._query(), (n_maps, W_DIFF)), 0.0)

    def main(self):
        k_pages = [self.kbuf_ref.at[self.slot, i] for i in range(self.pages)]
        v_pages = [self.vbuf_ref.at[self.slot, i] for i in range(self.pages)]
        _decode_pages(k_pages, v_pages, self.qblk_ref, self.m_ref, self.l_ref, self.acc_ref)

    def finalize(self):
        @pl.when(self.g == self.n_g - 1)
        def _():
            _decode_finish(self.qblk_ref[...].astype(BF16), self.kn_ref, self.vn_ref, self.ga_ref, self.dl_ref,
                           self.nw_ref, self.o_ref, self.m_ref, self.l_ref, self.acc_ref, self.lam_init)


def _decode_pages(kp_refs, vp_refs, qblk_ref, m_ref, l_ref, acc_ref):
    n_maps = 2 * H_DIFF
    page = kp_refs[0].shape[1]
    q_blk = qblk_ref[...].astype(BF16)
    s = jnp.concatenate(
        [jnp.dot(q_blk, kp[...].astype(BF16), preferred_element_type=F32) for kp in kp_refs], axis=1)
    m_prev = m_ref[:, 0:1]
    m_new = jnp.maximum(m_prev, jnp.max(s, axis=-1, keepdims=True))
    p = jnp.exp(s - m_new)
    alpha = jnp.exp(m_prev - m_new)
    l_new = alpha * l_ref[:, 0:1] + jnp.sum(p, axis=-1, keepdims=True)
    pb = p.astype(BF16)
    for h in range(H_DIFF):
        pv = jnp.zeros((n_maps, DV_DIFF), F32)
        for i, vp in enumerate(vp_refs):
            v_head = vp[pl.ds(h, page, stride=H_DIFF), :].astype(BF16)
            pv = pv + jnp.dot(pb[:, i * page:(i + 1) * page], v_head, preferred_element_type=F32)
        acc_ref[h] = alpha * acc_ref[h] + pv
    m_ref[...] = jnp.broadcast_to(m_new, m_ref.shape)
    l_ref[...] = jnp.broadcast_to(l_new, l_ref.shape)


def _decode_finish(q_blk, kn_ref, vn_ref, ga_ref, dl_ref, nw_ref, o_ref, m_ref, l_ref, acc_ref, lam_init):
    k_self = jnp.broadcast_to(kn_ref[0].astype(BF16), (SUBLANES, W_DIFF))
    s_self = lax.dot_general(q_blk, k_self, _NT, preferred_element_type=F32)[:, 0:1]
    m_past = m_ref[:, 0:1]
    m_f = jnp.maximum(m_past, s_self)
    a_f = jnp.exp(m_past - m_f)
    p_self = jnp.exp(s_self - m_f)
    l_f = a_f * l_ref[:, 0:1] + p_self
    pv_self = p_self.astype(BF16).astype(F32) * vn_ref[0].astype(BF16).astype(F32)
    lam = _diff_lambda_value(dl_ref, lam_init)
    for h in range(H_DIFF):
        hs = slice(h * DV_DIFF, (h + 1) * DV_DIFF)
        a = (a_f * acc_ref[h] + pv_self[:, hs]) / l_f
        o = a[2 * h:2 * h + 1] - lam * a[2 * h + 1:2 * h + 2]
        o_ref[0, :, hs] = _head_norm_gate(o, nw_ref[...], ga_ref[0, :, hs], lam_init).astype(BF16)


class _DecodeRider:
    def __init__(self, q, k_new, v_new, ga, cache_k, cache_v, layer, page_table, diff_lambda, norm_w, lam_init,
                 seq0, n_seq, grid):
        depth, n_phys, page = cache_k.shape[:3]
        n_pages = page_table.shape[1]
        n_steps = grid[0] * grid[1]
        assert page == LANES
        pages = n_seq * n_pages // n_steps
        self.ok = pages >= 1 and pages * n_steps == n_seq * n_pages and n_pages % pages == 0
        if not self.ok:
            return
        spq = n_pages // pages
        self.pages, self.spq, self.lam_init, self.grid = pages, spq, lam_init, grid
        self.first_page = seq0 * n_pages
        ck = jnp.transpose(cache_k, (0, 1, 3, 4, 5, 2)).reshape(depth * n_phys, W_DIFF, page)
        cv = cache_v.reshape(depth * n_phys, page * H_DIFF, DV_DIFF)
        self.page_ids = page_table.reshape(-1) + layer * n_phys
        step = lambda i0, i1: i0 * grid[1] + i1
        tok = pl.BlockSpec((1, 1, W_DIFF), lambda i0, i1, pt: (seq0 + step(i0, i1) // spq, 0, 0))
        const = lambda shape: pl.BlockSpec(shape, lambda i0, i1, pt: (0, 0))
        hbm = pl.BlockSpec(memory_space=pl.ANY)
        self.inputs = [q, k_new, v_new, ga, diff_lambda, norm_w.reshape(1, DV_DIFF), ck, cv]
        self.in_specs = [tok, tok, tok, tok, const(diff_lambda.shape), const((1, DV_DIFF)), hbm, hbm]
        self.out_spec = pl.BlockSpec((1, 1, W_DIFF), lambda i0, i1, pt: (step(i0, i1) // spq, 0, 0))
        self.out_shape = jax.ShapeDtypeStruct((n_seq, 1, W_DIFF), BF16)
        n_maps = 2 * H_DIFF
        self.scratch_shapes = [pltpu.VMEM((n_maps, W_DIFF), F32),
                               pltpu.VMEM((n_maps, LANES), F32), pltpu.VMEM((n_maps, LANES), F32),
                               pltpu.VMEM((H_DIFF, n_maps, DV_DIFF), F32),
                               pltpu.VMEM((2, pages, W_DIFF, page), F32),
                               pltpu.VMEM((2, pages, page * H_DIFF, DV_DIFF), F32),
                               pltpu.SemaphoreType.DMA((2, 2))]

    def step(self, pt_ref, in_refs, out_ref, scratch_refs):
        step = pl.program_id(0) * self.grid[1] + pl.program_id(1)
        return _DecodeStep(step, self.grid[0] * self.grid[1], self.spq, self.first_page, self.pages, pt_ref,
                           in_refs, out_ref, scratch_refs, self.lam_init)


def _ride(host_kernel, n_in, n_out, n_scratch, rider):
    n_rin = len(rider.inputs)

    def kernel(pt_ref, *refs):
        host_in, refs = refs[:n_in], refs[n_in:]
        rider_in, refs = refs[:n_rin], refs[n_rin:]
        host_out, refs = refs[:n_out], refs[n_out:]
        rider_out, refs = refs[0], refs[1:]
        host_scratch, rider_scratch = refs[:n_scratch], refs[n_scratch:]
        host_kernel(*host_in, *host_out, *host_scratch,
                    decode=rider.step(pt_ref, rider_in, rider_out, rider_scratch))

    return kernel


def _hosted_call(host_kernel, grid, in_specs, out_specs, out_shape, scratch_shapes, inputs, name, rider):
    params = pltpu.CompilerParams(dimension_semantics=("arbitrary", "arbitrary"), vmem_limit_bytes=VMEM_LIMIT)
    if rider is None:
        outs = pl.pallas_call(host_kernel, grid=grid, in_specs=in_specs, out_specs=out_specs, out_shape=out_shape,
                              scratch_shapes=scratch_shapes, compiler_params=params, name=name)(*inputs)
        return outs, None
    grid_spec = pltpu.PrefetchScalarGridSpec(
        num_scalar_prefetch=1, grid=grid,
        in_specs=list(in_specs) + rider.in_specs,
        out_specs=list(out_specs) + [rider.out_spec],
        scratch_shapes=list(scratch_shapes) + rider.scratch_shapes)
    outs = pl.pallas_call(
        _ride(host_kernel, len(in_specs), len(out_specs), len(scratch_shapes), rider),
        grid_spec=grid_spec, out_shape=list(out_shape) + [rider.out_shape],
        compiler_params=params, name=name)(rider.page_ids, *inputs, *rider.inputs)
    return outs[:-1], outs[-1]


def _decode_attn_kernel(pt_ref, *refs, rider):
    n_rin = len(rider.inputs)
    decode = rider.step(pt_ref, refs[:n_rin], refs[n_rin], refs[n_rin + 1:])
    decode.init()
    decode.main()
    decode.finalize()


def _decode_attn(rider_args, n_seq):
    n_pages = rider_args["page_table"].shape[1]
    grid = (n_seq, n_pages // math.gcd(DECODE_PAGES_PER_STEP, n_pages))
    rider = _DecodeRider(**rider_args, seq0=0, n_seq=n_seq, grid=grid)
    grid_spec = pltpu.PrefetchScalarGridSpec(
        num_scalar_prefetch=1, grid=grid, in_specs=rider.in_specs,
        out_specs=rider.out_spec, scratch_shapes=rider.scratch_shapes)
    return pl.pallas_call(
        functools.partial(_decode_attn_kernel, rider=rider), grid_spec=grid_spec, out_shape=rider.out_shape,
        compiler_params=pltpu.CompilerParams(
            dimension_semantics=("arbitrary", "arbitrary"), vmem_limit_bytes=VMEM_LIMIT),
        name="decode_attn")(rider.page_ids, *rider.inputs)


def _gate_rows(tail, alog_row, dtb_row):
    beta = _sigmoid(tail)
    g = -jnp.exp(alog_row) * _softplus(tail + dtb_row)
    return beta, g


def _lane_bcast(x, lane, rows):
    return jnp.broadcast_to(x[:, lane:lane + 1], (rows, LANES))


def _l2norm(x):
    return x * lax.rsqrt(jnp.sum(x * x, axis=-1, keepdims=True) + L2_EPS)


def _split_bf16(x):
    hi = x.astype(BF16)
    return hi, (x - hi.astype(F32)).astype(BF16)


def _dot_split(lhs, rhs):
    d = lambda a, b: jnp.dot(a, b, preferred_element_type=F32)
    return d(lhs[0], rhs[0]) + d(lhs[0], rhs[1]) + d(lhs[1], rhs[0])


def _unit_lower_inverses(a_mats):
    n = a_mats[0].shape[0]
    eye = (lax.broadcasted_iota(jnp.int32, (n, n), 0) == lax.broadcasted_iota(jnp.int32, (n, n), 1)).astype(F32)
    xs = [-a for a in a_mats]
    ps = [eye + x for x in xs]
    splits = [_split_bf16(x) for x in xs]
    xs = [_dot_split(s, s) for s in splits]
    power = 2
    while 2 * power < CHUNK:
        both = [_dot_split(_split_bf16(x), _split_bf16(jnp.concatenate([p, x], axis=1))) for p, x in zip(ps, xs)]
        ps = [p + b[:, :n] for p, b in zip(ps, both)]
        xs = [b[:, n:] for b in both]
        power *= 2
    return [p + _dot_split(_split_bf16(x), _split_bf16(p)) for p, x in zip(ps, xs)]


def _delta_prep_kernel(cin_ref, prev_ref, cw_ref, tail_ref, alog_ref, dtb_ref,
                       u_ref, w_ref, qg_ref, at_ref, kdt_ref, el_ref, ext_ref, *, pairs, decode=None):
    t = pl.program_id(1)
    pad = SUBLANES
    keep = CONV_W - 1
    rows = pairs * PAIR

    @pl.when(t == 0)
    def _():
        ext_ref[pad - keep:pad, :] = prev_ref[0]

    @pl.when(t > 0)
    def _():
        ext_ref[pad - keep:pad, :] = ext_ref[pad + rows - keep:pad + rows, :]

    if decode is not None:
        decode.init()

    ext_ref[pad:pad + rows, :] = cin_ref[0]

    r = lax.broadcasted_iota(jnp.int32, (PAIR, PAIR), 0)
    c = lax.broadcasted_iota(jnp.int32, (PAIR, PAIR), 1)
    same = (r >> CHUNK_BITS) == (c >> CHUNK_BITS)
    incl = same & (c <= r)
    strict = same & (c < r)
    incl_t = same & (r <= c)
    chunk_end = r == (c | (CHUNK - 1))

    chains = []
    for pi in range(pairs):
        base = pad - keep + pi * PAIR
        conv = ext_ref[base + keep:base + keep + PAIR, :] * cw_ref[keep:keep + 1, :]
        for j in reversed(range(keep)):
            conv = conv + ext_ref[base + j:base + j + PAIR, :] * cw_ref[j:j + 1, :]
        conv = _silu(conv)
        rs = slice(pi * PAIR, (pi + 1) * PAIR)
        beta_t, g_t = _gate_rows(tail_ref[0, :, rs], alog_ref[...], dtb_ref[...])
        gc_t = jnp.dot(g_t, incl_t.astype(F32), precision=HIGHEST, preferred_element_type=F32)
        g_last_t = jnp.dot(gc_t, chunk_end.astype(F32), precision=HIGHEST, preferred_element_type=F32)
        for h in range(H_DELTA):
            hs = slice(h * DK_DELTA, (h + 1) * DK_DELTA)
            qs = _l2norm(conv[:, hs]) * DK_DELTA ** -0.5
            kn = _l2norm(conv[:, W_DELTA + h * DK_DELTA:W_DELTA + (h + 1) * DK_DELTA])
            vh = conv[:, 2 * W_DELTA + h * DV_DELTA:2 * W_DELTA + (h + 1) * DV_DELTA]
            beta = jnp.broadcast_to(beta_t[h:h + 1, :], (PAIR, PAIR)).T
            gc_cols = jnp.broadcast_to(gc_t[H_DELTA + h:H_DELTA + h + 1, :], (PAIR, PAIR))
            g_last_cols = jnp.broadcast_to(g_last_t[H_DELTA + h:H_DELTA + h + 1, :], (PAIR, PAIR))
            chains.append(dict(pi=pi, h=h, rs=rs, qs=qs, kn=kn, vh=vh, beta=beta, gc=gc_cols.T, gc_cols=gc_cols,
                               g_last_cols=g_last_cols, kbeta=kn * beta))

    for ch in chains:
        ch["kn_t"] = ch["kn"].T
        kn_tb = ch["kn_t"].astype(BF16)
        ch["kk"] = jnp.dot(ch["kbeta"].astype(BF16), kn_tb, preferred_element_type=F32)
        ch["qk"] = jnp.dot(ch["qs"].astype(BF16), kn_tb, preferred_element_type=F32)
    for ch in chains:
        ch["decay"] = jnp.exp(jnp.where(incl, ch["gc"] - ch["gc_cols"], -jnp.inf))
    t_mats = _unit_lower_inverses([jnp.where(strict, ch["kk"] * ch["decay"], 0.0) for ch in chains])
    for ch, t_mat in zip(chains, t_mats):
        rhs = jnp.concatenate([ch["vh"] * ch["beta"], ch["kbeta"] * jnp.exp(ch["gc"])], axis=1).astype(BF16)
        ch["uw"] = jnp.dot(t_mat.astype(BF16), rhs, preferred_element_type=F32)
    for ch in chains:
        pi, h, rs, gc = ch["pi"], ch["h"], ch["rs"], ch["gc"]
        u_ref[0, h, rs, :] = ch["uw"][:, :DV_DELTA]
        w_ref[0, h, rs, :] = ch["uw"][:, DV_DELTA:].astype(BF16)
        attn = jnp.where(incl, ch["qk"] * ch["decay"], 0.0)
        at_ref[0, h, rs, :] = jnp.concatenate([attn[:CHUNK, :CHUNK], attn[CHUNK:, CHUNK:]], axis=0).astype(BF16)
        qg_ref[0, h, rs, :] = (ch["qs"] * jnp.exp(gc)).astype(BF16)
        kdt_ref[0, h, pi] = (ch["kn_t"] * jnp.exp(ch["g_last_cols"] - ch["gc_cols"])).astype(BF16)
        el_ref[0, h, pi] = jnp.exp(jnp.concatenate([jnp.broadcast_to(gc[CHUNK - 1:CHUNK], (SUBLANES, LANES)),
                                                    jnp.broadcast_to(gc[PAIR - 1:PAIR], (SUBLANES, LANES))], axis=0))

    if decode is not None:
        decode.main()
        decode.finalize()


def _delta_prep_grid(t):
    assert t % PAIR == 0
    npair = t // PAIR
    pairs = 2 if npair % 2 == 0 else 1
    return pairs, npair // pairs


def _delta_prep(cin, conv_prev, conv_w, tail, alog_row, dtb_row, rider):
    b, t, _ = cin.shape
    npair = t // PAIR
    pairs, nt = _delta_prep_grid(t)
    rows = pairs * PAIR
    per_head = lambda width: pl.BlockSpec((1, H_DELTA, rows, width), lambda bi, ti, *_: (bi, 0, ti, 0))
    const = lambda shape: pl.BlockSpec(shape, lambda bi, ti, *_: (0,) * len(shape))
    out_shape = [
        jax.ShapeDtypeStruct((b, H_DELTA, t, DV_DELTA), F32),
        jax.ShapeDtypeStruct((b, H_DELTA, t, DK_DELTA), BF16),
        jax.ShapeDtypeStruct((b, H_DELTA, t, DK_DELTA), BF16),
        jax.ShapeDtypeStruct((b, H_DELTA, t, CHUNK), BF16),
        jax.ShapeDtypeStruct((b, H_DELTA, npair, DK_DELTA, PAIR), BF16),
        jax.ShapeDtypeStruct((b, H_DELTA, npair, 2 * SUBLANES, LANES), F32),
    ]
    out_specs = [per_head(DV_DELTA), per_head(DK_DELTA), per_head(DK_DELTA), per_head(CHUNK),
                 pl.BlockSpec((1, H_DELTA, pairs, DK_DELTA, PAIR), lambda bi, ti, *_: (bi, 0, ti, 0, 0)),
                 pl.BlockSpec((1, H_DELTA, pairs, 2 * SUBLANES, LANES), lambda bi, ti, *_: (bi, 0, ti, 0, 0))]
    return _hosted_call(
        functools.partial(_delta_prep_kernel, pairs=pairs),
        grid=(b, nt),
        in_specs=[pl.BlockSpec((1, rows, CONV_CH), lambda bi, ti, *_: (bi, ti, 0)),
                  pl.BlockSpec((1, CONV_W - 1, CONV_CH), lambda bi, ti, *_: (bi, 0, 0)),
                  const(conv_w.shape),
                  pl.BlockSpec((1, 2 * H_DELTA, rows), lambda bi, ti, *_: (bi, 0, ti)),
                  const(alog_row.shape), const(dtb_row.shape)],
        out_specs=out_specs,
        out_shape=out_shape,
        scratch_shapes=[pltpu.VMEM((SUBLANES + rows, CONV_CH), F32)],
        inputs=(cin, conv_prev, conv_w, tail, alog_row, dtb_row),
        name="delta_prep", rider=rider)


def _delta_norm_gate(o, normw, z):
    ms = jnp.mean(o * o, axis=-1, keepdims=True)
    return o * lax.rsqrt(ms + DELTA_NORM_EPS) * normw * _silu(z)


def _delta_scan_kernel(u_ref, w_ref, qg_ref, at_ref, kdt_ref, el_ref, z_ref, s0_ref, nw_ref,
                       o_ref, sfin_ref, s_scr, *, nb, nblk):
    t = pl.program_id(1)

    @pl.when(t == 0)
    def _():
        s_scr[...] = s0_ref[...]

    def block(jb, carry):
        r0 = pl.multiple_of(jb * PAIR, PAIR)
        seqs = [(bi, h) for bi in range(nb) for h in range(H_DELTA)]
        for ci in range(2):
            rows = pl.ds(pl.multiple_of(r0 + ci * CHUNK, CHUNK), CHUNK)
            states = [s_scr[bi, h] for bi, h in seqs]
            res = [jnp.dot(jnp.concatenate([w_ref[bi, h, rows, :], qg_ref[bi, h, rows, :]], axis=0),
                           s.astype(BF16), preferred_element_type=F32) for (bi, h), s in zip(seqs, states)]
            v_new = [(u_ref[bi, h, rows, :] - r[:CHUNK]).astype(BF16) for (bi, h), r in zip(seqs, res)]
            upd = [jnp.dot(kdt_ref[bi, h, jb, :, ci * CHUNK:(ci + 1) * CHUNK], v, preferred_element_type=F32)
                   for (bi, h), v in zip(seqs, v_new)]
            intra = [jnp.dot(at_ref[bi, h, rows, :], v, preferred_element_type=F32) for (bi, h), v in zip(seqs, v_new)]
            for (bi, h), s, r, du, oi in zip(seqs, states, res, upd, intra):
                decay = jnp.broadcast_to(el_ref[bi, h, jb, ci * SUBLANES:ci * SUBLANES + 1, :], (DK_DELTA, DV_DELTA))
                s_scr[bi, h] = s * decay + du
                hs = slice(h * DV_DELTA, (h + 1) * DV_DELTA)
                o_ref[bi, rows, hs] = _delta_norm_gate(r[CHUNK:] + oi, nw_ref[...], z_ref[bi, rows, hs]).astype(BF16)
        return carry

    lax.fori_loop(0, nblk, block, 0)

    @pl.when(t == pl.num_programs(1) - 1)
    def _():
        sfin_ref[...] = s_scr[...]


def _delta_scan(prep, z, s0, norm_w):
    u, w, qg, at, kdt, el = prep
    b, _, t, _ = u.shape
    nb = math.gcd(b, 4)
    ts = min(512, t)
    assert t % ts == 0
    nblk = ts // PAIR
    per_head = lambda width: pl.BlockSpec((nb, H_DELTA, ts, width), lambda bi, ti: (bi, 0, ti, 0))
    state = pl.BlockSpec((nb, H_DELTA, DK_DELTA, DV_DELTA), lambda bi, ti: (bi, 0, 0, 0))
    tok = pl.BlockSpec((nb, ts, W_DELTA), lambda bi, ti: (bi, ti, 0))
    return pl.pallas_call(
        functools.partial(_delta_scan_kernel, nb=nb, nblk=nblk),
        grid=(b // nb, t // ts),
        in_specs=[per_head(DV_DELTA), per_head(DK_DELTA), per_head(DK_DELTA), per_head(CHUNK),
                  pl.BlockSpec((nb, H_DELTA, nblk, DK_DELTA, PAIR), lambda bi, ti: (bi, 0, ti, 0, 0)),
                  pl.BlockSpec((nb, H_DELTA, nblk, 2 * SUBLANES, LANES), lambda bi, ti: (bi, 0, ti, 0, 0)),
                  tok, state, pl.BlockSpec((1, DV_DELTA), lambda bi, ti: (0, 0))],
        out_specs=[tok, state],
        out_shape=[jax.ShapeDtypeStruct((b, t, W_DELTA), BF16),
                   jax.ShapeDtypeStruct((b, H_DELTA, DK_DELTA, DV_DELTA), F32)],
        scratch_shapes=[pltpu.VMEM((nb, H_DELTA, DK_DELTA, DV_DELTA), F32)],
        compiler_params=pltpu.CompilerParams(
            dimension_semantics=("arbitrary", "arbitrary"), vmem_limit_bytes=VMEM_LIMIT),
        name="delta_scan",
    )(u, w, qg, at, kdt, el, z, s0, norm_w.reshape(1, DV_DELTA))


def _delta_step_kernel(cin_ref, prev_ref, cw_ref, tail_ref, alog_ref, dtb_ref, z_ref, s0_ref, nw_ref,
                       o_ref, s_ref, *, seqs):
    items = []
    for b in range(seqs):
        prev = prev_ref[b]
        conv = prev[0:1] * cw_ref[0:1, :]
        for j in range(1, CONV_W - 1):
            conv = conv + prev[j:j + 1] * cw_ref[j:j + 1, :]
        conv = _silu(conv + cin_ref[b] * cw_ref[CONV_W - 1:CONV_W, :])
        beta_all, g_all = _gate_rows(tail_ref[b], alog_ref[...], dtb_ref[...])
        for h in range(H_DELTA):
            hs = slice(h * DK_DELTA, (h + 1) * DK_DELTA)
            qs = _l2norm(conv[:, hs]) * DK_DELTA ** -0.5
            kn = _l2norm(conv[:, W_DELTA + h * DK_DELTA:W_DELTA + (h + 1) * DK_DELTA])
            vh = conv[:, 2 * W_DELTA + h * DV_DELTA:2 * W_DELTA + (h + 1) * DV_DELTA]
            beta = beta_all[:, h:h + 1]
            eg = jnp.exp(g_all[:, H_DELTA + h:H_DELTA + h + 1])
            items.append((b, h, hs, qs, kn, vh, beta, eg, s0_ref[b, h]))
    res = [jnp.dot(jnp.concatenate([kn * (beta * eg), qs * eg, jnp.zeros((SUBLANES - 2, DK_DELTA), F32)],
                                   axis=0).astype(BF16),
                   s.astype(BF16), preferred_element_type=F32)
           for (b, h, hs, qs, kn, vh, beta, eg, s) in items]
    for (b, h, hs, qs, kn, vh, beta, eg, s), r in zip(items, res):
        v_new = vh * beta - r[0:1]
        qk = jnp.sum(qs.astype(BF16).astype(F32) * kn.astype(BF16).astype(F32), axis=-1, keepdims=True)
        o = r[1:2] + qk * v_new
        k_col = jnp.broadcast_to(kn, (DK_DELTA, DK_DELTA)).T
        s_ref[b, h] = s * eg + k_col * v_new
        o_ref[b, :, hs] = _delta_norm_gate(o, nw_ref[...], z_ref[b, :, hs]).astype(BF16)


def _delta_step(cin, conv_prev, conv_w, tail, alog_row, dtb_row, z, s0, norm_w):
    bs = cin.shape[0]
    seqs = math.gcd(bs, 4)
    tok = lambda width: pl.BlockSpec((seqs, 1, width), lambda b: (b, 0, 0))
    const = lambda shape: pl.BlockSpec(shape, lambda b: (0,) * len(shape))
    state = pl.BlockSpec((seqs, H_DELTA, DK_DELTA, DV_DELTA), lambda b: (b, 0, 0, 0))
    return pl.pallas_call(
        functools.partial(_delta_step_kernel, seqs=seqs),
        grid=(bs // seqs,),
        in_specs=[tok(CONV_CH), pl.BlockSpec((seqs, CONV_W - 1, CONV_CH), lambda b: (b, 0, 0)), const(conv_w.shape),
                  tok(LANES), const(alog_row.shape), const(dtb_row.shape), tok(W_DELTA), state,
                  const((1, DV_DELTA))],
        out_specs=[tok(W_DELTA), state],
        out_shape=[jax.ShapeDtypeStruct((bs, 1, W_DELTA), BF16),
                   jax.ShapeDtypeStruct((bs, H_DELTA, DK_DELTA, DV_DELTA), F32)],
        compiler_params=pltpu.CompilerParams(dimension_semantics=("arbitrary",)),
        name="delta_step",
    )(cin, conv_prev, conv_w, tail, alog_row, dtb_row, z, s0, norm_w.reshape(1, DV_DELTA))


def _out_proj_kernel(oa_ref, ob_ref, x_ref, w_ref, g_ref, b_ref, y_ref, *, alpha):
    mix = jnp.dot(oa_ref[...], w_ref[:W_DIFF, :], preferred_element_type=F32)
    mix = mix + jnp.dot(ob_ref[...], w_ref[W_DIFF:, :], preferred_element_type=F32)
    r = alpha * x_ref[...] + mix
    mu = jnp.mean(r, axis=-1, keepdims=True)
    var = jnp.mean(jnp.square(r - mu), axis=-1, keepdims=True)
    y_ref[...] = (r - mu) * lax.rsqrt(var + LN_EPS) * g_ref[...] + b_ref[...]


def _out_proj(oa, ob, x, w_out_b, ln_g, ln_b, alpha):
    m, d = x.shape
    tm = min(1024, m)
    assert m % tm == 0
    row = lambda width: pl.BlockSpec((tm, width), lambda i: (i, 0))
    const = lambda shape: pl.BlockSpec(shape, lambda i: (0, 0))
    return pl.pallas_call(
        functools.partial(_out_proj_kernel, alpha=alpha),
        grid=(m // tm,),
        in_specs=[row(W_DIFF), row(W_DELTA), row(d), const(w_out_b.shape), const((1, d)), const((1, d))],
        out_specs=row(d),
        out_shape=jax.ShapeDtypeStruct((m, d), F32),
        compiler_params=pltpu.CompilerParams(dimension_semantics=("arbitrary",), vmem_limit_bytes=VMEM_LIMIT),
        name="out_proj",
    )(oa, ob, x, w_out_b, ln_g.reshape(1, d), ln_b.reshape(1, d))


def _pad_lanes(vec, offset):
    return jnp.zeros((1, LANES), F32).at[0, offset:offset + vec.shape[0]].set(vec.astype(F32))


def _gate_rows_param(vec):
    col = jnp.broadcast_to(vec.astype(F32)[:, None], (H_DELTA, LANES))
    return jnp.concatenate([jnp.zeros((H_DELTA, LANES), F32), col], axis=0)


def kernel(x_prompt, x_sample, cache_k, cache_v, page_table, state_delta, state_conv, w_in, conv_w, a_log,
           dt_bias, delta_norm_w, diff_lambda, diff_norm_w, w_out, ln_g, ln_b):
    depth = w_in.shape[0]
    bp, tp, d = x_prompt.shape
    bs, ts, _ = x_sample.shape
    assert ts == 1 and w_in.shape[2] == P_MAIN + 2 * H_DELTA and d == w_out.shape[2]
    past_len = page_table.shape[1] * cache_k.shape[2]
    alpha = (2 * depth) ** 0.25
    tables_p = _rope_tables(jnp.arange(tp, dtype=jnp.int32))
    tables_s = _rope_tables(jnp.full((bs,), past_len, jnp.int32))

    hp, hs = x_prompt, x_sample
    outs = [[] for _ in range(8)]
    for l in range(depth):
        lam_init = _lambda_init(l)
        w_main = w_in[l].astype(BF16)
        w_tail = jnp.zeros((d, LANES), BF16).at[:, :2 * H_DELTA].set(w_in[l, :, P_MAIN:].astype(BF16))
        w_out_b = w_out[l].astype(BF16)
        alog_row = _pad_lanes(a_log[l], H_DELTA)
        dtb_row = _pad_lanes(dt_bias[l], H_DELTA)
        alog_col = _gate_rows_param(a_log[l])
        dtb_col = _gate_rows_param(dt_bias[l])

        sq, sk, _, sv, _, sga, scin, sz, stail = _in_proj(hs.reshape(1, bs, d), w_main, w_tail, tables_s, False)
        tok = lambda a: a.reshape(bs, 1, -1)
        decode_args = dict(q=tok(sq), k_new=tok(sk), v_new=tok(sv), ga=tok(sga), cache_k=cache_k, cache_v=cache_v,
                           layer=l, page_table=page_table, diff_lambda=diff_lambda[l], norm_w=diff_norm_w[l],
                           lam_init=lam_init)
        rider = _DecodeRider(**decode_args, seq0=0, n_seq=bs, grid=(bp, _delta_prep_grid(tp)[1]))
        if not rider.ok:
            rider = None

        q, k, kb, v, vb, ga, cin, z, _, tail_t = _in_proj(hp, w_main, w_tail, tables_p, True)
        oa = _prompt_attn(q, kb, vb, ga, diff_lambda[l], diff_norm_w[l], lam_init)
        prep, soa = _delta_prep(cin, jnp.zeros((bp, CONV_W - 1, CONV_CH), F32), conv_w[l], tail_t, alog_col,
                                dtb_col, rider)
        ob, sp = _delta_scan(prep, z, jnp.zeros((bp, H_DELTA, DK_DELTA, DV_DELTA), F32), delta_norm_w[l])
        hp = _out_proj(oa.reshape(bp * tp, W_DIFF), ob.reshape(bp * tp, W_DELTA), hp.reshape(bp * tp, d),
                       w_out_b, ln_g[l], ln_b[l], alpha).reshape(bp, tp, d)
        outs[0].append(k.reshape(bp, tp, H_DIFF, 2, DH_DIFF))
        outs[1].append(v.reshape(bp, tp, H_DIFF, DV_DIFF))
        outs[2].append(sp)
        outs[3].append(cin[:, tp - (CONV_W - 1):, :])

        if rider is None:
            soa = _decode_attn(decode_args, bs)
        sob, ss = _delta_step(tok(scin), state_conv[l], conv_w[l], tok(stail), alog_row, dtb_row, tok(sz),
                              state_delta[l], delta_norm_w[l])
        hs = _out_proj(soa.reshape(bs, W_DIFF), sob.reshape(bs, W_DELTA), hs.reshape(bs, d),
                       w_out_b, ln_g[l], ln_b[l], alpha).reshape(bs, 1, d)
        outs[4].append(sk.reshape(bs, 1, H_DIFF, 2, DH_DIFF))
        outs[5].append(sv.reshape(bs, 1, H_DIFF, DV_DIFF))
        outs[6].append(ss)
        outs[7].append(jnp.concatenate([state_conv[l][:, 1:, :], tok(scin)], axis=1))
    return (hp, hs) + tuple(jnp.stack(o) for o in outs)
```

```python
import functools
import math

import jax
import jax.numpy as jnp
from jax import lax
from jax.experimental import pallas as pl
from jax.experimental.pallas import tpu as pltpu

F32 = jnp.float32
BF16 = jnp.bfloat16
HIGHEST = lax.Precision.HIGHEST

H_DIFF = 4
DH_DIFF = 64
DV_DIFF = 2 * DH_DIFF
W_DIFF = H_DIFF * DV_DIFF
H_DELTA = 4
DK_DELTA = 128
DV_DELTA = 128
W_DELTA = H_DELTA * DK_DELTA
ROT_DIM = DH_DIFF // 4
ROPE_THETA = 500000.0
CONV_W = 4
CONV_CH = 3 * W_DELTA
CHUNK = 64
P_MAIN = 4 * W_DIFF + 4 * W_DELTA
LN_EPS = 1e-5
HEAD_NORM_EPS = 1e-5
DELTA_NORM_EPS = 1e-6
L2_EPS = 1e-6
NEG_INF = -1e30

LANES = 128
SUBLANES = 8
PAIR = 2 * CHUNK
CHUNK_BITS = CHUNK.bit_length() - 1
MAP_BITS = DH_DIFF.bit_length() - 1
VMEM_LIMIT = 56 * 1024 * 1024
DECODE_PAGES_PER_STEP = 16

_NT = (((1,), (1,)), ((), ()))


def _sigmoid(x):
    return 0.5 * jnp.tanh(0.5 * x) + 0.5


def _silu(x):
    return x * _sigmoid(x)


def _softplus(x):
    return jnp.maximum(x, 0.0) + jnp.log1p(jnp.exp(-jnp.abs(x)))


def _lambda_init(layer):
    return 0.8 - 0.6 * math.exp(-0.3 * layer)


def _rope_tables(pos):
    half = ROT_DIM // 2
    inv = ROPE_THETA ** (-jnp.arange(half, dtype=F32) / half)
    ang = pos.astype(F32)[:, None] * inv[None, :]
    cos, sin = jnp.cos(ang), jnp.sin(ang)
    t = pos.shape[0]
    rest = DH_DIFF - ROT_DIM
    cos_m = jnp.concatenate([cos, cos, jnp.ones((t, rest), F32)], axis=-1)
    sin_lo = jnp.concatenate([-sin, jnp.zeros((t, half + rest), F32)], axis=-1)
    sin_hi = jnp.concatenate([jnp.zeros((t, half), F32), sin, jnp.zeros((t, rest), F32)], axis=-1)
    reps = LANES // DH_DIFF
    return jnp.tile(cos_m, (1, reps)), jnp.tile(sin_lo, (1, reps)), jnp.tile(sin_hi, (1, reps))


def _in_proj_kernel(x_ref, w_ref, wt_ref, cos_ref, slo_ref, shi_ref,
                    q_ref, k_ref, kb_ref, v_ref, vb_ref, ga_ref, cin_ref, z_ref, tail_ref, *tail_t_ref):
    xb = x_ref[0].astype(BF16)
    half = ROT_DIM // 2
    tm = xb.shape[0]

    def proj(c0, width):
        return jnp.dot(xb, w_ref[:, c0:c0 + width], preferred_element_type=F32)

    def rope(h):
        blocks = []
        for c in range(W_DIFF // LANES):
            hb = h[:, c * LANES:(c + 1) * LANES]
            blocks.append(hb * cos_ref[...]
                          + pltpu.roll(hb, LANES - half, 1) * slo_ref[...]
                          + pltpu.roll(hb, half, 1) * shi_ref[...])
        return jnp.concatenate(blocks, axis=1)

    q_ref[0] = rope(proj(0, W_DIFF)).astype(BF16)
    k = rope(proj(W_DIFF, W_DIFF))
    k_ref[0] = k
    kb_ref[0] = k.astype(BF16)
    v = proj(2 * W_DIFF, W_DIFF)
    for h in range(H_DIFF):
        v_ref[0, pl.ds(h, tm, stride=H_DIFF), :] = v[:, h * DV_DIFF:(h + 1) * DV_DIFF]
    vb_ref[0] = v.astype(BF16)
    ga_ref[0] = proj(3 * W_DIFF, W_DIFF)
    for j in range(3):
        cin_ref[0, :, j * W_DELTA:(j + 1) * W_DELTA] = proj(4 * W_DIFF + j * W_DELTA, W_DELTA)
    z_ref[0] = proj(4 * W_DIFF + 3 * W_DELTA, W_DELTA)
    tail = jnp.dot(xb, wt_ref[...], preferred_element_type=F32)
    tail_ref[0] = tail
    if tail_t_ref:
        for c in range(tm // LANES):
            cs = slice(c * LANES, (c + 1) * LANES)
            tail_t_ref[0][0, :, cs] = tail[cs, :].T[:2 * H_DELTA, :]


def _in_proj(x, w_main, w_tail, tables, gates_time_major):
    b, t, d = x.shape
    tm = min(512, t)
    assert t % tm == 0 and (not gates_time_major or tm % LANES == 0)
    cos_t, slo_t, shi_t = tables
    row = lambda width: pl.BlockSpec((1, tm, width), lambda ti, bi: (bi, ti, 0))
    tab = pl.BlockSpec((tm, LANES), lambda ti, bi: (ti, 0))
    const = lambda shape: pl.BlockSpec(shape, lambda ti, bi: (0, 0))
    outs = [(W_DIFF, BF16), (W_DIFF, F32), (W_DIFF, BF16), None, (W_DIFF, BF16),
            (W_DIFF, F32), (CONV_CH, F32), (W_DELTA, F32), (LANES, F32)]
    out_specs = [row(o[0]) if o else pl.BlockSpec((1, tm * H_DIFF, DV_DIFF), lambda ti, bi: (bi, ti, 0))
                 for o in outs]
    out_shape = [jax.ShapeDtypeStruct((b, t, o[0]), o[1]) if o else
                 jax.ShapeDtypeStruct((b, t * H_DIFF, DV_DIFF), F32) for o in outs]
    if gates_time_major:
        out_specs.append(pl.BlockSpec((1, 2 * H_DELTA, tm), lambda ti, bi: (bi, 0, ti)))
        out_shape.append(jax.ShapeDtypeStruct((b, 2 * H_DELTA, t), F32))
    return pl.pallas_call(
        _in_proj_kernel,
        grid=(t // tm, b),
        in_specs=[row(d), const((d, P_MAIN)), const(w_tail.shape), tab, tab, tab],
        out_specs=out_specs,
        out_shape=out_shape,
        compiler_params=pltpu.CompilerParams(
            dimension_semantics=("arbitrary", "arbitrary"), vmem_limit_bytes=VMEM_LIMIT),
        name="in_proj",
    )(x, w_main, w_tail, cos_t, slo_t, shi_t)


def _diff_lambda_value(dl_ref, lam_init):
    dl = dl_ref[...]
    a = jnp.sum(dl[0:1] * dl[1:2], axis=1, keepdims=True)
    b = jnp.sum(dl[2:3] * dl[3:4], axis=1, keepdims=True)
    return jnp.exp(a) - jnp.exp(b) + lam_init


def _head_norm_gate(o, normw, gate, lam_init):
    ms = jnp.mean(o * o, axis=-1, keepdims=True)
    o = o * lax.rsqrt(ms + HEAD_NORM_EPS) * normw * (1.0 - lam_init)
    return o * _silu(gate)


def _prompt_attn_kernel(q0_ref, qn_ref, k_ref, v_ref, ga_ref, dl_ref, nw_ref, o_ref, s_scr, m_scr, l_scr, acc_scr,
                        *, blk, lam_init):
    i = pl.program_id(1)
    last = pl.num_programs(1) - 1
    slot = lax.rem(i, 2)
    lane = lax.broadcasted_iota(jnp.int32, (blk, DV_DIFF), 1)
    heads = [slice(h * DV_DIFF, (h + 1) * DV_DIFF) for h in range(H_DIFF)]
    lane_chunks = [slice(c * LANES, (c + 1) * LANES) for c in range(blk // LANES)]
    n_maps = 2 * H_DIFF

    def query_maps(q_ref):
        maps = []
        for hs in heads:
            qs = q_ref[0, :, hs].astype(F32) * (DH_DIFF ** -0.5 * math.log2(math.e))
            maps.append(jnp.where(lane < DH_DIFF, qs, 0.0).astype(BF16))
            maps.append(jnp.where(lane >= DH_DIFF, qs, 0.0).astype(BF16))
        return maps

    def fold(x, op):
        r = x[:, lane_chunks[0]]
        for c in lane_chunks[1:]:
            r = op(r, x[:, c])
        return r

    def score_block(q_maps, dst, j, masked):
        rows = pl.ds(pl.multiple_of(j * blk, blk), blk)
        for h, hs in enumerate(heads):
            k = k_ref[0, rows, hs]
            for mi in (2 * h, 2 * h + 1):
                s = lax.dot_general(q_maps[mi], k, _NT, preferred_element_type=F32)
                if masked:
                    row = lax.broadcasted_iota(jnp.int32, (blk, blk), 0)
                    col = lax.broadcasted_iota(jnp.int32, (blk, blk), 1)
                    s = jnp.where(col <= row, s, NEG_INF)
                s_scr[dst, j, mi] = s
                m_scr[dst, mi] = jnp.maximum(m_scr[dst, mi], fold(s, jnp.maximum))

    def finish_max(dst):
        for mi in range(n_maps):
            m_scr[dst, mi] = jnp.broadcast_to(jnp.max(m_scr[dst, mi], axis=-1, keepdims=True), (blk, LANES))

    def prob_block(j):
        rows = pl.ds(pl.multiple_of(j * blk, blk), blk)
        for h, hs in enumerate(heads):
            v = v_ref[0, rows, hs]
            for mi in (2 * h, 2 * h + 1):
                m = m_scr[slot, mi]
                p = jnp.concatenate([jnp.exp2(s_scr[slot, j, mi, :, c] - m) for c in lane_chunks], axis=1)
                l_scr[mi] = l_scr[mi] + fold(p, jnp.add)
                acc_scr[mi] = acc_scr[mi] + jnp.dot(p.astype(BF16), v, preferred_element_type=F32)

    def write_output():
        lam = _diff_lambda_value(dl_ref, lam_init)
        for h, hs in enumerate(heads):
            l1 = jnp.sum(l_scr[2 * h], axis=-1, keepdims=True)
            l2 = jnp.sum(l_scr[2 * h + 1], axis=-1, keepdims=True)
            o = acc_scr[2 * h] / l1 - lam * (acc_scr[2 * h + 1] / l2)
            o_ref[0, :, hs] = _head_norm_gate(o, nw_ref[...], ga_ref[0, :, hs], lam_init).astype(BF16)

    @pl.when(i == 0)
    def _():
        m_scr[0] = jnp.full(m_scr.shape[1:], NEG_INF, F32)
        score_block(query_maps(q0_ref), 0, 0, True)
        finish_max(0)

    l_scr[...] = jnp.zeros(l_scr.shape, F32)
    acc_scr[...] = jnp.zeros(acc_scr.shape, F32)

    @pl.when(i < last)
    def _():
        nxt = 1 - slot
        q_next = query_maps(qn_ref)
        m_scr[nxt] = jnp.full(m_scr.shape[1:], NEG_INF, F32)

        def both(j, carry):
            prob_block(j)
            score_block(q_next, nxt, j, False)
            return carry

        lax.fori_loop(0, i + 1, both, 0)
        write_output()
        score_block(q_next, nxt, i + 1, True)
        finish_max(nxt)

    @pl.when(i == last)
    def _():
        def only_probs(j, carry):
            prob_block(j)
            return carry

        lax.fori_loop(0, i + 1, only_probs, 0)
        write_output()


def _prompt_attn(q, kb, vb, ga, diff_lambda, norm_w, lam_init):
    b, t, _ = q.shape
    blk = min(256, t)
    assert t % blk == 0
    n_maps = 2 * H_DIFF
    nq = t // blk
    qspec = pl.BlockSpec((1, blk, W_DIFF), lambda bi, i: (bi, i, 0))
    q_first = pl.BlockSpec((1, blk, W_DIFF), lambda bi, i: (bi, 0, 0))
    q_next = pl.BlockSpec((1, blk, W_DIFF), lambda bi, i: (bi, jnp.minimum(i + 1, nq - 1), 0))
    kvspec = pl.BlockSpec((1, t, W_DIFF), lambda bi, i: (bi, 0, 0))
    return pl.pallas_call(
        functools.partial(_prompt_attn_kernel, blk=blk, lam_init=lam_init),
        grid=(b, nq),
        in_specs=[q_first, q_next, kvspec, kvspec, qspec,
                  pl.BlockSpec(diff_lambda.shape, lambda bi, i: (0, 0)),
                  pl.BlockSpec((1, DV_DIFF), lambda bi, i: (0, 0))],
        out_specs=qspec,
        out_shape=jax.ShapeDtypeStruct((b, t, W_DIFF), BF16),
        scratch_shapes=[pltpu.VMEM((2, nq, n_maps, blk, blk), F32),
                        pltpu.VMEM((2, n_maps, blk, LANES), F32),
                        pltpu.VMEM((n_maps, blk, LANES), F32),
                        pltpu.VMEM((n_maps, blk, DV_DIFF), F32)],
        compiler_params=pltpu.CompilerParams(
            dimension_semantics=("arbitrary", "arbitrary"), vmem_limit_bytes=VMEM_LIMIT),
        name="prompt_attn",
    )(q, q, kb, vb, ga, diff_lambda, norm_w.reshape(1, DV_DIFF))


class _DecodeStep:
    def __init__(self, step, n_steps, spq, first_page, pages, pt_ref, in_refs, o_ref, scratch_refs, lam_init):
        self.step, self.n_steps, self.pages, self.first_page = step, n_steps, pages, first_page
        self.g, self.n_g, self.lam_init, self.pt_ref = lax.rem(step, spq), spq, lam_init, pt_ref
        (self.q_ref, self.kn_ref, self.vn_ref, self.ga_ref, self.dl_ref, self.nw_ref,
         self.ck_ref, self.cv_ref) = in_refs
        self.o_ref = o_ref
        (self.qblk_ref, self.m_ref, self.l_ref, self.acc_ref,
         self.kbuf_ref, self.vbuf_ref, self.sem_ref) = scratch_refs
        self.slot = lax.rem(step, 2)

    def _query(self):
        return self.q_ref[0].astype(F32) * DH_DIFF ** -0.5

    def _page_copies(self, step, slot):
        base = self.first_page + step * self.pages
        copies = []
        for i in range(self.pages):
            pid = self.pt_ref[base + i]
            copies.append(pltpu.make_async_copy(self.ck_ref.at[pid], self.kbuf_ref.at[slot, i],
                                                self.sem_ref.at[slot, 0]))
            copies.append(pltpu.make_async_copy(self.cv_ref.at[pid], self.vbuf_ref.at[slot, i],
                                                self.sem_ref.at[slot, 1]))
        return copies

    def init(self):
        @pl.when(self.step == 0)
        def _():
            for cp in self._page_copies(0, 0):
                cp.start()

        for cp in self._page_copies(self.step, self.slot):
            cp.wait()

        @pl.when(self.step + 1 < self.n_steps)
        def _():
            for cp in self._page_copies(self.step + 1, 1 - self.slot):
                cp.start()

        @pl.when(self.g == 0)
        def _():
            self.m_ref[...] = jnp.full(self.m_ref.shape, NEG_INF, F32)
            self.l_ref[...] = jnp.zeros(self.l_ref.shape, F32)
            self.acc_ref[...] = jnp.zeros(self.acc_ref.shape, F32)
            n_maps = 2 * H_DIFF
            rows = lax.broadcasted_iota(jnp.int32, (n_maps, W_DIFF), 0)
            lanes = lax.broadcasted_iota(jnp.int32, (n_maps, W_DIFF), 1)
            self.qblk_ref[...] = jnp.where((lanes >> MAP_BITS) == rows,
                                           jnp.broadcast_to(self._query(), (n_maps, W_DIFF)), 0.0)

    def main(self):
        k_pages = [self.kbuf_ref.at[self.slot, i] for i in range(self.pages)]
        v_pages = [self.vbuf_ref.at[self.slot, i] for i in range(self.pages)]
        _decode_pages(k_pages, v_pages, self.qblk_ref, self.m_ref, self.l_ref, self.acc_ref)

    def finalize(self):
        @pl.when(self.g == self.n_g - 1)
        def _():
            _decode_finish(self.qblk_ref[...].astype(BF16), self.kn_ref, self.vn_ref, self.ga_ref, self.dl_ref,
                           self.nw_ref, self.o_ref, self.m_ref, self.l_ref, self.acc_ref, self.lam_init)


def _decode_pages(kp_refs, vp_refs, qblk_ref, m_ref, l_ref, acc_ref):
    n_maps = 2 * H_DIFF
    page = kp_refs[0].shape[1]
    q_blk = qblk_ref[...].astype(BF16)
    s = jnp.concatenate(
        [jnp.dot(q_blk, kp[...].astype(BF16), preferred_element_type=F32) for kp in kp_refs], axis=1)
    m_prev = m_ref[:, 0:1]
    m_new = jnp.maximum(m_prev, jnp.max(s, axis=-1, keepdims=True))
    p = jnp.exp(s - m_new)
    alpha = jnp.exp(m_prev - m_new)
    l_new = alpha * l_ref[:, 0:1] + jnp.sum(p, axis=-1, keepdims=True)
    wide = page * H_DIFF
    p_rows = jnp.concatenate([p[:, i * page:(i + 1) * page] for i in range(len(vp_refs))], axis=0)
    tok = lax.broadcasted_iota(jnp.int32, (page, wide), 0)
    col = lax.broadcasted_iota(jnp.int32, (page, wide), 1)
    spread = jnp.where(col // H_DIFF == tok, 1.0, 0.0).astype(BF16)
    p_wide = jnp.dot(p_rows.astype(BF16), spread, preferred_element_type=F32)
    row8 = lax.broadcasted_iota(jnp.int32, (n_maps, wide), 0)
    col8 = lax.broadcasted_iota(jnp.int32, (n_maps, wide), 1)
    own_head = lax.rem(col8, H_DIFF) == row8 // 2
    parts = [jnp.zeros((n_maps, DV_DIFF), F32) for _ in range(min(4, len(vp_refs)))]
    for i, vp in enumerate(vp_refs):
        p_page = jnp.where(own_head, p_wide[i * n_maps:(i + 1) * n_maps], 0.0).astype(BF16)
        parts[i % len(parts)] += jnp.dot(p_page, vp[...].astype(BF16), preferred_element_type=F32)
    acc_ref[...] = alpha * acc_ref[...] + functools.reduce(lambda a, b: a + b, parts)
    m_ref[...] = jnp.broadcast_to(m_new, m_ref.shape)
    l_ref[...] = jnp.broadcast_to(l_new, l_ref.shape)


def _decode_finish(q_blk, kn_ref, vn_ref, ga_ref, dl_ref, nw_ref, o_ref, m_ref, l_ref, acc_ref, lam_init):
    k_self = jnp.broadcast_to(kn_ref[0].astype(BF16), (SUBLANES, W_DIFF))
    s_self = lax.dot_general(q_blk, k_self, _NT, preferred_element_type=F32)[:, 0:1]
    m_past = m_ref[:, 0:1]
    m_f = jnp.maximum(m_past, s_self)
    a_f = jnp.exp(m_past - m_f)
    p_self = jnp.exp(s_self - m_f)
    l_f = a_f * l_ref[:, 0:1] + p_self
    pv_self = p_self.astype(BF16).astype(F32) * vn_ref[0].astype(BF16).astype(F32)
    lam = _diff_lambda_value(dl_ref, lam_init)
    for h in range(H_DIFF):
        hs = slice(h * DV_DIFF, (h + 1) * DV_DIFF)
        a = (a_f * acc_ref[...] + pv_self[:, hs]) / l_f
        o = a[2 * h:2 * h + 1] - lam * a[2 * h + 1:2 * h + 2]
        o_ref[0, :, hs] = _head_norm_gate(o, nw_ref[...], ga_ref[0, :, hs], lam_init).astype(BF16)


class _DecodeRider:
    def __init__(self, q, k_new, v_new, ga, cache_k, cache_v, layer, page_table, diff_lambda, norm_w, lam_init,
                 seq0, n_seq, grid):
        depth, n_phys, page = cache_k.shape[:3]
        n_pages = page_table.shape[1]
        n_steps = grid[0] * grid[1]
        assert page == LANES
        pages = n_seq * n_pages // n_steps
        self.ok = pages >= 1 and pages * n_steps == n_seq * n_pages and n_pages % pages == 0
        if not self.ok:
            return
        spq = n_pages // pages
        self.pages, self.spq, self.lam_init, self.grid = pages, spq, lam_init, grid
        self.first_page = seq0 * n_pages
        ck = jnp.transpose(cache_k, (0, 1, 3, 4, 5, 2)).reshape(depth * n_phys, W_DIFF, page)
        cv = cache_v.reshape(depth * n_phys, page * H_DIFF, DV_DIFF)
        self.page_ids = page_table.reshape(-1) + layer * n_phys
        step = lambda i0, i1: i0 * grid[1] + i1
        tok = pl.BlockSpec((1, 1, W_DIFF), lambda i0, i1, pt: (seq0 + step(i0, i1) // spq, 0, 0))
        const = lambda shape: pl.BlockSpec(shape, lambda i0, i1, pt: (0, 0))
        hbm = pl.BlockSpec(memory_space=pl.ANY)
        self.inputs = [q, k_new, v_new, ga, diff_lambda, norm_w.reshape(1, DV_DIFF), ck, cv]
        self.in_specs = [tok, tok, tok, tok, const(diff_lambda.shape), const((1, DV_DIFF)), hbm, hbm]
        self.out_spec = pl.BlockSpec((1, 1, W_DIFF), lambda i0, i1, pt: (step(i0, i1) // spq, 0, 0))
        self.out_shape = jax.ShapeDtypeStruct((n_seq, 1, W_DIFF), BF16)
        n_maps = 2 * H_DIFF
        self.scratch_shapes = [pltpu.VMEM((n_maps, W_DIFF), F32),
                               pltpu.VMEM((n_maps, LANES), F32), pltpu.VMEM((n_maps, LANES), F32),
                               pltpu.VMEM((n_maps, DV_DIFF), F32),
                               pltpu.VMEM((2, pages, W_DIFF, page), F32),
                               pltpu.VMEM((2, pages, page * H_DIFF, DV_DIFF), F32),
                               pltpu.SemaphoreType.DMA((2, 2))]

    def step(self, pt_ref, in_refs, out_ref, scratch_refs):
        step = pl.program_id(0) * self.grid[1] + pl.program_id(1)
        return _DecodeStep(step, self.grid[0] * self.grid[1], self.spq, self.first_page, self.pages, pt_ref,
                           in_refs, out_ref, scratch_refs, self.lam_init)


def _ride(host_kernel, n_in, n_out, n_scratch, rider):
    n_rin = len(rider.inputs)

    def kernel(pt_ref, *refs):
        host_in, refs = refs[:n_in], refs[n_in:]
        rider_in, refs = refs[:n_rin], refs[n_rin:]
        host_out, refs = refs[:n_out], refs[n_out:]
        rider_out, refs = refs[0], refs[1:]
        host_scratch, rider_scratch = refs[:n_scratch], refs[n_scratch:]
        host_kernel(*host_in, *host_out, *host_scratch,
                    decode=rider.step(pt_ref, rider_in, rider_out, rider_scratch))

    return kernel


def _hosted_call(host_kernel, grid, in_specs, out_specs, out_shape, scratch_shapes, inputs, name, rider):
    params = pltpu.CompilerParams(dimension_semantics=("arbitrary", "arbitrary"), vmem_limit_bytes=VMEM_LIMIT)
    if rider is None:
        outs = pl.pallas_call(host_kernel, grid=grid, in_specs=in_specs, out_specs=out_specs, out_shape=out_shape,
                              scratch_shapes=scratch_shapes, compiler_params=params, name=name)(*inputs)
        return outs, None
    grid_spec = pltpu.PrefetchScalarGridSpec(
        num_scalar_prefetch=1, grid=grid,
        in_specs=list(in_specs) + rider.in_specs,
        out_specs=list(out_specs) + [rider.out_spec],
        scratch_shapes=list(scratch_shapes) + rider.scratch_shapes)
    outs = pl.pallas_call(
        _ride(host_kernel, len(in_specs), len(out_specs), len(scratch_shapes), rider),
        grid_spec=grid_spec, out_shape=list(out_shape) + [rider.out_shape],
        compiler_params=params, name=name)(rider.page_ids, *inputs, *rider.inputs)
    return outs[:-1], outs[-1]


def _decode_attn_kernel(pt_ref, *refs, rider):
    n_rin = len(rider.inputs)
    decode = rider.step(pt_ref, refs[:n_rin], refs[n_rin], refs[n_rin + 1:])
    decode.init()
    decode.main()
    decode.finalize()


def _decode_attn(rider_args, n_seq):
    n_pages = rider_args["page_table"].shape[1]
    grid = (n_seq, n_pages // math.gcd(DECODE_PAGES_PER_STEP, n_pages))
    rider = _DecodeRider(**rider_args, seq0=0, n_seq=n_seq, grid=grid)
    grid_spec = pltpu.PrefetchScalarGridSpec(
        num_scalar_prefetch=1, grid=grid, in_specs=rider.in_specs,
        out_specs=rider.out_spec, scratch_shapes=rider.scratch_shapes)
    return pl.pallas_call(
        functools.partial(_decode_attn_kernel, rider=rider), grid_spec=grid_spec, out_shape=rider.out_shape,
        compiler_params=pltpu.CompilerParams(
            dimension_semantics=("arbitrary", "arbitrary"), vmem_limit_bytes=VMEM_LIMIT),
        name="decode_attn")(rider.page_ids, *rider.inputs)


def _gate_rows(tail, alog_row, dtb_row):
    beta = _sigmoid(tail)
    g = -jnp.exp(alog_row) * _softplus(tail + dtb_row)
    return beta, g


def _lane_bcast(x, lane, rows):
    return jnp.broadcast_to(x[:, lane:lane + 1], (rows, LANES))


def _l2norm(x):
    return x * lax.rsqrt(jnp.sum(x * x, axis=-1, keepdims=True) + L2_EPS)


def _split_bf16(x):
    hi = x.astype(BF16)
    return hi, (x - hi.astype(F32)).astype(BF16)


def _dot_split(lhs, rhs):
    d = lambda a, b: jnp.dot(a, b, preferred_element_type=F32)
    return d(lhs[0], rhs[0]) + d(lhs[0], rhs[1]) + d(lhs[1], rhs[0])


def _unit_lower_inverses(a_mats):
    n = a_mats[0].shape[0]
    eye = (lax.broadcasted_iota(jnp.int32, (n, n), 0) == lax.broadcasted_iota(jnp.int32, (n, n), 1)).astype(F32)
    xs = [-a for a in a_mats]
    ps = [eye + x for x in xs]
    splits = [_split_bf16(x) for x in xs]
    xs = [_dot_split(s, s) for s in splits]
    power = 2
    while 2 * power < CHUNK:
        both = [_dot_split(_split_bf16(x), _split_bf16(jnp.concatenate([p, x], axis=1))) for p, x in zip(ps, xs)]
        ps = [p + b[:, :n] for p, b in zip(ps, both)]
        xs = [b[:, n:] for b in both]
        power *= 2
    return [p + _dot_split(_split_bf16(x), _split_bf16(p)) for p, x in zip(ps, xs)]


def _delta_prep_kernel(cin_ref, prev_ref, cw_ref, tail_ref, alog_ref, dtb_ref,
                       u_ref, w_ref, qg_ref, at_ref, kdt_ref, el_ref, ext_ref, *, pairs, decode=None):
    t = pl.program_id(1)
    pad = SUBLANES
    keep = CONV_W - 1
    rows = pairs * PAIR

    @pl.when(t == 0)
    def _():
        ext_ref[pad - keep:pad, :] = prev_ref[0]

    @pl.when(t > 0)
    def _():
        ext_ref[pad - keep:pad, :] = ext_ref[pad + rows - keep:pad + rows, :]

    if decode is not None:
        decode.init()

    ext_ref[pad:pad + rows, :] = cin_ref[0]

    r = lax.broadcasted_iota(jnp.int32, (PAIR, PAIR), 0)
    c = lax.broadcasted_iota(jnp.int32, (PAIR, PAIR), 1)
    same = (r >> CHUNK_BITS) == (c >> CHUNK_BITS)
    incl = same & (c <= r)
    strict = same & (c < r)
    incl_t = same & (r <= c)
    chunk_end = r == (c | (CHUNK - 1))

    chains = []
    for pi in range(pairs):
        base = pad - keep + pi * PAIR
        conv = ext_ref[base + keep:base + keep + PAIR, :] * cw_ref[keep:keep + 1, :]
        for j in reversed(range(keep)):
            conv = conv + ext_ref[base + j:base + j + PAIR, :] * cw_ref[j:j + 1, :]
        conv = _silu(conv)
        rs = slice(pi * PAIR, (pi + 1) * PAIR)
        beta_t, g_t = _gate_rows(tail_ref[0, :, rs], alog_ref[...], dtb_ref[...])
        gc_t = jnp.dot(g_t, incl_t.astype(F32), precision=HIGHEST, preferred_element_type=F32)
        g_last_t = jnp.dot(gc_t, chunk_end.astype(F32), precision=HIGHEST, preferred_element_type=F32)
        for h in range(H_DELTA):
            hs = slice(h * DK_DELTA, (h + 1) * DK_DELTA)
            qs = _l2norm(conv[:, hs]) * DK_DELTA ** -0.5
            kn = _l2norm(conv[:, W_DELTA + h * DK_DELTA:W_DELTA + (h + 1) * DK_DELTA])
            vh = conv[:, 2 * W_DELTA + h * DV_DELTA:2 * W_DELTA + (h + 1) * DV_DELTA]
            beta = jnp.broadcast_to(beta_t[h:h + 1, :], (PAIR, PAIR)).T
            gc_cols = jnp.broadcast_to(gc_t[H_DELTA + h:H_DELTA + h + 1, :], (PAIR, PAIR))
            g_last_cols = jnp.broadcast_to(g_last_t[H_DELTA + h:H_DELTA + h + 1, :], (PAIR, PAIR))
            chains.append(dict(pi=pi, h=h, rs=rs, qs=qs, kn=kn, vh=vh, beta=beta, gc=gc_cols.T, gc_cols=gc_cols,
                               g_last_cols=g_last_cols, kbeta=kn * beta))

    for ch in chains:
        ch["kn_t"] = ch["kn"].T
        kn_tb = ch["kn_t"].astype(BF16)
        ch["kk"] = jnp.dot(ch["kbeta"].astype(BF16), kn_tb, preferred_element_type=F32)
        ch["qk"] = jnp.dot(ch["qs"].astype(BF16), kn_tb, preferred_element_type=F32)
    for ch in chains:
        ch["decay"] = jnp.exp(jnp.where(incl, ch["gc"] - ch["gc_cols"], -jnp.inf))
    t_mats = _unit_lower_inverses([jnp.where(strict, ch["kk"] * ch["decay"], 0.0) for ch in chains])
    for ch, t_mat in zip(chains, t_mats):
        rhs = jnp.concatenate([ch["vh"] * ch["beta"], ch["kbeta"] * jnp.exp(ch["gc"])], axis=1).astype(BF16)
        ch["uw"] = jnp.dot(t_mat.astype(BF16), rhs, preferred_element_type=F32)
    for ch in chains:
        pi, h, rs, gc = ch["pi"], ch["h"], ch["rs"], ch["gc"]
        u_ref[0, h, rs, :] = ch["uw"][:, :DV_DELTA]
        w_ref[0, h, rs, :] = ch["uw"][:, DV_DELTA:].astype(BF16)
        attn = jnp.where(incl, ch["qk"] * ch["decay"], 0.0)
        at_ref[0, h, rs, :] = jnp.concatenate([attn[:CHUNK, :CHUNK], attn[CHUNK:, CHUNK:]], axis=0).astype(BF16)
        qg_ref[0, h, rs, :] = (ch["qs"] * jnp.exp(gc)).astype(BF16)
        kdt_ref[0, h, pi] = (ch["kn_t"] * jnp.exp(ch["g_last_cols"] - ch["gc_cols"])).astype(BF16)
        el_ref[0, h, pi] = jnp.exp(jnp.concatenate([jnp.broadcast_to(gc[CHUNK - 1:CHUNK], (SUBLANES, LANES)),
                                                    jnp.broadcast_to(gc[PAIR - 1:PAIR], (SUBLANES, LANES))], axis=0))

    if decode is not None:
        decode.main()
        decode.finalize()


def _delta_prep_grid(t):
    assert t % PAIR == 0
    npair = t // PAIR
    pairs = 2 if npair % 2 == 0 else 1
    return pairs, npair // pairs


def _delta_prep(cin, conv_prev, conv_w, tail, alog_row, dtb_row, rider):
    b, t, _ = cin.shape
    npair = t // PAIR
    pairs, nt = _delta_prep_grid(t)
    rows = pairs * PAIR
    per_head = lambda width: pl.BlockSpec((1, H_DELTA, rows, width), lambda bi, ti, *_: (bi, 0, ti, 0))
    const = lambda shape: pl.BlockSpec(shape, lambda bi, ti, *_: (0,) * len(shape))
    out_shape = [
        jax.ShapeDtypeStruct((b, H_DELTA, t, DV_DELTA), F32),
        jax.ShapeDtypeStruct((b, H_DELTA, t, DK_DELTA), BF16),
        jax.ShapeDtypeStruct((b, H_DELTA, t, DK_DELTA), BF16),
        jax.ShapeDtypeStruct((b, H_DELTA, t, CHUNK), BF16),
        jax.ShapeDtypeStruct((b, H_DELTA, npair, DK_DELTA, PAIR), BF16),
        jax.ShapeDtypeStruct((b, H_DELTA, npair, 2 * SUBLANES, LANES), F32),
    ]
    out_specs = [per_head(DV_DELTA), per_head(DK_DELTA), per_head(DK_DELTA), per_head(CHUNK),
                 pl.BlockSpec((1, H_DELTA, pairs, DK_DELTA, PAIR), lambda bi, ti, *_: (bi, 0, ti, 0, 0)),
                 pl.BlockSpec((1, H_DELTA, pairs, 2 * SUBLANES, LANES), lambda bi, ti, *_: (bi, 0, ti, 0, 0))]
    return _hosted_call(
        functools.partial(_delta_prep_kernel, pairs=pairs),
        grid=(b, nt),
        in_specs=[pl.BlockSpec((1, rows, CONV_CH), lambda bi, ti, *_: (bi, ti, 0)),
                  pl.BlockSpec((1, CONV_W - 1, CONV_CH), lambda bi, ti, *_: (bi, 0, 0)),
                  const(conv_w.shape),
                  pl.BlockSpec((1, 2 * H_DELTA, rows), lambda bi, ti, *_: (bi, 0, ti)),
                  const(alog_row.shape), const(dtb_row.shape)],
        out_specs=out_specs,
        out_shape=out_shape,
        scratch_shapes=[pltpu.VMEM((SUBLANES + rows, CONV_CH), F32)],
        inputs=(cin, conv_prev, conv_w, tail, alog_row, dtb_row),
        name="delta_prep", rider=rider)


def _delta_norm_gate(o, normw, z):
    ms = jnp.mean(o * o, axis=-1, keepdims=True)
    return o * lax.rsqrt(ms + DELTA_NORM_EPS) * normw * _silu(z)


def _delta_scan_kernel(u_ref, w_ref, qg_ref, at_ref, kdt_ref, el_ref, z_ref, s0_ref, nw_ref,
                       o_ref, sfin_ref, s_scr, *, nb, nblk):
    t = pl.program_id(1)

    @pl.when(t == 0)
    def _():
        s_scr[...] = s0_ref[...]

    def block(jb, carry):
        r0 = pl.multiple_of(jb * PAIR, PAIR)
        seqs = [(bi, h) for bi in range(nb) for h in range(H_DELTA)]
        for ci in range(2):
            rows = pl.ds(pl.multiple_of(r0 + ci * CHUNK, CHUNK), CHUNK)
            states = [s_scr[bi, h] for bi, h in seqs]
            res = [jnp.dot(jnp.concatenate([w_ref[bi, h, rows, :], qg_ref[bi, h, rows, :]], axis=0),
                           s.astype(BF16), preferred_element_type=F32) for (bi, h), s in zip(seqs, states)]
            v_new = [(u_ref[bi, h, rows, :] - r[:CHUNK]).astype(BF16) for (bi, h), r in zip(seqs, res)]
            upd = [jnp.dot(kdt_ref[bi, h, jb, :, ci * CHUNK:(ci + 1) * CHUNK], v, preferred_element_type=F32)
                   for (bi, h), v in zip(seqs, v_new)]
            intra = [jnp.dot(at_ref[bi, h, rows, :], v, preferred_element_type=F32) for (bi, h), v in zip(seqs, v_new)]
            for (bi, h), s, r, du, oi in zip(seqs, states, res, upd, intra):
                decay = jnp.broadcast_to(el_ref[bi, h, jb, ci * SUBLANES:ci * SUBLANES + 1, :], (DK_DELTA, DV_DELTA))
                s_scr[bi, h] = s * decay + du
                hs = slice(h * DV_DELTA, (h + 1) * DV_DELTA)
                o_ref[bi, rows, hs] = _delta_norm_gate(r[CHUNK:] + oi, nw_ref[...], z_ref[bi, rows, hs]).astype(BF16)
        return carry

    lax.fori_loop(0, nblk, block, 0)

    @pl.when(t == pl.num_programs(1) - 1)
    def _():
        sfin_ref[...] = s_scr[...]


def _delta_scan(prep, z, s0, norm_w):
    u, w, qg, at, kdt, el = prep
    b, _, t, _ = u.shape
    nb = math.gcd(b, 4)
    ts = min(512, t)
    assert t % ts == 0
    nblk = ts // PAIR
    per_head = lambda width: pl.BlockSpec((nb, H_DELTA, ts, width), lambda bi, ti: (bi, 0, ti, 0))
    state = pl.BlockSpec((nb, H_DELTA, DK_DELTA, DV_DELTA), lambda bi, ti: (bi, 0, 0, 0))
    tok = pl.BlockSpec((nb, ts, W_DELTA), lambda bi, ti: (bi, ti, 0))
    return pl.pallas_call(
        functools.partial(_delta_scan_kernel, nb=nb, nblk=nblk),
        grid=(b // nb, t // ts),
        in_specs=[per_head(DV_DELTA), per_head(DK_DELTA), per_head(DK_DELTA), per_head(CHUNK),
                  pl.BlockSpec((nb, H_DELTA, nblk, DK_DELTA, PAIR), lambda bi, ti: (bi, 0, ti, 0, 0)),
                  pl.BlockSpec((nb, H_DELTA, nblk, 2 * SUBLANES, LANES), lambda bi, ti: (bi, 0, ti, 0, 0)),
                  tok, state, pl.BlockSpec((1, DV_DELTA), lambda bi, ti: (0, 0))],
        out_specs=[tok, state],
        out_shape=[jax.ShapeDtypeStruct((b, t, W_DELTA), BF16),
                   jax.ShapeDtypeStruct((b, H_DELTA, DK_DELTA, DV_DELTA), F32)],
        scratch_shapes=[pltpu.VMEM((nb, H_DELTA, DK_DELTA, DV_DELTA), F32)],
        compiler_params=pltpu.CompilerParams(
            dimension_semantics=("arbitrary", "arbitrary"), vmem_limit_bytes=VMEM_LIMIT),
        name="delta_scan",
    )(u, w, qg, at, kdt, el, z, s0, norm_w.reshape(1, DV_DELTA))


def _delta_step_kernel(cin_ref, prev_ref, cw_ref, tail_ref, alog_ref, dtb_ref, z_ref, s0_ref, nw_ref,
                       o_ref, s_ref, *, seqs):
    items = []
    for b in range(seqs):
        prev = prev_ref[b]
        conv = prev[0:1] * cw_ref[0:1, :]
        for j in range(1, CONV_W - 1):
            conv = conv + prev[j:j + 1] * cw_ref[j:j + 1, :]
        conv = _silu(conv + cin_ref[b] * cw_ref[CONV_W - 1:CONV_W, :])
        beta_all, g_all = _gate_rows(tail_ref[b], alog_ref[...], dtb_ref[...])
        for h in range(H_DELTA):
            hs = slice(h * DK_DELTA, (h + 1) * DK_DELTA)
            qs = _l2norm(conv[:, hs]) * DK_DELTA ** -0.5
            kn = _l2norm(conv[:, W_DELTA + h * DK_DELTA:W_DELTA + (h + 1) * DK_DELTA])
            vh = conv[:, 2 * W_DELTA + h * DV_DELTA:2 * W_DELTA + (h + 1) * DV_DELTA]
            beta = beta_all[:, h:h + 1]
            eg = jnp.exp(g_all[:, H_DELTA + h:H_DELTA + h + 1])
            items.append((b, h, hs, qs, kn, vh, beta, eg, s0_ref[b, h]))
    res = [jnp.dot(jnp.concatenate([kn * (beta * eg), qs * eg, jnp.zeros((SUBLANES - 2, DK_DELTA), F32)],
                                   axis=0).astype(BF16),
                   s.astype(BF16), preferred_element_type=F32)
           for (b, h, hs, qs, kn, vh, beta, eg, s) in items]
    for (b, h, hs, qs, kn, vh, beta, eg, s), r in zip(items, res):
        v_new = vh * beta - r[0:1]
        qk = jnp.sum(qs.astype(BF16).astype(F32) * kn.astype(BF16).astype(F32), axis=-1, keepdims=True)
        o = r[1:2] + qk * v_new
        k_col = jnp.broadcast_to(kn, (DK_DELTA, DK_DELTA)).T
        s_ref[b, h] = s * eg + k_col * v_new
        o_ref[b, :, hs] = _delta_norm_gate(o, nw_ref[...], z_ref[b, :, hs]).astype(BF16)


def _delta_step(cin, conv_prev, conv_w, tail, alog_row, dtb_row, z, s0, norm_w):
    bs = cin.shape[0]
    seqs = math.gcd(bs, 4)
    tok = lambda width: pl.BlockSpec((seqs, 1, width), lambda b: (b, 0, 0))
    const = lambda shape: pl.BlockSpec(shape, lambda b: (0,) * len(shape))
    state = pl.BlockSpec((seqs, H_DELTA, DK_DELTA, DV_DELTA), lambda b: (b, 0, 0, 0))
    return pl.pallas_call(
        functools.partial(_delta_step_kernel, seqs=seqs),
        grid=(bs // seqs,),
        in_specs=[tok(CONV_CH), pl.BlockSpec((seqs, CONV_W - 1, CONV_CH), lambda b: (b, 0, 0)), const(conv_w.shape),
                  tok(LANES), const(alog_row.shape), const(dtb_row.shape), tok(W_DELTA), state,
                  const((1, DV_DELTA))],
        out_specs=[tok(W_DELTA), state],
        out_shape=[jax.ShapeDtypeStruct((bs, 1, W_DELTA), BF16),
                   jax.ShapeDtypeStruct((bs, H_DELTA, DK_DELTA, DV_DELTA), F32)],
        compiler_params=pltpu.CompilerParams(dimension_semantics=("arbitrary",)),
        name="delta_step",
    )(cin, conv_prev, conv_w, tail, alog_row, dtb_row, z, s0, norm_w.reshape(1, DV_DELTA))


def _out_proj_kernel(oa_ref, ob_ref, x_ref, w_ref, g_ref, b_ref, y_ref, *, alpha):
    mix = jnp.dot(oa_ref[...], w_ref[:W_DIFF, :], preferred_element_type=F32)
    mix = mix + jnp.dot(ob_ref[...], w_ref[W_DIFF:, :], preferred_element_type=F32)
    r = alpha * x_ref[...] + mix
    mu = jnp.mean(r, axis=-1, keepdims=True)
    var = jnp.mean(jnp.square(r - mu), axis=-1, keepdims=True)
    y_ref[...] = (r - mu) * lax.rsqrt(var + LN_EPS) * g_ref[...] + b_ref[...]


def _out_proj(oa, ob, x, w_out_b, ln_g, ln_b, alpha):
    m, d = x.shape
    tm = min(1024, m)
    assert m % tm == 0
    row = lambda width: pl.BlockSpec((tm, width), lambda i: (i, 0))
    const = lambda shape: pl.BlockSpec(shape, lambda i: (0, 0))
    return pl.pallas_call(
        functools.partial(_out_proj_kernel, alpha=alpha),
        grid=(m // tm,),
        in_specs=[row(W_DIFF), row(W_DELTA), row(d), const(w_out_b.shape), const((1, d)), const((1, d))],
        out_specs=row(d),
        out_shape=jax.ShapeDtypeStruct((m, d), F32),
        compiler_params=pltpu.CompilerParams(dimension_semantics=("arbitrary",), vmem_limit_bytes=VMEM_LIMIT),
        name="out_proj",
    )(oa, ob, x, w_out_b, ln_g.reshape(1, d), ln_b.reshape(1, d))


def _pad_lanes(vec, offset):
    return jnp.zeros((1, LANES), F32).at[0, offset:offset + vec.shape[0]].set(vec.astype(F32))


def _gate_rows_param(vec):
    col = jnp.broadcast_to(vec.astype(F32)[:, None], (H_DELTA, LANES))
    return jnp.concatenate([jnp.zeros((H_DELTA, LANES), F32), col], axis=0)


def kernel(x_prompt, x_sample, cache_k, cache_v, page_table, state_delta, state_conv, w_in, conv_w, a_log,
           dt_bias, delta_norm_w, diff_lambda, diff_norm_w, w_out, ln_g, ln_b):
    depth = w_in.shape[0]
    bp, tp, d = x_prompt.shape
    bs, ts, _ = x_sample.shape
    assert ts == 1 and w_in.shape[2] == P_MAIN + 2 * H_DELTA and d == w_out.shape[2]
    past_len = page_table.shape[1] * cache_k.shape[2]
    alpha = (2 * depth) ** 0.25
    tables_p = _rope_tables(jnp.arange(tp, dtype=jnp.int32))
    tables_s = _rope_tables(jnp.full((bs,), past_len, jnp.int32))

    hp, hs = x_prompt, x_sample
    outs = [[] for _ in range(8)]
    for l in range(depth):
        lam_init = _lambda_init(l)
        w_main = w_in[l].astype(BF16)
        w_tail = jnp.zeros((d, LANES), BF16).at[:, :2 * H_DELTA].set(w_in[l, :, P_MAIN:].astype(BF16))
        w_out_b = w_out[l].astype(BF16)
        alog_row = _pad_lanes(a_log[l], H_DELTA)
        dtb_row = _pad_lanes(dt_bias[l], H_DELTA)
        alog_col = _gate_rows_param(a_log[l])
        dtb_col = _gate_rows_param(dt_bias[l])

        sq, sk, _, sv, _, sga, scin, sz, stail = _in_proj(hs.reshape(1, bs, d), w_main, w_tail, tables_s, False)
        tok = lambda a: a.reshape(bs, 1, -1)
        decode_args = dict(q=tok(sq), k_new=tok(sk), v_new=tok(sv), ga=tok(sga), cache_k=cache_k, cache_v=cache_v,
                           layer=l, page_table=page_table, diff_lambda=diff_lambda[l], norm_w=diff_norm_w[l],
                           lam_init=lam_init)
        rider = _DecodeRider(**decode_args, seq0=0, n_seq=bs, grid=(bp, _delta_prep_grid(tp)[1]))
        if not rider.ok:
            rider = None

        q, k, kb, v, vb, ga, cin, z, _, tail_t = _in_proj(hp, w_main, w_tail, tables_p, True)
        oa = _prompt_attn(q, kb, vb, ga, diff_lambda[l], diff_norm_w[l], lam_init)
        prep, soa = _delta_prep(cin, jnp.zeros((bp, CONV_W - 1, CONV_CH), F32), conv_w[l], tail_t, alog_col,
                                dtb_col, rider)
        ob, sp = _delta_scan(prep, z, jnp.zeros((bp, H_DELTA, DK_DELTA, DV_DELTA), F32), delta_norm_w[l])
        hp = _out_proj(oa.reshape(bp * tp, W_DIFF), ob.reshape(bp * tp, W_DELTA), hp.reshape(bp * tp, d),
                       w_out_b, ln_g[l], ln_b[l], alpha).reshape(bp, tp, d)
        outs[0].append(k.reshape(bp, tp, H_DIFF, 2, DH_DIFF))
        outs[1].append(v.reshape(bp, tp, H_DIFF, DV_DIFF))
        outs[2].append(sp)
        outs[3].append(cin[:, tp - (CONV_W - 1):, :])

        if rider is None:
            soa = _decode_attn(decode_args, bs)
        sob, ss = _delta_step(tok(scin), state_conv[l], conv_w[l], tok(stail), alog_row, dtb_row, tok(sz),
                              state_delta[l], delta_norm_w[l])
        hs = _out_proj(soa.reshape(bs, W_DIFF), sob.reshape(bs, W_DELTA), hs.reshape(bs, d),
                       w_out_b, ln_g[l], ln_b[l], alpha).reshape(bs, 1, d)
        outs[4].append(sk.reshape(bs, 1, H_DIFF, 2, DH_DIFF))
        outs[5].append(sv.reshape(bs, 1, H_DIFF, DV_DIFF))
        outs[6].append(ss)
        outs[7].append(jnp.concatenate([state_conv[l][:, 1:, :], tok(scin)], axis=1))
    return (hp, hs) + tuple(jnp.stack(o) for o in outs)
```

```python
import functools
import math

import jax
import jax.numpy as jnp
from jax import lax
from jax.experimental import pallas as pl
from jax.experimental.pallas import tpu as pltpu

F32 = jnp.float32
BF16 = jnp.bfloat16
HIGHEST = lax.Precision.HIGHEST

H_DIFF = 4
DH_DIFF = 64
DV_DIFF = 2 * DH_DIFF
W_DIFF = H_DIFF * DV_DIFF
H_DELTA = 4
DK_DELTA = 128
DV_DELTA = 128
W_DELTA = H_DELTA * DK_DELTA
ROT_DIM = DH_DIFF // 4
ROPE_THETA = 500000.0
CONV_W = 4
CONV_CH = 3 * W_DELTA
CHUNK = 64
P_MAIN = 4 * W_DIFF + 4 * W_DELTA
LN_EPS = 1e-5
HEAD_NORM_EPS = 1e-5
DELTA_NORM_EPS = 1e-6
L2_EPS = 1e-6
NEG_INF = -1e30

LANES = 128
SUBLANES = 8
PAIR = 2 * CHUNK
CHUNK_BITS = CHUNK.bit_length() - 1
MAP_BITS = DH_DIFF.bit_length() - 1
VMEM_LIMIT = 56 * 1024 * 1024
DECODE_PAGES_PER_STEP = 16

_NT = (((1,), (1,)), ((), ()))


def _sigmoid(x):
    return 0.5 * jnp.tanh(0.5 * x) + 0.5


def _silu(x):
    return x * _sigmoid(x)


def _softplus(x):
    return jnp.maximum(x, 0.0) + jnp.log1p(jnp.exp(-jnp.abs(x)))


def _lambda_init(layer):
    return 0.8 - 0.6 * math.exp(-0.3 * layer)


def _rope_tables(pos):
    half = ROT_DIM // 2
    inv = ROPE_THETA ** (-jnp.arange(half, dtype=F32) / half)
    ang = pos.astype(F32)[:, None] * inv[None, :]
    cos, sin = jnp.cos(ang), jnp.sin(ang)
    t = pos.shape[0]
    rest = DH_DIFF - ROT_DIM
    cos_m = jnp.concatenate([cos, cos, jnp.ones((t, rest), F32)], axis=-1)
    sin_lo = jnp.concatenate([-sin, jnp.zeros((t, half + rest), F32)], axis=-1)
    sin_hi = jnp.concatenate([jnp.zeros((t, half), F32), sin, jnp.zeros((t, rest), F32)], axis=-1)
    reps = LANES // DH_DIFF
    return jnp.tile(cos_m, (1, reps)), jnp.tile(sin_lo, (1, reps)), jnp.tile(sin_hi, (1, reps))


def _in_proj_kernel(x_ref, w_ref, wt_ref, cos_ref, slo_ref, shi_ref,
                    q_ref, k_ref, kb_ref, v_ref, vb_ref, ga_ref, cin_ref, z_ref, tail_ref, *tail_t_ref):
    xb = x_ref[0].astype(BF16)
    half = ROT_DIM // 2
    tm = xb.shape[0]

    def proj(c0, width):
        return jnp.dot(xb, w_ref[:, c0:c0 + width], preferred_element_type=F32)

    def rope(h):
        blocks = []
        for c in range(W_DIFF // LANES):
            hb = h[:, c * LANES:(c + 1) * LANES]
            blocks.append(hb * cos_ref[...]
                          + pltpu.roll(hb, LANES - half, 1) * slo_ref[...]
                          + pltpu.roll(hb, half, 1) * shi_ref[...])
        return jnp.concatenate(blocks, axis=1)

    q_ref[0] = rope(proj(0, W_DIFF)).astype(BF16)
    k = rope(proj(W_DIFF, W_DIFF))
    k_ref[0] = k
    kb_ref[0] = k.astype(BF16)
    v = proj(2 * W_DIFF, W_DIFF)
    for h in range(H_DIFF):
        v_ref[0, pl.ds(h, tm, stride=H_DIFF), :] = v[:, h * DV_DIFF:(h + 1) * DV_DIFF]
    vb_ref[0] = v.astype(BF16)
    ga_ref[0] = proj(3 * W_DIFF, W_DIFF)
    for j in range(3):
        cin_ref[0, :, j * W_DELTA:(j + 1) * W_DELTA] = proj(4 * W_DIFF + j * W_DELTA, W_DELTA)
    z_ref[0] = proj(4 * W_DIFF + 3 * W_DELTA, W_DELTA)
    tail = jnp.dot(xb, wt_ref[...], preferred_element_type=F32)
    tail_ref[0] = tail
    if tail_t_ref:
        for c in range(tm // LANES):
            cs = slice(c * LANES, (c + 1) * LANES)
            tail_t_ref[0][0, :, cs] = tail[cs, :].T[:2 * H_DELTA, :]


def _in_proj(x, w_main, w_tail, tables, gates_time_major):
    b, t, d = x.shape
    tm = min(512, t)
    assert t % tm == 0 and (not gates_time_major or tm % LANES == 0)
    cos_t, slo_t, shi_t = tables
    row = lambda width: pl.BlockSpec((1, tm, width), lambda ti, bi: (bi, ti, 0))
    tab = pl.BlockSpec((tm, LANES), lambda ti, bi: (ti, 0))
    const = lambda shape: pl.BlockSpec(shape, lambda ti, bi: (0, 0))
    outs = [(W_DIFF, BF16), (W_DIFF, F32), (W_DIFF, BF16), None, (W_DIFF, BF16),
            (W_DIFF, F32), (CONV_CH, F32), (W_DELTA, F32), (LANES, F32)]
    out_specs = [row(o[0]) if o else pl.BlockSpec((1, tm * H_DIFF, DV_DIFF), lambda ti, bi: (bi, ti, 0))
                 for o in outs]
    out_shape = [jax.ShapeDtypeStruct((b, t, o[0]), o[1]) if o else
                 jax.ShapeDtypeStruct((b, t * H_DIFF, DV_DIFF), F32) for o in outs]
    if gates_time_major:
        out_specs.append(pl.BlockSpec((1, 2 * H_DELTA, tm), lambda ti, bi: (bi, 0, ti)))
        out_shape.append(jax.ShapeDtypeStruct((b, 2 * H_DELTA, t), F32))
    return pl.pallas_call(
        _in_proj_kernel,
        grid=(t // tm, b),
        in_specs=[row(d), const((d, P_MAIN)), const(w_tail.shape), tab, tab, tab],
        out_specs=out_specs,
        out_shape=out_shape,
        compiler_params=pltpu.CompilerParams(
            dimension_semantics=("arbitrary", "arbitrary"), vmem_limit_bytes=VMEM_LIMIT),
        name="in_proj",
    )(x, w_main, w_tail, cos_t, slo_t, shi_t)


def _diff_lambda_value(dl_ref, lam_init):
    dl = dl_ref[...]
    a = jnp.sum(dl[0:1] * dl[1:2], axis=1, keepdims=True)
    b = jnp.sum(dl[2:3] * dl[3:4], axis=1, keepdims=True)
    return jnp.exp(a) - jnp.exp(b) + lam_init


def _head_norm_gate(o, normw, gate, lam_init):
    ms = jnp.mean(o * o, axis=-1, keepdims=True)
    o = o * lax.rsqrt(ms + HEAD_NORM_EPS) * normw * (1.0 - lam_init)
    return o * _silu(gate)


def _prompt_attn_kernel(q0_ref, qn_ref, k_ref, v_ref, ga_ref, dl_ref, nw_ref, o_ref, s_scr, m_scr, l_scr, acc_scr,
                        *, blk, lam_init):
    i = pl.program_id(1)
    last = pl.num_programs(1) - 1
    slot = lax.rem(i, 2)
    lane = lax.broadcasted_iota(jnp.int32, (blk, DV_DIFF), 1)
    heads = [slice(h * DV_DIFF, (h + 1) * DV_DIFF) for h in range(H_DIFF)]
    lane_chunks = [slice(c * LANES, (c + 1) * LANES) for c in range(blk // LANES)]
    n_maps = 2 * H_DIFF

    def query_maps(q_ref):
        maps = []
        for hs in heads:
            qs = q_ref[0, :, hs].astype(F32) * (DH_DIFF ** -0.5 * math.log2(math.e))
            maps.append(jnp.concatenate([jnp.where(lane < DH_DIFF, qs, 0.0),
                                         jnp.where(lane >= DH_DIFF, qs, 0.0)], axis=0).astype(BF16))
        return maps

    def fold(x, op):
        r = x[:, lane_chunks[0]]
        for c in lane_chunks[1:]:
            r = op(r, x[:, c])
        return r

    def score_block(q_maps, dst, j, masked):
        rows = pl.ds(pl.multiple_of(j * blk, blk), blk)
        for h, hs in enumerate(heads):
            s_pair = lax.dot_general(q_maps[h], k_ref[0, rows, hs], _NT, preferred_element_type=F32)
            for half, mi in enumerate((2 * h, 2 * h + 1)):
                s = s_pair[half * blk:(half + 1) * blk]
                if masked:
                    row = lax.broadcasted_iota(jnp.int32, (blk, blk), 0)
                    col = lax.broadcasted_iota(jnp.int32, (blk, blk), 1)
                    s = jnp.where(col <= row, s, NEG_INF)
                s_scr[dst, j, mi] = s
                m_scr[dst, mi] = jnp.maximum(m_scr[dst, mi], fold(s, jnp.maximum))

    def finish_max(dst):
        for mi in range(n_maps):
            m_scr[dst, mi] = jnp.broadcast_to(jnp.max(m_scr[dst, mi], axis=-1, keepdims=True), (blk, LANES))

    def prob_block(j):
        rows = pl.ds(pl.multiple_of(j * blk, blk), blk)
        for h, hs in enumerate(heads):
            pair = []
            for mi in (2 * h, 2 * h + 1):
                m = m_scr[slot, mi]
                p = jnp.concatenate([jnp.exp2(s_scr[slot, j, mi, :, c] - m) for c in lane_chunks], axis=1)
                l_scr[mi] = l_scr[mi] + fold(p, jnp.add)
                pair.append(p.astype(BF16))
            pv = jnp.dot(jnp.concatenate(pair, axis=0), v_ref[0, rows, hs], preferred_element_type=F32)
            acc_scr[2 * h] = acc_scr[2 * h] + pv[:blk]
            acc_scr[2 * h + 1] = acc_scr[2 * h + 1] + pv[blk:]

    def write_output():
        lam = _diff_lambda_value(dl_ref, lam_init)
        for h, hs in enumerate(heads):
            l1 = jnp.sum(l_scr[2 * h], axis=-1, keepdims=True)
            l2 = jnp.sum(l_scr[2 * h + 1], axis=-1, keepdims=True)
            o = acc_scr[2 * h] / l1 - lam * (acc_scr[2 * h + 1] / l2)
            o_ref[0, :, hs] = _head_norm_gate(o, nw_ref[...], ga_ref[0, :, hs], lam_init).astype(BF16)

    @pl.when(i == 0)
    def _():
        m_scr[0] = jnp.full(m_scr.shape[1:], NEG_INF, F32)
        score_block(query_maps(q0_ref), 0, 0, True)
        finish_max(0)

    l_scr[...] = jnp.zeros(l_scr.shape, F32)
    acc_scr[...] = jnp.zeros(acc_scr.shape, F32)

    @pl.when(i < last)
    def _():
        nxt = 1 - slot
        q_next = query_maps(qn_ref)
        m_scr[nxt] = jnp.full(m_scr.shape[1:], NEG_INF, F32)

        def both(j, carry):
            prob_block(j)
            score_block(q_next, nxt, j, False)
            return carry

        lax.fori_loop(0, i + 1, both, 0)
        write_output()
        score_block(q_next, nxt, i + 1, True)
        finish_max(nxt)

    @pl.when(i == last)
    def _():
        def only_probs(j, carry):
            prob_block(j)
            return carry

        lax.fori_loop(0, i + 1, only_probs, 0)
        write_output()


def _prompt_attn(q, kb, vb, ga, diff_lambda, norm_w, lam_init):
    b, t, _ = q.shape
    blk = min(256, t)
    assert t % blk == 0
    n_maps = 2 * H_DIFF
    nq = t // blk
    qspec = pl.BlockSpec((1, blk, W_DIFF), lambda bi, i: (bi, i, 0))
    q_first = pl.BlockSpec((1, blk, W_DIFF), lambda bi, i: (bi, 0, 0))
    q_next = pl.BlockSpec((1, blk, W_DIFF), lambda bi, i: (bi, jnp.minimum(i + 1, nq - 1), 0))
    kvspec = pl.BlockSpec((1, t, W_DIFF), lambda bi, i: (bi, 0, 0))
    return pl.pallas_call(
        functools.partial(_prompt_attn_kernel, blk=blk, lam_init=lam_init),
        grid=(b, nq),
        in_specs=[q_first, q_next, kvspec, kvspec, qspec,
                  pl.BlockSpec(diff_lambda.shape, lambda bi, i: (0, 0)),
                  pl.BlockSpec((1, DV_DIFF), lambda bi, i: (0, 0))],
        out_specs=qspec,
        out_shape=jax.ShapeDtypeStruct((b, t, W_DIFF), BF16),
        scratch_shapes=[pltpu.VMEM((2, nq, n_maps, blk, blk), F32),
                        pltpu.VMEM((2, n_maps, blk, LANES), F32),
                        pltpu.VMEM((n_maps, blk, LANES), F32),
                        pltpu.VMEM((n_maps, blk, DV_DIFF), F32)],
        compiler_params=pltpu.CompilerParams(
            dimension_semantics=("arbitrary", "arbitrary"), vmem_limit_bytes=VMEM_LIMIT),
        name="prompt_attn",
    )(q, q, kb, vb, ga, diff_lambda, norm_w.reshape(1, DV_DIFF))


class _DecodeStep:
    def __init__(self, step, n_steps, spq, first_page, pages, pt_ref, in_refs, o_ref, scratch_refs, lam_init):
        self.step, self.n_steps, self.pages, self.first_page = step, n_steps, pages, first_page
        self.g, self.n_g, self.lam_init, self.pt_ref = lax.rem(step, spq), spq, lam_init, pt_ref
        (self.q_ref, self.kn_ref, self.vn_ref, self.ga_ref, self.dl_ref, self.nw_ref,
         self.ck_ref, self.cv_ref) = in_refs
        self.o_ref = o_ref
        (self.qblk_ref, self.m_ref, self.l_ref, self.acc_ref,
         self.kbuf_ref, self.vbuf_ref, self.sem_ref) = scratch_refs
        self.slot = lax.rem(step, 2)

    def _query(self):
        return self.q_ref[0].astype(F32) * DH_DIFF ** -0.5

    def _page_copies(self, step, slot):
        base = self.first_page + step * self.pages
        copies = []
        for i in range(self.pages):
            pid = self.pt_ref[base + i]
            copies.append(pltpu.make_async_copy(self.ck_ref.at[pid], self.kbuf_ref.at[slot, i],
                                                self.sem_ref.at[slot, 0]))
            copies.append(pltpu.make_async_copy(self.cv_ref.at[pid], self.vbuf_ref.at[slot, i],
                                                self.sem_ref.at[slot, 1]))
        return copies

    def init(self):
        @pl.when(self.step == 0)
        def _():
            for cp in self._page_copies(0, 0):
                cp.start()

        for cp in self._page_copies(self.step, self.slot):
            cp.wait()

        @pl.when(self.step + 1 < self.n_steps)
        def _():
            for cp in self._page_copies(self.step + 1, 1 - self.slot):
                cp.start()

        @pl.when(self.g == 0)
        def _():
            self.m_ref[...] = jnp.full(self.m_ref.shape, NEG_INF, F32)
            self.l_ref[...] = jnp.zeros(self.l_ref.shape, F32)
            self.acc_ref[...] = jnp.zeros(self.acc_ref.shape, F32)
            n_maps = 2 * H_DIFF
            rows = lax.broadcasted_iota(jnp.int32, (n_maps, W_DIFF), 0)
            lanes = lax.broadcasted_iota(jnp.int32, (n_maps, W_DIFF), 1)
            self.qblk_ref[...] = jnp.where((lanes >> MAP_BITS) == rows,
                                           jnp.broadcast_to(self._query(), (n_maps, W_DIFF)), 0.0)

    def main(self):
        k_pages = [self.kbuf_ref.at[self.slot, i] for i in range(self.pages)]
        v_pages = [self.vbuf_ref.at[self.slot, i] for i in range(self.pages)]
        _decode_pages(k_pages, v_pages, self.qblk_ref, self.m_ref, self.l_ref, self.acc_ref)

    def finalize(self):
        @pl.when(self.g == self.n_g - 1)
        def _():
            _decode_finish(self.qblk_ref[...].astype(BF16), self.kn_ref, self.vn_ref, self.ga_ref, self.dl_ref,
                           self.nw_ref, self.o_ref, self.m_ref, self.l_ref, self.acc_ref, self.lam_init)


def _decode_pages(kp_refs, vp_refs, qblk_ref, m_ref, l_ref, acc_ref):
    n_maps = 2 * H_DIFF
    page = kp_refs[0].shape[1]
    q_blk = qblk_ref[...].astype(BF16)
    s = jnp.concatenate(
        [jnp.dot(q_blk, kp[...].astype(BF16), preferred_element_type=F32) for kp in kp_refs], axis=1)
    m_prev = m_ref[:, 0:1]
    m_new = jnp.maximum(m_prev, jnp.max(s, axis=-1, keepdims=True))
    p = jnp.exp(s - m_new)
    alpha = jnp.exp(m_prev - m_new)
    l_new = alpha * l_ref[:, 0:1] + jnp.sum(p, axis=-1, keepdims=True)
    wide = page * H_DIFF
    p_rows = jnp.concatenate([p[:, i * page:(i + 1) * page] for i in range(len(vp_refs))], axis=0)
    tok = lax.broadcasted_iota(jnp.int32, (page, wide), 0)
    col = lax.broadcasted_iota(jnp.int32, (page, wide), 1)
    spread = jnp.where(col // H_DIFF == tok, 1.0, 0.0).astype(BF16)
    p_wide = jnp.dot(p_rows.astype(BF16), spread, preferred_element_type=F32)
    row8 = lax.broadcasted_iota(jnp.int32, (n_maps, wide), 0)
    col8 = lax.broadcasted_iota(jnp.int32, (n_maps, wide), 1)
    own_head = lax.rem(col8, H_DIFF) == row8 // 2
    parts = [jnp.zeros((n_maps, DV_DIFF), F32) for _ in range(min(4, len(vp_refs)))]
    for i, vp in enumerate(vp_refs):
        p_page = jnp.where(own_head, p_wide[i * n_maps:(i + 1) * n_maps], 0.0).astype(BF16)
        parts[i % len(parts)] += jnp.dot(p_page, vp[...].astype(BF16), preferred_element_type=F32)
    acc_ref[...] = alpha * acc_ref[...] + functools.reduce(lambda a, b: a + b, parts)
    m_ref[...] = jnp.broadcast_to(m_new, m_ref.shape)
    l_ref[...] = jnp.broadcast_to(l_new, l_ref.shape)


def _decode_finish(q_blk, kn_ref, vn_ref, ga_ref, dl_ref, nw_ref, o_ref, m_ref, l_ref, acc_ref, lam_init):
    k_self = jnp.broadcast_to(kn_ref[0].astype(BF16), (SUBLANES, W_DIFF))
    s_self = lax.dot_general(q_blk, k_self, _NT, preferred_element_type=F32)[:, 0:1]
    m_past = m_ref[:, 0:1]
    m_f = jnp.maximum(m_past, s_self)
    a_f = jnp.exp(m_past - m_f)
    p_self = jnp.exp(s_self - m_f)
    l_f = a_f * l_ref[:, 0:1] + p_self
    pv_self = p_self.astype(BF16).astype(F32) * vn_ref[0].astype(BF16).astype(F32)
    lam = _diff_lambda_value(dl_ref, lam_init)
    for h in range(H_DIFF):
        hs = slice(h * DV_DIFF, (h + 1) * DV_DIFF)
        a = (a_f * acc_ref[...] + pv_self[:, hs]) / l_f
        o = a[2 * h:2 * h + 1] - lam * a[2 * h + 1:2 * h + 2]
        o_ref[0, :, hs] = _head_norm_gate(o, nw_ref[...], ga_ref[0, :, hs], lam_init).astype(BF16)


class _DecodeRider:
    def __init__(self, q, k_new, v_new, ga, cache_k, cache_v, layer, page_table, diff_lambda, norm_w, lam_init,
                 seq0, n_seq, grid):
        depth, n_phys, page = cache_k.shape[:3]
        n_pages = page_table.shape[1]
        n_steps = grid[0] * grid[1]
        assert page == LANES
        pages = n_seq * n_pages // n_steps
        self.ok = pages >= 1 and pages * n_steps == n_seq * n_pages and n_pages % pages == 0
        if not self.ok:
            return
        spq = n_pages // pages
        self.pages, self.spq, self.lam_init, self.grid = pages, spq, lam_init, grid
        self.first_page = seq0 * n_pages
        ck = jnp.transpose(cache_k, (0, 1, 3, 4, 5, 2)).reshape(depth * n_phys, W_DIFF, page)
        cv = cache_v.reshape(depth * n_phys, page * H_DIFF, DV_DIFF)
        self.page_ids = page_table.reshape(-1) + layer * n_phys
        step = lambda i0, i1: i0 * grid[1] + i1
        tok = pl.BlockSpec((1, 1, W_DIFF), lambda i0, i1, pt: (seq0 + step(i0, i1) // spq, 0, 0))
        const = lambda shape: pl.BlockSpec(shape, lambda i0, i1, pt: (0, 0))
        hbm = pl.BlockSpec(memory_space=pl.ANY)
        self.inputs = [q, k_new, v_new, ga, diff_lambda, norm_w.reshape(1, DV_DIFF), ck, cv]
        self.in_specs = [tok, tok, tok, tok, const(diff_lambda.shape), const((1, DV_DIFF)), hbm, hbm]
        self.out_spec = pl.BlockSpec((1, 1, W_DIFF), lambda i0, i1, pt: (step(i0, i1) // spq, 0, 0))
        self.out_shape = jax.ShapeDtypeStruct((n_seq, 1, W_DIFF), BF16)
        n_maps = 2 * H_DIFF
        self.scratch_shapes = [pltpu.VMEM((n_maps, W_DIFF), F32),
                               pltpu.VMEM((n_maps, LANES), F32), pltpu.VMEM((n_maps, LANES), F32),
                               pltpu.VMEM((n_maps, DV_DIFF), F32),
                               pltpu.VMEM((2, pages, W_DIFF, page), F32),
                               pltpu.VMEM((2, pages, page * H_DIFF, DV_DIFF), F32),
                               pltpu.SemaphoreType.DMA((2, 2))]

    def step(self, pt_ref, in_refs, out_ref, scratch_refs):
        step = pl.program_id(0) * self.grid[1] + pl.program_id(1)
        return _DecodeStep(step, self.grid[0] * self.grid[1], self.spq, self.first_page, self.pages, pt_ref,
                           in_refs, out_ref, scratch_refs, self.lam_init)


def _ride(host_kernel, n_in, n_out, n_scratch, rider):
    n_rin = len(rider.inputs)

    def kernel(pt_ref, *refs):
        host_in, refs = refs[:n_in], refs[n_in:]
        rider_in, refs = refs[:n_rin], refs[n_rin:]
        host_out, refs = refs[:n_out], refs[n_out:]
        rider_out, refs = refs[0], refs[1:]
        host_scratch, rider_scratch = refs[:n_scratch], refs[n_scratch:]
        host_kernel(*host_in, *host_out, *host_scratch,
                    decode=rider.step(pt_ref, rider_in, rider_out, rider_scratch))

    return kernel


def _hosted_call(host_kernel, grid, in_specs, out_specs, out_shape, scratch_shapes, inputs, name, rider):
    params = pltpu.CompilerParams(dimension_semantics=("arbitrary", "arbitrary"), vmem_limit_bytes=VMEM_LIMIT)
    if rider is None:
        outs = pl.pallas_call(host_kernel, grid=grid, in_specs=in_specs, out_specs=out_specs, out_shape=out_shape,
                              scratch_shapes=scratch_shapes, compiler_params=params, name=name)(*inputs)
        return outs, None
    grid_spec = pltpu.PrefetchScalarGridSpec(
        num_scalar_prefetch=1, grid=grid,
        in_specs=list(in_specs) + rider.in_specs,
        out_specs=list(out_specs) + [rider.out_spec],
        scratch_shapes=list(scratch_shapes) + rider.scratch_shapes)
    outs = pl.pallas_call(
        _ride(host_kernel, len(in_specs), len(out_specs), len(scratch_shapes), rider),
        grid_spec=grid_spec, out_shape=list(out_shape) + [rider.out_shape],
        compiler_params=params, name=name)(rider.page_ids, *inputs, *rider.inputs)
    return outs[:-1], outs[-1]


def _decode_attn_kernel(pt_ref, *refs, rider):
    n_rin = len(rider.inputs)
    decode = rider.step(pt_ref, refs[:n_rin], refs[n_rin], refs[n_rin + 1:])
    decode.init()
    decode.main()
    decode.finalize()


def _decode_attn(rider_args, n_seq):
    n_pages = rider_args["page_table"].shape[1]
    grid = (n_seq, n_pages // math.gcd(DECODE_PAGES_PER_STEP, n_pages))
    rider = _DecodeRider(**rider_args, seq0=0, n_seq=n_seq, grid=grid)
    grid_spec = pltpu.PrefetchScalarGridSpec(
        num_scalar_prefetch=1, grid=grid, in_specs=rider.in_specs,
        out_specs=rider.out_spec, scratch_shapes=rider.scratch_shapes)
    return pl.pallas_call(
        functools.partial(_decode_attn_kernel, rider=rider), grid_spec=grid_spec, out_shape=rider.out_shape,
        compiler_params=pltpu.CompilerParams(
            dimension_semantics=("arbitrary", "arbitrary"), vmem_limit_bytes=VMEM_LIMIT),
        name="decode_attn")(rider.page_ids, *rider.inputs)


def _gate_rows(tail, alog_row, dtb_row):
    beta = _sigmoid(tail)
    g = -jnp.exp(alog_row) * _softplus(tail + dtb_row)
    return beta, g


def _lane_bcast(x, lane, rows):
    return jnp.broadcast_to(x[:, lane:lane + 1], (rows, LANES))


def _l2norm(x):
    return x * lax.rsqrt(jnp.sum(x * x, axis=-1, keepdims=True) + L2_EPS)


def _split_bf16(x):
    hi = x.astype(BF16)
    return hi, (x - hi.astype(F32)).astype(BF16)


def _dot_split(lhs, rhs):
    d = lambda a, b: jnp.dot(a, b, preferred_element_type=F32)
    return d(lhs[0], rhs[0]) + d(lhs[0], rhs[1]) + d(lhs[1], rhs[0])


def _unit_lower_inverses(a_mats):
    n = a_mats[0].shape[0]
    eye = (lax.broadcasted_iota(jnp.int32, (n, n), 0) == lax.broadcasted_iota(jnp.int32, (n, n), 1)).astype(F32)
    xs = [-a for a in a_mats]
    ps = [eye + x for x in xs]
    splits = [_split_bf16(x) for x in xs]
    xs = [_dot_split(s, s) for s in splits]
    power = 2
    while 2 * power < CHUNK:
        both = [_dot_split(_split_bf16(x), _split_bf16(jnp.concatenate([p, x], axis=1))) for p, x in zip(ps, xs)]
        ps = [p + b[:, :n] for p, b in zip(ps, both)]
        xs = [b[:, n:] for b in both]
        power *= 2
    return [p + _dot_split(_split_bf16(x), _split_bf16(p)) for p, x in zip(ps, xs)]


def _delta_prep_kernel(cin_ref, prev_ref, cw_ref, tail_ref, alog_ref, dtb_ref,
                       u_ref, w_ref, qg_ref, at_ref, kdt_ref, el_ref, ext_ref, *, pairs, decode=None):
    t = pl.program_id(1)
    pad = SUBLANES
    keep = CONV_W - 1
    rows = pairs * PAIR

    @pl.when(t == 0)
    def _():
        ext_ref[pad - keep:pad, :] = prev_ref[0]

    @pl.when(t > 0)
    def _():
        ext_ref[pad - keep:pad, :] = ext_ref[pad + rows - keep:pad + rows, :]

    if decode is not None:
        decode.init()

    ext_ref[pad:pad + rows, :] = cin_ref[0]

    r = lax.broadcasted_iota(jnp.int32, (PAIR, PAIR), 0)
    c = lax.broadcasted_iota(jnp.int32, (PAIR, PAIR), 1)
    same = (r >> CHUNK_BITS) == (c >> CHUNK_BITS)
    incl = same & (c <= r)
    strict = same & (c < r)
    incl_t = same & (r <= c)
    chunk_end = r == (c | (CHUNK - 1))

    chains = []
    for pi in range(pairs):
        base = pad - keep + pi * PAIR
        conv = ext_ref[base + keep:base + keep + PAIR, :] * cw_ref[keep:keep + 1, :]
        for j in reversed(range(keep)):
            conv = conv + ext_ref[base + j:base + j + PAIR, :] * cw_ref[j:j + 1, :]
        conv = _silu(conv)
        rs = slice(pi * PAIR, (pi + 1) * PAIR)
        beta_t, g_t = _gate_rows(tail_ref[0, :, rs], alog_ref[...], dtb_ref[...])
        gc_t = jnp.dot(g_t, incl_t.astype(F32), precision=HIGHEST, preferred_element_type=F32)
        g_last_t = jnp.dot(gc_t, chunk_end.astype(F32), precision=HIGHEST, preferred_element_type=F32)
        for h in range(H_DELTA):
            hs = slice(h * DK_DELTA, (h + 1) * DK_DELTA)
            qs = _l2norm(conv[:, hs]) * DK_DELTA ** -0.5
            kn = _l2norm(conv[:, W_DELTA + h * DK_DELTA:W_DELTA + (h + 1) * DK_DELTA])
            vh = conv[:, 2 * W_DELTA + h * DV_DELTA:2 * W_DELTA + (h + 1) * DV_DELTA]
            beta = jnp.broadcast_to(beta_t[h:h + 1, :], (PAIR, PAIR)).T
            gc_cols = jnp.broadcast_to(gc_t[H_DELTA + h:H_DELTA + h + 1, :], (PAIR, PAIR))
            g_last_cols = jnp.broadcast_to(g_last_t[H_DELTA + h:H_DELTA + h + 1, :], (PAIR, PAIR))
            chains.append(dict(pi=pi, h=h, rs=rs, qs=qs, kn=kn, vh=vh, beta=beta, gc=gc_cols.T, gc_cols=gc_cols,
                               g_last_cols=g_last_cols, kbeta=kn * beta))

    for ch in chains:
        ch["kn_t"] = ch["kn"].T
        kn_tb = ch["kn_t"].astype(BF16)
        ch["kk"] = jnp.dot(ch["kbeta"].astype(BF16), kn_tb, preferred_element_type=F32)
        ch["qk"] = jnp.dot(ch["qs"].astype(BF16), kn_tb, preferred_element_type=F32)
    for ch in chains:
        ch["decay"] = jnp.exp(jnp.where(incl, ch["gc"] - ch["gc_cols"], -jnp.inf))
    t_mats = _unit_lower_inverses([jnp.where(strict, ch["kk"] * ch["decay"], 0.0) for ch in chains])
    for ch, t_mat in zip(chains, t_mats):
        rhs = jnp.concatenate([ch["vh"] * ch["beta"], ch["kbeta"] * jnp.exp(ch["gc"])], axis=1).astype(BF16)
        ch["uw"] = jnp.dot(t_mat.astype(BF16), rhs, preferred_element_type=F32)
    for ch in chains:
        pi, h, rs, gc = ch["pi"], ch["h"], ch["rs"], ch["gc"]
        u_ref[0, h, rs, :] = ch["uw"][:, :DV_DELTA]
        w_ref[0, h, rs, :] = ch["uw"][:, DV_DELTA:].astype(BF16)
        attn = jnp.where(incl, ch["qk"] * ch["decay"], 0.0)
        at_ref[0, h, rs, :] = jnp.concatenate([attn[:CHUNK, :CHUNK], attn[CHUNK:, CHUNK:]], axis=0).astype(BF16)
        qg_ref[0, h, rs, :] = (ch["qs"] * jnp.exp(gc)).astype(BF16)
        kdt_ref[0, h, pi] = (ch["kn_t"] * jnp.exp(ch["g_last_cols"] - ch["gc_cols"])).astype(BF16)
        el_ref[0, h, pi] = jnp.exp(jnp.concatenate([jnp.broadcast_to(gc[CHUNK - 1:CHUNK], (SUBLANES, LANES)),
                                                    jnp.broadcast_to(gc[PAIR - 1:PAIR], (SUBLANES, LANES))], axis=0))

    if decode is not None:
        decode.main()
        decode.finalize()


def _delta_prep_grid(t):
    assert t % PAIR == 0
    npair = t // PAIR
    pairs = 2 if npair % 2 == 0 else 1
    return pairs, npair // pairs


def _delta_prep(cin, conv_prev, conv_w, tail, alog_row, dtb_row, rider):
    b, t, _ = cin.shape
    npair = t // PAIR
    pairs, nt = _delta_prep_grid(t)
    rows = pairs * PAIR
    per_head = lambda width: pl.BlockSpec((1, H_DELTA, rows, width), lambda bi, ti, *_: (bi, 0, ti, 0))
    const = lambda shape: pl.BlockSpec(shape, lambda bi, ti, *_: (0,) * len(shape))
    out_shape = [
        jax.ShapeDtypeStruct((b, H_DELTA, t, DV_DELTA), F32),
        jax.ShapeDtypeStruct((b, H_DELTA, t, DK_DELTA), BF16),
        jax.ShapeDtypeStruct((b, H_DELTA, t, DK_DELTA), BF16),
        jax.ShapeDtypeStruct((b, H_DELTA, t, CHUNK), BF16),
        jax.ShapeDtypeStruct((b, H_DELTA, npair, DK_DELTA, PAIR), BF16),
        jax.ShapeDtypeStruct((b, H_DELTA, npair, 2 * SUBLANES, LANES), F32),
    ]
    out_specs = [per_head(DV_DELTA), per_head(DK_DELTA), per_head(DK_DELTA), per_head(CHUNK),
                 pl.BlockSpec((1, H_DELTA, pairs, DK_DELTA, PAIR), lambda bi, ti, *_: (bi, 0, ti, 0, 0)),
                 pl.BlockSpec((1, H_DELTA, pairs, 2 * SUBLANES, LANES), lambda bi, ti, *_: (bi, 0, ti, 0, 0))]
    return _hosted_call(
        functools.partial(_delta_prep_kernel, pairs=pairs),
        grid=(b, nt),
        in_specs=[pl.BlockSpec((1, rows, CONV_CH), lambda bi, ti, *_: (bi, ti, 0)),
                  pl.BlockSpec((1, CONV_W - 1, CONV_CH), lambda bi, ti, *_: (bi, 0, 0)),
                  const(conv_w.shape),
                  pl.BlockSpec((1, 2 * H_DELTA, rows), lambda bi, ti, *_: (bi, 0, ti)),
                  const(alog_row.shape), const(dtb_row.shape)],
        out_specs=out_specs,
        out_shape=out_shape,
        scratch_shapes=[pltpu.VMEM((SUBLANES + rows, CONV_CH), F32)],
        inputs=(cin, conv_prev, conv_w, tail, alog_row, dtb_row),
        name="delta_prep", rider=rider)


def _delta_norm_gate(o, normw, z):
    ms = jnp.mean(o * o, axis=-1, keepdims=True)
    return o * lax.rsqrt(ms + DELTA_NORM_EPS) * normw * _silu(z)


def _delta_scan_kernel(u_ref, w_ref, qg_ref, at_ref, kdt_ref, el_ref, z_ref, s0_ref, nw_ref,
                       o_ref, sfin_ref, s_scr, *, nb, nblk):
    t = pl.program_id(1)

    @pl.when(t == 0)
    def _():
        s_scr[...] = s0_ref[...]

    def block(jb, carry):
        r0 = pl.multiple_of(jb * PAIR, PAIR)
        seqs = [(bi, h) for bi in range(nb) for h in range(H_DELTA)]
        for ci in range(2):
            rows = pl.ds(pl.multiple_of(r0 + ci * CHUNK, CHUNK), CHUNK)
            states = [s_scr[bi, h] for bi, h in seqs]
            res = [jnp.dot(jnp.concatenate([w_ref[bi, h, rows, :], qg_ref[bi, h, rows, :]], axis=0),
                           s.astype(BF16), preferred_element_type=F32) for (bi, h), s in zip(seqs, states)]
            v_new = [(u_ref[bi, h, rows, :] - r[:CHUNK]).astype(BF16) for (bi, h), r in zip(seqs, res)]
            upd = [jnp.dot(kdt_ref[bi, h, jb, :, ci * CHUNK:(ci + 1) * CHUNK], v, preferred_element_type=F32)
                   for (bi, h), v in zip(seqs, v_new)]
            intra = [jnp.dot(at_ref[bi, h, rows, :], v, preferred_element_type=F32) for (bi, h), v in zip(seqs, v_new)]
            for (bi, h), s, r, du, oi in zip(seqs, states, res, upd, intra):
                decay = jnp.broadcast_to(el_ref[bi, h, jb, ci * SUBLANES:ci * SUBLANES + 1, :], (DK_DELTA, DV_DELTA))
                s_scr[bi, h] = s * decay + du
                hs = slice(h * DV_DELTA, (h + 1) * DV_DELTA)
                o_ref[bi, rows, hs] = _delta_norm_gate(r[CHUNK:] + oi, nw_ref[...], z_ref[bi, rows, hs]).astype(BF16)
        return carry

    lax.fori_loop(0, nblk, block, 0)

    @pl.when(t == pl.num_programs(1) - 1)
    def _():
        sfin_ref[...] = s_scr[...]


def _delta_scan(prep, z, s0, norm_w):
    u, w, qg, at, kdt, el = prep
    b, _, t, _ = u.shape
    nb = math.gcd(b, 4)
    ts = min(512, t)
    assert t % ts == 0
    nblk = ts // PAIR
    per_head = lambda width: pl.BlockSpec((nb, H_DELTA, ts, width), lambda bi, ti: (bi, 0, ti, 0))
    state = pl.BlockSpec((nb, H_DELTA, DK_DELTA, DV_DELTA), lambda bi, ti: (bi, 0, 0, 0))
    tok = pl.BlockSpec((nb, ts, W_DELTA), lambda bi, ti: (bi, ti, 0))
    return pl.pallas_call(
        functools.partial(_delta_scan_kernel, nb=nb, nblk=nblk),
        grid=(b // nb, t // ts),
        in_specs=[per_head(DV_DELTA), per_head(DK_DELTA), per_head(DK_DELTA), per_head(CHUNK),
                  pl.BlockSpec((nb, H_DELTA, nblk, DK_DELTA, PAIR), lambda bi, ti: (bi, 0, ti, 0, 0)),
                  pl.BlockSpec((nb, H_DELTA, nblk, 2 * SUBLANES, LANES), lambda bi, ti: (bi, 0, ti, 0, 0)),
                  tok, state, pl.BlockSpec((1, DV_DELTA), lambda bi, ti: (0, 0))],
        out_specs=[tok, state],
        out_shape=[jax.ShapeDtypeStruct((b, t, W_DELTA), BF16),
                   jax.ShapeDtypeStruct((b, H_DELTA, DK_DELTA, DV_DELTA), F32)],
        scratch_shapes=[pltpu.VMEM((nb, H_DELTA, DK_DELTA, DV_DELTA), F32)],
        compiler_params=pltpu.CompilerParams(
            dimension_semantics=("arbitrary", "arbitrary"), vmem_limit_bytes=VMEM_LIMIT),
        name="delta_scan",
    )(u, w, qg, at, kdt, el, z, s0, norm_w.reshape(1, DV_DELTA))


def _delta_step_kernel(cin_ref, prev_ref, cw_ref, tail_ref, alog_ref, dtb_ref, z_ref, s0_ref, nw_ref,
                       o_ref, s_ref, *, seqs):
    items = []
    for b in range(seqs):
        prev = prev_ref[b]
        conv = prev[0:1] * cw_ref[0:1, :]
        for j in range(1, CONV_W - 1):
            conv = conv + prev[j:j + 1] * cw_ref[j:j + 1, :]
        conv = _silu(conv + cin_ref[b] * cw_ref[CONV_W - 1:CONV_W, :])
        beta_all, g_all = _gate_rows(tail_ref[b], alog_ref[...], dtb_ref[...])
        for h in range(H_DELTA):
            hs = slice(h * DK_DELTA, (h + 1) * DK_DELTA)
            qs = _l2norm(conv[:, hs]) * DK_DELTA ** -0.5
            kn = _l2norm(conv[:, W_DELTA + h * DK_DELTA:W_DELTA + (h + 1) * DK_DELTA])
            vh = conv[:, 2 * W_DELTA + h * DV_DELTA:2 * W_DELTA + (h + 1) * DV_DELTA]
            beta = beta_all[:, h:h + 1]
            eg = jnp.exp(g_all[:, H_DELTA + h:H_DELTA + h + 1])
            items.append((b, h, hs, qs, kn, vh, beta, eg, s0_ref[b, h]))
    res = [jnp.dot(jnp.concatenate([kn * (beta * eg), qs * eg, jnp.zeros((SUBLANES - 2, DK_DELTA), F32)],
                                   axis=0).astype(BF16),
                   s.astype(BF16), preferred_element_type=F32)
           for (b, h, hs, qs, kn, vh, beta, eg, s) in items]
    for (b, h, hs, qs, kn, vh, beta, eg, s), r in zip(items, res):
        v_new = vh * beta - r[0:1]
        qk = jnp.sum(qs.astype(BF16).astype(F32) * kn.astype(BF16).astype(F32), axis=-1, keepdims=True)
        o = r[1:2] + qk * v_new
        k_col = jnp.broadcast_to(kn, (DK_DELTA, DK_DELTA)).T
        s_ref[b, h] = s * eg + k_col * v_new
        o_ref[b, :, hs] = _delta_norm_gate(o, nw_ref[...], z_ref[b, :, hs]).astype(BF16)


def _delta_step(cin, conv_prev, conv_w, tail, alog_row, dtb_row, z, s0, norm_w):
    bs = cin.shape[0]
    seqs = math.gcd(bs, 4)
    tok = lambda width: pl.BlockSpec((seqs, 1, width), lambda b: (b, 0, 0))
    const = lambda shape: pl.BlockSpec(shape, lambda b: (0,) * len(shape))
    state = pl.BlockSpec((seqs, H_DELTA, DK_DELTA, DV_DELTA), lambda b: (b, 0, 0, 0))
    return pl.pallas_call(
        functools.partial(_delta_step_kernel, seqs=seqs),
        grid=(bs // seqs,),
        in_specs=[tok(CONV_CH), pl.BlockSpec((seqs, CONV_W - 1, CONV_CH), lambda b: (b, 0, 0)), const(conv_w.shape),
                  tok(LANES), const(alog_row.shape), const(dtb_row.shape), tok(W_DELTA), state,
                  const((1, DV_DELTA))],
        out_specs=[tok(W_DELTA), state],
        out_shape=[jax.ShapeDtypeStruct((bs, 1, W_DELTA), BF16),
                   jax.ShapeDtypeStruct((bs, H_DELTA, DK_DELTA, DV_DELTA), F32)],
        compiler_params=pltpu.CompilerParams(dimension_semantics=("arbitrary",)),
        name="delta_step",
    )(cin, conv_prev, conv_w, tail, alog_row, dtb_row, z, s0, norm_w.reshape(1, DV_DELTA))


def _out_proj_kernel(oa_ref, ob_ref, x_ref, w_ref, g_ref, b_ref, y_ref, *, alpha):
    mix = jnp.dot(jnp.concatenate([oa_ref[...], ob_ref[...]], axis=1), w_ref[...], preferred_element_type=F32)
    r = alpha * x_ref[...] + mix
    mu = jnp.mean(r, axis=-1, keepdims=True)
    var = jnp.mean(jnp.square(r - mu), axis=-1, keepdims=True)
    y_ref[...] = (r - mu) * lax.rsqrt(var + LN_EPS) * g_ref[...] + b_ref[...]


def _out_proj(oa, ob, x, w_out_b, ln_g, ln_b, alpha):
    m, d = x.shape
    tm = min(1024, m)
    assert m % tm == 0
    row = lambda width: pl.BlockSpec((tm, width), lambda i: (i, 0))
    const = lambda shape: pl.BlockSpec(shape, lambda i: (0, 0))
    return pl.pallas_call(
        functools.partial(_out_proj_kernel, alpha=alpha),
        grid=(m // tm,),
        in_specs=[row(W_DIFF), row(W_DELTA), row(d), const(w_out_b.shape), const((1, d)), const((1, d))],
        out_specs=row(d),
        out_shape=jax.ShapeDtypeStruct((m, d), F32),
        compiler_params=pltpu.CompilerParams(dimension_semantics=("arbitrary",), vmem_limit_bytes=VMEM_LIMIT),
        name="out_proj",
    )(oa, ob, x, w_out_b, ln_g.reshape(1, d), ln_b.reshape(1, d))


def _pad_lanes(vec, offset):
    return jnp.zeros((1, LANES), F32).at[0, offset:offset + vec.shape[0]].set(vec.astype(F32))


def _gate_rows_param(vec):
    col = jnp.broadcast_to(vec.astype(F32)[:, None], (H_DELTA, LANES))
    return jnp.concatenate([jnp.zeros((H_DELTA, LANES), F32), col], axis=0)


def kernel(x_prompt, x_sample, cache_k, cache_v, page_table, state_delta, state_conv, w_in, conv_w, a_log,
           dt_bias, delta_norm_w, diff_lambda, diff_norm_w, w_out, ln_g, ln_b):
    depth = w_in.shape[0]
    bp, tp, d = x_prompt.shape
    bs, ts, _ = x_sample.shape
    assert ts == 1 and w_in.shape[2] == P_MAIN + 2 * H_DELTA and d == w_out.shape[2]
    past_len = page_table.shape[1] * cache_k.shape[2]
    alpha = (2 * depth) ** 0.25
    tables_p = _rope_tables(jnp.arange(tp, dtype=jnp.int32))
    tables_s = _rope_tables(jnp.full((bs,), past_len, jnp.int32))

    hp, hs = x_prompt, x_sample
    outs = [[] for _ in range(8)]
    for l in range(depth):
        lam_init = _lambda_init(l)
        w_main = w_in[l].astype(BF16)
        w_tail = jnp.zeros((d, LANES), BF16).at[:, :2 * H_DELTA].set(w_in[l, :, P_MAIN:].astype(BF16))
        w_out_b = w_out[l].astype(BF16)
        alog_row = _pad_lanes(a_log[l], H_DELTA)
        dtb_row = _pad_lanes(dt_bias[l], H_DELTA)
        alog_col = _gate_rows_param(a_log[l])
        dtb_col = _gate_rows_param(dt_bias[l])

        sq, sk, _, sv, _, sga, scin, sz, stail = _in_proj(hs.reshape(1, bs, d), w_main, w_tail, tables_s, False)
        tok = lambda a: a.reshape(bs, 1, -1)
        decode_args = dict(q=tok(sq), k_new=tok(sk), v_new=tok(sv), ga=tok(sga), cache_k=cache_k, cache_v=cache_v,
                           layer=l, page_table=page_table, diff_lambda=diff_lambda[l], norm_w=diff_norm_w[l],
                           lam_init=lam_init)
        rider = _DecodeRider(**decode_args, seq0=0, n_seq=bs, grid=(bp, _delta_prep_grid(tp)[1]))
        if not rider.ok:
            rider = None

        q, k, kb, v, vb, ga, cin, z, _, tail_t = _in_proj(hp, w_main, w_tail, tables_p, True)
        oa = _prompt_attn(q, kb, vb, ga, diff_lambda[l], diff_norm_w[l], lam_init)
        prep, soa = _delta_prep(cin, jnp.zeros((bp, CONV_W - 1, CONV_CH), F32), conv_w[l], tail_t, alog_col,
                                dtb_col, rider)
        ob, sp = _delta_scan(prep, z, jnp.zeros((bp, H_DELTA, DK_DELTA, DV_DELTA), F32), delta_norm_w[l])
        hp = _out_proj(oa.reshape(bp * tp, W_DIFF), ob.reshape(bp * tp, W_DELTA), hp.reshape(bp * tp, d),
                       w_out_b, ln_g[l], ln_b[l], alpha).reshape(bp, tp, d)
        outs[0].append(k.reshape(bp, tp, H_DIFF, 2, DH_DIFF))
        outs[1].append(v.reshape(bp, tp, H_DIFF, DV_DIFF))
        outs[2].append(sp)
        outs[3].append(cin[:, tp - (CONV_W - 1):, :])

        if rider is None:
            soa = _decode_attn(decode_args, bs)
        sob, ss = _delta_step(tok(scin), state_conv[l], conv_w[l], tok(stail), alog_row, dtb_row, tok(sz),
                              state_delta[l], delta_norm_w[l])
        hs = _out_proj(soa.reshape(bs, W_DIFF), sob.reshape(bs, W_DELTA), hs.reshape(bs, d),
                       w_out_b, ln_g[l], ln_b[l], alpha).reshape(bs, 1, d)
        outs[4].append(sk.reshape(bs, 1, H_DIFF, 2, DH_DIFF))
        outs[5].append(sv.reshape(bs, 1, H_DIFF, DV_DIFF))
        outs[6].append(ss)
        outs[7].append(jnp.concatenate([state_conv[l][:, 1:, :], tok(scin)], axis=1))
    return (hp, hs) + tuple(jnp.stack(o) for o in outs)
```

```python
import functools
import math

import jax
import jax.numpy as jnp
from jax import lax
from jax.experimental import pallas as pl
from jax.experimental.pallas import tpu as pltpu

F32 = jnp.float32
BF16 = jnp.bfloat16
HIGHEST = lax.Precision.HIGHEST

H_DIFF = 4
DH_DIFF = 64
DV_DIFF = 2 * DH_DIFF
W_DIFF = H_DIFF * DV_DIFF
H_DELTA = 4
DK_DELTA = 128
DV_DELTA = 128
W_DELTA = H_DELTA * DK_DELTA
ROT_DIM = DH_DIFF // 4
ROPE_THETA = 500000.0
CONV_W = 4
CONV_CH = 3 * W_DELTA
CHUNK = 64
P_MAIN = 4 * W_DIFF + 4 * W_DELTA
LN_EPS = 1e-5
HEAD_NORM_EPS = 1e-5
DELTA_NORM_EPS = 1e-6
L2_EPS = 1e-6
NEG_INF = -1e30

LANES = 128
SUBLANES = 8
PAIR = 2 * CHUNK
CHUNK_BITS = CHUNK.bit_length() - 1
MAP_BITS = DH_DIFF.bit_length() - 1
VMEM_LIMIT = 56 * 1024 * 1024
DECODE_PAGES_PER_STEP = 16

_NT = (((1,), (1,)), ((), ()))


def _sigmoid(x):
    return 0.5 * jnp.tanh(0.5 * x) + 0.5


def _silu(x):
    return x * _sigmoid(x)


def _softplus(x):
    return jnp.maximum(x, 0.0) + jnp.log1p(jnp.exp(-jnp.abs(x)))


def _lambda_init(layer):
    return 0.8 - 0.6 * math.exp(-0.3 * layer)


def _rope_tables(pos):
    half = ROT_DIM // 2
    inv = ROPE_THETA ** (-jnp.arange(half, dtype=F32) / half)
    ang = pos.astype(F32)[:, None] * inv[None, :]
    cos, sin = jnp.cos(ang), jnp.sin(ang)
    t = pos.shape[0]
    rest = DH_DIFF - ROT_DIM
    cos_m = jnp.concatenate([cos, cos, jnp.ones((t, rest), F32)], axis=-1)
    sin_lo = jnp.concatenate([-sin, jnp.zeros((t, half + rest), F32)], axis=-1)
    sin_hi = jnp.concatenate([jnp.zeros((t, half), F32), sin, jnp.zeros((t, rest), F32)], axis=-1)
    reps = LANES // DH_DIFF
    return jnp.tile(cos_m, (1, reps)), jnp.tile(sin_lo, (1, reps)), jnp.tile(sin_hi, (1, reps))


def _in_proj_kernel(x_ref, w_ref, wt_ref, cos_ref, slo_ref, shi_ref,
                    q_ref, k_ref, kb_ref, v_ref, vb_ref, ga_ref, cin_ref, z_ref, tail_ref, *tail_t_ref):
    xb = x_ref[0].astype(BF16)
    half = ROT_DIM // 2
    tm = xb.shape[0]

    def proj(c0, width):
        return jnp.dot(xb, w_ref[:, c0:c0 + width], preferred_element_type=F32)

    def rope(h):
        blocks = []
        for c in range(W_DIFF // LANES):
            hb = h[:, c * LANES:(c + 1) * LANES]
            blocks.append(hb * cos_ref[...]
                          + pltpu.roll(hb, LANES - half, 1) * slo_ref[...]
                          + pltpu.roll(hb, half, 1) * shi_ref[...])
        return jnp.concatenate(blocks, axis=1)

    q_ref[0] = rope(proj(0, W_DIFF)).astype(BF16)
    k = rope(proj(W_DIFF, W_DIFF))
    k_ref[0] = k
    kb_ref[0] = k.astype(BF16)
    v = proj(2 * W_DIFF, W_DIFF)
    for h in range(H_DIFF):
        v_ref[0, pl.ds(h, tm, stride=H_DIFF), :] = v[:, h * DV_DIFF:(h + 1) * DV_DIFF]
    vb_ref[0] = v.astype(BF16)
    ga_ref[0] = proj(3 * W_DIFF, W_DIFF)
    for j in range(3):
        cin_ref[0, :, j * W_DELTA:(j + 1) * W_DELTA] = proj(4 * W_DIFF + j * W_DELTA, W_DELTA)
    z_ref[0] = proj(4 * W_DIFF + 3 * W_DELTA, W_DELTA)
    tail = jnp.dot(xb, wt_ref[...], preferred_element_type=F32)
    tail_ref[0] = tail
    if tail_t_ref:
        for c in range(tm // LANES):
            cs = slice(c * LANES, (c + 1) * LANES)
            tail_t_ref[0][0, :, cs] = tail[cs, :].T[:2 * H_DELTA, :]


def _in_proj(x, w_main, w_tail, tables, gates_time_major):
    b, t, d = x.shape
    tm = min(512, t)
    assert t % tm == 0 and (not gates_time_major or tm % LANES == 0)
    cos_t, slo_t, shi_t = tables
    row = lambda width: pl.BlockSpec((1, tm, width), lambda ti, bi: (bi, ti, 0))
    tab = pl.BlockSpec((tm, LANES), lambda ti, bi: (ti, 0))
    const = lambda shape: pl.BlockSpec(shape, lambda ti, bi: (0, 0))
    outs = [(W_DIFF, BF16), (W_DIFF, F32), (W_DIFF, BF16), None, (W_DIFF, BF16),
            (W_DIFF, F32), (CONV_CH, F32), (W_DELTA, F32), (LANES, F32)]
    out_specs = [row(o[0]) if o else pl.BlockSpec((1, tm * H_DIFF, DV_DIFF), lambda ti, bi: (bi, ti, 0))
                 for o in outs]
    out_shape = [jax.ShapeDtypeStruct((b, t, o[0]), o[1]) if o else
                 jax.ShapeDtypeStruct((b, t * H_DIFF, DV_DIFF), F32) for o in outs]
    if gates_time_major:
        out_specs.append(pl.BlockSpec((1, 2 * H_DELTA, tm), lambda ti, bi: (bi, 0, ti)))
        out_shape.append(jax.ShapeDtypeStruct((b, 2 * H_DELTA, t), F32))
    return pl.pallas_call(
        _in_proj_kernel,
        grid=(t // tm, b),
        in_specs=[row(d), const((d, P_MAIN)), const(w_tail.shape), tab, tab, tab],
        out_specs=out_specs,
        out_shape=out_shape,
        compiler_params=pltpu.CompilerParams(
            dimension_semantics=("arbitrary", "arbitrary"), vmem_limit_bytes=VMEM_LIMIT),
        name="in_proj",
    )(x, w_main, w_tail, cos_t, slo_t, shi_t)


def _diff_lambda_value(dl_ref, lam_init):
    dl = dl_ref[...]
    a = jnp.sum(dl[0:1] * dl[1:2], axis=1, keepdims=True)
    b = jnp.sum(dl[2:3] * dl[3:4], axis=1, keepdims=True)
    return jnp.exp(a) - jnp.exp(b) + lam_init


def _head_norm_gate(o, normw, gate, lam_init):
    ms = jnp.mean(o * o, axis=-1, keepdims=True)
    o = o * lax.rsqrt(ms + HEAD_NORM_EPS) * normw * (1.0 - lam_init)
    return o * _silu(gate)


def _prompt_attn_kernel(q0_ref, qn_ref, k_ref, v_ref, ga_ref, dl_ref, nw_ref, o_ref, s_scr, m_scr, l_scr, acc_scr,
                        *, blk, lam_init):
    i = pl.program_id(1)
    last = pl.num_programs(1) - 1
    slot = lax.rem(i, 2)
    lane = lax.broadcasted_iota(jnp.int32, (blk, DV_DIFF), 1)
    heads = [slice(h * DV_DIFF, (h + 1) * DV_DIFF) for h in range(H_DIFF)]
    lane_chunks = [slice(c * LANES, (c + 1) * LANES) for c in range(blk // LANES)]
    n_maps = 2 * H_DIFF

    def query_maps(q_ref):
        maps = []
        for hs in heads:
            qs = q_ref[0, :, hs].astype(F32) * (DH_DIFF ** -0.5 * math.log2(math.e))
            maps.append(jnp.concatenate([jnp.where(lane < DH_DIFF, qs, 0.0),
                                         jnp.where(lane >= DH_DIFF, qs, 0.0)], axis=0).astype(BF16))
        return maps

    def fold(x, op):
        r = x[:, lane_chunks[0]]
        for c in lane_chunks[1:]:
            r = op(r, x[:, c])
        return r

    def score_block(q_maps, dst, j, masked):
        rows = pl.ds(pl.multiple_of(j * blk, blk), blk)
        for h, hs in enumerate(heads):
            s_pair = lax.dot_general(q_maps[h], k_ref[0, rows, hs], _NT, preferred_element_type=F32)
            for half, mi in enumerate((2 * h, 2 * h + 1)):
                s = s_pair[half * blk:(half + 1) * blk]
                if masked:
                    row = lax.broadcasted_iota(jnp.int32, (blk, blk), 0)
                    col = lax.broadcasted_iota(jnp.int32, (blk, blk), 1)
                    s = jnp.where(col <= row, s, NEG_INF)
                s_scr[dst, j, mi] = s
                m_scr[dst, mi] = jnp.maximum(m_scr[dst, mi], fold(s, jnp.maximum))

    def finish_max(dst):
        for mi in range(n_maps):
            m_scr[dst, mi] = jnp.broadcast_to(jnp.max(m_scr[dst, mi], axis=-1, keepdims=True), (blk, LANES))

    def prob_block(j):
        rows = pl.ds(pl.multiple_of(j * blk, blk), blk)
        for h, hs in enumerate(heads):
            pair = []
            for mi in (2 * h, 2 * h + 1):
                m = m_scr[slot, mi]
                p = jnp.concatenate([jnp.exp2(s_scr[slot, j, mi, :, c] - m) for c in lane_chunks], axis=1)
                l_scr[mi] = l_scr[mi] + fold(p, jnp.add)
                pair.append(p.astype(BF16))
            pv = jnp.dot(jnp.concatenate(pair, axis=0), v_ref[0, rows, hs], preferred_element_type=F32)
            acc_scr[2 * h] = acc_scr[2 * h] + pv[:blk]
            acc_scr[2 * h + 1] = acc_scr[2 * h + 1] + pv[blk:]

    def write_output():
        lam = _diff_lambda_value(dl_ref, lam_init)
        for h, hs in enumerate(heads):
            l1 = jnp.sum(l_scr[2 * h], axis=-1, keepdims=True)
            l2 = jnp.sum(l_scr[2 * h + 1], axis=-1, keepdims=True)
            o = acc_scr[2 * h] / l1 - lam * (acc_scr[2 * h + 1] / l2)
            o_ref[0, :, hs] = _head_norm_gate(o, nw_ref[...], ga_ref[0, :, hs], lam_init).astype(BF16)

    @pl.when(i == 0)
    def _():
        m_scr[0] = jnp.full(m_scr.shape[1:], NEG_INF, F32)
        score_block(query_maps(q0_ref), 0, 0, True)
        finish_max(0)

    l_scr[...] = jnp.zeros(l_scr.shape, F32)
    acc_scr[...] = jnp.zeros(acc_scr.shape, F32)

    @pl.when(i < last)
    def _():
        nxt = 1 - slot
        q_next = query_maps(qn_ref)
        m_scr[nxt] = jnp.full(m_scr.shape[1:], NEG_INF, F32)

        def both(j, carry):
            prob_block(j)
            score_block(q_next, nxt, j, False)
            return carry

        lax.fori_loop(0, i + 1, both, 0)
        write_output()
        score_block(q_next, nxt, i + 1, True)
        finish_max(nxt)

    @pl.when(i == last)
    def _():
        def only_probs(j, carry):
            prob_block(j)
            return carry

        lax.fori_loop(0, i + 1, only_probs, 0)
        write_output()


def _prompt_attn(q, kb, vb, ga, diff_lambda, norm_w, lam_init):
    b, t, _ = q.shape
    blk = min(256, t)
    assert t % blk == 0
    n_maps = 2 * H_DIFF
    nq = t // blk
    qspec = pl.BlockSpec((1, blk, W_DIFF), lambda bi, i: (bi, i, 0))
    q_first = pl.BlockSpec((1, blk, W_DIFF), lambda bi, i: (bi, 0, 0))
    q_next = pl.BlockSpec((1, blk, W_DIFF), lambda bi, i: (bi, jnp.minimum(i + 1, nq - 1), 0))
    kvspec = pl.BlockSpec((1, t, W_DIFF), lambda bi, i: (bi, 0, 0))
    return pl.pallas_call(
        functools.partial(_prompt_attn_kernel, blk=blk, lam_init=lam_init),
        grid=(b, nq),
        in_specs=[q_first, q_next, kvspec, kvspec, qspec,
                  pl.BlockSpec(diff_lambda.shape, lambda bi, i: (0, 0)),
                  pl.BlockSpec((1, DV_DIFF), lambda bi, i: (0, 0))],
        out_specs=qspec,
        out_shape=jax.ShapeDtypeStruct((b, t, W_DIFF), BF16),
        scratch_shapes=[pltpu.VMEM((2, nq, n_maps, blk, blk), F32),
                        pltpu.VMEM((2, n_maps, blk, LANES), F32),
                        pltpu.VMEM((n_maps, blk, LANES), F32),
                        pltpu.VMEM((n_maps, blk, DV_DIFF), F32)],
        compiler_params=pltpu.CompilerParams(
            dimension_semantics=("arbitrary", "arbitrary"), vmem_limit_bytes=VMEM_LIMIT),
        name="prompt_attn",
    )(q, q, kb, vb, ga, diff_lambda, norm_w.reshape(1, DV_DIFF))


class _DecodeStep:
    def __init__(self, step, n_steps, spq, first_page, pages, pt_ref, in_refs, o_ref, scratch_refs, lam_init):
        self.step, self.n_steps, self.pages, self.first_page = step, n_steps, pages, first_page
        self.g, self.n_g, self.lam_init, self.pt_ref = lax.rem(step, spq), spq, lam_init, pt_ref
        (self.q_ref, self.kn_ref, self.vn_ref, self.ga_ref, self.dl_ref, self.nw_ref,
         self.ck_ref, self.cv_ref) = in_refs
        self.o_ref = o_ref
        (self.qblk_ref, self.m_ref, self.l_ref, self.acc_ref,
         self.kbuf_ref, self.vbuf_ref, self.sem_ref) = scratch_refs
        self.slot = lax.rem(step, 2)

    def _query(self):
        return self.q_ref[0].astype(F32) * DH_DIFF ** -0.5

    def _page_copies(self, step, slot):
        base = self.first_page + step * self.pages
        copies = []
        for i in range(self.pages):
            pid = self.pt_ref[base + i]
            copies.append(pltpu.make_async_copy(self.ck_ref.at[pid], self.kbuf_ref.at[slot, i],
                                                self.sem_ref.at[slot, 0]))
            copies.append(pltpu.make_async_copy(self.cv_ref.at[pid], self.vbuf_ref.at[slot, i],
                                                self.sem_ref.at[slot, 1]))
        return copies

    @staticmethod
    def _start(copies):
        for n, cp in enumerate(copies):
            cp.start(priority=n % 2)

    def init(self):
        @pl.when(self.step == 0)
        def _():
            self._start(self._page_copies(0, 0))

        for cp in self._page_copies(self.step, self.slot):
            cp.wait()

        @pl.when(self.step + 1 < self.n_steps)
        def _():
            self._start(self._page_copies(self.step + 1, 1 - self.slot))

        @pl.when(self.g == 0)
        def _():
            self.m_ref[...] = jnp.full(self.m_ref.shape, NEG_INF, F32)
            self.l_ref[...] = jnp.zeros(self.l_ref.shape, F32)
            self.acc_ref[...] = jnp.zeros(self.acc_ref.shape, F32)
            n_maps = 2 * H_DIFF
            rows = lax.broadcasted_iota(jnp.int32, (n_maps, W_DIFF), 0)
            lanes = lax.broadcasted_iota(jnp.int32, (n_maps, W_DIFF), 1)
            self.qblk_ref[...] = jnp.where((lanes >> MAP_BITS) == rows,
                                           jnp.broadcast_to(self._query(), (n_maps, W_DIFF)), 0.0)

    def main(self):
        k_pages = [self.kbuf_ref.at[self.slot, i] for i in range(self.pages)]
        v_pages = [self.vbuf_ref.at[self.slot, i] for i in range(self.pages)]
        _decode_pages(k_pages, v_pages, self.qblk_ref, self.m_ref, self.l_ref, self.acc_ref)

    def finalize(self):
        @pl.when(self.g == self.n_g - 1)
        def _():
            _decode_finish(self.qblk_ref[...].astype(BF16), self.kn_ref, self.vn_ref, self.ga_ref, self.dl_ref,
                           self.nw_ref, self.o_ref, self.m_ref, self.l_ref, self.acc_ref, self.lam_init)


def _decode_pages(kp_refs, vp_refs, qblk_ref, m_ref, l_ref, acc_ref):
    n_maps = 2 * H_DIFF
    page = kp_refs[0].shape[1]
    q_blk = qblk_ref[...].astype(BF16)
    s = jnp.concatenate(
        [jnp.dot(q_blk, kp[...].astype(BF16), preferred_element_type=F32) for kp in kp_refs], axis=1)
    m_prev = m_ref[:, 0:1]
    m_new = jnp.maximum(m_prev, jnp.max(s, axis=-1, keepdims=True))
    p = jnp.exp(s - m_new)
    alpha = jnp.exp(m_prev - m_new)
    l_new = alpha * l_ref[:, 0:1] + jnp.sum(p, axis=-1, keepdims=True)
    wide = page * H_DIFF
    p_rows = jnp.concatenate([p[:, i * page:(i + 1) * page] for i in range(len(vp_refs))], axis=0)
    tok = lax.broadcasted_iota(jnp.int32, (page, wide), 0)
    col = lax.broadcasted_iota(jnp.int32, (page, wide), 1)
    spread = jnp.where(col // H_DIFF == tok, 1.0, 0.0).astype(BF16)
    p_wide = jnp.dot(p_rows.astype(BF16), spread, preferred_element_type=F32)
    row8 = lax.broadcasted_iota(jnp.int32, (n_maps, wide), 0)
    col8 = lax.broadcasted_iota(jnp.int32, (n_maps, wide), 1)
    own_head = lax.rem(col8, H_DIFF) == row8 // 2
    parts = [jnp.zeros((n_maps, DV_DIFF), F32) for _ in range(min(4, len(vp_refs)))]
    for i, vp in enumerate(vp_refs):
        p_page = jnp.where(own_head, p_wide[i * n_maps:(i + 1) * n_maps], 0.0).astype(BF16)
        parts[i % len(parts)] += jnp.dot(p_page, vp[...].astype(BF16), preferred_element_type=F32)
    acc_ref[...] = alpha * acc_ref[...] + functools.reduce(lambda a, b: a + b, parts)
    m_ref[...] = jnp.broadcast_to(m_new, m_ref.shape)
    l_ref[...] = jnp.broadcast_to(l_new, l_ref.shape)


def _decode_finish(q_blk, kn_ref, vn_ref, ga_ref, dl_ref, nw_ref, o_ref, m_ref, l_ref, acc_ref, lam_init):
    k_self = jnp.broadcast_to(kn_ref[0].astype(BF16), (SUBLANES, W_DIFF))
    s_self = lax.dot_general(q_blk, k_self, _NT, preferred_element_type=F32)[:, 0:1]
    m_past = m_ref[:, 0:1]
    m_f = jnp.maximum(m_past, s_self)
    a_f = jnp.exp(m_past - m_f)
    p_self = jnp.exp(s_self - m_f)
    l_f = a_f * l_ref[:, 0:1] + p_self
    pv_self = p_self.astype(BF16).astype(F32) * vn_ref[0].astype(BF16).astype(F32)
    lam = _diff_lambda_value(dl_ref, lam_init)
    for h in range(H_DIFF):
        hs = slice(h * DV_DIFF, (h + 1) * DV_DIFF)
        a = (a_f * acc_ref[...] + pv_self[:, hs]) / l_f
        o = a[2 * h:2 * h + 1] - lam * a[2 * h + 1:2 * h + 2]
        o_ref[0, :, hs] = _head_norm_gate(o, nw_ref[...], ga_ref[0, :, hs], lam_init).astype(BF16)


class _DecodeRider:
    def __init__(self, q, k_new, v_new, ga, cache_k, cache_v, layer, page_table, diff_lambda, norm_w, lam_init,
                 seq0, n_seq, grid):
        depth, n_phys, page = cache_k.shape[:3]
        n_pages = page_table.shape[1]
        n_steps = grid[0] * grid[1]
        assert page == LANES
        pages = n_seq * n_pages // n_steps
        self.ok = pages >= 1 and pages * n_steps == n_seq * n_pages and n_pages % pages == 0
        if not self.ok:
            return
        spq = n_pages // pages
        self.pages, self.spq, self.lam_init, self.grid = pages, spq, lam_init, grid
        self.first_page = seq0 * n_pages
        ck = jnp.transpose(cache_k, (0, 1, 3, 4, 5, 2)).reshape(depth * n_phys, W_DIFF, page)
        cv = cache_v.reshape(depth * n_phys, page * H_DIFF, DV_DIFF)
        self.page_ids = page_table.reshape(-1) + layer * n_phys
        step = lambda i0, i1: i0 * grid[1] + i1
        tok = pl.BlockSpec((1, 1, W_DIFF), lambda i0, i1, pt: (seq0 + step(i0, i1) // spq, 0, 0))
        const = lambda shape: pl.BlockSpec(shape, lambda i0, i1, pt: (0, 0))
        hbm = pl.BlockSpec(memory_space=pl.ANY)
        self.inputs = [q, k_new, v_new, ga, diff_lambda, norm_w.reshape(1, DV_DIFF), ck, cv]
        self.in_specs = [tok, tok, tok, tok, const(diff_lambda.shape), const((1, DV_DIFF)), hbm, hbm]
        self.out_spec = pl.BlockSpec((1, 1, W_DIFF), lambda i0, i1, pt: (step(i0, i1) // spq, 0, 0))
        self.out_shape = jax.ShapeDtypeStruct((n_seq, 1, W_DIFF), BF16)
        n_maps = 2 * H_DIFF
        self.scratch_shapes = [pltpu.VMEM((n_maps, W_DIFF), F32),
                               pltpu.VMEM((n_maps, LANES), F32), pltpu.VMEM((n_maps, LANES), F32),
                               pltpu.VMEM((n_maps, DV_DIFF), F32),
                               pltpu.VMEM((2, pages, W_DIFF, page), F32),
                               pltpu.VMEM((2, pages, page * H_DIFF, DV_DIFF), F32),
                               pltpu.SemaphoreType.DMA((2, 2))]

    def step(self, pt_ref, in_refs, out_ref, scratch_refs):
        step = pl.program_id(0) * self.grid[1] + pl.program_id(1)
        return _DecodeStep(step, self.grid[0] * self.grid[1], self.spq, self.first_page, self.pages, pt_ref,
                           in_refs, out_ref, scratch_refs, self.lam_init)


def _ride(host_kernel, n_in, n_out, n_scratch, rider):
    n_rin = len(rider.inputs)

    def kernel(pt_ref, *refs):
        host_in, refs = refs[:n_in], refs[n_in:]
        rider_in, refs = refs[:n_rin], refs[n_rin:]
        host_out, refs = refs[:n_out], refs[n_out:]
        rider_out, refs = refs[0], refs[1:]
        host_scratch, rider_scratch = refs[:n_scratch], refs[n_scratch:]
        host_kernel(*host_in, *host_out, *host_scratch,
                    decode=rider.step(pt_ref, rider_in, rider_out, rider_scratch))

    return kernel


def _hosted_call(host_kernel, grid, in_specs, out_specs, out_shape, scratch_shapes, inputs, name, rider):
    params = pltpu.CompilerParams(dimension_semantics=("arbitrary", "arbitrary"), vmem_limit_bytes=VMEM_LIMIT)
    if rider is None:
        outs = pl.pallas_call(host_kernel, grid=grid, in_specs=in_specs, out_specs=out_specs, out_shape=out_shape,
                              scratch_shapes=scratch_shapes, compiler_params=params, name=name)(*inputs)
        return outs, None
    grid_spec = pltpu.PrefetchScalarGridSpec(
        num_scalar_prefetch=1, grid=grid,
        in_specs=list(in_specs) + rider.in_specs,
        out_specs=list(out_specs) + [rider.out_spec],
        scratch_shapes=list(scratch_shapes) + rider.scratch_shapes)
    outs = pl.pallas_call(
        _ride(host_kernel, len(in_specs), len(out_specs), len(scratch_shapes), rider),
        grid_spec=grid_spec, out_shape=list(out_shape) + [rider.out_shape],
        compiler_params=params, name=name)(rider.page_ids, *inputs, *rider.inputs)
    return outs[:-1], outs[-1]


def _decode_attn_kernel(pt_ref, *refs, rider):
    n_rin = len(rider.inputs)
    decode = rider.step(pt_ref, refs[:n_rin], refs[n_rin], refs[n_rin + 1:])
    decode.init()
    decode.main()
    decode.finalize()


def _decode_attn(rider_args, n_seq):
    n_pages = rider_args["page_table"].shape[1]
    grid = (n_seq, n_pages // math.gcd(DECODE_PAGES_PER_STEP, n_pages))
    rider = _DecodeRider(**rider_args, seq0=0, n_seq=n_seq, grid=grid)
    grid_spec = pltpu.PrefetchScalarGridSpec(
        num_scalar_prefetch=1, grid=grid, in_specs=rider.in_specs,
        out_specs=rider.out_spec, scratch_shapes=rider.scratch_shapes)
    return pl.pallas_call(
        functools.partial(_decode_attn_kernel, rider=rider), grid_spec=grid_spec, out_shape=rider.out_shape,
        compiler_params=pltpu.CompilerParams(
            dimension_semantics=("arbitrary", "arbitrary"), vmem_limit_bytes=VMEM_LIMIT),
        name="decode_attn")(rider.page_ids, *rider.inputs)


def _gate_rows(tail, alog_row, dtb_row):
    beta = _sigmoid(tail)
    g = -jnp.exp(alog_row) * _softplus(tail + dtb_row)
    return beta, g


def _lane_bcast(x, lane, rows):
    return jnp.broadcast_to(x[:, lane:lane + 1], (rows, LANES))


def _l2norm(x):
    return x * lax.rsqrt(jnp.sum(x * x, axis=-1, keepdims=True) + L2_EPS)


def _split_bf16(x):
    hi = x.astype(BF16)
    return hi, (x - hi.astype(F32)).astype(BF16)


def _dot_split(lhs, rhs):
    d = lambda a, b: jnp.dot(a, b, preferred_element_type=F32)
    return d(lhs[0], rhs[0]) + d(lhs[0], rhs[1]) + d(lhs[1], rhs[0])


def _unit_lower_inverses(a_mats):
    n = a_mats[0].shape[0]
    eye = (lax.broadcasted_iota(jnp.int32, (n, n), 0) == lax.broadcasted_iota(jnp.int32, (n, n), 1)).astype(F32)
    xs = [-a for a in a_mats]
    ps = [eye + x for x in xs]
    splits = [_split_bf16(x) for x in xs]
    xs = [_dot_split(s, s) for s in splits]
    power = 2
    while 2 * power < CHUNK:
        both = [_dot_split(_split_bf16(x), _split_bf16(jnp.concatenate([p, x], axis=1))) for p, x in zip(ps, xs)]
        ps = [p + b[:, :n] for p, b in zip(ps, both)]
        xs = [b[:, n:] for b in both]
        power *= 2
    return [p + _dot_split(_split_bf16(x), _split_bf16(p)) for p, x in zip(ps, xs)]


def _delta_prep_kernel(cin_ref, prev_ref, cw_ref, tail_ref, alog_ref, dtb_ref,
                       u_ref, w_ref, qg_ref, at_ref, kdt_ref, el_ref, ext_ref, *, pairs, decode=None):
    t = pl.program_id(1)
    pad = SUBLANES
    keep = CONV_W - 1
    rows = pairs * PAIR

    @pl.when(t == 0)
    def _():
        ext_ref[pad - keep:pad, :] = prev_ref[0]

    @pl.when(t > 0)
    def _():
        ext_ref[pad - keep:pad, :] = ext_ref[pad + rows - keep:pad + rows, :]

    if decode is not None:
        decode.init()

    ext_ref[pad:pad + rows, :] = cin_ref[0]

    r = lax.broadcasted_iota(jnp.int32, (PAIR, PAIR), 0)
    c = lax.broadcasted_iota(jnp.int32, (PAIR, PAIR), 1)
    same = (r >> CHUNK_BITS) == (c >> CHUNK_BITS)
    incl = same & (c <= r)
    strict = same & (c < r)
    incl_t = same & (r <= c)
    chunk_end = r == (c | (CHUNK - 1))

    chains = []
    for pi in range(pairs):
        base = pad - keep + pi * PAIR
        conv = ext_ref[base + keep:base + keep + PAIR, :] * cw_ref[keep:keep + 1, :]
        for j in reversed(range(keep)):
            conv = conv + ext_ref[base + j:base + j + PAIR, :] * cw_ref[j:j + 1, :]
        conv = _silu(conv)
        rs = slice(pi * PAIR, (pi + 1) * PAIR)
        beta_t, g_t = _gate_rows(tail_ref[0, :, rs], alog_ref[...], dtb_ref[...])
        gc_t = jnp.dot(g_t, incl_t.astype(F32), precision=HIGHEST, preferred_element_type=F32)
        g_last_t = jnp.dot(gc_t, chunk_end.astype(F32), precision=HIGHEST, preferred_element_type=F32)
        for h in range(H_DELTA):
            hs = slice(h * DK_DELTA, (h + 1) * DK_DELTA)
            qs = _l2norm(conv[:, hs]) * DK_DELTA ** -0.5
            kn = _l2norm(conv[:, W_DELTA + h * DK_DELTA:W_DELTA + (h + 1) * DK_DELTA])
            vh = conv[:, 2 * W_DELTA + h * DV_DELTA:2 * W_DELTA + (h + 1) * DV_DELTA]
            beta = jnp.broadcast_to(beta_t[h:h + 1, :], (PAIR, PAIR)).T
            gc_cols = jnp.broadcast_to(gc_t[H_DELTA + h:H_DELTA + h + 1, :], (PAIR, PAIR))
            g_last_cols = jnp.broadcast_to(g_last_t[H_DELTA + h:H_DELTA + h + 1, :], (PAIR, PAIR))
            chains.append(dict(pi=pi, h=h, rs=rs, qs=qs, kn=kn, vh=vh, beta=beta, gc=gc_cols.T, gc_cols=gc_cols,
                               g_last_cols=g_last_cols, kbeta=kn * beta))

    for ch in chains:
        ch["kn_t"] = ch["kn"].T
        kn_tb = ch["kn_t"].astype(BF16)
        ch["kk"] = jnp.dot(ch["kbeta"].astype(BF16), kn_tb, preferred_element_type=F32)
        ch["qk"] = jnp.dot(ch["qs"].astype(BF16), kn_tb, preferred_element_type=F32)
    for ch in chains:
        ch["decay"] = jnp.exp(jnp.where(incl, ch["gc"] - ch["gc_cols"], -jnp.inf))
    t_mats = _unit_lower_inverses([jnp.where(strict, ch["kk"] * ch["decay"], 0.0) for ch in chains])
    for ch, t_mat in zip(chains, t_mats):
        rhs = jnp.concatenate([ch["vh"] * ch["beta"], ch["kbeta"] * jnp.exp(ch["gc"])], axis=1).astype(BF16)
        ch["uw"] = jnp.dot(t_mat.astype(BF16), rhs, preferred_element_type=F32)
    for ch in chains:
        pi, h, rs, gc = ch["pi"], ch["h"], ch["rs"], ch["gc"]
        u_ref[0, h, rs, :] = ch["uw"][:, :DV_DELTA]
        w_ref[0, h, rs, :] = ch["uw"][:, DV_DELTA:].astype(BF16)
        attn = jnp.where(incl, ch["qk"] * ch["decay"], 0.0)
        at_ref[0, h, rs, :] = jnp.concatenate([attn[:CHUNK, :CHUNK], attn[CHUNK:, CHUNK:]], axis=0).astype(BF16)
        qg_ref[0, h, rs, :] = (ch["qs"] * jnp.exp(gc)).astype(BF16)
        kdt_ref[0, h, pi] = (ch["kn_t"] * jnp.exp(ch["g_last_cols"] - ch["gc_cols"])).astype(BF16)
        el_ref[0, h, pi] = jnp.exp(jnp.concatenate([jnp.broadcast_to(gc[CHUNK - 1:CHUNK], (SUBLANES, LANES)),
                                                    jnp.broadcast_to(gc[PAIR - 1:PAIR], (SUBLANES, LANES))], axis=0))

    if decode is not None:
        decode.main()
        decode.finalize()


def _delta_prep_grid(t):
    assert t % PAIR == 0
    npair = t // PAIR
    pairs = 2 if npair % 2 == 0 else 1
    return pairs, npair // pairs


def _delta_prep(cin, conv_prev, conv_w, tail, alog_row, dtb_row, rider):
    b, t, _ = cin.shape
    npair = t // PAIR
    pairs, nt = _delta_prep_grid(t)
    rows = pairs * PAIR
    per_head = lambda width: pl.BlockSpec((1, H_DELTA, rows, width), lambda bi, ti, *_: (bi, 0, ti, 0))
    const = lambda shape: pl.BlockSpec(shape, lambda bi, ti, *_: (0,) * len(shape))
    out_shape = [
        jax.ShapeDtypeStruct((b, H_DELTA, t, DV_DELTA), F32),
        jax.ShapeDtypeStruct((b, H_DELTA, t, DK_DELTA), BF16),
        jax.ShapeDtypeStruct((b, H_DELTA, t, DK_DELTA), BF16),
        jax.ShapeDtypeStruct((b, H_DELTA, t, CHUNK), BF16),
        jax.ShapeDtypeStruct((b, H_DELTA, npair, DK_DELTA, PAIR), BF16),
        jax.ShapeDtypeStruct((b, H_DELTA, npair, 2 * SUBLANES, LANES), F32),
    ]
    out_specs = [per_head(DV_DELTA), per_head(DK_DELTA), per_head(DK_DELTA), per_head(CHUNK),
                 pl.BlockSpec((1, H_DELTA, pairs, DK_DELTA, PAIR), lambda bi, ti, *_: (bi, 0, ti, 0, 0)),
                 pl.BlockSpec((1, H_DELTA, pairs, 2 * SUBLANES, LANES), lambda bi, ti, *_: (bi, 0, ti, 0, 0))]
    return _hosted_call(
        functools.partial(_delta_prep_kernel, pairs=pairs),
        grid=(b, nt),
        in_specs=[pl.BlockSpec((1, rows, CONV_CH), lambda bi, ti, *_: (bi, ti, 0)),
                  pl.BlockSpec((1, CONV_W - 1, CONV_CH), lambda bi, ti, *_: (bi, 0, 0)),
                  const(conv_w.shape),
                  pl.BlockSpec((1, 2 * H_DELTA, rows), lambda bi, ti, *_: (bi, 0, ti)),
                  const(alog_row.shape), const(dtb_row.shape)],
        out_specs=out_specs,
        out_shape=out_shape,
        scratch_shapes=[pltpu.VMEM((SUBLANES + rows, CONV_CH), F32)],
        inputs=(cin, conv_prev, conv_w, tail, alog_row, dtb_row),
        name="delta_prep", rider=rider)


def _delta_norm_gate(o, normw, z):
    ms = jnp.mean(o * o, axis=-1, keepdims=True)
    return o * lax.rsqrt(ms + DELTA_NORM_EPS) * normw * _silu(z)


def _delta_scan_kernel(u_ref, w_ref, qg_ref, at_ref, kdt_ref, el_ref, z_ref, s0_ref, nw_ref,
                       o_ref, sfin_ref, s_scr, *, nb, nblk):
    t = pl.program_id(1)

    @pl.when(t == 0)
    def _():
        s_scr[...] = s0_ref[...]

    def block(jb, carry):
        r0 = pl.multiple_of(jb * PAIR, PAIR)
        seqs = [(bi, h) for bi in range(nb) for h in range(H_DELTA)]
        for ci in range(2):
            rows = pl.ds(pl.multiple_of(r0 + ci * CHUNK, CHUNK), CHUNK)
            states = [s_scr[bi, h] for bi, h in seqs]
            res = [jnp.dot(jnp.concatenate([w_ref[bi, h, rows, :], qg_ref[bi, h, rows, :]], axis=0),
                           s.astype(BF16), preferred_element_type=F32) for (bi, h), s in zip(seqs, states)]
            v_new = [(u_ref[bi, h, rows, :] - r[:CHUNK]).astype(BF16) for (bi, h), r in zip(seqs, res)]
            upd = [jnp.dot(kdt_ref[bi, h, jb, :, ci * CHUNK:(ci + 1) * CHUNK], v, preferred_element_type=F32)
                   for (bi, h), v in zip(seqs, v_new)]
            intra = [jnp.dot(at_ref[bi, h, rows, :], v, preferred_element_type=F32) for (bi, h), v in zip(seqs, v_new)]
            for (bi, h), s, r, du, oi in zip(seqs, states, res, upd, intra):
                decay = jnp.broadcast_to(el_ref[bi, h, jb, ci * SUBLANES:ci * SUBLANES + 1, :], (DK_DELTA, DV_DELTA))
                s_scr[bi, h] = s * decay + du
                hs = slice(h * DV_DELTA, (h + 1) * DV_DELTA)
                o_ref[bi, rows, hs] = _delta_norm_gate(r[CHUNK:] + oi, nw_ref[...], z_ref[bi, rows, hs]).astype(BF16)
        return carry

    lax.fori_loop(0, nblk, block, 0)

    @pl.when(t == pl.num_programs(1) - 1)
    def _():
        sfin_ref[...] = s_scr[...]


def _delta_scan(prep, z, s0, norm_w):
    u, w, qg, at, kdt, el = prep
    b, _, t, _ = u.shape
    nb = math.gcd(b, 4)
    ts = min(512, t)
    assert t % ts == 0
    nblk = ts // PAIR
    per_head = lambda width: pl.BlockSpec((nb, H_DELTA, ts, width), lambda bi, ti: (bi, 0, ti, 0))
    state = pl.BlockSpec((nb, H_DELTA, DK_DELTA, DV_DELTA), lambda bi, ti: (bi, 0, 0, 0))
    tok = pl.BlockSpec((nb, ts, W_DELTA), lambda bi, ti: (bi, ti, 0))
    return pl.pallas_call(
        functools.partial(_delta_scan_kernel, nb=nb, nblk=nblk),
        grid=(b // nb, t // ts),
        in_specs=[per_head(DV_DELTA), per_head(DK_DELTA), per_head(DK_DELTA), per_head(CHUNK),
                  pl.BlockSpec((nb, H_DELTA, nblk, DK_DELTA, PAIR), lambda bi, ti: (bi, 0, ti, 0, 0)),
                  pl.BlockSpec((nb, H_DELTA, nblk, 2 * SUBLANES, LANES), lambda bi, ti: (bi, 0, ti, 0, 0)),
                  tok, state, pl.BlockSpec((1, DV_DELTA), lambda bi, ti: (0, 0))],
        out_specs=[tok, state],
        out_shape=[jax.ShapeDtypeStruct((b, t, W_DELTA), BF16),
                   jax.ShapeDtypeStruct((b, H_DELTA, DK_DELTA, DV_DELTA), F32)],
        scratch_shapes=[pltpu.VMEM((nb, H_DELTA, DK_DELTA, DV_DELTA), F32)],
        compiler_params=pltpu.CompilerParams(
            dimension_semantics=("arbitrary", "arbitrary"), vmem_limit_bytes=VMEM_LIMIT),
        name="delta_scan",
    )(u, w, qg, at, kdt, el, z, s0, norm_w.reshape(1, DV_DELTA))


def _delta_step_kernel(cin_ref, prev_ref, cw_ref, tail_ref, alog_ref, dtb_ref, z_ref, s0_ref, nw_ref,
                       o_ref, s_ref, *, seqs):
    items = []
    for b in range(seqs):
        prev = prev_ref[b]
        conv = prev[0:1] * cw_ref[0:1, :]
        for j in range(1, CONV_W - 1):
            conv = conv + prev[j:j + 1] * cw_ref[j:j + 1, :]
        conv = _silu(conv + cin_ref[b] * cw_ref[CONV_W - 1:CONV_W, :])
        beta_all, g_all = _gate_rows(tail_ref[b], alog_ref[...], dtb_ref[...])
        for h in range(H_DELTA):
            hs = slice(h * DK_DELTA, (h + 1) * DK_DELTA)
            qs = _l2norm(conv[:, hs]) * DK_DELTA ** -0.5
            kn = _l2norm(conv[:, W_DELTA + h * DK_DELTA:W_DELTA + (h + 1) * DK_DELTA])
            vh = conv[:, 2 * W_DELTA + h * DV_DELTA:2 * W_DELTA + (h + 1) * DV_DELTA]
            beta = beta_all[:, h:h + 1]
            eg = jnp.exp(g_all[:, H_DELTA + h:H_DELTA + h + 1])
            items.append((b, h, hs, qs, kn, vh, beta, eg, s0_ref[b, h]))
    res = [jnp.dot(jnp.concatenate([kn * (beta * eg), qs * eg, jnp.zeros((SUBLANES - 2, DK_DELTA), F32)],
                                   axis=0).astype(BF16),
                   s.astype(BF16), preferred_element_type=F32)
           for (b, h, hs, qs, kn, vh, beta, eg, s) in items]
    for (b, h, hs, qs, kn, vh, beta, eg, s), r in zip(items, res):
        v_new = vh * beta - r[0:1]
        qk = jnp.sum(qs.astype(BF16).astype(F32) * kn.astype(BF16).astype(F32), axis=-1, keepdims=True)
        o = r[1:2] + qk * v_new
        k_col = jnp.broadcast_to(kn, (DK_DELTA, DK_DELTA)).T
        s_ref[b, h] = s * eg + k_col * v_new
        o_ref[b, :, hs] = _delta_norm_gate(o, nw_ref[...], z_ref[b, :, hs]).astype(BF16)


def _delta_step(cin, conv_prev, conv_w, tail, alog_row, dtb_row, z, s0, norm_w):
    bs = cin.shape[0]
    seqs = math.gcd(bs, 4)
    tok = lambda width: pl.BlockSpec((seqs, 1, width), lambda b: (b, 0, 0))
    const = lambda shape: pl.BlockSpec(shape, lambda b: (0,) * len(shape))
    state = pl.BlockSpec((seqs, H_DELTA, DK_DELTA, DV_DELTA), lambda b: (b, 0, 0, 0))
    return pl.pallas_call(
        functools.partial(_delta_step_kernel, seqs=seqs),
        grid=(bs // seqs,),
        in_specs=[tok(CONV_CH), pl.BlockSpec((seqs, CONV_W - 1, CONV_CH), lambda b: (b, 0, 0)), const(conv_w.shape),
                  tok(LANES), const(alog_row.shape), const(dtb_row.shape), tok(W_DELTA), state,
                  const((1, DV_DELTA))],
        out_specs=[tok(W_DELTA), state],
        out_shape=[jax.ShapeDtypeStruct((bs, 1, W_DELTA), BF16),
                   jax.ShapeDtypeStruct((bs, H_DELTA, DK_DELTA, DV_DELTA), F32)],
        compiler_params=pltpu.CompilerParams(dimension_semantics=("arbitrary",)),
        name="delta_step",
    )(cin, conv_prev, conv_w, tail, alog_row, dtb_row, z, s0, norm_w.reshape(1, DV_DELTA))


def _out_proj_kernel(oa_ref, ob_ref, x_ref, w_ref, g_ref, b_ref, y_ref, *, alpha):
    mix = jnp.dot(jnp.concatenate([oa_ref[...], ob_ref[...]], axis=1), w_ref[...], preferred_element_type=F32)
    r = alpha * x_ref[...] + mix
    mu = jnp.mean(r, axis=-1, keepdims=True)
    var = jnp.mean(jnp.square(r - mu), axis=-1, keepdims=True)
    y_ref[...] = (r - mu) * lax.rsqrt(var + LN_EPS) * g_ref[...] + b_ref[...]


def _out_proj(oa, ob, x, w_out_b, ln_g, ln_b, alpha):
    m, d = x.shape
    tm = min(1024, m)
    assert m % tm == 0
    row = lambda width: pl.BlockSpec((tm, width), lambda i: (i, 0))
    const = lambda shape: pl.BlockSpec(shape, lambda i: (0, 0))
    return pl.pallas_call(
        functools.partial(_out_proj_kernel, alpha=alpha),
        grid=(m // tm,),
        in_specs=[row(W_DIFF), row(W_DELTA), row(d), const(w_out_b.shape), const((1, d)), const((1, d))],
        out_specs=row(d),
        out_shape=jax.ShapeDtypeStruct((m, d), F32),
        compiler_params=pltpu.CompilerParams(dimension_semantics=("arbitrary",), vmem_limit_bytes=VMEM_LIMIT),
        name="out_proj",
    )(oa, ob, x, w_out_b, ln_g.reshape(1, d), ln_b.reshape(1, d))


def _pad_lanes(vec, offset):
    return jnp.zeros((1, LANES), F32).at[0, offset:offset + vec.shape[0]].set(vec.astype(F32))


def _gate_rows_param(vec):
    col = jnp.broadcast_to(vec.astype(F32)[:, None], (H_DELTA, LANES))
    return jnp.concatenate([jnp.zeros((H_DELTA, LANES), F32), col], axis=0)


def kernel(x_prompt, x_sample, cache_k, cache_v, page_table, state_delta, state_conv, w_in, conv_w, a_log,
           dt_bias, delta_norm_w, diff_lambda, diff_norm_w, w_out, ln_g, ln_b):
    depth = w_in.shape[0]
    bp, tp, d = x_prompt.shape
    bs, ts, _ = x_sample.shape
    assert ts == 1 and w_in.shape[2] == P_MAIN + 2 * H_DELTA and d == w_out.shape[2]
    past_len = page_table.shape[1] * cache_k.shape[2]
    alpha = (2 * depth) ** 0.25
    tables_p = _rope_tables(jnp.arange(tp, dtype=jnp.int32))
    tables_s = _rope_tables(jnp.full((bs,), past_len, jnp.int32))

    hp, hs = x_prompt, x_sample
    outs = [[] for _ in range(8)]
    for l in range(depth):
        lam_init = _lambda_init(l)
        w_main = w_in[l].astype(BF16)
        w_tail = jnp.zeros((d, LANES), BF16).at[:, :2 * H_DELTA].set(w_in[l, :, P_MAIN:].astype(BF16))
        w_out_b = w_out[l].astype(BF16)
        alog_row = _pad_lanes(a_log[l], H_DELTA)
        dtb_row = _pad_lanes(dt_bias[l], H_DELTA)
        alog_col = _gate_rows_param(a_log[l])
        dtb_col = _gate_rows_param(dt_bias[l])

        sq, sk, _, sv, _, sga, scin, sz, stail = _in_proj(hs.reshape(1, bs, d), w_main, w_tail, tables_s, False)
        tok = lambda a: a.reshape(bs, 1, -1)
        decode_args = dict(q=tok(sq), k_new=tok(sk), v_new=tok(sv), ga=tok(sga), cache_k=cache_k, cache_v=cache_v,
                           layer=l, page_table=page_table, diff_lambda=diff_lambda[l], norm_w=diff_norm_w[l],
                           lam_init=lam_init)
        rider = _DecodeRider(**decode_args, seq0=0, n_seq=bs, grid=(bp, _delta_prep_grid(tp)[1]))
        if not rider.ok:
            rider = None

        q, k, kb, v, vb, ga, cin, z, _, tail_t = _in_proj(hp, w_main, w_tail, tables_p, True)
        oa = _prompt_attn(q, kb, vb, ga, diff_lambda[l], diff_norm_w[l], lam_init)
        prep, soa = _delta_prep(cin, jnp.zeros((bp, CONV_W - 1, CONV_CH), F32), conv_w[l], tail_t, alog_col,
                                dtb_col, rider)
        ob, sp = _delta_scan(prep, z, jnp.zeros((bp, H_DELTA, DK_DELTA, DV_DELTA), F32), delta_norm_w[l])
        hp = _out_proj(oa.reshape(bp * tp, W_DIFF), ob.reshape(bp * tp, W_DELTA), hp.reshape(bp * tp, d),
                       w_out_b, ln_g[l], ln_b[l], alpha).reshape(bp, tp, d)
        outs[0].append(k.reshape(bp, tp, H_DIFF, 2, DH_DIFF))
        outs[1].append(v.reshape(bp, tp, H_DIFF, DV_DIFF))
        outs[2].append(sp)
        outs[3].append(cin[:, tp - (CONV_W - 1):, :])

        if rider is None:
            soa = _decode_attn(decode_args, bs)
        sob, ss = _delta_step(tok(scin), state_conv[l], conv_w[l], tok(stail), alog_row, dtb_row, tok(sz),
                              state_delta[l], delta_norm_w[l])
        hs = _out_proj(soa.reshape(bs, W_DIFF), sob.reshape(bs, W_DELTA), hs.reshape(bs, d),
                       w_out_b, ln_g[l], ln_b[l], alpha).reshape(bs, 1, d)
        outs[4].append(sk.reshape(bs, 1, H_DIFF, 2, DH_DIFF))
        outs[5].append(sv.reshape(bs, 1, H_DIFF, DV_DIFF))
        outs[6].append(ss)
        outs[7].append(jnp.concatenate([state_conv[l][:, 1:, :], tok(scin)], axis=1))
    return (hp, hs) + tuple(jnp.stack(o) for o in outs)
```
